```python
import math
import jax, jax.numpy as jnp
from jax import lax
import numpy as np

D_MODEL = 1024
BATCH = 4
SEQ = 4096
DEPTH = 4

HEAD_DIM = 64
N_Q_HEADS = 8
N_KV_HEADS = 2
GQA_GROUPS = N_Q_HEADS // N_KV_HEADS
ATTN_WIDTH = N_Q_HEADS * HEAD_DIM
KV_WIDTH = N_KV_HEADS * HEAD_DIM
MIX_WIDTH = D_MODEL // 2
CONV_CH = MIX_WIDTH
CONV_WIDTH = 31
SGU_CH = MIX_WIDTH
SGU_GROUPS = 4
SGU_GROUP_CH = SGU_CH // SGU_GROUPS
CHUNK = 128
WINDOW = 128
BLOCK_Q = 128
GRID_W = 64
ROPE_THETA = 10000.0
IN_WIDTH = 2 * MIX_WIDTH + ATTN_WIDTH + 2 * KV_WIDTH
OUT_WIDTH = 2 * MIX_WIDTH
PEER_HEADS = 8
PEER_NKEYS = 128
PEER_EXPERTS = PEER_NKEYS * PEER_NKEYS
PEER_DKEY = 128
PEER_TOPK = 16
PEER_TOK_CHUNK = 128
EPS = 1e-6
NEG = -1e30
N_EVEN = (DEPTH + 1) // 2
N_ODD = DEPTH // 2

kernel_name = 'hybrid_conv_swa_axial_sgu_peer_encoder'


def rmsnorm(x, g):
    xf = x.astype(jnp.float32)
    y = xf * lax.rsqrt(jnp.mean(xf * xf, axis=-1, keepdims=True) + EPS)
    return (y * g.astype(jnp.float32)).astype(x.dtype)


def layernorm(x, g, b):
    xf = x.astype(jnp.float32)
    mu = jnp.mean(xf, axis=-1, keepdims=True)
    xc = xf - mu
    var = jnp.mean(xc * xc, axis=-1, keepdims=True)
    return (xc * lax.rsqrt(var + EPS) * g.astype(jnp.float32) + b.astype(jnp.float32)).astype(x.dtype)


def rope_angles(pos, dim):
    inv = ROPE_THETA ** (-jnp.arange(0, dim, 2, dtype=jnp.float32) / dim)
    ang = pos.astype(jnp.float32)[:, None] * inv[None, :]
    ang = jnp.concatenate([ang, ang], axis=-1)
    return jnp.cos(ang), jnp.sin(ang)


def apply_rope(x, cos, sin):
    xf = x.astype(jnp.float32)
    x1, x2 = jnp.split(xf, 2, axis=-1)
    rot = jnp.concatenate([-x2, x1], axis=-1)
    return (xf * cos[:, None, :] + rot * sin[:, None, :]).astype(x.dtype)


def apply_axial_rope(x, cos_r, sin_r, cos_c, sin_c):
    half = x.shape[-1] // 2
    return jnp.concatenate([apply_rope(x[..., :half], cos_r, sin_r),
                            apply_rope(x[..., half:], cos_c, sin_c)], axis=-1)


def conformer_conv(a_in, conv_w, conv_b, ln_g, ln_b):
    a = a_in[..., :CONV_CH] * jax.nn.sigmoid(a_in[..., CONV_CH:])
    a = lax.conv_general_dilated(a, conv_w.astype(a.dtype), window_strides=(1,),
                                 padding=[(CONV_WIDTH // 2, CONV_WIDTH // 2)],
                                 dimension_numbers=('NWC', 'WIO', 'NWC'),
                                 feature_group_count=CONV_CH)
    a = a + conv_b.astype(a.dtype)
    a = layernorm(a, ln_g, ln_b)
    return jax.nn.silu(a)


def windowed_sink_attention(q, k, v, sink):
    B, S, _, Dh = q.shape
    nb = S // BLOCK_Q
    qb = q.reshape(B, nb, BLOCK_Q, N_KV_HEADS, GQA_GROUPS, Dh)
    pad = ((0, 0), (BLOCK_Q, BLOCK_Q), (0, 0), (0, 0))
    kp = jnp.pad(k, pad).reshape(B, nb + 2, BLOCK_Q, N_KV_HEADS, Dh)
    vp = jnp.pad(v, pad).reshape(B, nb + 2, BLOCK_Q, N_KV_HEADS, Dh)
    kb = jnp.concatenate([kp[:, :-2], kp[:, 1:-1], kp[:, 2:]], axis=2)
    vb = jnp.concatenate([vp[:, :-2], vp[:, 1:-1], vp[:, 2:]], axis=2)
    s = jnp.einsum('bnqhgd,bnkhd->bhgnqk', qb, kb).astype(jnp.float32) * (Dh ** -0.5)
    blk = jnp.arange(nb)[:, None, None]
    qpos = blk * BLOCK_Q + jnp.arange(BLOCK_Q)[None, :, None]
    kpos = (blk - 1) * BLOCK_Q + jnp.arange(3 * BLOCK_Q)[None, None, :]
    mask = (jnp.abs(kpos - qpos) <= WINDOW) & (kpos >= 0) & (kpos < S)
    s = jnp.where(mask, s, NEG)
    sink_b = sink.astype(jnp.float32).reshape(1, N_KV_HEADS, GQA_GROUPS, 1, 1, 1)
    m = jnp.maximum(jnp.max(s, axis=-1, keepdims=True), sink_b)
    p = jnp.exp(s - m)
    denom = jnp.sum(p, axis=-1, keepdims=True) + jnp.exp(sink_b - m)
    p = (p / denom).astype(v.dtype)
    o = jnp.einsum('bhgnqk,bnkhd->bnqhgd', p, vb)
    return o.reshape(B, S, N_Q_HEADS * Dh)


def dense_block_attention(q, k, v):
    B, S, _, Dh = q.shape
    nb = S // BLOCK_Q
    qb = q.reshape(B, nb, BLOCK_Q, N_KV_HEADS, GQA_GROUPS, Dh).transpose(1, 0, 2, 3, 4, 5)
    scale = Dh ** -0.5

    def one_block(qblk):
        s = jnp.einsum('bqhgd,bkhd->bhgqk', qblk, k).astype(jnp.float32) * scale
        p = jax.nn.softmax(s, axis=-1).astype(v.dtype)
        return jnp.einsum('bhgqk,bkhd->bqhgd', p, v)

    o = lax.map(one_block, qb)
    return o.transpose(1, 0, 2, 3, 4, 5).reshape(B, S, N_Q_HEADS * Dh)


def spatial_gating(d_in, ln_g, ln_b, w_s, b_s):
    z = jax.nn.gelu(d_in)
    u, v = z[..., :SGU_CH], z[..., SGU_CH:]
    v = layernorm(v, ln_g, ln_b)
    B, S, _ = v.shape
    nc = S // CHUNK
    vc = v.reshape(B, nc, CHUNK, SGU_GROUPS, SGU_GROUP_CH)
    mixed = jnp.einsum('gpq,bnqgc->bnpgc', w_s.astype(v.dtype), vc) + b_s.T.astype(v.dtype)[:, :, None]
    return u * mixed.reshape(B, S, SGU_CH)


def peer(h, wq, subkeys, u_tab, v_tab):
    B, S, D = h.shape
    T = B * S
    ht = h.reshape(T, D)
    q = (ht @ wq).reshape(T, PEER_HEADS, 2, PEER_DKEY // 2)
    s = jnp.einsum('thpd,hpkd->thpk', q, subkeys.astype(q.dtype)).astype(jnp.float32)
    sv, si = lax.top_k(s, PEER_TOPK)
    cand = (sv[:, :, 0, :, None] + sv[:, :, 1, None, :]).reshape(T, PEER_HEADS, PEER_TOPK * PEER_TOPK)
    fv, fi = lax.top_k(cand, PEER_TOPK)
    i1 = jnp.take_along_axis(si[:, :, 0, :], fi // PEER_TOPK, axis=-1)
    i2 = jnp.take_along_axis(si[:, :, 1, :], fi % PEER_TOPK, axis=-1)
    eidx = i1 * PEER_NKEYS + i2
    gate = jax.nn.softmax(fv, axis=-1)
    nchunk = T // PEER_TOK_CHUNK

    def chunk_fn(args):
        xc, ec, gc = args
        uc = u_tab[ec]
        a = jnp.einsum('chkd,cd->chk', uc, xc).astype(jnp.float32)
        a = (jax.nn.gelu(a) * gc).astype(xc.dtype)
        vc = v_tab[ec]
        return jnp.einsum('chk,chkd->cd', a, vc)

    out = lax.map(chunk_fn, (ht.reshape(nchunk, PEER_TOK_CHUNK, D),
                             eidx.reshape(nchunk, PEER_TOK_CHUNK, PEER_HEADS, PEER_TOPK),
                             gate.reshape(nchunk, PEER_TOK_CHUNK, PEER_HEADS, PEER_TOPK)))
    return out.reshape(B, S, D)


def setup_inputs(seed: int = 0) -> dict:
    key = jax.random.key(seed)
    ks = jax.random.split(key, 24)
    f32 = jnp.float32

    def nrm(k, shape, scale):
        return jax.random.normal(k, shape, f32) * scale

    return {
        'x': nrm(ks[0], (BATCH, SEQ, D_MODEL), 1.0),
        'mix_norm_g': 1.0 + nrm(ks[1], (DEPTH, D_MODEL), 0.02),
        'ffn_norm_g': 1.0 + nrm(ks[2], (DEPTH, D_MODEL), 0.02),
        'final_norm_g': 1.0 + nrm(ks[3], (D_MODEL,), 0.02),
        'even_w_in': nrm(ks[4], (N_EVEN, D_MODEL, IN_WIDTH), D_MODEL ** -0.5),
        'even_w_out': nrm(ks[5], (N_EVEN, OUT_WIDTH, D_MODEL), OUT_WIDTH ** -0.5),
        'conv_w': nrm(ks[6], (N_EVEN, CONV_WIDTH, 1, CONV_CH), CONV_WIDTH ** -0.5),
        'conv_b': nrm(ks[7], (N_EVEN, CONV_CH), 0.02),
        'conv_ln_g': 1.0 + nrm(ks[8], (N_EVEN, CONV_CH), 0.02),
        'conv_ln_b': nrm(ks[9], (N_EVEN, CONV_CH), 0.02),
        'sink_logits': nrm(ks[10], (N_EVEN, N_Q_HEADS), 0.5),
        'odd_w_in': nrm(ks[11], (N_ODD, D_MODEL, IN_WIDTH), D_MODEL ** -0.5),
        'odd_w_out': nrm(ks[12], (N_ODD, OUT_WIDTH, D_MODEL), OUT_WIDTH ** -0.5),
        'q_norm_g': 1.0 + nrm(ks[13], (N_ODD, HEAD_DIM), 0.02),
        'k_norm_g': 1.0 + nrm(ks[14], (N_ODD, HEAD_DIM), 0.02),
        'sgu_ln_g': 1.0 + nrm(ks[15], (N_ODD, SGU_CH), 0.02),
        'sgu_ln_b': nrm(ks[16], (N_ODD, SGU_CH), 0.02),
        'sgu_w': nrm(ks[17], (N_ODD, SGU_GROUPS, CHUNK, CHUNK), CHUNK ** -0.5),
        'sgu_b': 1.0 + nrm(ks[18], (N_ODD, SGU_GROUPS, CHUNK), 0.02),
        'peer_wq': nrm(ks[19], (DEPTH, D_MODEL, PEER_HEADS * PEER_DKEY), D_MODEL ** -0.5),
        'peer_subkeys': nrm(ks[20], (DEPTH, PEER_HEADS, 2, PEER_NKEYS, PEER_DKEY // 2), (PEER_DKEY // 2) ** -0.5),
        'peer_u': nrm(ks[21], (DEPTH, PEER_EXPERTS, D_MODEL), D_MODEL ** -0.5),
        'peer_v': nrm(ks[22], (DEPTH, PEER_EXPERTS, D_MODEL), (PEER_HEADS * PEER_TOPK) ** -0.5),
    }


def reference(x, mix_norm_g, ffn_norm_g, final_norm_g, even_w_in, even_w_out, conv_w, conv_b,
              conv_ln_g, conv_ln_b, sink_logits, odd_w_in, odd_w_out, q_norm_g, k_norm_g,
              sgu_ln_g, sgu_ln_b, sgu_w, sgu_b, peer_wq, peer_subkeys, peer_u, peer_v):
    B, S, _ = x.shape
    ROWS = S // GRID_W
    pos = jnp.arange(S)
    cos1, sin1 = rope_angles(pos, HEAD_DIM)
    rows = jnp.repeat(jnp.arange(ROWS), GRID_W, total_repeat_length=S)
    cols = jnp.tile(jnp.arange(GRID_W), ROWS)
    cos_r, sin_r = rope_angles(rows, HEAD_DIM // 2)
    cos_c, sin_c = rope_angles(cols, HEAD_DIM // 2)
    o1 = 2 * MIX_WIDTH
    o2 = o1 + ATTN_WIDTH
    o3 = o2 + KV_WIDTH
    for layer in range(DEPTH):
        i = layer // 2
        h = rmsnorm(x, mix_norm_g[layer])
        if layer % 2 == 0:
            z = h @ even_w_in[i]
            a_out = conformer_conv(z[..., :o1], conv_w[i], conv_b[i], conv_ln_g[i], conv_ln_b[i])
            q = apply_rope(z[..., o1:o2].reshape(B, S, N_Q_HEADS, HEAD_DIM), cos1, sin1)
            k = apply_rope(z[..., o2:o3].reshape(B, S, N_KV_HEADS, HEAD_DIM), cos1, sin1)
            v = z[..., o3:].reshape(B, S, N_KV_HEADS, HEAD_DIM)
            b_out = windowed_sink_attention(q, k, v, sink_logits[i])
            y = jnp.concatenate([a_out, b_out], axis=-1) @ even_w_out[i]
        else:
            z = h @ odd_w_in[i]
            c1 = ATTN_WIDTH
            c2 = c1 + KV_WIDTH
            c3 = c2 + KV_WIDTH
            q = rmsnorm(z[..., :c1].reshape(B, S, N_Q_HEADS, HEAD_DIM), q_norm_g[i])
            k = rmsnorm(z[..., c1:c2].reshape(B, S, N_KV_HEADS, HEAD_DIM), k_norm_g[i])
            v = z[..., c2:c3].reshape(B, S, N_KV_HEADS, HEAD_DIM)
            q = apply_axial_rope(q, cos_r, sin_r, cos_c, sin_c)
            k = apply_axial_rope(k, cos_r, sin_r, cos_c, sin_c)
            c_out = dense_block_attention(q, k, v)
            d_out = spatial_gating(z[..., c3:], sgu_ln_g[i], sgu_ln_b[i], sgu_w[i], sgu_b[i])
            y = jnp.concatenate([c_out, d_out], axis=-1) @ odd_w_out[i]
        x = x + y
        h = rmsnorm(x, ffn_norm_g[layer])
        x = x + peer(h, peer_wq[layer], peer_subkeys[layer], peer_u[layer], peer_v[layer])
    return rmsnorm(x, final_norm_g)
```

```python
import functools
import math

import jax
import jax.numpy as jnp
from jax import lax
from jax.experimental import pallas as pl
from jax.experimental.pallas import tpu as pltpu

F32 = jnp.float32
BF16 = jnp.bfloat16

D_MODEL = 1024
HEAD_DIM = 64
N_Q_HEADS = 8
N_KV_HEADS = 2
GQA_GROUPS = N_Q_HEADS // N_KV_HEADS
ATTN_WIDTH = N_Q_HEADS * HEAD_DIM
KV_WIDTH = N_KV_HEADS * HEAD_DIM
MIX_WIDTH = D_MODEL // 2
CONV_WIDTH = 31
CONV_HALO = 16
SGU_GROUPS = 4
CHUNK = 128
WINDOW = 128
BLOCK_Q = 128
GRID_W = 64
ROPE_THETA = 10000.0
PEER_HEADS = 8
PEER_NKEYS = 128
PEER_DKEY = 128
PEER_TOPK = 16
EPS = 1e-6
NEG = -1e30

VMEM_LIMIT_BYTES = 56 * 1024 * 1024

ROW_TILE = 512
CONV_ROWS = 64
ATT_TQ = 256
ATT_TK = 512
PEER_TT = 512
PEER_A = 8
LANE = 128


def _cparams(sem):
    return pltpu.CompilerParams(dimension_semantics=sem, vmem_limit_bytes=VMEM_LIMIT_BYTES)


def _dot(a, b):
    return jnp.dot(a, b, preferred_element_type=F32)


def _dot_nt(a, b):
    return lax.dot_general(a, b, (((1,), (1,)), ((), ())), preferred_element_type=F32)


def _rms(x, g):
    return x * lax.rsqrt(jnp.mean(x * x, axis=-1, keepdims=True) + EPS) * g


def _layernorm(x, g, b):
    mu = jnp.mean(x, axis=-1, keepdims=True)
    xc = x - mu
    var = jnp.mean(xc * xc, axis=-1, keepdims=True)
    return xc * lax.rsqrt(var + EPS) * g + b


def _rope(x, cos, sin_signed, group, half):
    n = x.shape[-1]
    lane = lax.broadcasted_iota(jnp.int32, x.shape, x.ndim - 1)
    first = (lane % group) < half
    partner = jnp.where(first, pltpu.roll(x, n - half, x.ndim - 1), pltpu.roll(x, half, x.ndim - 1))
    return x * cos + partner * sin_signed


def _inproj_even_kernel(x_ref, g_ref, w_ref, cos_ref, sin_ref, a_ref, q_ref, k_ref, v_ref):
    h = _rms(x_ref[...], g_ref[...]).astype(BF16)
    o1 = 2 * MIX_WIDTH
    o2 = o1 + ATTN_WIDTH
    o3 = o2 + KV_WIDTH
    a_ref[...] = _dot(h, w_ref[:, :o1])
    cos = cos_ref[...]
    sin = sin_ref[...]
    q = _rope(_dot(h, w_ref[:, o1:o2]), cos, sin, HEAD_DIM, HEAD_DIM // 2)
    q_ref[...] = (q * (HEAD_DIM ** -0.5)).astype(BF16)
    k = _rope(_dot(h, w_ref[:, o2:o3]), cos[:, :KV_WIDTH], sin[:, :KV_WIDTH], HEAD_DIM, HEAD_DIM // 2)
    k_ref[...] = k.astype(BF16)
    v_ref[...] = _dot(h, w_ref[:, o3:]).astype(BF16)


def _inproj_even(x, g, w, cos, sin, seq):
    t = x.shape[0]
    tm = ROW_TILE
    nseq = seq // tm
    in_w = w.shape[1]
    row = lambda i: (i, 0)
    fixed = lambda i: (0, 0)
    pos = lambda i: (i % nseq, 0)
    return pl.pallas_call(
        _inproj_even_kernel,
        grid=(t // tm,),
        in_specs=[
            pl.BlockSpec((tm, D_MODEL), row),
            pl.BlockSpec((1, D_MODEL), fixed),
            pl.BlockSpec((D_MODEL, in_w), fixed),
            pl.BlockSpec((tm, ATTN_WIDTH), pos),
            pl.BlockSpec((tm, ATTN_WIDTH), pos),
        ],
        out_specs=[
            pl.BlockSpec((tm, 2 * MIX_WIDTH), row),
            pl.BlockSpec((tm, ATTN_WIDTH), row),
            pl.BlockSpec((tm, KV_WIDTH), row),
            pl.BlockSpec((tm, KV_WIDTH), row),
        ],
        out_shape=[
            jax.ShapeDtypeStruct((t, 2 * MIX_WIDTH), F32),
            jax.ShapeDtypeStruct((t, ATTN_WIDTH), BF16),
            jax.ShapeDtypeStruct((t, KV_WIDTH), BF16),
            jax.ShapeDtypeStruct((t, KV_WIDTH), BF16),
        ],
        compiler_params=_cparams(("parallel",)),
        name="inproj_even",
    )(x, g, w, cos, sin)


def _head_sumsq(x):
    n = x.shape[-1]
    r = lax.broadcasted_iota(jnp.int32, (n, n), 0) // HEAD_DIM
    c = lax.broadcasted_iota(jnp.int32, (n, n), 1) // HEAD_DIM
    ones = jnp.where(r == c, 1.0, 0.0).astype(BF16)
    sq = x * x
    hi = sq.astype(BF16)
    lo = (sq - hi.astype(F32)).astype(BF16)
    return _dot(hi, ones) + _dot(lo, ones)


def _inproj_odd_kernel(x_ref, g_ref, w_ref, qg_ref, kg_ref, cos_ref, sin_ref, q_ref, k_ref, v_ref, d_ref):
    h = _rms(x_ref[...], g_ref[...]).astype(BF16)
    c1 = ATTN_WIDTH
    c2 = c1 + KV_WIDTH
    c3 = c2 + KV_WIDTH
    cos = cos_ref[...]
    sin = sin_ref[...]
    half = HEAD_DIM // 2
    q = _dot(h, w_ref[:, :c1])
    q = q * lax.rsqrt(_head_sumsq(q) * (1.0 / HEAD_DIM) + EPS) * qg_ref[...]
    q = _rope(q, cos, sin, half, half // 2)
    q_ref[...] = (q * (HEAD_DIM ** -0.5)).astype(BF16)
    k = _dot(h, w_ref[:, c1:c2])
    k = k * lax.rsqrt(_head_sumsq(k) * (1.0 / HEAD_DIM) + EPS) * kg_ref[...]
    k = _rope(k, cos[:, :KV_WIDTH], sin[:, :KV_WIDTH], half, half // 2).astype(BF16)
    v = _dot(h, w_ref[:, c2:c3]).astype(BF16)
    for hh in range(N_KV_HEADS):
        k_ref[hh] = k[:, hh * HEAD_DIM:(hh + 1) * HEAD_DIM]
        v_ref[hh] = v[:, hh * HEAD_DIM:(hh + 1) * HEAD_DIM]
    d_ref[...] = _dot(h, w_ref[:, c3:])


def _inproj_odd(x, g, w, qg, kg, cos, sin, seq):
    t = x.shape[0]
    tm = ROW_TILE
    nseq = seq // tm
    in_w = w.shape[1]
    row = lambda i: (i, 0)
    fixed = lambda i: (0, 0)
    pos = lambda i: (i % nseq, 0)
    kv = lambda i: (0, i, 0)
    return pl.pallas_call(
        _inproj_odd_kernel,
        grid=(t // tm,),
        in_specs=[
            pl.BlockSpec((tm, D_MODEL), row),
            pl.BlockSpec((1, D_MODEL), fixed),
            pl.BlockSpec((D_MODEL, in_w), fixed),
            pl.BlockSpec((1, ATTN_WIDTH), fixed),
            pl.BlockSpec((1, KV_WIDTH), fixed),
            pl.BlockSpec((tm, ATTN_WIDTH), pos),
            pl.BlockSpec((tm, ATTN_WIDTH), pos),
        ],
        out_specs=[
            pl.BlockSpec((tm, ATTN_WIDTH), row),
            pl.BlockSpec((N_KV_HEADS, tm, HEAD_DIM), kv),
            pl.BlockSpec((N_KV_HEADS, tm, HEAD_DIM), kv),
            pl.BlockSpec((tm, 2 * MIX_WIDTH), row),
        ],
        out_shape=[
            jax.ShapeDtypeStruct((t, ATTN_WIDTH), BF16),
            jax.ShapeDtypeStruct((N_KV_HEADS, t, HEAD_DIM), BF16),
            jax.ShapeDtypeStruct((N_KV_HEADS, t, HEAD_DIM), BF16),
            jax.ShapeDtypeStruct((t, 2 * MIX_WIDTH), F32),
        ],
        compiler_params=_cparams(("parallel",)),
        name="inproj_odd",
    )(x, g, w, qg, kg, cos, sin)


def _glu(x):
    return x[:, :MIX_WIDTH] * jax.nn.sigmoid(x[:, MIX_WIDTH:])


def _conv_kernel(prev_ref, cur_ref, next_ref, w_ref, b_ref, g_ref, beta_ref, o_ref, buf_ref):
    i = pl.program_id(1)
    ns = pl.num_programs(1)
    ts = cur_ref.shape[0]
    halo = CONV_HALO
    buf_ref[0:halo, :] = jnp.where(i > 0, _glu(prev_ref[...]), 0.0)
    buf_ref[halo:halo + ts, :] = _glu(cur_ref[...])
    buf_ref[halo + ts:, :] = jnp.where(i < ns - 1, _glu(next_ref[...]), 0.0)
    w = w_ref[...]
    bias = b_ref[...]
    g = g_ref[...]
    beta = beta_ref[...]
    first = halo - CONV_WIDTH // 2
    for c in range(ts // CONV_ROWS):
        r0 = c * CONV_ROWS
        acc = jnp.zeros((CONV_ROWS, MIX_WIDTH), F32)
        for k in range(CONV_WIDTH):
            acc = acc + w[k:k + 1, :] * buf_ref[r0 + first + k:r0 + first + k + CONV_ROWS, :]
        y = _layernorm(acc + bias, g, beta)
        o_ref[r0:r0 + CONV_ROWS, :] = (y * jax.nn.sigmoid(y)).astype(o_ref.dtype)


def _conv_module(a_in, w, b, g, beta, batch, seq):
    t = a_in.shape[0]
    ts = ROW_TILE
    ns = seq // ts
    hb = ts // CONV_HALO
    last = t // CONV_HALO - 1
    fixed = lambda bb, i: (0, 0)
    return pl.pallas_call(
        _conv_kernel,
        grid=(batch, ns),
        in_specs=[
            pl.BlockSpec((CONV_HALO, 2 * MIX_WIDTH), lambda bb, i: (jnp.maximum((bb * ns + i) * hb - 1, 0), 0)),
            pl.BlockSpec((ts, 2 * MIX_WIDTH), lambda bb, i: (bb * ns + i, 0)),
            pl.BlockSpec((CONV_HALO, 2 * MIX_WIDTH), lambda bb, i: (jnp.minimum((bb * ns + i + 1) * hb, last), 0)),
            pl.BlockSpec((CONV_WIDTH, MIX_WIDTH), fixed),
            pl.BlockSpec((1, MIX_WIDTH), fixed),
            pl.BlockSpec((1, MIX_WIDTH), fixed),
            pl.BlockSpec((1, MIX_WIDTH), fixed),
        ],
        out_specs=pl.BlockSpec((ts, MIX_WIDTH), lambda bb, i: (bb * ns + i, 0)),
        out_shape=jax.ShapeDtypeStruct((t, MIX_WIDTH), BF16),
        scratch_shapes=[pltpu.VMEM((ts + 2 * CONV_HALO, MIX_WIDTH), F32)],
        compiler_params=_cparams(("parallel", "parallel")),
        name="conv_module",
    )(a_in, a_in, a_in, w, b, g, beta)


def _swa_kernel(q_ref, kp_ref, kc_ref, kn_ref, vp_ref, vc_ref, vn_ref, sink_ref, o_ref):
    n = pl.program_id(1)
    nb = pl.num_programs(1)
    bq = q_ref.shape[0]
    ii = lax.broadcasted_iota(jnp.int32, (bq, 3 * bq), 0)
    jj = lax.broadcasted_iota(jnp.int32, (bq, 3 * bq), 1)
    lo = jnp.where(n > 0, 0, bq)
    hi = jnp.where(n < nb - 1, 3 * bq, 2 * bq)
    mask = (jj >= ii) & (jj <= ii + 2 * WINDOW) & (jj >= lo) & (jj < hi)
    kcat = jnp.concatenate([kp_ref[...], kc_ref[...], kn_ref[...]], axis=0)
    vcat = jnp.concatenate([vp_ref[...], vc_ref[...], vn_ref[...]], axis=0)
    q = q_ref[...]
    outs = []
    for hd in range(N_Q_HEADS):
        hk = hd // GQA_GROUPS
        kh = kcat[:, hk * HEAD_DIM:(hk + 1) * HEAD_DIM]
        vh = vcat[:, hk * HEAD_DIM:(hk + 1) * HEAD_DIM]
        s = _dot_nt(q[:, hd * HEAD_DIM:(hd + 1) * HEAD_DIM], kh)
        s = jnp.where(mask, s, NEG)
        sink = sink_ref[hd:hd + 1, 0:1]
        m = jnp.maximum(jnp.max(s, axis=-1, keepdims=True), sink)
        p = jnp.exp(s - m)
        denom = jnp.sum(p, axis=-1, keepdims=True) + jnp.exp(sink - m)
        outs.append(_dot(p.astype(BF16), vh) / denom)
    o_ref[...] = jnp.concatenate(outs, axis=-1).astype(o_ref.dtype)


def _swa(q, k, v, sink_b, batch, seq):
    t = q.shape[0]
    bq = BLOCK_Q
    nb = seq // bq
    last = t // bq - 1
    cur = lambda bb, n: (bb * nb + n, 0)
    prev = lambda bb, n: (jnp.maximum(bb * nb + n - 1, 0), 0)
    nxt = lambda bb, n: (jnp.minimum(bb * nb + n + 1, last), 0)
    kvs = lambda im: pl.BlockSpec((bq, KV_WIDTH), im)
    return pl.pallas_call(
        _swa_kernel,
        grid=(batch, nb),
        in_specs=[pl.BlockSpec((bq, ATTN_WIDTH), cur), kvs(prev), kvs(cur), kvs(nxt), kvs(prev), kvs(cur), kvs(nxt),
                  pl.BlockSpec((N_Q_HEADS, LANE), lambda bb, n: (0, 0))],
        out_specs=pl.BlockSpec((bq, ATTN_WIDTH), cur),
        out_shape=jax.ShapeDtypeStruct((t, ATTN_WIDTH), BF16),
        compiler_params=_cparams(("parallel", "parallel")),
        name="window_attn",
    )(q, k, k, k, v, v, v, sink_b)


def _dense_attn_kernel(q_ref, k_ref, v_ref, o_ref, qs_ref, m_ref, l_ref, acc_ref):
    ki = pl.program_id(3)
    tq = q_ref.shape[0]

    @pl.when(ki == 0)
    def _():
        q = q_ref[...]
        for g in range(GQA_GROUPS):
            qs_ref[g * tq:(g + 1) * tq, :] = q[:, g * HEAD_DIM:(g + 1) * HEAD_DIM]
        m_ref[...] = jnp.full(m_ref.shape, -jnp.inf, F32)
        l_ref[...] = jnp.zeros(l_ref.shape, F32)
        acc_ref[...] = jnp.zeros(acc_ref.shape, F32)

    s = _dot_nt(qs_ref[...], k_ref[...])
    m_old = m_ref[...]
    m_new = jnp.maximum(m_old, jnp.max(s, axis=-1, keepdims=True))
    alpha = jnp.exp(m_old - m_new)
    p = jnp.exp(s - m_new)
    l_ref[...] = alpha * l_ref[...] + jnp.sum(p, axis=-1, keepdims=True)
    acc_ref[...] = alpha * acc_ref[...] + _dot(p.astype(BF16), v_ref[...])
    m_ref[...] = m_new

    @pl.when(ki == pl.num_programs(3) - 1)
    def _():
        o = acc_ref[...] / l_ref[...]
        o_ref[...] = jnp.concatenate([o[g * tq:(g + 1) * tq, :] for g in range(GQA_GROUPS)], axis=-1).astype(o_ref.dtype)


def _dense_attn(q, k, v, batch, seq):
    t = q.shape[0]
    tq = min(ATT_TQ, seq)
    tk = min(ATT_TK, seq)
    nq = seq // tq
    nk = seq // tk
    gw = GQA_GROUPS * HEAD_DIM
    qmap = lambda bb, hk, qi, ki: (bb * nq + qi, hk)
    kmap = lambda bb, hk, qi, ki: (hk, bb * nk + ki, 0)
    return pl.pallas_call(
        _dense_attn_kernel,
        grid=(batch, N_KV_HEADS, nq, nk),
        in_specs=[
            pl.BlockSpec((tq, gw), qmap),
            pl.BlockSpec((None, tk, HEAD_DIM), kmap),
            pl.BlockSpec((None, tk, HEAD_DIM), kmap),
        ],
        out_specs=pl.BlockSpec((tq, gw), qmap),
        out_shape=jax.ShapeDtypeStruct((t, ATTN_WIDTH), BF16),
        scratch_shapes=[
            pltpu.VMEM((GQA_GROUPS * tq, HEAD_DIM), BF16),
            pltpu.VMEM((GQA_GROUPS * tq, 1), F32),
            pltpu.VMEM((GQA_GROUPS * tq, 1), F32),
            pltpu.VMEM((GQA_GROUPS * tq, HEAD_DIM), F32),
        ],
        compiler_params=_cparams(("parallel", "parallel", "parallel", "arbitrary")),
        name="dense_attn",
    )(q, k, v)


def _sgu_kernel(d_ref, g_ref, b_ref, w_ref, bs_ref, o_ref):
    z = jax.nn.gelu(d_ref[...])
    u = z[:, :MIX_WIDTH]
    v = _layernorm(z[:, MIX_WIDTH:], g_ref[...], b_ref[...]).astype(BF16)
    gc = MIX_WIDTH // SGU_GROUPS
    for n in range(d_ref.shape[0] // CHUNK):
        rows = slice(n * CHUNK, (n + 1) * CHUNK)
        for g in range(SGU_GROUPS):
            cols = slice(g * gc, (g + 1) * gc)
            mixed = _dot(w_ref[g], v[rows, cols]) + bs_ref[g]
            o_ref[rows, cols] = (u[rows, cols] * mixed).astype(o_ref.dtype)


def _sgu(d_in, g, b, w, bs):
    t = d_in.shape[0]
    ts = ROW_TILE
    gc = MIX_WIDTH // SGU_GROUPS
    row = lambda i: (i, 0)
    fixed = lambda i: (0, 0)
    fixed3 = lambda i: (0, 0, 0)
    return pl.pallas_call(
        _sgu_kernel,
        grid=(t // ts,),
        in_specs=[
            pl.BlockSpec((ts, 2 * MIX_WIDTH), row),
            pl.BlockSpec((1, MIX_WIDTH), fixed),
            pl.BlockSpec((1, MIX_WIDTH), fixed),
            pl.BlockSpec((SGU_GROUPS, CHUNK, CHUNK), fixed3),
            pl.BlockSpec((SGU_GROUPS, CHUNK, gc), fixed3),
        ],
        out_specs=pl.BlockSpec((ts, MIX_WIDTH), row),
        out_shape=jax.ShapeDtypeStruct((t, MIX_WIDTH), BF16),
        compiler_params=_cparams(("parallel",)),
        name="spatial_gating",
    )(d_in, g, b, w, bs)


def _outproj_kernel(x_ref, a_ref, b_ref, w_ref, g_ref, xn_ref, hnt_ref):
    y = _dot(a_ref[...], w_ref[:MIX_WIDTH, :]) + _dot(b_ref[...], w_ref[MIX_WIDTH:, :])
    xn = x_ref[...] + y
    xn_ref[...] = xn
    hnt_ref[...] = _rms(xn, g_ref[...]).T.astype(BF16)


def _outproj(x, a, b, w, g):
    t = x.shape[0]
    tm = ROW_TILE
    row = lambda i: (i, 0)
    fixed = lambda i: (0, 0)
    return pl.pallas_call(
        _outproj_kernel,
        grid=(t // tm,),
        in_specs=[
            pl.BlockSpec((tm, D_MODEL), row),
            pl.BlockSpec((tm, MIX_WIDTH), row),
            pl.BlockSpec((tm, MIX_WIDTH), row),
            pl.BlockSpec((2 * MIX_WIDTH, D_MODEL), fixed),
            pl.BlockSpec((1, D_MODEL), fixed),
        ],
        out_specs=[pl.BlockSpec((tm, D_MODEL), row), pl.BlockSpec((D_MODEL, tm), lambda i: (0, i))],
        out_shape=[jax.ShapeDtypeStruct((t, D_MODEL), F32), jax.ShapeDtypeStruct((D_MODEL, t), BF16)],
        compiler_params=_cparams(("parallel",)),
        name="outproj",
    )(x, a, b, w, g)


_PEER_PAIRS = [(i, j) for i in range(PEER_TOPK) for j in range(PEER_TOPK) if (i + 1) * (j + 1) <= PEER_TOPK]
_PEER_CAND_ROWS = -(-len(_PEER_PAIRS) // 8) * 8


def _top_rows(x, k):
    rows = []
    for _ in range(k):
        mx = jnp.max(x, axis=0, keepdims=True)
        rows.append(mx)
        x = jnp.where(x == mx, -jnp.inf, x)
    return rows


def _peer_select(h, s1_ref, s2_ref, e1_ref, e2_ref, thr_ref, cand_ref):
    tt = s1_ref.shape[-1]
    for tc in range(tt // LANE):
        lanes = slice(tc * LANE, (tc + 1) * LANE)
        s1 = s1_ref[h, :, lanes]
        s2 = s2_ref[h, :, lanes]
        v1 = _top_rows(s1, PEER_TOPK)
        v2 = _top_rows(s2, PEER_TOPK)
        for r, (i, j) in enumerate(_PEER_PAIRS):
            cand_ref[r:r + 1, :] = v1[i] + v2[j]
        for r in range(len(_PEER_PAIRS), _PEER_CAND_ROWS):
            cand_ref[r:r + 1, :] = jnp.full((1, LANE), -jnp.inf, F32)
        f = _top_rows(cand_ref[...], PEER_TOPK)
        z = jnp.ones((1, LANE), F32)
        for r in range(1, PEER_TOPK):
            z = z + jnp.exp(f[r] - f[0])
        thr_ref[h, :, lanes] = f[PEER_TOPK - 1]
        e1_ref[h, :, lanes] = jnp.exp(s1 - v1[0]) / z
        e2_ref[h, :, lanes] = jnp.exp(s2 - v2[0])


def _peer_kernel(x_ref, hnt_ref, wqt_ref, sk_ref, u_ref, vt_ref, o_ref,
                 s1_ref, s2_ref, e1_ref, e2_ref, thr_ref, cand_ref, acc_ref, ht_ref):
    j = pl.program_id(1)
    tt = hnt_ref.shape[1]
    half = PEER_DKEY // 2

    @pl.when(j == 0)
    def _():
        qt = _dot(wqt_ref[...], hnt_ref[...]).astype(BF16)
        for h in range(PEER_HEADS):
            r0 = h * PEER_DKEY
            s1_ref[h] = _dot(sk_ref[2 * h], qt[r0:r0 + half, :])
            s2_ref[h] = _dot(sk_ref[2 * h + 1], qt[r0 + half:r0 + PEER_DKEY, :])

        def body(h, carry):
            _peer_select(h, s1_ref, s2_ref, e1_ref, e2_ref, thr_ref, cand_ref)
            return carry

        lax.fori_loop(0, PEER_HEADS, body, 0)
        acc_ref[...] = jnp.zeros(acc_ref.shape, F32)

    at = _dot(u_ref[...], hnt_ref[...])
    a0 = pl.multiple_of(j * PEER_A, PEER_A)
    for tc in range(tt // LANE):
        lanes = slice(tc * LANE, (tc + 1) * LANE)
        s1a = [s1_ref[h, pl.ds(a0, PEER_A), lanes] for h in range(PEER_HEADS)]
        e1a = [e1_ref[h, pl.ds(a0, PEER_A), lanes] for h in range(PEER_HEADS)]
        for al in range(PEER_A):
            gate = jnp.zeros((PEER_NKEYS, LANE), F32)
            for h in range(PEER_HEADS):
                val = s2_ref[h, :, lanes] + s1a[h][al:al + 1, :]
                w = e2_ref[h, :, lanes] * e1a[h][al:al + 1, :]
                gate = gate + jnp.where(val >= thr_ref[h, :, lanes], w, 0.0)
            act = jax.nn.gelu(at[al * PEER_NKEYS:(al + 1) * PEER_NKEYS, lanes])
            ht_ref[al * PEER_NKEYS:(al + 1) * PEER_NKEYS, lanes] = (act * gate).astype(BF16)
    acc_ref[...] += _dot(vt_ref[...], ht_ref[...])

    @pl.when(j == pl.num_programs(1) - 1)
    def _():
        o_ref[...] = x_ref[...] + acc_ref[...].T


def _peer(x, hnt, wqt, sk, u, vt):
    t = x.shape[0]
    tt = PEER_TT
    ne = u.shape[0]
    et = PEER_A * PEER_NKEYS
    hk = PEER_HEADS * PEER_DKEY
    sel = lambda: pltpu.VMEM((PEER_HEADS, PEER_NKEYS, tt), F32)
    return pl.pallas_call(
        _peer_kernel,
        grid=(t // tt, ne // et),
        in_specs=[
            pl.BlockSpec((tt, D_MODEL), lambda i, j: (i, 0)),
            pl.BlockSpec((D_MODEL, tt), lambda i, j: (0, i)),
            pl.BlockSpec((hk, D_MODEL), lambda i, j: (0, 0)),
            pl.BlockSpec((2 * PEER_HEADS, PEER_NKEYS, PEER_DKEY // 2), lambda i, j: (0, 0, 0)),
            pl.BlockSpec((et, D_MODEL), lambda i, j: (j, 0)),
            pl.BlockSpec((D_MODEL, et), lambda i, j: (0, j)),
        ],
        out_specs=pl.BlockSpec((tt, D_MODEL), lambda i, j: (i, 0)),
        out_shape=jax.ShapeDtypeStruct((t, D_MODEL), F32),
        scratch_shapes=[
            sel(), sel(), sel(), sel(),
            pltpu.VMEM((PEER_HEADS, 1, tt), F32),
            pltpu.VMEM((_PEER_CAND_ROWS, LANE), F32),
            pltpu.VMEM((D_MODEL, tt), F32),
            pltpu.VMEM((et, tt), BF16),
        ],
        compiler_params=_cparams(("parallel", "arbitrary")),
        name="peer",
    )(x, hnt, wqt, sk, u, vt)


def _final_norm_kernel(x_ref, g_ref, o_ref):
    o_ref[...] = _rms(x_ref[...], g_ref[...])


def _final_norm(x, g):
    t = x.shape[0]
    tm = ROW_TILE
    return pl.pallas_call(
        _final_norm_kernel,
        grid=(t // tm,),
        in_specs=[pl.BlockSpec((tm, D_MODEL), lambda i: (i, 0)), pl.BlockSpec((1, D_MODEL), lambda i: (0, 0))],
        out_specs=pl.BlockSpec((tm, D_MODEL), lambda i: (i, 0)),
        out_shape=jax.ShapeDtypeStruct((t, D_MODEL), F32),
        compiler_params=_cparams(("parallel",)),
        name="final_norm",
    )(x, g)


def _rope_table(pos, dim):
    inv = ROPE_THETA ** (-jnp.arange(0, dim, 2, dtype=F32) / dim)
    ang = pos.astype(F32)[:, None] * inv[None, :]
    ang = jnp.concatenate([ang, ang], axis=-1)
    sign = jnp.concatenate([-jnp.ones((dim // 2,), F32), jnp.ones((dim // 2,), F32)])
    return jnp.cos(ang), jnp.sin(ang) * sign


def _rope_tables(seq):
    pos = jnp.arange(seq)
    cos1, sin1 = _rope_table(pos, HEAD_DIM)
    cr, sr = _rope_table(pos // GRID_W, HEAD_DIM // 2)
    cc, sc = _rope_table(pos % GRID_W, HEAD_DIM // 2)
    cos2 = jnp.concatenate([cr, cc], axis=-1)
    sin2 = jnp.concatenate([sr, sc], axis=-1)
    tile = lambda a: jnp.tile(a, (1, N_Q_HEADS))
    return tile(cos1), tile(sin1), tile(cos2), tile(sin2)


def kernel(x, mix_norm_g, ffn_norm_g, final_norm_g, even_w_in, even_w_out, conv_w, conv_b, conv_ln_g, conv_ln_b, sink_logits, odd_w_in, odd_w_out, q_norm_g, k_norm_g, sgu_ln_g, sgu_ln_b, sgu_w, sgu_b, peer_wq, peer_subkeys, peer_u, peer_v):
    batch, seq, d = x.shape
    depth = mix_norm_g.shape[0]
    t = batch * seq
    cos1, sin1, cos2, sin2 = _rope_tables(seq)
    row = lambda a: a.reshape(1, -1).astype(F32)
    xf = x.reshape(t, d)
    for layer in range(depth):
        i = layer // 2
        if layer % 2 == 0:
            a_in, q, k, v = _inproj_even(xf, row(mix_norm_g[layer]), even_w_in[i].astype(BF16), cos1, sin1, seq)
            m1 = _conv_module(a_in, conv_w[i, :, 0, :], row(conv_b[i]), row(conv_ln_g[i]), row(conv_ln_b[i]), batch, seq)
            sink_b = jnp.broadcast_to(sink_logits[i].astype(F32)[:, None], (N_Q_HEADS, LANE))
            m2 = _swa(q, k, v, sink_b, batch, seq)
            w_out = even_w_out[i]
        else:
            q, k, v, d_in = _inproj_odd(xf, row(mix_norm_g[layer]), odd_w_in[i].astype(BF16),
                                        row(jnp.tile(q_norm_g[i], N_Q_HEADS)), row(jnp.tile(k_norm_g[i], N_KV_HEADS)),
                                        cos2, sin2, seq)
            m1 = _dense_attn(q, k, v, batch, seq)
            bs = jnp.broadcast_to(sgu_b[i].astype(F32)[:, :, None], (SGU_GROUPS, CHUNK, MIX_WIDTH // SGU_GROUPS))
            m2 = _sgu(d_in, row(sgu_ln_g[i]), row(sgu_ln_b[i]), sgu_w[i].astype(BF16), bs)
            w_out = odd_w_out[i]
        xn, hnt = _outproj(xf, m1, m2, w_out.astype(BF16), row(ffn_norm_g[layer]))
        sk = peer_subkeys[layer].reshape(2 * PEER_HEADS, PEER_NKEYS, PEER_DKEY // 2).astype(BF16)
        xf = _peer(xn, hnt, peer_wq[layer].T.astype(BF16), sk, peer_u[layer].astype(BF16),
                   peer_v[layer].T.astype(BF16))
    return _final_norm(xf, row(final_norm_g)).reshape(batch, seq, d)
```

```python
import functools
import math

import jax
import jax.numpy as jnp
from jax import lax
from jax.experimental import pallas as pl
from jax.experimental.pallas import tpu as pltpu

F32 = jnp.float32
BF16 = jnp.bfloat16

D_MODEL = 1024
HEAD_DIM = 64
N_Q_HEADS = 8
N_KV_HEADS = 2
GQA_GROUPS = N_Q_HEADS // N_KV_HEADS
ATTN_WIDTH = N_Q_HEADS * HEAD_DIM
KV_WIDTH = N_KV_HEADS * HEAD_DIM
MIX_WIDTH = D_MODEL // 2
CONV_WIDTH = 31
CONV_HALO = 16
SGU_GROUPS = 4
CHUNK = 128
WINDOW = 128
BLOCK_Q = 128
GRID_W = 64
ROPE_THETA = 10000.0
PEER_HEADS = 8
PEER_NKEYS = 128
PEER_DKEY = 128
PEER_TOPK = 16
EPS = 1e-6
NEG = -1e30

VMEM_LIMIT_BYTES = 56 * 1024 * 1024

ROW_TILE = 512
CONV_ROWS = 64
ATT_TQ = 256
ATT_CHAINS = 2
PEER_TT = 512
PEER_A = 16
PEER_SUB = 2
LANE = 128


def _cparams(sem):
    return pltpu.CompilerParams(dimension_semantics=sem, vmem_limit_bytes=VMEM_LIMIT_BYTES)


def _dot(a, b):
    return jnp.dot(a, b, preferred_element_type=F32)


def _dot_nt(a, b):
    return lax.dot_general(a, b, (((1,), (1,)), ((), ())), preferred_element_type=F32)


def _rms(x, g):
    return x * lax.rsqrt(jnp.mean(x * x, axis=-1, keepdims=True) + EPS) * g


def _layernorm(x, g, b):
    mu = jnp.mean(x, axis=-1, keepdims=True)
    xc = x - mu
    var = jnp.mean(xc * xc, axis=-1, keepdims=True)
    return xc * lax.rsqrt(var + EPS) * g + b


def _rope(x, cos, sin_signed, group, half):
    n = x.shape[-1]
    lane = lax.broadcasted_iota(jnp.int32, x.shape, x.ndim - 1)
    first = (lane % group) < half
    partner = jnp.where(first, pltpu.roll(x, n - half, x.ndim - 1), pltpu.roll(x, half, x.ndim - 1))
    return x * cos + partner * sin_signed


def _inproj_even_kernel(x_ref, g_ref, w_ref, cos_ref, sin_ref, a_ref, q_ref, k_ref, v_ref):
    h = _rms(x_ref[...], g_ref[...]).astype(BF16)
    o1 = 2 * MIX_WIDTH
    o2 = o1 + ATTN_WIDTH
    o3 = o2 + KV_WIDTH
    a_ref[...] = _dot(h, w_ref[:, :o1])
    cos = cos_ref[...]
    sin = sin_ref[...]
    q = _rope(_dot(h, w_ref[:, o1:o2]), cos, sin, HEAD_DIM, HEAD_DIM // 2)
    q_ref[...] = (q * (HEAD_DIM ** -0.5)).astype(BF16)
    k = _rope(_dot(h, w_ref[:, o2:o3]), cos[:, :KV_WIDTH], sin[:, :KV_WIDTH], HEAD_DIM, HEAD_DIM // 2)
    k_ref[...] = k.astype(BF16)
    v_ref[...] = _dot(h, w_ref[:, o3:]).astype(BF16)


def _inproj_even(x, g, w, cos, sin, seq):
    t = x.shape[0]
    tm = ROW_TILE
    nseq = seq // tm
    in_w = w.shape[1]
    row = lambda i: (i, 0)
    fixed = lambda i: (0, 0)
    pos = lambda i: (i % nseq, 0)
    return pl.pallas_call(
        _inproj_even_kernel,
        grid=(t // tm,),
        in_specs=[
            pl.BlockSpec((tm, D_MODEL), row),
            pl.BlockSpec((1, D_MODEL), fixed),
            pl.BlockSpec((D_MODEL, in_w), fixed),
            pl.BlockSpec((tm, ATTN_WIDTH), pos),
            pl.BlockSpec((tm, ATTN_WIDTH), pos),
        ],
        out_specs=[
            pl.BlockSpec((tm, 2 * MIX_WIDTH), row),
            pl.BlockSpec((tm, ATTN_WIDTH), row),
            pl.BlockSpec((tm, KV_WIDTH), row),
            pl.BlockSpec((tm, KV_WIDTH), row),
        ],
        out_shape=[
            jax.ShapeDtypeStruct((t, 2 * MIX_WIDTH), F32),
            jax.ShapeDtypeStruct((t, ATTN_WIDTH), BF16),
            jax.ShapeDtypeStruct((t, KV_WIDTH), BF16),
            jax.ShapeDtypeStruct((t, KV_WIDTH), BF16),
        ],
        compiler_params=_cparams(("parallel",)),
        name="inproj_even",
    )(x, g, w, cos, sin)


def _head_sumsq(x):
    n = x.shape[-1]
    r = lax.broadcasted_iota(jnp.int32, (n, n), 0) // HEAD_DIM
    c = lax.broadcasted_iota(jnp.int32, (n, n), 1) // HEAD_DIM
    ones = jnp.where(r == c, 1.0, 0.0).astype(BF16)
    sq = x * x
    hi = sq.astype(BF16)
    lo = (sq - hi.astype(F32)).astype(BF16)
    return _dot(hi, ones) + _dot(lo, ones)


def _inproj_odd_kernel(x_ref, g_ref, w_ref, qg_ref, kg_ref, cos_ref, sin_ref, q_ref, k_ref, v_ref, d_ref):
    h = _rms(x_ref[...], g_ref[...]).astype(BF16)
    c1 = ATTN_WIDTH
    c2 = c1 + KV_WIDTH
    c3 = c2 + KV_WIDTH
    cos = cos_ref[...]
    sin = sin_ref[...]
    half = HEAD_DIM // 2
    q = _dot(h, w_ref[:, :c1])
    q = q * lax.rsqrt(_head_sumsq(q) * (1.0 / HEAD_DIM) + EPS) * qg_ref[...]
    q = _rope(q, cos, sin, half, half // 2)
    q_ref[...] = (q * (HEAD_DIM ** -0.5)).astype(BF16)
    k = _dot(h, w_ref[:, c1:c2])
    k = k * lax.rsqrt(_head_sumsq(k) * (1.0 / HEAD_DIM) + EPS) * kg_ref[...]
    k = _rope(k, cos[:, :KV_WIDTH], sin[:, :KV_WIDTH], half, half // 2).astype(BF16)
    v = _dot(h, w_ref[:, c2:c3]).astype(BF16)
    for hh in range(N_KV_HEADS):
        k_ref[hh] = k[:, hh * HEAD_DIM:(hh + 1) * HEAD_DIM]
        v_ref[hh] = v[:, hh * HEAD_DIM:(hh + 1) * HEAD_DIM]
    d_ref[...] = _dot(h, w_ref[:, c3:])


def _inproj_odd(x, g, w, qg, kg, cos, sin, seq):
    t = x.shape[0]
    tm = ROW_TILE
    nseq = seq // tm
    in_w = w.shape[1]
    row = lambda i: (i, 0)
    fixed = lambda i: (0, 0)
    pos = lambda i: (i % nseq, 0)
    kv = lambda i: (0, i, 0)
    return pl.pallas_call(
        _inproj_odd_kernel,
        grid=(t // tm,),
        in_specs=[
            pl.BlockSpec((tm, D_MODEL), row),
            pl.BlockSpec((1, D_MODEL), fixed),
            pl.BlockSpec((D_MODEL, in_w), fixed),
            pl.BlockSpec((1, ATTN_WIDTH), fixed),
            pl.BlockSpec((1, KV_WIDTH), fixed),
            pl.BlockSpec((tm, ATTN_WIDTH), pos),
            pl.BlockSpec((tm, ATTN_WIDTH), pos),
        ],
        out_specs=[
            pl.BlockSpec((tm, ATTN_WIDTH), row),
            pl.BlockSpec((N_KV_HEADS, tm, HEAD_DIM), kv),
            pl.BlockSpec((N_KV_HEADS, tm, HEAD_DIM), kv),
            pl.BlockSpec((tm, 2 * MIX_WIDTH), row),
        ],
        out_shape=[
            jax.ShapeDtypeStruct((t, ATTN_WIDTH), BF16),
            jax.ShapeDtypeStruct((N_KV_HEADS, t, HEAD_DIM), BF16),
            jax.ShapeDtypeStruct((N_KV_HEADS, t, HEAD_DIM), BF16),
            jax.ShapeDtypeStruct((t, 2 * MIX_WIDTH), F32),
        ],
        compiler_params=_cparams(("parallel",)),
        name="inproj_odd",
    )(x, g, w, qg, kg, cos, sin)


def _glu(x):
    return x[:, :MIX_WIDTH] * jax.nn.sigmoid(x[:, MIX_WIDTH:])


def _conv_kernel(prev_ref, cur_ref, next_ref, w_ref, b_ref, g_ref, beta_ref, o_ref, buf_ref):
    i = pl.program_id(1)
    ns = pl.num_programs(1)
    ts = cur_ref.shape[0]
    halo = CONV_HALO
    buf_ref[0:halo, :] = jnp.where(i > 0, _glu(prev_ref[...]), 0.0)
    buf_ref[halo:halo + ts, :] = _glu(cur_ref[...])
    buf_ref[halo + ts:, :] = jnp.where(i < ns - 1, _glu(next_ref[...]), 0.0)
    w = w_ref[...]
    bias = b_ref[...]
    g = g_ref[...]
    beta = beta_ref[...]
    first = halo - CONV_WIDTH // 2
    for c in range(ts // CONV_ROWS):
        r0 = c * CONV_ROWS
        acc = jnp.zeros((CONV_ROWS, MIX_WIDTH), F32)
        for k in range(CONV_WIDTH):
            acc = acc + w[k:k + 1, :] * buf_ref[r0 + first + k:r0 + first + k + CONV_ROWS, :]
        y = _layernorm(acc + bias, g, beta)
        o_ref[r0:r0 + CONV_ROWS, :] = (y * jax.nn.sigmoid(y)).astype(o_ref.dtype)


def _conv_module(a_in, w, b, g, beta, batch, seq):
    t = a_in.shape[0]
    ts = ROW_TILE
    ns = seq // ts
    hb = ts // CONV_HALO
    last = t // CONV_HALO - 1
    fixed = lambda bb, i: (0, 0)
    return pl.pallas_call(
        _conv_kernel,
        grid=(batch, ns),
        in_specs=[
            pl.BlockSpec((CONV_HALO, 2 * MIX_WIDTH), lambda bb, i: (jnp.maximum((bb * ns + i) * hb - 1, 0), 0)),
            pl.BlockSpec((ts, 2 * MIX_WIDTH), lambda bb, i: (bb * ns + i, 0)),
            pl.BlockSpec((CONV_HALO, 2 * MIX_WIDTH), lambda bb, i: (jnp.minimum((bb * ns + i + 1) * hb, last), 0)),
            pl.BlockSpec((CONV_WIDTH, MIX_WIDTH), fixed),
            pl.BlockSpec((1, MIX_WIDTH), fixed),
            pl.BlockSpec((1, MIX_WIDTH), fixed),
            pl.BlockSpec((1, MIX_WIDTH), fixed),
        ],
        out_specs=pl.BlockSpec((ts, MIX_WIDTH), lambda bb, i: (bb * ns + i, 0)),
        out_shape=jax.ShapeDtypeStruct((t, MIX_WIDTH), BF16),
        scratch_shapes=[pltpu.VMEM((ts + 2 * CONV_HALO, MIX_WIDTH), F32)],
        compiler_params=_cparams(("parallel", "parallel")),
        name="conv_module",
    )(a_in, a_in, a_in, w, b, g, beta)


def _swa_kernel(q_ref, kp_ref, kc_ref, kn_ref, vp_ref, vc_ref, vn_ref, sink_ref, o_ref):
    n = pl.program_id(1)
    nb = pl.num_programs(1)
    bq = q_ref.shape[0]
    ii = lax.broadcasted_iota(jnp.int32, (bq, 3 * bq), 0)
    jj = lax.broadcasted_iota(jnp.int32, (bq, 3 * bq), 1)
    lo = jnp.where(n > 0, 0, bq)
    hi = jnp.where(n < nb - 1, 3 * bq, 2 * bq)
    mask = (jj >= ii) & (jj <= ii + 2 * WINDOW) & (jj >= lo) & (jj < hi)
    kcat = jnp.concatenate([kp_ref[...], kc_ref[...], kn_ref[...]], axis=0)
    vcat = jnp.concatenate([vp_ref[...], vc_ref[...], vn_ref[...]], axis=0)
    q = q_ref[...]
    outs = []
    for hd in range(N_Q_HEADS):
        hk = hd // GQA_GROUPS
        kh = kcat[:, hk * HEAD_DIM:(hk + 1) * HEAD_DIM]
        vh = vcat[:, hk * HEAD_DIM:(hk + 1) * HEAD_DIM]
        s = _dot_nt(q[:, hd * HEAD_DIM:(hd + 1) * HEAD_DIM], kh)
        s = jnp.where(mask, s, NEG)
        sink = sink_ref[hd:hd + 1, 0:1]
        m = jnp.maximum(jnp.max(s, axis=-1, keepdims=True), sink)
        p = jnp.exp(s - m)
        denom = jnp.sum(p, axis=-1, keepdims=True) + jnp.exp(sink - m)
        outs.append(_dot(p.astype(BF16), vh) / denom)
    o_ref[...] = jnp.concatenate(outs, axis=-1).astype(o_ref.dtype)


def _swa(q, k, v, sink_b, batch, seq):
    t = q.shape[0]
    bq = BLOCK_Q
    nb = seq // bq
    last = t // bq - 1
    cur = lambda bb, n: (bb * nb + n, 0)
    prev = lambda bb, n: (jnp.maximum(bb * nb + n - 1, 0), 0)
    nxt = lambda bb, n: (jnp.minimum(bb * nb + n + 1, last), 0)
    kvs = lambda im: pl.BlockSpec((bq, KV_WIDTH), im)
    return pl.pallas_call(
        _swa_kernel,
        grid=(batch, nb),
        in_specs=[pl.BlockSpec((bq, ATTN_WIDTH), cur), kvs(prev), kvs(cur), kvs(nxt), kvs(prev), kvs(cur), kvs(nxt),
                  pl.BlockSpec((N_Q_HEADS, LANE), lambda bb, n: (0, 0))],
        out_specs=pl.BlockSpec((bq, ATTN_WIDTH), cur),
        out_shape=jax.ShapeDtypeStruct((t, ATTN_WIDTH), BF16),
        compiler_params=_cparams(("parallel", "parallel")),
        name="window_attn",
    )(q, k, k, k, v, v, v, sink_b)


def _dense_attn_kernel(q_ref, k_ref, v_ref, o_ref):
    ts = q_ref.shape[0] // ATT_CHAINS
    k = k_ref[...]
    v = v_ref[...]
    for c in range(ATT_CHAINS):
        q = q_ref[c * ts:(c + 1) * ts, :]
        qs = jnp.concatenate([q[:, g * HEAD_DIM:(g + 1) * HEAD_DIM] for g in range(GQA_GROUPS)], axis=0)
        s = _dot_nt(qs, k)
        p = jnp.exp(s - jnp.max(s, axis=-1, keepdims=True))
        o = _dot(p.astype(BF16), v) / jnp.sum(p, axis=-1, keepdims=True)
        o_ref[c * ts:(c + 1) * ts, :] = jnp.concatenate(
            [o[g * ts:(g + 1) * ts, :] for g in range(GQA_GROUPS)], axis=-1).astype(o_ref.dtype)


def _dense_attn(q, k, v, batch, seq):
    t = q.shape[0]
    tq = min(ATT_TQ, seq)
    nq = seq // tq
    gw = GQA_GROUPS * HEAD_DIM
    qmap = lambda bb, hk, qi: (bb * nq + qi, hk)
    kmap = lambda bb, hk, qi: (hk, bb, 0)
    return pl.pallas_call(
        _dense_attn_kernel,
        grid=(batch, N_KV_HEADS, nq),
        in_specs=[
            pl.BlockSpec((tq, gw), qmap),
            pl.BlockSpec((None, seq, HEAD_DIM), kmap),
            pl.BlockSpec((None, seq, HEAD_DIM), kmap),
        ],
        out_specs=pl.BlockSpec((tq, gw), qmap),
        out_shape=jax.ShapeDtypeStruct((t, ATTN_WIDTH), BF16),
        compiler_params=_cparams(("parallel", "parallel", "parallel")),
        name="dense_attn",
    )(q, k, v)


def _sgu_kernel(d_ref, g_ref, b_ref, w_ref, bs_ref, o_ref):
    z = jax.nn.gelu(d_ref[...])
    u = z[:, :MIX_WIDTH]
    v = _layernorm(z[:, MIX_WIDTH:], g_ref[...], b_ref[...]).astype(BF16)
    gc = MIX_WIDTH // SGU_GROUPS
    for n in range(d_ref.shape[0] // CHUNK):
        rows = slice(n * CHUNK, (n + 1) * CHUNK)
        for g in range(SGU_GROUPS):
            cols = slice(g * gc, (g + 1) * gc)
            mixed = _dot(w_ref[g], v[rows, cols]) + bs_ref[g]
            o_ref[rows, cols] = (u[rows, cols] * mixed).astype(o_ref.dtype)


def _sgu(d_in, g, b, w, bs):
    t = d_in.shape[0]
    ts = ROW_TILE
    gc = MIX_WIDTH // SGU_GROUPS
    row = lambda i: (i, 0)
    fixed = lambda i: (0, 0)
    fixed3 = lambda i: (0, 0, 0)
    return pl.pallas_call(
        _sgu_kernel,
        grid=(t // ts,),
        in_specs=[
            pl.BlockSpec((ts, 2 * MIX_WIDTH), row),
            pl.BlockSpec((1, MIX_WIDTH), fixed),
            pl.BlockSpec((1, MIX_WIDTH), fixed),
            pl.BlockSpec((SGU_GROUPS, CHUNK, CHUNK), fixed3),
            pl.BlockSpec((SGU_GROUPS, CHUNK, gc), fixed3),
        ],
        out_specs=pl.BlockSpec((ts, MIX_WIDTH), row),
        out_shape=jax.ShapeDtypeStruct((t, MIX_WIDTH), BF16),
        compiler_params=_cparams(("parallel",)),
        name="spatial_gating",
    )(d_in, g, b, w, bs)


def _outproj_kernel(x_ref, a_ref, b_ref, w_ref, g_ref, xn_ref, hnt_ref):
    y = _dot(a_ref[...], w_ref[:MIX_WIDTH, :]) + _dot(b_ref[...], w_ref[MIX_WIDTH:, :])
    xn = x_ref[...] + y
    xn_ref[...] = xn
    hnt_ref[...] = _rms(xn, g_ref[...]).T.astype(BF16)


def _outproj(x, a, b, w, g):
    t = x.shape[0]
    tm = ROW_TILE
    row = lambda i: (i, 0)
    fixed = lambda i: (0, 0)
    return pl.pallas_call(
        _outproj_kernel,
        grid=(t // tm,),
        in_specs=[
            pl.BlockSpec((tm, D_MODEL), row),
            pl.BlockSpec((tm, MIX_WIDTH), row),
            pl.BlockSpec((tm, MIX_WIDTH), row),
            pl.BlockSpec((2 * MIX_WIDTH, D_MODEL), fixed),
            pl.BlockSpec((1, D_MODEL), fixed),
        ],
        out_specs=[pl.BlockSpec((tm, D_MODEL), row), pl.BlockSpec((D_MODEL, tm), lambda i: (0, i))],
        out_shape=[jax.ShapeDtypeStruct((t, D_MODEL), F32), jax.ShapeDtypeStruct((D_MODEL, t), BF16)],
        compiler_params=_cparams(("parallel",)),
        name="outproj",
    )(x, a, b, w, g)


_PEER_PAIRS = [(i, j) for i in range(PEER_TOPK) for j in range(PEER_TOPK) if (i + 1) * (j + 1) <= PEER_TOPK]
_PEER_CAND_ROWS = -(-len(_PEER_PAIRS) // 8) * 8


def _top_rows(x, k, ranked=False):
    rows = []
    rank = jnp.full(x.shape, float(k), F32) if ranked else None
    for r in range(k):
        mx = jnp.max(x, axis=0, keepdims=True)
        rows.append(mx)
        hit = x == mx
        if ranked:
            rank = jnp.where(hit, float(r), rank)
        x = jnp.where(hit, -jnp.inf, x)
    return rows, rank


def _peer_select(h, s1_ref, s2_ref, n_ref, e1_ref, r2_ref, e2_ref, cand_ref):
    tt = s1_ref.shape[-1]
    for tc in range(tt // LANE):
        lanes = slice(tc * LANE, (tc + 1) * LANE)
        s1 = s1_ref[h, :, lanes]
        s2 = s2_ref[h, :, lanes]
        v1, _ = _top_rows(s1, PEER_TOPK)
        v2, rank2 = _top_rows(s2, PEER_TOPK, ranked=True)
        for r, (i, j) in enumerate(_PEER_PAIRS):
            cand_ref[r:r + 1, :] = v1[i] + v2[j]
        for r in range(len(_PEER_PAIRS), _PEER_CAND_ROWS):
            cand_ref[r:r + 1, :] = jnp.full((1, LANE), -jnp.inf, F32)
        f, _ = _top_rows(cand_ref[...], PEER_TOPK)
        z = jnp.ones((1, LANE), F32)
        for r in range(1, PEER_TOPK):
            z = z + jnp.exp(f[r] - f[0])
        thr = f[PEER_TOPK - 1]
        n = jnp.zeros(s1.shape, F32)
        for r in range(PEER_TOPK):
            n = n + jnp.where(s1 + v2[r] >= thr, 1.0, 0.0)
        n_ref[h, :, lanes] = n
        e1_ref[h, :, lanes] = jnp.exp(s1 - v1[0]) / z
        r2_ref[h, :, lanes] = rank2.astype(BF16)
        e2_ref[h, :, lanes] = jnp.exp(s2 - v2[0]).astype(BF16)


def _peer_kernel(x_ref, hnt_ref, wqt_ref, sk_ref, u_ref, vt_ref, o_ref,
                 s1_ref, s2_ref, n_ref, e1_ref, r2_ref, e2_ref, cand_ref, acc_ref, ht_ref):
    j = pl.program_id(1)
    tt = hnt_ref.shape[1]
    half = PEER_DKEY // 2

    @pl.when(j == 0)
    def _():
        qt = _dot(wqt_ref[...], hnt_ref[...]).astype(BF16)
        for h in range(PEER_HEADS):
            r0 = h * PEER_DKEY
            s1_ref[h] = _dot(sk_ref[2 * h], qt[r0:r0 + half, :])
            s2_ref[h] = _dot(sk_ref[2 * h + 1], qt[r0 + half:r0 + PEER_DKEY, :])

        def body(h, carry):
            _peer_select(h, s1_ref, s2_ref, n_ref, e1_ref, r2_ref, e2_ref, cand_ref)
            return carry

        lax.fori_loop(0, PEER_HEADS, body, 0)
        acc_ref[...] = jnp.zeros(acc_ref.shape, F32)

    a0 = pl.multiple_of(j * PEER_A, PEER_A)
    hnt = hnt_ref[...]
    nrows = [[n_ref[h, pl.ds(a0, PEER_A), tc * LANE:(tc + 1) * LANE].astype(BF16) for h in range(PEER_HEADS)]
             for tc in range(tt // LANE)]
    erows = [[e1_ref[h, pl.ds(a0, PEER_A), tc * LANE:(tc + 1) * LANE].astype(BF16) for h in range(PEER_HEADS)]
             for tc in range(tt // LANE)]
    nsub = PEER_A // PEER_SUB
    sub_rows = lambda sub: slice(sub * PEER_SUB * PEER_NKEYS, (sub + 1) * PEER_SUB * PEER_NKEYS)
    def value_matmul(pair):
        rows = slice(pair * 2 * PEER_SUB * PEER_NKEYS, (pair + 1) * 2 * PEER_SUB * PEER_NKEYS)
        acc_ref[...] += _dot(vt_ref[:, rows], ht_ref[rows, :])

    at_next = _dot(u_ref[sub_rows(0), :], hnt).astype(BF16)
    for sub in range(nsub):
        at = at_next
        if sub + 1 < nsub:
            at_next = _dot(u_ref[sub_rows(sub + 1), :], hnt).astype(BF16)
        for k in range(PEER_SUB):
            al = sub * PEER_SUB + k
            for tc in range(tt // LANE):
                lanes = slice(tc * LANE, (tc + 1) * LANE)
                gate = jnp.zeros((PEER_NKEYS, LANE), BF16)
                for h in range(PEER_HEADS):
                    keep = r2_ref[h, :, lanes] < nrows[tc][h][al:al + 1, :]
                    gate = gate + jnp.where(keep, e2_ref[h, :, lanes], 0.0) * erows[tc][h][al:al + 1, :]
                act = jax.nn.gelu(at[k * PEER_NKEYS:(k + 1) * PEER_NKEYS, lanes])
                ht_ref[al * PEER_NKEYS:(al + 1) * PEER_NKEYS, lanes] = act * gate
        if sub % 2 == 1 and sub >= 3:
            value_matmul(sub // 2 - 1)
    value_matmul(nsub // 2 - 1)

    @pl.when(j == pl.num_programs(1) - 1)
    def _():
        o_ref[...] = x_ref[...] + acc_ref[...].T


def _peer(x, hnt, wqt, sk, u, vt):
    t = x.shape[0]
    tt = PEER_TT
    ne = u.shape[0]
    et = PEER_A * PEER_NKEYS
    hk = PEER_HEADS * PEER_DKEY
    sel = lambda dt: pltpu.VMEM((PEER_HEADS, PEER_NKEYS, tt), dt)
    return pl.pallas_call(
        _peer_kernel,
        grid=(t // tt, ne // et),
        in_specs=[
            pl.BlockSpec((tt, D_MODEL), lambda i, j: (i, 0)),
            pl.BlockSpec((D_MODEL, tt), lambda i, j: (0, i)),
            pl.BlockSpec((hk, D_MODEL), lambda i, j: (0, 0)),
            pl.BlockSpec((2 * PEER_HEADS, PEER_NKEYS, PEER_DKEY // 2), lambda i, j: (0, 0, 0)),
            pl.BlockSpec((et, D_MODEL), lambda i, j: (j, 0)),
            pl.BlockSpec((D_MODEL, et), lambda i, j: (0, j)),
        ],
        out_specs=pl.BlockSpec((tt, D_MODEL), lambda i, j: (i, 0)),
        out_shape=jax.ShapeDtypeStruct((t, D_MODEL), F32),
        scratch_shapes=[
            sel(F32), sel(F32), sel(F32), sel(F32), sel(BF16), sel(BF16),
            pltpu.VMEM((_PEER_CAND_ROWS, LANE), F32),
            pltpu.VMEM((D_MODEL, tt), F32),
            pltpu.VMEM((et, tt), BF16),
        ],
        compiler_params=_cparams(("parallel", "arbitrary")),
        name="peer",
    )(x, hnt, wqt, sk, u, vt)


def _final_norm_kernel(x_ref, g_ref, o_ref):
    o_ref[...] = _rms(x_ref[...], g_ref[...])


def _final_norm(x, g):
    t = x.shape[0]
    tm = ROW_TILE
    return pl.pallas_call(
        _final_norm_kernel,
        grid=(t // tm,),
        in_specs=[pl.BlockSpec((tm, D_MODEL), lambda i: (i, 0)), pl.BlockSpec((1, D_MODEL), lambda i: (0, 0))],
        out_specs=pl.BlockSpec((tm, D_MODEL), lambda i: (i, 0)),
        out_shape=jax.ShapeDtypeStruct((t, D_MODEL), F32),
        compiler_params=_cparams(("parallel",)),
        name="final_norm",
    )(x, g)


def _rope_table(pos, dim):
    inv = ROPE_THETA ** (-jnp.arange(0, dim, 2, dtype=F32) / dim)
    ang = pos.astype(F32)[:, None] * inv[None, :]
    ang = jnp.concatenate([ang, ang], axis=-1)
    sign = jnp.concatenate([-jnp.ones((dim // 2,), F32), jnp.ones((dim // 2,), F32)])
    return jnp.cos(ang), jnp.sin(ang) * sign


def _rope_tables(seq):
    pos = jnp.arange(seq)
    cos1, sin1 = _rope_table(pos, HEAD_DIM)
    cr, sr = _rope_table(pos // GRID_W, HEAD_DIM // 2)
    cc, sc = _rope_table(pos % GRID_W, HEAD_DIM // 2)
    cos2 = jnp.concatenate([cr, cc], axis=-1)
    sin2 = jnp.concatenate([sr, sc], axis=-1)
    tile = lambda a: jnp.tile(a, (1, N_Q_HEADS))
    return tile(cos1), tile(sin1), tile(cos2), tile(sin2)


def kernel(x, mix_norm_g, ffn_norm_g, final_norm_g, even_w_in, even_w_out, conv_w, conv_b, conv_ln_g, conv_ln_b, sink_logits, odd_w_in, odd_w_out, q_norm_g, k_norm_g, sgu_ln_g, sgu_ln_b, sgu_w, sgu_b, peer_wq, peer_subkeys, peer_u, peer_v):
    batch, seq, d = x.shape
    depth = mix_norm_g.shape[0]
    t = batch * seq
    cos1, sin1, cos2, sin2 = _rope_tables(seq)
    row = lambda a: a.reshape(1, -1).astype(F32)
    xf = x.reshape(t, d)
    for layer in range(depth):
        i = layer // 2
        if layer % 2 == 0:
            a_in, q, k, v = _inproj_even(xf, row(mix_norm_g[layer]), even_w_in[i].astype(BF16), cos1, sin1, seq)
            m1 = _conv_module(a_in, conv_w[i, :, 0, :], row(conv_b[i]), row(conv_ln_g[i]), row(conv_ln_b[i]), batch, seq)
            sink_b = jnp.broadcast_to(sink_logits[i].astype(F32)[:, None], (N_Q_HEADS, LANE))
            m2 = _swa(q, k, v, sink_b, batch, seq)
            w_out = even_w_out[i]
        else:
            q, k, v, d_in = _inproj_odd(xf, row(mix_norm_g[layer]), odd_w_in[i].astype(BF16),
                                        row(jnp.tile(q_norm_g[i], N_Q_HEADS)), row(jnp.tile(k_norm_g[i], N_KV_HEADS)),
                                        cos2, sin2, seq)
            m1 = _dense_attn(q, k, v, batch, seq)
            bs = jnp.broadcast_to(sgu_b[i].astype(F32)[:, :, None], (SGU_GROUPS, CHUNK, MIX_WIDTH // SGU_GROUPS))
            m2 = _sgu(d_in, row(sgu_ln_g[i]), row(sgu_ln_b[i]), sgu_w[i].astype(BF16), bs)
            w_out = odd_w_out[i]
        xn, hnt = _outproj(xf, m1, m2, w_out.astype(BF16), row(ffn_norm_g[layer]))
        sk = peer_subkeys[layer].reshape(2 * PEER_HEADS, PEER_NKEYS, PEER_DKEY // 2).astype(BF16)
        xf = _peer(xn, hnt, peer_wq[layer].T.astype(BF16), sk, peer_u[layer].astype(BF16),
                   peer_v[layer].T.astype(BF16))
    return _final_norm(xf, row(final_norm_g)).reshape(batch, seq, d)
```

```python
import functools
import math

import jax
import jax.numpy as jnp
from jax import lax
from jax.experimental import pallas as pl
from jax.experimental.pallas import tpu as pltpu

F32 = jnp.float32
BF16 = jnp.bfloat16

D_MODEL = 1024
HEAD_DIM = 64
N_Q_HEADS = 8
N_KV_HEADS = 2
GQA_GROUPS = N_Q_HEADS // N_KV_HEADS
ATTN_WIDTH = N_Q_HEADS * HEAD_DIM
KV_WIDTH = N_KV_HEADS * HEAD_DIM
MIX_WIDTH = D_MODEL // 2
CONV_WIDTH = 31
CONV_HALO = 16
SGU_GROUPS = 4
CHUNK = 128
WINDOW = 128
BLOCK_Q = 128
GRID_W = 64
ROPE_THETA = 10000.0
PEER_HEADS = 8
PEER_NKEYS = 128
PEER_DKEY = 128
PEER_TOPK = 16
EPS = 1e-6
NEG = -1e30

VMEM_LIMIT_BYTES = 56 * 1024 * 1024

ROW_TILE = 512
CONV_ROWS = 64
ATT_TQ = 256
ATT_CHAINS = 2
PEER_TT = 512
PEER_A = 16
PEER_SUB = 4
LANE = 128


def _cparams(sem):
    return pltpu.CompilerParams(dimension_semantics=sem, vmem_limit_bytes=VMEM_LIMIT_BYTES)


def _dot(a, b):
    return jnp.dot(a, b, preferred_element_type=F32)


def _dot_nt(a, b):
    return lax.dot_general(a, b, (((1,), (1,)), ((), ())), preferred_element_type=F32)


def _rms(x, g):
    return x * lax.rsqrt(jnp.mean(x * x, axis=-1, keepdims=True) + EPS) * g


def _layernorm(x, g, b):
    mu = jnp.mean(x, axis=-1, keepdims=True)
    xc = x - mu
    var = jnp.mean(xc * xc, axis=-1, keepdims=True)
    return xc * lax.rsqrt(var + EPS) * g + b


def _rope(x, cos, sin_signed, group, half):
    n = x.shape[-1]
    lane = lax.broadcasted_iota(jnp.int32, x.shape, x.ndim - 1)
    first = (lane % group) < half
    partner = jnp.where(first, pltpu.roll(x, n - half, x.ndim - 1), pltpu.roll(x, half, x.ndim - 1))
    return x * cos + partner * sin_signed


def _inproj_even_kernel(x_ref, g_ref, w_ref, cos_ref, sin_ref, a_ref, q_ref, k_ref, v_ref):
    h = _rms(x_ref[...], g_ref[...]).astype(BF16)
    o1 = 2 * MIX_WIDTH
    o2 = o1 + ATTN_WIDTH
    o3 = o2 + KV_WIDTH
    a_ref[...] = _dot(h, w_ref[:, :o1])
    cos = cos_ref[...]
    sin = sin_ref[...]
    q = _rope(_dot(h, w_ref[:, o1:o2]), cos, sin, HEAD_DIM, HEAD_DIM // 2)
    q_ref[...] = (q * (HEAD_DIM ** -0.5)).astype(BF16)
    k = _rope(_dot(h, w_ref[:, o2:o3]), cos[:, :KV_WIDTH], sin[:, :KV_WIDTH], HEAD_DIM, HEAD_DIM // 2)
    k_ref[...] = k.astype(BF16)
    v_ref[...] = _dot(h, w_ref[:, o3:]).astype(BF16)


def _inproj_even(x, g, w, cos, sin, seq):
    t = x.shape[0]
    tm = ROW_TILE
    nseq = seq // tm
    in_w = w.shape[1]
    row = lambda i: (i, 0)
    fixed = lambda i: (0, 0)
    pos = lambda i: (i % nseq, 0)
    return pl.pallas_call(
        _inproj_even_kernel,
        grid=(t // tm,),
        in_specs=[
            pl.BlockSpec((tm, D_MODEL), row),
            pl.BlockSpec((1, D_MODEL), fixed),
            pl.BlockSpec((D_MODEL, in_w), fixed),
            pl.BlockSpec((tm, ATTN_WIDTH), pos),
            pl.BlockSpec((tm, ATTN_WIDTH), pos),
        ],
        out_specs=[
            pl.BlockSpec((tm, 2 * MIX_WIDTH), row),
            pl.BlockSpec((tm, ATTN_WIDTH), row),
            pl.BlockSpec((tm, KV_WIDTH), row),
            pl.BlockSpec((tm, KV_WIDTH), row),
        ],
        out_shape=[
            jax.ShapeDtypeStruct((t, 2 * MIX_WIDTH), F32),
            jax.ShapeDtypeStruct((t, ATTN_WIDTH), BF16),
            jax.ShapeDtypeStruct((t, KV_WIDTH), BF16),
            jax.ShapeDtypeStruct((t, KV_WIDTH), BF16),
        ],
        compiler_params=_cparams(("parallel",)),
        name="inproj_even",
    )(x, g, w, cos, sin)


def _head_sumsq(x):
    n = x.shape[-1]
    r = lax.broadcasted_iota(jnp.int32, (n, n), 0) // HEAD_DIM
    c = lax.broadcasted_iota(jnp.int32, (n, n), 1) // HEAD_DIM
    ones = jnp.where(r == c, 1.0, 0.0).astype(BF16)
    sq = x * x
    hi = sq.astype(BF16)
    lo = (sq - hi.astype(F32)).astype(BF16)
    return _dot(hi, ones) + _dot(lo, ones)


def _inproj_odd_kernel(x_ref, g_ref, w_ref, qg_ref, kg_ref, cos_ref, sin_ref, q_ref, k_ref, v_ref, d_ref):
    h = _rms(x_ref[...], g_ref[...]).astype(BF16)
    c1 = ATTN_WIDTH
    c2 = c1 + KV_WIDTH
    c3 = c2 + KV_WIDTH
    cos = cos_ref[...]
    sin = sin_ref[...]
    half = HEAD_DIM // 2
    q = _dot(h, w_ref[:, :c1])
    q = q * lax.rsqrt(_head_sumsq(q) * (1.0 / HEAD_DIM) + EPS) * qg_ref[...]
    q = _rope(q, cos, sin, half, half // 2)
    q_ref[...] = (q * (HEAD_DIM ** -0.5)).astype(BF16)
    k = _dot(h, w_ref[:, c1:c2])
    k = k * lax.rsqrt(_head_sumsq(k) * (1.0 / HEAD_DIM) + EPS) * kg_ref[...]
    k = _rope(k, cos[:, :KV_WIDTH], sin[:, :KV_WIDTH], half, half // 2).astype(BF16)
    v = _dot(h, w_ref[:, c2:c3]).astype(BF16)
    for hh in range(N_KV_HEADS):
        k_ref[hh] = k[:, hh * HEAD_DIM:(hh + 1) * HEAD_DIM]
        v_ref[hh] = v[:, hh * HEAD_DIM:(hh + 1) * HEAD_DIM]
    d_ref[...] = _dot(h, w_ref[:, c3:])


def _inproj_odd(x, g, w, qg, kg, cos, sin, seq):
    t = x.shape[0]
    tm = ROW_TILE
    nseq = seq // tm
    in_w = w.shape[1]
    row = lambda i: (i, 0)
    fixed = lambda i: (0, 0)
    pos = lambda i: (i % nseq, 0)
    kv = lambda i: (0, i, 0)
    return pl.pallas_call(
        _inproj_odd_kernel,
        grid=(t // tm,),
        in_specs=[
            pl.BlockSpec((tm, D_MODEL), row),
            pl.BlockSpec((1, D_MODEL), fixed),
            pl.BlockSpec((D_MODEL, in_w), fixed),
            pl.BlockSpec((1, ATTN_WIDTH), fixed),
            pl.BlockSpec((1, KV_WIDTH), fixed),
            pl.BlockSpec((tm, ATTN_WIDTH), pos),
            pl.BlockSpec((tm, ATTN_WIDTH), pos),
        ],
        out_specs=[
            pl.BlockSpec((tm, ATTN_WIDTH), row),
            pl.BlockSpec((N_KV_HEADS, tm, HEAD_DIM), kv),
            pl.BlockSpec((N_KV_HEADS, tm, HEAD_DIM), kv),
            pl.BlockSpec((tm, 2 * MIX_WIDTH), row),
        ],
        out_shape=[
            jax.ShapeDtypeStruct((t, ATTN_WIDTH), BF16),
            jax.ShapeDtypeStruct((N_KV_HEADS, t, HEAD_DIM), BF16),
            jax.ShapeDtypeStruct((N_KV_HEADS, t, HEAD_DIM), BF16),
            jax.ShapeDtypeStruct((t, 2 * MIX_WIDTH), F32),
        ],
        compiler_params=_cparams(("parallel",)),
        name="inproj_odd",
    )(x, g, w, qg, kg, cos, sin)


def _glu(x):
    return x[:, :MIX_WIDTH] * jax.nn.sigmoid(x[:, MIX_WIDTH:])


def _conv_kernel(prev_ref, cur_ref, next_ref, w_ref, b_ref, g_ref, beta_ref, o_ref, buf_ref):
    i = pl.program_id(1)
    ns = pl.num_programs(1)
    ts = cur_ref.shape[0]
    halo = CONV_HALO
    buf_ref[0:halo, :] = jnp.where(i > 0, _glu(prev_ref[...]), 0.0)
    buf_ref[halo:halo + ts, :] = _glu(cur_ref[...])
    buf_ref[halo + ts:, :] = jnp.where(i < ns - 1, _glu(next_ref[...]), 0.0)
    w = w_ref[...]
    bias = b_ref[...]
    g = g_ref[...]
    beta = beta_ref[...]
    first = halo - CONV_WIDTH // 2
    for c in range(ts // CONV_ROWS):
        r0 = c * CONV_ROWS
        acc = jnp.zeros((CONV_ROWS, MIX_WIDTH), F32)
        for k in range(CONV_WIDTH):
            acc = acc + w[k:k + 1, :] * buf_ref[r0 + first + k:r0 + first + k + CONV_ROWS, :]
        y = _layernorm(acc + bias, g, beta)
        o_ref[r0:r0 + CONV_ROWS, :] = (y * jax.nn.sigmoid(y)).astype(o_ref.dtype)


def _conv_module(a_in, w, b, g, beta, batch, seq):
    t = a_in.shape[0]
    ts = ROW_TILE
    ns = seq // ts
    hb = ts // CONV_HALO
    last = t // CONV_HALO - 1
    fixed = lambda bb, i: (0, 0)
    return pl.pallas_call(
        _conv_kernel,
        grid=(batch, ns),
        in_specs=[
            pl.BlockSpec((CONV_HALO, 2 * MIX_WIDTH), lambda bb, i: (jnp.maximum((bb * ns + i) * hb - 1, 0), 0)),
            pl.BlockSpec((ts, 2 * MIX_WIDTH), lambda bb, i: (bb * ns + i, 0)),
            pl.BlockSpec((CONV_HALO, 2 * MIX_WIDTH), lambda bb, i: (jnp.minimum((bb * ns + i + 1) * hb, last), 0)),
            pl.BlockSpec((CONV_WIDTH, MIX_WIDTH), fixed),
            pl.BlockSpec((1, MIX_WIDTH), fixed),
            pl.BlockSpec((1, MIX_WIDTH), fixed),
            pl.BlockSpec((1, MIX_WIDTH), fixed),
        ],
        out_specs=pl.BlockSpec((ts, MIX_WIDTH), lambda bb, i: (bb * ns + i, 0)),
        out_shape=jax.ShapeDtypeStruct((t, MIX_WIDTH), BF16),
        scratch_shapes=[pltpu.VMEM((ts + 2 * CONV_HALO, MIX_WIDTH), F32)],
        compiler_params=_cparams(("parallel", "parallel")),
        name="conv_module",
    )(a_in, a_in, a_in, w, b, g, beta)


def _swa_kernel(q_ref, kp_ref, kc_ref, kn_ref, vp_ref, vc_ref, vn_ref, sink_ref, o_ref):
    n = pl.program_id(1)
    nb = pl.num_programs(1)
    bq = q_ref.shape[0]
    ii = lax.broadcasted_iota(jnp.int32, (bq, 3 * bq), 0)
    jj = lax.broadcasted_iota(jnp.int32, (bq, 3 * bq), 1)
    lo = jnp.where(n > 0, 0, bq)
    hi = jnp.where(n < nb - 1, 3 * bq, 2 * bq)
    mask = (jj >= ii) & (jj <= ii + 2 * WINDOW) & (jj >= lo) & (jj < hi)
    kcat = jnp.concatenate([kp_ref[...], kc_ref[...], kn_ref[...]], axis=0)
    vcat = jnp.concatenate([vp_ref[...], vc_ref[...], vn_ref[...]], axis=0)
    q = q_ref[...]
    outs = []
    for hd in range(N_Q_HEADS):
        hk = hd // GQA_GROUPS
        kh = kcat[:, hk * HEAD_DIM:(hk + 1) * HEAD_DIM]
        vh = vcat[:, hk * HEAD_DIM:(hk + 1) * HEAD_DIM]
        s = _dot_nt(q[:, hd * HEAD_DIM:(hd + 1) * HEAD_DIM], kh)
        s = jnp.where(mask, s, NEG)
        sink = sink_ref[hd:hd + 1, 0:1]
        m = jnp.maximum(jnp.max(s, axis=-1, keepdims=True), sink)
        p = jnp.exp(s - m)
        denom = jnp.sum(p, axis=-1, keepdims=True) + jnp.exp(sink - m)
        outs.append(_dot(p.astype(BF16), vh) / denom)
    o_ref[...] = jnp.concatenate(outs, axis=-1).astype(o_ref.dtype)


def _swa(q, k, v, sink_b, batch, seq):
    t = q.shape[0]
    bq = BLOCK_Q
    nb = seq // bq
    last = t // bq - 1
    cur = lambda bb, n: (bb * nb + n, 0)
    prev = lambda bb, n: (jnp.maximum(bb * nb + n - 1, 0), 0)
    nxt = lambda bb, n: (jnp.minimum(bb * nb + n + 1, last), 0)
    kvs = lambda im: pl.BlockSpec((bq, KV_WIDTH), im)
    return pl.pallas_call(
        _swa_kernel,
        grid=(batch, nb),
        in_specs=[pl.BlockSpec((bq, ATTN_WIDTH), cur), kvs(prev), kvs(cur), kvs(nxt), kvs(prev), kvs(cur), kvs(nxt),
                  pl.BlockSpec((N_Q_HEADS, LANE), lambda bb, n: (0, 0))],
        out_specs=pl.BlockSpec((bq, ATTN_WIDTH), cur),
        out_shape=jax.ShapeDtypeStruct((t, ATTN_WIDTH), BF16),
        compiler_params=_cparams(("parallel", "parallel")),
        name="window_attn",
    )(q, k, k, k, v, v, v, sink_b)


def _dense_attn_kernel(q_ref, k_ref, v_ref, o_ref):
    ts = q_ref.shape[0] // ATT_CHAINS
    k = k_ref[...]
    v = v_ref[...]
    for c in range(ATT_CHAINS):
        q = q_ref[c * ts:(c + 1) * ts, :]
        qs = jnp.concatenate([q[:, g * HEAD_DIM:(g + 1) * HEAD_DIM] for g in range(GQA_GROUPS)], axis=0)
        s = _dot_nt(qs, k)
        p = jnp.exp(s - jnp.max(s, axis=-1, keepdims=True))
        o = _dot(p.astype(BF16), v) / jnp.sum(p, axis=-1, keepdims=True)
        o_ref[c * ts:(c + 1) * ts, :] = jnp.concatenate(
            [o[g * ts:(g + 1) * ts, :] for g in range(GQA_GROUPS)], axis=-1).astype(o_ref.dtype)


def _dense_attn(q, k, v, batch, seq):
    t = q.shape[0]
    tq = min(ATT_TQ, seq)
    nq = seq // tq
    gw = GQA_GROUPS * HEAD_DIM
    qmap = lambda bb, hk, qi: (bb * nq + qi, hk)
    kmap = lambda bb, hk, qi: (hk, bb, 0)
    return pl.pallas_call(
        _dense_attn_kernel,
        grid=(batch, N_KV_HEADS, nq),
        in_specs=[
            pl.BlockSpec((tq, gw), qmap),
            pl.BlockSpec((None, seq, HEAD_DIM), kmap),
            pl.BlockSpec((None, seq, HEAD_DIM), kmap),
        ],
        out_specs=pl.BlockSpec((tq, gw), qmap),
        out_shape=jax.ShapeDtypeStruct((t, ATTN_WIDTH), BF16),
        compiler_params=_cparams(("parallel", "parallel", "parallel")),
        name="dense_attn",
    )(q, k, v)


def _sgu_kernel(d_ref, g_ref, b_ref, w_ref, bs_ref, o_ref):
    z = jax.nn.gelu(d_ref[...])
    u = z[:, :MIX_WIDTH]
    v = _layernorm(z[:, MIX_WIDTH:], g_ref[...], b_ref[...]).astype(BF16)
    gc = MIX_WIDTH // SGU_GROUPS
    for n in range(d_ref.shape[0] // CHUNK):
        rows = slice(n * CHUNK, (n + 1) * CHUNK)
        for g in range(SGU_GROUPS):
            cols = slice(g * gc, (g + 1) * gc)
            mixed = _dot(w_ref[g], v[rows, cols]) + bs_ref[g]
            o_ref[rows, cols] = (u[rows, cols] * mixed).astype(o_ref.dtype)


def _sgu(d_in, g, b, w, bs):
    t = d_in.shape[0]
    ts = ROW_TILE
    gc = MIX_WIDTH // SGU_GROUPS
    row = lambda i: (i, 0)
    fixed = lambda i: (0, 0)
    fixed3 = lambda i: (0, 0, 0)
    return pl.pallas_call(
        _sgu_kernel,
        grid=(t // ts,),
        in_specs=[
            pl.BlockSpec((ts, 2 * MIX_WIDTH), row),
            pl.BlockSpec((1, MIX_WIDTH), fixed),
            pl.BlockSpec((1, MIX_WIDTH), fixed),
            pl.BlockSpec((SGU_GROUPS, CHUNK, CHUNK), fixed3),
            pl.BlockSpec((SGU_GROUPS, CHUNK, gc), fixed3),
        ],
        out_specs=pl.BlockSpec((ts, MIX_WIDTH), row),
        out_shape=jax.ShapeDtypeStruct((t, MIX_WIDTH), BF16),
        compiler_params=_cparams(("parallel",)),
        name="spatial_gating",
    )(d_in, g, b, w, bs)


def _outproj_kernel(x_ref, a_ref, b_ref, w_ref, g_ref, xn_ref, hnt_ref):
    y = _dot(a_ref[...], w_ref[:MIX_WIDTH, :]) + _dot(b_ref[...], w_ref[MIX_WIDTH:, :])
    xn = x_ref[...] + y
    xn_ref[...] = xn
    hnt_ref[...] = _rms(xn, g_ref[...]).T.astype(BF16)


def _outproj(x, a, b, w, g):
    t = x.shape[0]
    tm = ROW_TILE
    row = lambda i: (i, 0)
    fixed = lambda i: (0, 0)
    return pl.pallas_call(
        _outproj_kernel,
        grid=(t // tm,),
        in_specs=[
            pl.BlockSpec((tm, D_MODEL), row),
            pl.BlockSpec((tm, MIX_WIDTH), row),
            pl.BlockSpec((tm, MIX_WIDTH), row),
            pl.BlockSpec((2 * MIX_WIDTH, D_MODEL), fixed),
            pl.BlockSpec((1, D_MODEL), fixed),
        ],
        out_specs=[pl.BlockSpec((tm, D_MODEL), row), pl.BlockSpec((D_MODEL, tm), lambda i: (0, i))],
        out_shape=[jax.ShapeDtypeStruct((t, D_MODEL), F32), jax.ShapeDtypeStruct((D_MODEL, t), BF16)],
        compiler_params=_cparams(("parallel",)),
        name="outproj",
    )(x, a, b, w, g)


_PEER_PAIRS = [(i, j) for i in range(PEER_TOPK) for j in range(PEER_TOPK) if (i + 1) * (j + 1) <= PEER_TOPK]
_PEER_CAND_ROWS = -(-len(_PEER_PAIRS) // 8) * 8


SUBLANES = 8


def _bitonic_merge_desc(v, lo, n):
    step = n // 2
    while step >= 1:
        for i in range(lo, lo + n):
            if (i - lo) & step == 0:
                hi_, lo_ = jnp.maximum(v[i], v[i + step]), jnp.minimum(v[i], v[i + step])
                v[i], v[i + step] = hi_, lo_
        step //= 2


def _sort16_desc(v):
    n = len(v)
    size = 2
    while size <= n:
        for lo in range(0, n, size):
            half = size // 2
            v[lo + half:lo + size] = v[lo + half:lo + size][::-1]
            _bitonic_merge_desc(v, lo, size)
        size *= 2
    return v


def _sublane_all(x, op):
    for shift in (4, 2, 1):
        x = op(x, pltpu.roll(x, shift, 0))
    return x


def _top16_sorted(s):
    v = _sort16_desc([s[k * SUBLANES:(k + 1) * SUBLANES, :] for k in range(PEER_NKEYS // SUBLANES)])
    for shift in (4, 2, 1):
        v = [jnp.maximum(v[k], pltpu.roll(v[PEER_TOPK - 1 - k], shift, 0)) for k in range(PEER_TOPK)]
        _bitonic_merge_desc(v, 0, PEER_TOPK)
    return v


def _peer_select(h, s1_ref, s2_ref, n_ref, e1_ref, r2_ref, e2_ref):
    tt = s1_ref.shape[-1]
    nv = PEER_NKEYS // SUBLANES
    sub = lax.broadcasted_iota(jnp.int32, (SUBLANES, LANE), 0)
    for tc in range(tt // LANE):
        lanes = slice(tc * LANE, (tc + 1) * LANE)
        s1 = s1_ref[h, :, lanes]
        s2 = s2_ref[h, :, lanes]
        v1 = _top16_sorted(s1)
        v2 = _top16_sorted(s2)
        pair_sum = {p: v1[p[0]] + v2[p[1]] for p in _PEER_PAIRS}
        cand = []
        for g in range(_PEER_CAND_ROWS // SUBLANES):
            c = jnp.full((SUBLANES, LANE), -jnp.inf, F32)
            for r, p in enumerate(_PEER_PAIRS[g * SUBLANES:(g + 1) * SUBLANES]):
                c = jnp.where(sub == r, pair_sum[p], c)
            cand.append(c)
        f = []
        for _ in range(PEER_TOPK):
            mx = cand[0]
            for c in cand[1:]:
                mx = jnp.maximum(mx, c)
            mx = _sublane_all(mx, jnp.maximum)
            f.append(mx)
            cand = [jnp.where(c == mx, -jnp.inf, c) for c in cand]
        thr = f[PEER_TOPK - 1]
        z = jnp.ones((SUBLANES, LANE), F32)
        for r in range(1, PEER_TOPK):
            z = z + jnp.exp(f[r] - f[0])
        rz = 1.0 / z
        cnt = []
        for i in range(PEER_TOPK):
            c = jnp.zeros((SUBLANES, LANE), F32)
            for j in range(PEER_TOPK // (i + 1)):
                c = c + jnp.where(pair_sum[(i, j)] >= thr, 1.0, 0.0)
            cnt.append(c)
        ranks = []
        e2s = []
        for k in range(nv):
            rows = slice(k * SUBLANES, (k + 1) * SUBLANES)
            a = s1[rows, :]
            b = s2[rows, :]
            n = jnp.zeros((SUBLANES, LANE), F32)
            rank = jnp.full((SUBLANES, LANE), float(PEER_TOPK), F32)
            for r in range(PEER_TOPK):
                n = jnp.where(a == v1[r], cnt[r], n)
                rank = jnp.where(b == v2[r], float(r), rank)
            n_ref[h, rows, lanes] = n
            e1_ref[h, rows, lanes] = jnp.exp(a - v1[0]) * rz
            ranks.append(rank)
            e2s.append(jnp.exp(b - v2[0]))
        r2_ref[h, :, lanes] = jnp.concatenate(ranks, axis=0).astype(BF16)
        e2_ref[h, :, lanes] = jnp.concatenate(e2s, axis=0).astype(BF16)


def _peer_kernel(x_ref, hnt_ref, wqt_ref, sk_ref, u_ref, vt_ref, o_ref,
                 s1_ref, s2_ref, n_ref, e1_ref, r2_ref, e2_ref, acc_ref, ht_ref):
    j = pl.program_id(1)
    tt = hnt_ref.shape[1]
    half = PEER_DKEY // 2

    @pl.when(j == 0)
    def _():
        qt = _dot(wqt_ref[...], hnt_ref[...]).astype(BF16)
        for h in range(PEER_HEADS):
            r0 = h * PEER_DKEY
            s1_ref[h] = _dot(sk_ref[2 * h], qt[r0:r0 + half, :])
            s2_ref[h] = _dot(sk_ref[2 * h + 1], qt[r0 + half:r0 + PEER_DKEY, :])

        def body(h, carry):
            _peer_select(h, s1_ref, s2_ref, n_ref, e1_ref, r2_ref, e2_ref)
            return carry

        lax.fori_loop(0, PEER_HEADS, body, 0)
        acc_ref[...] = jnp.zeros(acc_ref.shape, F32)

    a0 = pl.multiple_of(j * PEER_A, PEER_A)
    hnt = hnt_ref[...]
    nrows = [[n_ref[h, pl.ds(a0, PEER_A), tc * LANE:(tc + 1) * LANE].astype(BF16) for h in range(PEER_HEADS)]
             for tc in range(tt // LANE)]
    erows = [[e1_ref[h, pl.ds(a0, PEER_A), tc * LANE:(tc + 1) * LANE].astype(BF16) for h in range(PEER_HEADS)]
             for tc in range(tt // LANE)]
    nsub = PEER_A // PEER_SUB
    sub_rows = lambda sub: slice(sub * PEER_SUB * PEER_NKEYS, (sub + 1) * PEER_SUB * PEER_NKEYS)
    def value_matmul(sub):
        acc_ref[...] += _dot(vt_ref[:, sub_rows(sub)], ht_ref[sub_rows(sub), :])

    at_next = _dot(u_ref[sub_rows(0), :], hnt).astype(BF16)
    for sub in range(nsub):
        at = at_next
        if sub + 1 < nsub:
            at_next = _dot(u_ref[sub_rows(sub + 1), :], hnt).astype(BF16)
        for k in range(PEER_SUB):
            al = sub * PEER_SUB + k
            for tc in range(tt // LANE):
                lanes = slice(tc * LANE, (tc + 1) * LANE)
                gate = jnp.zeros((PEER_NKEYS, LANE), BF16)
                for h in range(PEER_HEADS):
                    keep = r2_ref[h, :, lanes] < nrows[tc][h][al:al + 1, :]
                    gate = gate + jnp.where(keep, e2_ref[h, :, lanes], 0.0) * erows[tc][h][al:al + 1, :]
                act = jax.nn.gelu(at[k * PEER_NKEYS:(k + 1) * PEER_NKEYS, lanes])
                ht_ref[al * PEER_NKEYS:(al + 1) * PEER_NKEYS, lanes] = act * gate
        if sub >= 1:
            value_matmul(sub - 1)
    value_matmul(nsub - 1)

    @pl.when(j == pl.num_programs(1) - 1)
    def _():
        o_ref[...] = x_ref[...] + acc_ref[...].T


def _peer(x, hnt, wqt, sk, u, vt):
    t = x.shape[0]
    tt = PEER_TT
    ne = u.shape[0]
    et = PEER_A * PEER_NKEYS
    hk = PEER_HEADS * PEER_DKEY
    sel = lambda dt: pltpu.VMEM((PEER_HEADS, PEER_NKEYS, tt), dt)
    return pl.pallas_call(
        _peer_kernel,
        grid=(t // tt, ne // et),
        in_specs=[
            pl.BlockSpec((tt, D_MODEL), lambda i, j: (i, 0)),
            pl.BlockSpec((D_MODEL, tt), lambda i, j: (0, i)),
            pl.BlockSpec((hk, D_MODEL), lambda i, j: (0, 0)),
            pl.BlockSpec((2 * PEER_HEADS, PEER_NKEYS, PEER_DKEY // 2), lambda i, j: (0, 0, 0)),
            pl.BlockSpec((et, D_MODEL), lambda i, j: (j, 0)),
            pl.BlockSpec((D_MODEL, et), lambda i, j: (0, j)),
        ],
        out_specs=pl.BlockSpec((tt, D_MODEL), lambda i, j: (i, 0)),
        out_shape=jax.ShapeDtypeStruct((t, D_MODEL), F32),
        scratch_shapes=[
            sel(F32), sel(F32), sel(F32), sel(F32), sel(BF16), sel(BF16),
            pltpu.VMEM((D_MODEL, tt), F32),
            pltpu.VMEM((et, tt), BF16),
        ],
        compiler_params=_cparams(("parallel", "arbitrary")),
        name="peer",
    )(x, hnt, wqt, sk, u, vt)


def _final_norm_kernel(x_ref, g_ref, o_ref):
    o_ref[...] = _rms(x_ref[...], g_ref[...])


def _final_norm(x, g):
    t = x.shape[0]
    tm = ROW_TILE
    return pl.pallas_call(
        _final_norm_kernel,
        grid=(t // tm,),
        in_specs=[pl.BlockSpec((tm, D_MODEL), lambda i: (i, 0)), pl.BlockSpec((1, D_MODEL), lambda i: (0, 0))],
        out_specs=pl.BlockSpec((tm, D_MODEL), lambda i: (i, 0)),
        out_shape=jax.ShapeDtypeStruct((t, D_MODEL), F32),
        compiler_params=_cparams(("parallel",)),
        name="final_norm",
    )(x, g)


def _rope_table(pos, dim):
    inv = ROPE_THETA ** (-jnp.arange(0, dim, 2, dtype=F32) / dim)
    ang = pos.astype(F32)[:, None] * inv[None, :]
    ang = jnp.concatenate([ang, ang], axis=-1)
    sign = jnp.concatenate([-jnp.ones((dim // 2,), F32), jnp.ones((dim // 2,), F32)])
    return jnp.cos(ang), jnp.sin(ang) * sign


def _rope_tables(seq):
    pos = jnp.arange(seq)
    cos1, sin1 = _rope_table(pos, HEAD_DIM)
    cr, sr = _rope_table(pos // GRID_W, HEAD_DIM // 2)
    cc, sc = _rope_table(pos % GRID_W, HEAD_DIM // 2)
    cos2 = jnp.concatenate([cr, cc], axis=-1)
    sin2 = jnp.concatenate([sr, sc], axis=-1)
    tile = lambda a: jnp.tile(a, (1, N_Q_HEADS))
    return tile(cos1), tile(sin1), tile(cos2), tile(sin2)


def kernel(x, mix_norm_g, ffn_norm_g, final_norm_g, even_w_in, even_w_out, conv_w, conv_b, conv_ln_g, conv_ln_b, sink_logits, odd_w_in, odd_w_out, q_norm_g, k_norm_g, sgu_ln_g, sgu_ln_b, sgu_w, sgu_b, peer_wq, peer_subkeys, peer_u, peer_v):
    batch, seq, d = x.shape
    depth = mix_norm_g.shape[0]
    t = batch * seq
    cos1, sin1, cos2, sin2 = _rope_tables(seq)
    row = lambda a: a.reshape(1, -1).astype(F32)
    xf = x.reshape(t, d)
    for layer in range(depth):
        i = layer // 2
        if layer % 2 == 0:
            a_in, q, k, v = _inproj_even(xf, row(mix_norm_g[layer]), even_w_in[i].astype(BF16), cos1, sin1, seq)
            m1 = _conv_module(a_in, conv_w[i, :, 0, :], row(conv_b[i]), row(conv_ln_g[i]), row(conv_ln_b[i]), batch, seq)
            sink_b = jnp.broadcast_to(sink_logits[i].astype(F32)[:, None], (N_Q_HEADS, LANE))
            m2 = _swa(q, k, v, sink_b, batch, seq)
            w_out = even_w_out[i]
        else:
            q, k, v, d_in = _inproj_odd(xf, row(mix_norm_g[layer]), odd_w_in[i].astype(BF16),
                                        row(jnp.tile(q_norm_g[i], N_Q_HEADS)), row(jnp.tile(k_norm_g[i], N_KV_HEADS)),
                                        cos2, sin2, seq)
            m1 = _dense_attn(q, k, v, batch, seq)
            bs = jnp.broadcast_to(sgu_b[i].astype(F32)[:, :, None], (SGU_GROUPS, CHUNK, MIX_WIDTH // SGU_GROUPS))
            m2 = _sgu(d_in, row(sgu_ln_g[i]), row(sgu_ln_b[i]), sgu_w[i].astype(BF16), bs)
            w_out = odd_w_out[i]
        xn, hnt = _outproj(xf, m1, m2, w_out.astype(BF16), row(ffn_norm_g[layer]))
        sk = peer_subkeys[layer].reshape(2 * PEER_HEADS, PEER_NKEYS, PEER_DKEY // 2).astype(BF16)
        xf = _peer(xn, hnt, peer_wq[layer].T.astype(BF16), sk, peer_u[layer].astype(BF16),
                   peer_v[layer].T.astype(BF16))
    return _final_norm(xf, row(final_norm_g)).reshape(batch, seq, d)
```

```python
import functools
import math

import jax
import jax.numpy as jnp
from jax import lax
from jax.experimental import pallas as pl
from jax.experimental.pallas import tpu as pltpu

F32 = jnp.float32
BF16 = jnp.bfloat16

D_MODEL = 1024
HEAD_DIM = 64
N_Q_HEADS = 8
N_KV_HEADS = 2
GQA_GROUPS = N_Q_HEADS // N_KV_HEADS
ATTN_WIDTH = N_Q_HEADS * HEAD_DIM
KV_WIDTH = N_KV_HEADS * HEAD_DIM
MIX_WIDTH = D_MODEL // 2
CONV_WIDTH = 31
CONV_HALO = 16
SGU_GROUPS = 4
CHUNK = 128
WINDOW = 128
BLOCK_Q = 128
GRID_W = 64
ROPE_THETA = 10000.0
PEER_HEADS = 8
PEER_NKEYS = 128
PEER_DKEY = 128
PEER_TOPK = 16
EPS = 1e-6
NEG = -1e30

VMEM_LIMIT_BYTES = 56 * 1024 * 1024

ROW_TILE = 512
CONV_ROWS = 64
ATT_TQ = 256
ATT_CHAINS = 2
PEER_TT = 1024
PEER_A = 8
PEER_SUB = 4
LANE = 128


def _cparams(sem):
    return pltpu.CompilerParams(dimension_semantics=sem, vmem_limit_bytes=VMEM_LIMIT_BYTES)


def _dot(a, b):
    return jnp.dot(a, b, preferred_element_type=F32)


def _dot_nt(a, b):
    return lax.dot_general(a, b, (((1,), (1,)), ((), ())), preferred_element_type=F32)


def _rms(x, g):
    return x * lax.rsqrt(jnp.mean(x * x, axis=-1, keepdims=True) + EPS) * g


def _layernorm(x, g, b):
    mu = jnp.mean(x, axis=-1, keepdims=True)
    xc = x - mu
    var = jnp.mean(xc * xc, axis=-1, keepdims=True)
    return xc * lax.rsqrt(var + EPS) * g + b


def _rope(x, cos, sin_signed, group, half):
    n = x.shape[-1]
    lane = lax.broadcasted_iota(jnp.int32, x.shape, x.ndim - 1)
    first = (lane % group) < half
    partner = jnp.where(first, pltpu.roll(x, n - half, x.ndim - 1), pltpu.roll(x, half, x.ndim - 1))
    return x * cos + partner * sin_signed


def _inproj_even_kernel(x_ref, g_ref, w_ref, cos_ref, sin_ref, a_ref, q_ref, k_ref, v_ref):
    h = _rms(x_ref[...], g_ref[...]).astype(BF16)
    o1 = 2 * MIX_WIDTH
    o2 = o1 + ATTN_WIDTH
    o3 = o2 + KV_WIDTH
    a_ref[...] = _dot(h, w_ref[:, :o1])
    cos = cos_ref[...]
    sin = sin_ref[...]
    q = _rope(_dot(h, w_ref[:, o1:o2]), cos, sin, HEAD_DIM, HEAD_DIM // 2)
    q_ref[...] = (q * (HEAD_DIM ** -0.5)).astype(BF16)
    k = _rope(_dot(h, w_ref[:, o2:o3]), cos[:, :KV_WIDTH], sin[:, :KV_WIDTH], HEAD_DIM, HEAD_DIM // 2)
    k_ref[...] = k.astype(BF16)
    v_ref[...] = _dot(h, w_ref[:, o3:]).astype(BF16)


def _inproj_even(x, g, w, cos, sin, seq):
    t = x.shape[0]
    tm = ROW_TILE
    nseq = seq // tm
    in_w = w.shape[1]
    row = lambda i: (i, 0)
    fixed = lambda i: (0, 0)
    pos = lambda i: (i % nseq, 0)
    return pl.pallas_call(
        _inproj_even_kernel,
        grid=(t // tm,),
        in_specs=[
            pl.BlockSpec((tm, D_MODEL), row),
            pl.BlockSpec((1, D_MODEL), fixed),
            pl.BlockSpec((D_MODEL, in_w), fixed),
            pl.BlockSpec((tm, ATTN_WIDTH), pos),
            pl.BlockSpec((tm, ATTN_WIDTH), pos),
        ],
        out_specs=[
            pl.BlockSpec((tm, 2 * MIX_WIDTH), row),
            pl.BlockSpec((tm, ATTN_WIDTH), row),
            pl.BlockSpec((tm, KV_WIDTH), row),
            pl.BlockSpec((tm, KV_WIDTH), row),
        ],
        out_shape=[
            jax.ShapeDtypeStruct((t, 2 * MIX_WIDTH), F32),
            jax.ShapeDtypeStruct((t, ATTN_WIDTH), BF16),
            jax.ShapeDtypeStruct((t, KV_WIDTH), BF16),
            jax.ShapeDtypeStruct((t, KV_WIDTH), BF16),
        ],
        compiler_params=_cparams(("parallel",)),
        name="inproj_even",
    )(x, g, w, cos, sin)


def _head_sumsq(x):
    n = x.shape[-1]
    r = lax.broadcasted_iota(jnp.int32, (n, n), 0) // HEAD_DIM
    c = lax.broadcasted_iota(jnp.int32, (n, n), 1) // HEAD_DIM
    ones = jnp.where(r == c, 1.0, 0.0).astype(BF16)
    sq = x * x
    hi = sq.astype(BF16)
    lo = (sq - hi.astype(F32)).astype(BF16)
    return _dot(hi, ones) + _dot(lo, ones)


def _inproj_odd_kernel(x_ref, g_ref, w_ref, qg_ref, kg_ref, cos_ref, sin_ref, q_ref, k_ref, v_ref, d_ref):
    h = _rms(x_ref[...], g_ref[...]).astype(BF16)
    c1 = ATTN_WIDTH
    c2 = c1 + KV_WIDTH
    c3 = c2 + KV_WIDTH
    cos = cos_ref[...]
    sin = sin_ref[...]
    half = HEAD_DIM // 2
    q = _dot(h, w_ref[:, :c1])
    q = q * lax.rsqrt(_head_sumsq(q) * (1.0 / HEAD_DIM) + EPS) * qg_ref[...]
    q = _rope(q, cos, sin, half, half // 2)
    q_ref[...] = (q * (HEAD_DIM ** -0.5)).astype(BF16)
    k = _dot(h, w_ref[:, c1:c2])
    k = k * lax.rsqrt(_head_sumsq(k) * (1.0 / HEAD_DIM) + EPS) * kg_ref[...]
    k = _rope(k, cos[:, :KV_WIDTH], sin[:, :KV_WIDTH], half, half // 2).astype(BF16)
    v = _dot(h, w_ref[:, c2:c3]).astype(BF16)
    for hh in range(N_KV_HEADS):
        k_ref[hh] = k[:, hh * HEAD_DIM:(hh + 1) * HEAD_DIM]
        v_ref[hh] = v[:, hh * HEAD_DIM:(hh + 1) * HEAD_DIM]
    d_ref[...] = _dot(h, w_ref[:, c3:])


def _inproj_odd(x, g, w, qg, kg, cos, sin, seq):
    t = x.shape[0]
    tm = ROW_TILE
    nseq = seq // tm
    in_w = w.shape[1]
    row = lambda i: (i, 0)
    fixed = lambda i: (0, 0)
    pos = lambda i: (i % nseq, 0)
    kv = lambda i: (0, i, 0)
    return pl.pallas_call(
        _inproj_odd_kernel,
        grid=(t // tm,),
        in_specs=[
            pl.BlockSpec((tm, D_MODEL), row),
            pl.BlockSpec((1, D_MODEL), fixed),
            pl.BlockSpec((D_MODEL, in_w), fixed),
            pl.BlockSpec((1, ATTN_WIDTH), fixed),
            pl.BlockSpec((1, KV_WIDTH), fixed),
            pl.BlockSpec((tm, ATTN_WIDTH), pos),
            pl.BlockSpec((tm, ATTN_WIDTH), pos),
        ],
        out_specs=[
            pl.BlockSpec((tm, ATTN_WIDTH), row),
            pl.BlockSpec((N_KV_HEADS, tm, HEAD_DIM), kv),
            pl.BlockSpec((N_KV_HEADS, tm, HEAD_DIM), kv),
            pl.BlockSpec((tm, 2 * MIX_WIDTH), row),
        ],
        out_shape=[
            jax.ShapeDtypeStruct((t, ATTN_WIDTH), BF16),
            jax.ShapeDtypeStruct((N_KV_HEADS, t, HEAD_DIM), BF16),
            jax.ShapeDtypeStruct((N_KV_HEADS, t, HEAD_DIM), BF16),
            jax.ShapeDtypeStruct((t, 2 * MIX_WIDTH), F32),
        ],
        compiler_params=_cparams(("parallel",)),
        name="inproj_odd",
    )(x, g, w, qg, kg, cos, sin)


def _glu(x):
    return x[:, :MIX_WIDTH] * jax.nn.sigmoid(x[:, MIX_WIDTH:])


def _conv_kernel(prev_ref, cur_ref, next_ref, w_ref, b_ref, g_ref, beta_ref, o_ref, buf_ref):
    i = pl.program_id(1)
    ns = pl.num_programs(1)
    ts = cur_ref.shape[0]
    halo = CONV_HALO
    buf_ref[0:halo, :] = jnp.where(i > 0, _glu(prev_ref[...]), 0.0)
    buf_ref[halo:halo + ts, :] = _glu(cur_ref[...])
    buf_ref[halo + ts:, :] = jnp.where(i < ns - 1, _glu(next_ref[...]), 0.0)
    w = w_ref[...]
    bias = b_ref[...]
    g = g_ref[...]
    beta = beta_ref[...]
    first = halo - CONV_WIDTH // 2
    for c in range(ts // CONV_ROWS):
        r0 = c * CONV_ROWS
        acc = jnp.zeros((CONV_ROWS, MIX_WIDTH), F32)
        for k in range(CONV_WIDTH):
            acc = acc + w[k:k + 1, :] * buf_ref[r0 + first + k:r0 + first + k + CONV_ROWS, :]
        y = _layernorm(acc + bias, g, beta)
        o_ref[r0:r0 + CONV_ROWS, :] = (y * jax.nn.sigmoid(y)).astype(o_ref.dtype)


def _conv_module(a_in, w, b, g, beta, batch, seq):
    t = a_in.shape[0]
    ts = ROW_TILE
    ns = seq // ts
    hb = ts // CONV_HALO
    last = t // CONV_HALO - 1
    fixed = lambda bb, i: (0, 0)
    return pl.pallas_call(
        _conv_kernel,
        grid=(batch, ns),
        in_specs=[
            pl.BlockSpec((CONV_HALO, 2 * MIX_WIDTH), lambda bb, i: (jnp.maximum((bb * ns + i) * hb - 1, 0), 0)),
            pl.BlockSpec((ts, 2 * MIX_WIDTH), lambda bb, i: (bb * ns + i, 0)),
            pl.BlockSpec((CONV_HALO, 2 * MIX_WIDTH), lambda bb, i: (jnp.minimum((bb * ns + i + 1) * hb, last), 0)),
            pl.BlockSpec((CONV_WIDTH, MIX_WIDTH), fixed),
            pl.BlockSpec((1, MIX_WIDTH), fixed),
            pl.BlockSpec((1, MIX_WIDTH), fixed),
            pl.BlockSpec((1, MIX_WIDTH), fixed),
        ],
        out_specs=pl.BlockSpec((ts, MIX_WIDTH), lambda bb, i: (bb * ns + i, 0)),
        out_shape=jax.ShapeDtypeStruct((t, MIX_WIDTH), BF16),
        scratch_shapes=[pltpu.VMEM((ts + 2 * CONV_HALO, MIX_WIDTH), F32)],
        compiler_params=_cparams(("parallel", "parallel")),
        name="conv_module",
    )(a_in, a_in, a_in, w, b, g, beta)


def _swa_kernel(q_ref, kp_ref, kc_ref, kn_ref, vp_ref, vc_ref, vn_ref, sink_ref, o_ref):
    n = pl.program_id(1)
    nb = pl.num_programs(1)
    bq = q_ref.shape[0]
    ii = lax.broadcasted_iota(jnp.int32, (bq, 3 * bq), 0)
    jj = lax.broadcasted_iota(jnp.int32, (bq, 3 * bq), 1)
    lo = jnp.where(n > 0, 0, bq)
    hi = jnp.where(n < nb - 1, 3 * bq, 2 * bq)
    mask = (jj >= ii) & (jj <= ii + 2 * WINDOW) & (jj >= lo) & (jj < hi)
    kcat = jnp.concatenate([kp_ref[...], kc_ref[...], kn_ref[...]], axis=0)
    vcat = jnp.concatenate([vp_ref[...], vc_ref[...], vn_ref[...]], axis=0)
    q = q_ref[...]
    outs = []
    for hd in range(N_Q_HEADS):
        hk = hd // GQA_GROUPS
        kh = kcat[:, hk * HEAD_DIM:(hk + 1) * HEAD_DIM]
        vh = vcat[:, hk * HEAD_DIM:(hk + 1) * HEAD_DIM]
        s = _dot_nt(q[:, hd * HEAD_DIM:(hd + 1) * HEAD_DIM], kh)
        s = jnp.where(mask, s, NEG)
        sink = sink_ref[hd:hd + 1, 0:1]
        m = jnp.maximum(jnp.max(s, axis=-1, keepdims=True), sink)
        p = jnp.exp(s - m)
        denom = jnp.sum(p, axis=-1, keepdims=True) + jnp.exp(sink - m)
        outs.append(_dot(p.astype(BF16), vh) / denom)
    o_ref[...] = jnp.concatenate(outs, axis=-1).astype(o_ref.dtype)


def _swa(q, k, v, sink_b, batch, seq):
    t = q.shape[0]
    bq = BLOCK_Q
    nb = seq // bq
    last = t // bq - 1
    cur = lambda bb, n: (bb * nb + n, 0)
    prev = lambda bb, n: (jnp.maximum(bb * nb + n - 1, 0), 0)
    nxt = lambda bb, n: (jnp.minimum(bb * nb + n + 1, last), 0)
    kvs = lambda im: pl.BlockSpec((bq, KV_WIDTH), im)
    return pl.pallas_call(
        _swa_kernel,
        grid=(batch, nb),
        in_specs=[pl.BlockSpec((bq, ATTN_WIDTH), cur), kvs(prev), kvs(cur), kvs(nxt), kvs(prev), kvs(cur), kvs(nxt),
                  pl.BlockSpec((N_Q_HEADS, LANE), lambda bb, n: (0, 0))],
        out_specs=pl.BlockSpec((bq, ATTN_WIDTH), cur),
        out_shape=jax.ShapeDtypeStruct((t, ATTN_WIDTH), BF16),
        compiler_params=_cparams(("parallel", "parallel")),
        name="window_attn",
    )(q, k, k, k, v, v, v, sink_b)


def _dense_attn_kernel(q_ref, k_ref, v_ref, o_ref):
    ts = q_ref.shape[0] // ATT_CHAINS
    k = k_ref[...]
    v = v_ref[...]
    for c in range(ATT_CHAINS):
        q = q_ref[c * ts:(c + 1) * ts, :]
        qs = jnp.concatenate([q[:, g * HEAD_DIM:(g + 1) * HEAD_DIM] for g in range(GQA_GROUPS)], axis=0)
        s = _dot_nt(qs, k)
        p = jnp.exp(s - jnp.max(s, axis=-1, keepdims=True))
        o = _dot(p.astype(BF16), v) / jnp.sum(p, axis=-1, keepdims=True)
        o_ref[c * ts:(c + 1) * ts, :] = jnp.concatenate(
            [o[g * ts:(g + 1) * ts, :] for g in range(GQA_GROUPS)], axis=-1).astype(o_ref.dtype)


def _dense_attn(q, k, v, batch, seq):
    t = q.shape[0]
    tq = min(ATT_TQ, seq)
    nq = seq // tq
    gw = GQA_GROUPS * HEAD_DIM
    qmap = lambda bb, hk, qi: (bb * nq + qi, hk)
    kmap = lambda bb, hk, qi: (hk, bb, 0)
    return pl.pallas_call(
        _dense_attn_kernel,
        grid=(batch, N_KV_HEADS, nq),
        in_specs=[
            pl.BlockSpec((tq, gw), qmap),
            pl.BlockSpec((None, seq, HEAD_DIM), kmap),
            pl.BlockSpec((None, seq, HEAD_DIM), kmap),
        ],
        out_specs=pl.BlockSpec((tq, gw), qmap),
        out_shape=jax.ShapeDtypeStruct((t, ATTN_WIDTH), BF16),
        compiler_params=_cparams(("parallel", "parallel", "parallel")),
        name="dense_attn",
    )(q, k, v)


def _sgu_kernel(d_ref, g_ref, b_ref, w_ref, bs_ref, o_ref):
    z = jax.nn.gelu(d_ref[...])
    u = z[:, :MIX_WIDTH]
    v = _layernorm(z[:, MIX_WIDTH:], g_ref[...], b_ref[...]).astype(BF16)
    gc = MIX_WIDTH // SGU_GROUPS
    for n in range(d_ref.shape[0] // CHUNK):
        rows = slice(n * CHUNK, (n + 1) * CHUNK)
        for g in range(SGU_GROUPS):
            cols = slice(g * gc, (g + 1) * gc)
            mixed = _dot(w_ref[g], v[rows, cols]) + bs_ref[g]
            o_ref[rows, cols] = (u[rows, cols] * mixed).astype(o_ref.dtype)


def _sgu(d_in, g, b, w, bs):
    t = d_in.shape[0]
    ts = ROW_TILE
    gc = MIX_WIDTH // SGU_GROUPS
    row = lambda i: (i, 0)
    fixed = lambda i: (0, 0)
    fixed3 = lambda i: (0, 0, 0)
    return pl.pallas_call(
        _sgu_kernel,
        grid=(t // ts,),
        in_specs=[
            pl.BlockSpec((ts, 2 * MIX_WIDTH), row),
            pl.BlockSpec((1, MIX_WIDTH), fixed),
            pl.BlockSpec((1, MIX_WIDTH), fixed),
            pl.BlockSpec((SGU_GROUPS, CHUNK, CHUNK), fixed3),
            pl.BlockSpec((SGU_GROUPS, CHUNK, gc), fixed3),
        ],
        out_specs=pl.BlockSpec((ts, MIX_WIDTH), row),
        out_shape=jax.ShapeDtypeStruct((t, MIX_WIDTH), BF16),
        compiler_params=_cparams(("parallel",)),
        name="spatial_gating",
    )(d_in, g, b, w, bs)


def _outproj_kernel(x_ref, a_ref, b_ref, w_ref, g_ref, xn_ref, hnt_ref):
    y = _dot(a_ref[...], w_ref[:MIX_WIDTH, :]) + _dot(b_ref[...], w_ref[MIX_WIDTH:, :])
    xn = x_ref[...] + y
    xn_ref[...] = xn
    hnt_ref[...] = _rms(xn, g_ref[...]).T.astype(BF16)


def _outproj(x, a, b, w, g):
    t = x.shape[0]
    tm = ROW_TILE
    row = lambda i: (i, 0)
    fixed = lambda i: (0, 0)
    return pl.pallas_call(
        _outproj_kernel,
        grid=(t // tm,),
        in_specs=[
            pl.BlockSpec((tm, D_MODEL), row),
            pl.BlockSpec((tm, MIX_WIDTH), row),
            pl.BlockSpec((tm, MIX_WIDTH), row),
            pl.BlockSpec((2 * MIX_WIDTH, D_MODEL), fixed),
            pl.BlockSpec((1, D_MODEL), fixed),
        ],
        out_specs=[pl.BlockSpec((tm, D_MODEL), row), pl.BlockSpec((D_MODEL, tm), lambda i: (0, i))],
        out_shape=[jax.ShapeDtypeStruct((t, D_MODEL), F32), jax.ShapeDtypeStruct((D_MODEL, t), BF16)],
        compiler_params=_cparams(("parallel",)),
        name="outproj",
    )(x, a, b, w, g)


_PEER_PAIRS = [(i, j) for i in range(PEER_TOPK) for j in range(PEER_TOPK) if (i + 1) * (j + 1) <= PEER_TOPK]
_PEER_CAND_ROWS = -(-len(_PEER_PAIRS) // 8) * 8


SUBLANES = 8


def _bitonic_merge_desc(v, lo, n):
    step = n // 2
    while step >= 1:
        for i in range(lo, lo + n):
            if (i - lo) & step == 0:
                hi_, lo_ = jnp.maximum(v[i], v[i + step]), jnp.minimum(v[i], v[i + step])
                v[i], v[i + step] = hi_, lo_
        step //= 2


def _sort16_desc(v):
    n = len(v)
    size = 2
    while size <= n:
        for lo in range(0, n, size):
            half = size // 2
            v[lo + half:lo + size] = v[lo + half:lo + size][::-1]
            _bitonic_merge_desc(v, lo, size)
        size *= 2
    return v


def _sublane_all(x, op):
    for shift in (4, 2, 1):
        x = op(x, pltpu.roll(x, shift, 0))
    return x


def _top16_sorted(s):
    v = _sort16_desc([s[k * SUBLANES:(k + 1) * SUBLANES, :] for k in range(PEER_NKEYS // SUBLANES)])
    for shift in (4, 2, 1):
        v = [jnp.maximum(v[k], pltpu.roll(v[PEER_TOPK - 1 - k], shift, 0)) for k in range(PEER_TOPK)]
        _bitonic_merge_desc(v, 0, PEER_TOPK)
    return v


def _peer_select(h, s1_all, s2_all, n_ref, e1_ref, r2_ref, e2_ref):
    tt = s1_all.shape[-1]
    nv = PEER_NKEYS // SUBLANES
    sub = lax.broadcasted_iota(jnp.int32, (SUBLANES, LANE), 0)
    for tc in range(tt // LANE):
        lanes = slice(tc * LANE, (tc + 1) * LANE)
        s1 = s1_all[:, lanes]
        s2 = s2_all[:, lanes]
        v1 = _top16_sorted(s1)
        v2 = _top16_sorted(s2)
        pair_sum = {p: v1[p[0]] + v2[p[1]] for p in _PEER_PAIRS}
        cand = []
        for g in range(_PEER_CAND_ROWS // SUBLANES):
            c = jnp.full((SUBLANES, LANE), -jnp.inf, F32)
            for r, p in enumerate(_PEER_PAIRS[g * SUBLANES:(g + 1) * SUBLANES]):
                c = jnp.where(sub == r, pair_sum[p], c)
            cand.append(c)
        f = []
        for _ in range(PEER_TOPK):
            mx = cand[0]
            for c in cand[1:]:
                mx = jnp.maximum(mx, c)
            mx = _sublane_all(mx, jnp.maximum)
            f.append(mx)
            cand = [jnp.where(c == mx, -jnp.inf, c) for c in cand]
        thr = f[PEER_TOPK - 1]
        z = jnp.ones((SUBLANES, LANE), F32)
        for r in range(1, PEER_TOPK):
            z = z + jnp.exp(f[r] - f[0])
        rz = 1.0 / z
        cnt = []
        for i in range(PEER_TOPK):
            c = jnp.zeros((SUBLANES, LANE), F32)
            for j in range(PEER_TOPK // (i + 1)):
                c = c + jnp.where(pair_sum[(i, j)] >= thr, 1.0, 0.0)
            cnt.append(c)
        ranks = []
        e2s = []
        for k in range(nv):
            rows = slice(k * SUBLANES, (k + 1) * SUBLANES)
            a = s1[rows, :]
            b = s2[rows, :]
            n = jnp.zeros((SUBLANES, LANE), F32)
            rank = jnp.full((SUBLANES, LANE), float(PEER_TOPK), F32)
            for r in range(PEER_TOPK):
                n = jnp.where(a == v1[r], cnt[r], n)
                rank = jnp.where(b == v2[r], float(r), rank)
            n_ref[h, rows, lanes] = n
            e1_ref[h, rows, lanes] = jnp.exp(a - v1[0]) * rz
            ranks.append(rank)
            e2s.append(jnp.exp(b - v2[0]))
        r2_ref[h, :, lanes] = jnp.concatenate(ranks, axis=0).astype(BF16)
        e2_ref[h, :, lanes] = jnp.concatenate(e2s, axis=0).astype(BF16)


def _peer_kernel(x_ref, hnt_ref, wqt_ref, sk_ref, u_ref, vt_ref, o_ref,
                 qt_ref, n_ref, e1_ref, r2_ref, e2_ref, acc_ref, ht_ref):
    j = pl.program_id(1)
    tt = hnt_ref.shape[1]
    half = PEER_DKEY // 2

    @pl.when(j == 0)
    def _():
        qt_ref[...] = _dot(wqt_ref[...], hnt_ref[...]).astype(BF16)

        def body(h, carry):
            r0 = pl.multiple_of(h * PEER_DKEY, PEER_DKEY)
            s1 = _dot(sk_ref[2 * h], qt_ref[pl.ds(r0, half), :])
            s2 = _dot(sk_ref[2 * h + 1], qt_ref[pl.ds(r0 + half, half), :])
            _peer_select(h, s1, s2, n_ref, e1_ref, r2_ref, e2_ref)
            return carry

        lax.fori_loop(0, PEER_HEADS, body, 0)
        acc_ref[...] = jnp.zeros(acc_ref.shape, F32)

    a0 = pl.multiple_of(j * PEER_A, PEER_A)
    nrows = [[n_ref[h, pl.ds(a0, PEER_A), tc * LANE:(tc + 1) * LANE].astype(BF16) for h in range(PEER_HEADS)]
             for tc in range(tt // LANE)]
    erows = [[e1_ref[h, pl.ds(a0, PEER_A), tc * LANE:(tc + 1) * LANE].astype(BF16) for h in range(PEER_HEADS)]
             for tc in range(tt // LANE)]
    nsub = PEER_A // PEER_SUB
    sub_rows = lambda sub: slice(sub * PEER_SUB * PEER_NKEYS, (sub + 1) * PEER_SUB * PEER_NKEYS)
    def value_matmul(sub):
        acc_ref[...] += _dot(vt_ref[:, sub_rows(sub)], ht_ref[sub_rows(sub), :])

    at_next = _dot(u_ref[sub_rows(0), :], hnt_ref[...]).astype(BF16)
    for sub in range(nsub):
        at = at_next
        if sub + 1 < nsub:
            at_next = _dot(u_ref[sub_rows(sub + 1), :], hnt_ref[...]).astype(BF16)
        for tc in range(tt // LANE):
            lanes = slice(tc * LANE, (tc + 1) * LANE)
            gates = [jnp.zeros((PEER_NKEYS, LANE), BF16) for _ in range(PEER_SUB)]
            for h in range(PEER_HEADS):
                r2 = r2_ref[h, :, lanes]
                e2 = e2_ref[h, :, lanes]
                for k in range(PEER_SUB):
                    al = sub * PEER_SUB + k
                    keep = r2 < nrows[tc][h][al:al + 1, :]
                    gates[k] = gates[k] + jnp.where(keep, e2, 0.0) * erows[tc][h][al:al + 1, :]
            for k in range(PEER_SUB):
                al = sub * PEER_SUB + k
                act = jax.nn.gelu(at[k * PEER_NKEYS:(k + 1) * PEER_NKEYS, lanes])
                ht_ref[al * PEER_NKEYS:(al + 1) * PEER_NKEYS, lanes] = act * gates[k]
        if sub >= 1:
            value_matmul(sub - 1)
    value_matmul(nsub - 1)

    @pl.when(j == pl.num_programs(1) - 1)
    def _():
        o_ref[...] = x_ref[...] + acc_ref[...].T


def _peer(x, hnt, wqt, sk, u, vt):
    t = x.shape[0]
    tt = PEER_TT
    ne = u.shape[0]
    et = PEER_A * PEER_NKEYS
    hk = PEER_HEADS * PEER_DKEY
    sel = lambda dt: pltpu.VMEM((PEER_HEADS, PEER_NKEYS, tt), dt)
    return pl.pallas_call(
        _peer_kernel,
        grid=(t // tt, ne // et),
        in_specs=[
            pl.BlockSpec((tt, D_MODEL), lambda i, j: (i, 0), pipeline_mode=pl.Buffered(1)),
            pl.BlockSpec((D_MODEL, tt), lambda i, j: (0, i)),
            pl.BlockSpec((hk, D_MODEL), lambda i, j: (0, 0), pipeline_mode=pl.Buffered(1)),
            pl.BlockSpec((2 * PEER_HEADS, PEER_NKEYS, PEER_DKEY // 2), lambda i, j: (0, 0, 0)),
            pl.BlockSpec((et, D_MODEL), lambda i, j: (j, 0)),
            pl.BlockSpec((D_MODEL, et), lambda i, j: (0, j)),
        ],
        out_specs=pl.BlockSpec((tt, D_MODEL), lambda i, j: (i, 0)),
        out_shape=jax.ShapeDtypeStruct((t, D_MODEL), F32),
        scratch_shapes=[
            pltpu.VMEM((hk, tt), BF16),
            sel(F32), sel(F32), sel(BF16), sel(BF16),
            pltpu.VMEM((D_MODEL, tt), F32),
            pltpu.VMEM((et, tt), BF16),
        ],
        compiler_params=_cparams(("parallel", "arbitrary")),
        name="peer",
    )(x, hnt, wqt, sk, u, vt)


def _final_norm_kernel(x_ref, g_ref, o_ref):
    o_ref[...] = _rms(x_ref[...], g_ref[...])


def _final_norm(x, g):
    t = x.shape[0]
    tm = ROW_TILE
    return pl.pallas_call(
        _final_norm_kernel,
        grid=(t // tm,),
        in_specs=[pl.BlockSpec((tm, D_MODEL), lambda i: (i, 0)), pl.BlockSpec((1, D_MODEL), lambda i: (0, 0))],
        out_specs=pl.BlockSpec((tm, D_MODEL), lambda i: (i, 0)),
        out_shape=jax.ShapeDtypeStruct((t, D_MODEL), F32),
        compiler_params=_cparams(("parallel",)),
        name="final_norm",
    )(x, g)


def _rope_table(pos, dim):
    inv = ROPE_THETA ** (-jnp.arange(0, dim, 2, dtype=F32) / dim)
    ang = pos.astype(F32)[:, None] * inv[None, :]
    ang = jnp.concatenate([ang, ang], axis=-1)
    sign = jnp.concatenate([-jnp.ones((dim // 2,), F32), jnp.ones((dim // 2,), F32)])
    return jnp.cos(ang), jnp.sin(ang) * sign


def _rope_tables(seq):
    pos = jnp.arange(seq)
    cos1, sin1 = _rope_table(pos, HEAD_DIM)
    cr, sr = _rope_table(pos // GRID_W, HEAD_DIM // 2)
    cc, sc = _rope_table(pos % GRID_W, HEAD_DIM // 2)
    cos2 = jnp.concatenate([cr, cc], axis=-1)
    sin2 = jnp.concatenate([sr, sc], axis=-1)
    tile = lambda a: jnp.tile(a, (1, N_Q_HEADS))
    return tile(cos1), tile(sin1), tile(cos2), tile(sin2)


def kernel(x, mix_norm_g, ffn_norm_g, final_norm_g, even_w_in, even_w_out, conv_w, conv_b, conv_ln_g, conv_ln_b, sink_logits, odd_w_in, odd_w_out, q_norm_g, k_norm_g, sgu_ln_g, sgu_ln_b, sgu_w, sgu_b, peer_wq, peer_subkeys, peer_u, peer_v):
    batch, seq, d = x.shape
    depth = mix_norm_g.shape[0]
    t = batch * seq
    cos1, sin1, cos2, sin2 = _rope_tables(seq)
    row = lambda a: a.reshape(1, -1).astype(F32)
    xf = x.reshape(t, d)
    for layer in range(depth):
        i = layer // 2
        if layer % 2 == 0:
            a_in, q, k, v = _inproj_even(xf, row(mix_norm_g[layer]), even_w_in[i].astype(BF16), cos1, sin1, seq)
            m1 = _conv_module(a_in, conv_w[i, :, 0, :], row(conv_b[i]), row(conv_ln_g[i]), row(conv_ln_b[i]), batch, seq)
            sink_b = jnp.broadcast_to(sink_logits[i].astype(F32)[:, None], (N_Q_HEADS, LANE))
            m2 = _swa(q, k, v, sink_b, batch, seq)
            w_out = even_w_out[i]
        else:
            q, k, v, d_in = _inproj_odd(xf, row(mix_norm_g[layer]), odd_w_in[i].astype(BF16),
                                        row(jnp.tile(q_norm_g[i], N_Q_HEADS)), row(jnp.tile(k_norm_g[i], N_KV_HEADS)),
                                        cos2, sin2, seq)
            m1 = _dense_attn(q, k, v, batch, seq)
            bs = jnp.broadcast_to(sgu_b[i].astype(F32)[:, :, None], (SGU_GROUPS, CHUNK, MIX_WIDTH // SGU_GROUPS))
            m2 = _sgu(d_in, row(sgu_ln_g[i]), row(sgu_ln_b[i]), sgu_w[i].astype(BF16), bs)
            w_out = odd_w_out[i]
        xn, hnt = _outproj(xf, m1, m2, w_out.astype(BF16), row(ffn_norm_g[layer]))
        sk = peer_subkeys[layer].reshape(2 * PEER_HEADS, PEER_NKEYS, PEER_DKEY // 2).astype(BF16)
        xf = _peer(xn, hnt, peer_wq[layer].T.astype(BF16), sk, peer_u[layer].astype(BF16),
                   peer_v[layer].T.astype(BF16))
    return _final_norm(xf, row(final_norm_g)).reshape(batch, seq, d)
```

```python
import functools
import math

import jax
import jax.numpy as jnp
from jax import lax
from jax.experimental import pallas as pl
from jax.experimental.pallas import tpu as pltpu

F32 = jnp.float32
BF16 = jnp.bfloat16

D_MODEL = 1024
HEAD_DIM = 64
N_Q_HEADS = 8
N_KV_HEADS = 2
GQA_GROUPS = N_Q_HEADS // N_KV_HEADS
ATTN_WIDTH = N_Q_HEADS * HEAD_DIM
KV_WIDTH = N_KV_HEADS * HEAD_DIM
MIX_WIDTH = D_MODEL // 2
CONV_WIDTH = 31
CONV_HALO = 16
SGU_GROUPS = 4
CHUNK = 128
WINDOW = 128
BLOCK_Q = 128
GRID_W = 64
ROPE_THETA = 10000.0
PEER_HEADS = 8
PEER_NKEYS = 128
PEER_DKEY = 128
PEER_TOPK = 16
EPS = 1e-6
NEG = -1e30

VMEM_LIMIT_BYTES = 56 * 1024 * 1024

ROW_TILE = 512
CONV_ROWS = 64
SWA_BLOCKS = 4
ATT_TQ = 512
ATT_CHAINS = 4
PEER_TT = 512
PEER_A = 16
PEER_SUB = 8
PEER_GATE_BLOCK = 4
PEER_GATE_DTYPE = BF16
LANE = 128


def _cparams(sem):
    return pltpu.CompilerParams(dimension_semantics=sem, vmem_limit_bytes=VMEM_LIMIT_BYTES)


def _dot(a, b):
    return jnp.dot(a, b, preferred_element_type=F32)


def _dot_nt(a, b):
    return lax.dot_general(a, b, (((1,), (1,)), ((), ())), preferred_element_type=F32)


def _rms(x, g):
    return x * lax.rsqrt(jnp.mean(x * x, axis=-1, keepdims=True) + EPS) * g


def _layernorm(x, g, b):
    mu = jnp.mean(x, axis=-1, keepdims=True)
    xc = x - mu
    var = jnp.mean(xc * xc, axis=-1, keepdims=True)
    return xc * lax.rsqrt(var + EPS) * g + b


def _rope(x, cos, sin_signed, group, half):
    n = x.shape[-1]
    lane = lax.broadcasted_iota(jnp.int32, x.shape, x.ndim - 1)
    first = (lane % group) < half
    partner = jnp.where(first, pltpu.roll(x, n - half, x.ndim - 1), pltpu.roll(x, half, x.ndim - 1))
    return x * cos + partner * sin_signed


def _inproj_even_kernel(x_ref, g_ref, w_ref, cos_ref, sin_ref, a_ref, q_ref, k_ref, v_ref):
    h = _rms(x_ref[...], g_ref[...]).astype(BF16)
    o1 = 2 * MIX_WIDTH
    o2 = o1 + ATTN_WIDTH
    o3 = o2 + KV_WIDTH
    a_ref[...] = _dot(h, w_ref[:, :o1])
    cos = cos_ref[...]
    sin = sin_ref[...]
    q = _rope(_dot(h, w_ref[:, o1:o2]), cos, sin, HEAD_DIM, HEAD_DIM // 2)
    q_ref[...] = (q * (HEAD_DIM ** -0.5)).astype(BF16)
    k = _rope(_dot(h, w_ref[:, o2:o3]), cos[:, :KV_WIDTH], sin[:, :KV_WIDTH], HEAD_DIM, HEAD_DIM // 2)
    k_ref[...] = k.astype(BF16)
    v_ref[...] = _dot(h, w_ref[:, o3:]).astype(BF16)


def _inproj_even(x, g, w, cos, sin, seq):
    t = x.shape[0]
    tm = ROW_TILE
    nseq = seq // tm
    in_w = w.shape[1]
    row = lambda i: (i, 0)
    fixed = lambda i: (0, 0)
    pos = lambda i: (i % nseq, 0)
    return pl.pallas_call(
        _inproj_even_kernel,
        grid=(t // tm,),
        in_specs=[
            pl.BlockSpec((tm, D_MODEL), row),
            pl.BlockSpec((1, D_MODEL), fixed),
            pl.BlockSpec((D_MODEL, in_w), fixed),
            pl.BlockSpec((tm, ATTN_WIDTH), pos),
            pl.BlockSpec((tm, ATTN_WIDTH), pos),
        ],
        out_specs=[
            pl.BlockSpec((tm, 2 * MIX_WIDTH), row),
            pl.BlockSpec((tm, ATTN_WIDTH), row),
            pl.BlockSpec((tm, KV_WIDTH), row),
            pl.BlockSpec((tm, KV_WIDTH), row),
        ],
        out_shape=[
            jax.ShapeDtypeStruct((t, 2 * MIX_WIDTH), F32),
            jax.ShapeDtypeStruct((t, ATTN_WIDTH), BF16),
            jax.ShapeDtypeStruct((t, KV_WIDTH), BF16),
            jax.ShapeDtypeStruct((t, KV_WIDTH), BF16),
        ],
        compiler_params=_cparams(("parallel",)),
        name="inproj_even",
    )(x, g, w, cos, sin)


def _head_sumsq(x):
    n = x.shape[-1]
    r = lax.broadcasted_iota(jnp.int32, (n, n), 0) // HEAD_DIM
    c = lax.broadcasted_iota(jnp.int32, (n, n), 1) // HEAD_DIM
    ones = jnp.where(r == c, 1.0, 0.0).astype(BF16)
    sq = x * x
    hi = sq.astype(BF16)
    lo = (sq - hi.astype(F32)).astype(BF16)
    return _dot(hi, ones) + _dot(lo, ones)


def _inproj_odd_kernel(x_ref, g_ref, w_ref, qg_ref, kg_ref, cos_ref, sin_ref, q_ref, k_ref, v_ref, d_ref):
    h = _rms(x_ref[...], g_ref[...]).astype(BF16)
    c1 = ATTN_WIDTH
    c2 = c1 + KV_WIDTH
    c3 = c2 + KV_WIDTH
    cos = cos_ref[...]
    sin = sin_ref[...]
    half = HEAD_DIM // 2
    q = _dot(h, w_ref[:, :c1])
    q = q * lax.rsqrt(_head_sumsq(q) * (1.0 / HEAD_DIM) + EPS) * qg_ref[...]
    q = _rope(q, cos, sin, half, half // 2)
    q_ref[...] = (q * (HEAD_DIM ** -0.5)).astype(BF16)
    k = _dot(h, w_ref[:, c1:c2])
    k = k * lax.rsqrt(_head_sumsq(k) * (1.0 / HEAD_DIM) + EPS) * kg_ref[...]
    k = _rope(k, cos[:, :KV_WIDTH], sin[:, :KV_WIDTH], half, half // 2).astype(BF16)
    v = _dot(h, w_ref[:, c2:c3]).astype(BF16)
    for hh in range(N_KV_HEADS):
        k_ref[hh] = k[:, hh * HEAD_DIM:(hh + 1) * HEAD_DIM]
        v_ref[hh] = v[:, hh * HEAD_DIM:(hh + 1) * HEAD_DIM]
    d_ref[...] = _dot(h, w_ref[:, c3:])


def _inproj_odd(x, g, w, qg, kg, cos, sin, seq):
    t = x.shape[0]
    tm = ROW_TILE
    nseq = seq // tm
    in_w = w.shape[1]
    row = lambda i: (i, 0)
    fixed = lambda i: (0, 0)
    pos = lambda i: (i % nseq, 0)
    kv = lambda i: (0, i, 0)
    return pl.pallas_call(
        _inproj_odd_kernel,
        grid=(t // tm,),
        in_specs=[
            pl.BlockSpec((tm, D_MODEL), row),
            pl.BlockSpec((1, D_MODEL), fixed),
            pl.BlockSpec((D_MODEL, in_w), fixed),
            pl.BlockSpec((1, ATTN_WIDTH), fixed),
            pl.BlockSpec((1, KV_WIDTH), fixed),
            pl.BlockSpec((tm, ATTN_WIDTH), pos),
            pl.BlockSpec((tm, ATTN_WIDTH), pos),
        ],
        out_specs=[
            pl.BlockSpec((tm, ATTN_WIDTH), row),
            pl.BlockSpec((N_KV_HEADS, tm, HEAD_DIM), kv),
            pl.BlockSpec((N_KV_HEADS, tm, HEAD_DIM), kv),
            pl.BlockSpec((tm, 2 * MIX_WIDTH), row),
        ],
        out_shape=[
            jax.ShapeDtypeStruct((t, ATTN_WIDTH), BF16),
            jax.ShapeDtypeStruct((N_KV_HEADS, t, HEAD_DIM), BF16),
            jax.ShapeDtypeStruct((N_KV_HEADS, t, HEAD_DIM), BF16),
            jax.ShapeDtypeStruct((t, 2 * MIX_WIDTH), F32),
        ],
        compiler_params=_cparams(("parallel",)),
        name="inproj_odd",
    )(x, g, w, qg, kg, cos, sin)


def _glu(x):
    return x[:, :MIX_WIDTH] * jax.nn.sigmoid(x[:, MIX_WIDTH:])


def _conv_kernel(prev_ref, cur_ref, next_ref, w_ref, b_ref, g_ref, beta_ref, o_ref, buf_ref):
    i = pl.program_id(1)
    ns = pl.num_programs(1)
    ts = cur_ref.shape[0]
    halo = CONV_HALO
    buf_ref[0:halo, :] = jnp.where(i > 0, _glu(prev_ref[...]), 0.0)
    buf_ref[halo:halo + ts, :] = _glu(cur_ref[...])
    buf_ref[halo + ts:, :] = jnp.where(i < ns - 1, _glu(next_ref[...]), 0.0)
    w = w_ref[...]
    bias = b_ref[...]
    g = g_ref[...]
    beta = beta_ref[...]
    first = halo - CONV_WIDTH // 2
    for c in range(ts // CONV_ROWS):
        r0 = c * CONV_ROWS
        acc = jnp.zeros((CONV_ROWS, MIX_WIDTH), F32)
        for k in range(CONV_WIDTH):
            acc = acc + w[k:k + 1, :] * buf_ref[r0 + first + k:r0 + first + k + CONV_ROWS, :]
        y = _layernorm(acc + bias, g, beta)
        o_ref[r0:r0 + CONV_ROWS, :] = (y * jax.nn.sigmoid(y)).astype(o_ref.dtype)


def _conv_module(a_in, w, b, g, beta, batch, seq):
    t = a_in.shape[0]
    ts = ROW_TILE
    ns = seq // ts
    hb = ts // CONV_HALO
    last = t // CONV_HALO - 1
    fixed = lambda bb, i: (0, 0)
    return pl.pallas_call(
        _conv_kernel,
        grid=(batch, ns),
        in_specs=[
            pl.BlockSpec((CONV_HALO, 2 * MIX_WIDTH), lambda bb, i: (jnp.maximum((bb * ns + i) * hb - 1, 0), 0)),
            pl.BlockSpec((ts, 2 * MIX_WIDTH), lambda bb, i: (bb * ns + i, 0)),
            pl.BlockSpec((CONV_HALO, 2 * MIX_WIDTH), lambda bb, i: (jnp.minimum((bb * ns + i + 1) * hb, last), 0)),
            pl.BlockSpec((CONV_WIDTH, MIX_WIDTH), fixed),
            pl.BlockSpec((1, MIX_WIDTH), fixed),
            pl.BlockSpec((1, MIX_WIDTH), fixed),
            pl.BlockSpec((1, MIX_WIDTH), fixed),
        ],
        out_specs=pl.BlockSpec((ts, MIX_WIDTH), lambda bb, i: (bb * ns + i, 0)),
        out_shape=jax.ShapeDtypeStruct((t, MIX_WIDTH), BF16),
        scratch_shapes=[pltpu.VMEM((ts + 2 * CONV_HALO, MIX_WIDTH), F32)],
        compiler_params=_cparams(("parallel", "parallel")),
        name="conv_module",
    )(a_in, a_in, a_in, w, b, g, beta)


def _swa_kernel(q_ref, kp_ref, kc_ref, kn_ref, vp_ref, vc_ref, vn_ref, sink_ref, o_ref):
    n = pl.program_id(1)
    nsteps = pl.num_programs(1)
    bq = BLOCK_Q
    rows = GQA_GROUPS * bq
    ii = lax.broadcasted_iota(jnp.int32, (rows, 3 * bq), 0) % bq
    jj = lax.broadcasted_iota(jnp.int32, (rows, 3 * bq), 1)
    band = (jj >= ii) & (jj <= ii + 2 * WINDOW)
    kcat = jnp.concatenate([kp_ref[...], kc_ref[...], kn_ref[...]], axis=0)
    vcat = jnp.concatenate([vp_ref[...], vc_ref[...], vn_ref[...]], axis=0)
    for qb in range(SWA_BLOCKS):
        mask = band
        if qb == 0:
            mask = mask & (jj >= jnp.where(n > 0, 0, bq))
        if qb == SWA_BLOCKS - 1:
            mask = mask & (jj < jnp.where(n < nsteps - 1, 3 * bq, 2 * bq))
        q = q_ref[qb * bq:(qb + 1) * bq, :]
        outs = []
        for hk in range(N_KV_HEADS):
            heads = range(hk * GQA_GROUPS, (hk + 1) * GQA_GROUPS)
            kh = kcat[qb * bq:(qb + 3) * bq, hk * HEAD_DIM:(hk + 1) * HEAD_DIM]
            vh = vcat[qb * bq:(qb + 3) * bq, hk * HEAD_DIM:(hk + 1) * HEAD_DIM]
            qs = jnp.concatenate([q[:, hd * HEAD_DIM:(hd + 1) * HEAD_DIM] for hd in heads], axis=0)
            sink = jnp.concatenate([jnp.broadcast_to(sink_ref[hd:hd + 1, 0:1], (bq, 1)) for hd in heads], axis=0)
            s = jnp.where(mask, _dot_nt(qs, kh), NEG)
            m = jnp.maximum(jnp.max(s, axis=-1, keepdims=True), sink)
            p = jnp.exp(s - m)
            denom = jnp.sum(p, axis=-1, keepdims=True) + jnp.exp(sink - m)
            o = _dot(p.astype(BF16), vh) / denom
            outs.extend(o[g * bq:(g + 1) * bq, :] for g in range(GQA_GROUPS))
        o_ref[qb * bq:(qb + 1) * bq, :] = jnp.concatenate(outs, axis=-1).astype(o_ref.dtype)


def _swa(q, k, v, sink_b, batch, seq):
    t = q.shape[0]
    bq = BLOCK_Q
    tq = SWA_BLOCKS * bq
    ns = seq // tq
    last = t // bq - 1
    cur = lambda bb, n: (bb * ns + n, 0)
    prev = lambda bb, n: (jnp.maximum((bb * ns + n) * SWA_BLOCKS - 1, 0), 0)
    nxt = lambda bb, n: (jnp.minimum((bb * ns + n + 1) * SWA_BLOCKS, last), 0)
    halo = lambda im: pl.BlockSpec((bq, KV_WIDTH), im)
    body = pl.BlockSpec((tq, KV_WIDTH), cur)
    return pl.pallas_call(
        _swa_kernel,
        grid=(batch, ns),
        in_specs=[pl.BlockSpec((tq, ATTN_WIDTH), cur), halo(prev), body, halo(nxt), halo(prev), body, halo(nxt),
                  pl.BlockSpec((N_Q_HEADS, LANE), lambda bb, n: (0, 0))],
        out_specs=pl.BlockSpec((tq, ATTN_WIDTH), cur),
        out_shape=jax.ShapeDtypeStruct((t, ATTN_WIDTH), BF16),
        compiler_params=_cparams(("parallel", "parallel")),
        name="window_attn",
    )(q, k, k, k, v, v, v, sink_b)


def _dense_attn_kernel(q_ref, k_ref, v_ref, o_ref):
    ts = q_ref.shape[0] // ATT_CHAINS
    k = k_ref[...]
    v = v_ref[...]
    for c in range(ATT_CHAINS):
        q = q_ref[c * ts:(c + 1) * ts, :]
        qs = jnp.concatenate([q[:, g * HEAD_DIM:(g + 1) * HEAD_DIM] for g in range(GQA_GROUPS)], axis=0)
        s = _dot_nt(qs, k)
        p = jnp.exp(s - jnp.max(s, axis=-1, keepdims=True))
        o = _dot(p.astype(BF16), v) / jnp.sum(p, axis=-1, keepdims=True)
        o_ref[c * ts:(c + 1) * ts, :] = jnp.concatenate(
            [o[g * ts:(g + 1) * ts, :] for g in range(GQA_GROUPS)], axis=-1).astype(o_ref.dtype)


def _dense_attn(q, k, v, batch, seq):
    t = q.shape[0]
    tq = min(ATT_TQ, seq)
    nq = seq // tq
    gw = GQA_GROUPS * HEAD_DIM
    qmap = lambda bb, hk, qi: (bb * nq + qi, hk)
    kmap = lambda bb, hk, qi: (hk, bb, 0)
    return pl.pallas_call(
        _dense_attn_kernel,
        grid=(batch, N_KV_HEADS, nq),
        in_specs=[
            pl.BlockSpec((tq, gw), qmap),
            pl.BlockSpec((None, seq, HEAD_DIM), kmap),
            pl.BlockSpec((None, seq, HEAD_DIM), kmap),
        ],
        out_specs=pl.BlockSpec((tq, gw), qmap),
        out_shape=jax.ShapeDtypeStruct((t, ATTN_WIDTH), BF16),
        compiler_params=_cparams(("parallel", "parallel", "parallel")),
        name="dense_attn",
    )(q, k, v)


def _sgu_kernel(d_ref, g_ref, b_ref, w_ref, bs_ref, o_ref):
    z = jax.nn.gelu(d_ref[...])
    u = z[:, :MIX_WIDTH]
    v = _layernorm(z[:, MIX_WIDTH:], g_ref[...], b_ref[...]).astype(BF16)
    gc = MIX_WIDTH // SGU_GROUPS
    for n in range(d_ref.shape[0] // CHUNK):
        rows = slice(n * CHUNK, (n + 1) * CHUNK)
        for g in range(SGU_GROUPS):
            cols = slice(g * gc, (g + 1) * gc)
            mixed = _dot(w_ref[g], v[rows, cols]) + bs_ref[g]
            o_ref[rows, cols] = (u[rows, cols] * mixed).astype(o_ref.dtype)


def _sgu(d_in, g, b, w, bs):
    t = d_in.shape[0]
    ts = ROW_TILE
    gc = MIX_WIDTH // SGU_GROUPS
    row = lambda i: (i, 0)
    fixed = lambda i: (0, 0)
    fixed3 = lambda i: (0, 0, 0)
    return pl.pallas_call(
        _sgu_kernel,
        grid=(t // ts,),
        in_specs=[
            pl.BlockSpec((ts, 2 * MIX_WIDTH), row),
            pl.BlockSpec((1, MIX_WIDTH), fixed),
            pl.BlockSpec((1, MIX_WIDTH), fixed),
            pl.BlockSpec((SGU_GROUPS, CHUNK, CHUNK), fixed3),
            pl.BlockSpec((SGU_GROUPS, CHUNK, gc), fixed3),
        ],
        out_specs=pl.BlockSpec((ts, MIX_WIDTH), row),
        out_shape=jax.ShapeDtypeStruct((t, MIX_WIDTH), BF16),
        compiler_params=_cparams(("parallel",)),
        name="spatial_gating",
    )(d_in, g, b, w, bs)


def _outproj_kernel(x_ref, a_ref, b_ref, w_ref, g_ref, xn_ref, hnt_ref):
    y = _dot(a_ref[...], w_ref[:MIX_WIDTH, :]) + _dot(b_ref[...], w_ref[MIX_WIDTH:, :])
    xn = x_ref[...] + y
    xn_ref[...] = xn
    hnt_ref[...] = _rms(xn, g_ref[...]).T.astype(BF16)


def _outproj(x, a, b, w, g):
    t = x.shape[0]
    tm = ROW_TILE
    row = lambda i: (i, 0)
    fixed = lambda i: (0, 0)
    return pl.pallas_call(
        _outproj_kernel,
        grid=(t // tm,),
        in_specs=[
            pl.BlockSpec((tm, D_MODEL), row),
            pl.BlockSpec((tm, MIX_WIDTH), row),
            pl.BlockSpec((tm, MIX_WIDTH), row),
            pl.BlockSpec((2 * MIX_WIDTH, D_MODEL), fixed),
            pl.BlockSpec((1, D_MODEL), fixed),
        ],
        out_specs=[pl.BlockSpec((tm, D_MODEL), row), pl.BlockSpec((D_MODEL, tm), lambda i: (0, i))],
        out_shape=[jax.ShapeDtypeStruct((t, D_MODEL), F32), jax.ShapeDtypeStruct((D_MODEL, t), BF16)],
        compiler_params=_cparams(("parallel",)),
        name="outproj",
    )(x, a, b, w, g)


_PEER_PAIRS = [(i, j) for i in range(PEER_TOPK) for j in range(PEER_TOPK) if (i + 1) * (j + 1) <= PEER_TOPK]
_PEER_CAND_ROWS = -(-len(_PEER_PAIRS) // 8) * 8


SUBLANES = 8


def _bitonic_merge_desc(v, lo, n):
    step = n // 2
    while step >= 1:
        for i in range(lo, lo + n):
            if (i - lo) & step == 0:
                hi_, lo_ = jnp.maximum(v[i], v[i + step]), jnp.minimum(v[i], v[i + step])
                v[i], v[i + step] = hi_, lo_
        step //= 2


def _sort16_desc(v):
    n = len(v)
    size = 2
    while size <= n:
        for lo in range(0, n, size):
            half = size // 2
            v[lo + half:lo + size] = v[lo + half:lo + size][::-1]
            _bitonic_merge_desc(v, lo, size)
        size *= 2
    return v


def _sublane_all(x, op):
    for shift in (4, 2, 1):
        x = op(x, pltpu.roll(x, shift, 0))
    return x


def _top16_sorted(s):
    v = _sort16_desc([s[k * SUBLANES:(k + 1) * SUBLANES, :] for k in range(PEER_NKEYS // SUBLANES)])
    for shift in (4, 2, 1):
        v = [jnp.maximum(v[k], pltpu.roll(v[PEER_TOPK - 1 - k], shift, 0)) for k in range(PEER_TOPK)]
        _bitonic_merge_desc(v, 0, PEER_TOPK)
    return v


def _peer_select(h, s1_all, s2_all, n_ref, e1_ref, r2_ref, e2_ref):
    tt = s1_all.shape[-1]
    nv = PEER_NKEYS // SUBLANES
    sub = lax.broadcasted_iota(jnp.int32, (SUBLANES, LANE), 0)
    for tc in range(tt // LANE):
        lanes = slice(tc * LANE, (tc + 1) * LANE)
        s1 = s1_all[:, lanes]
        s2 = s2_all[:, lanes]
        v1 = _top16_sorted(s1)
        v2 = _top16_sorted(s2)
        pair_sum = {p: v1[p[0]] + v2[p[1]] for p in _PEER_PAIRS}
        cand = []
        for g in range(_PEER_CAND_ROWS // SUBLANES):
            c = jnp.full((SUBLANES, LANE), -jnp.inf, F32)
            for r, p in enumerate(_PEER_PAIRS[g * SUBLANES:(g + 1) * SUBLANES]):
                c = jnp.where(sub == r, pair_sum[p], c)
            cand.append(c)
        f = []
        for _ in range(PEER_TOPK):
            mx = cand[0]
            for c in cand[1:]:
                mx = jnp.maximum(mx, c)
            mx = _sublane_all(mx, jnp.maximum)
            f.append(mx)
            cand = [jnp.where(c == mx, -jnp.inf, c) for c in cand]
        thr = f[PEER_TOPK - 1]
        z = jnp.ones((SUBLANES, LANE), F32)
        for r in range(1, PEER_TOPK):
            z = z + jnp.exp(f[r] - f[0])
        rz = 1.0 / z
        cnt = []
        for i in range(PEER_TOPK):
            c = jnp.zeros((SUBLANES, LANE), F32)
            for j in range(PEER_TOPK // (i + 1)):
                c = c + jnp.where(pair_sum[(i, j)] >= thr, 1.0, 0.0)
            cnt.append(c)
        ranks = []
        e2s = []
        for k in range(nv):
            rows = slice(k * SUBLANES, (k + 1) * SUBLANES)
            a = s1[rows, :]
            b = s2[rows, :]
            n = jnp.zeros((SUBLANES, LANE), F32)
            rank = jnp.full((SUBLANES, LANE), float(PEER_TOPK), F32)
            for r in range(PEER_TOPK):
                n = jnp.where(a == v1[r], cnt[r], n)
                rank = jnp.where(b == v2[r], float(r), rank)
            n_ref[h, rows, lanes] = n
            e1_ref[h, rows, lanes] = jnp.exp(a - v1[0]) * rz
            ranks.append(rank)
            e2s.append(jnp.exp(b - v2[0]))
        r2_ref[h, :, lanes] = jnp.concatenate(ranks, axis=0).astype(r2_ref.dtype)
        e2_ref[h, :, lanes] = jnp.concatenate(e2s, axis=0).astype(e2_ref.dtype)


def _peer_gates(step, sub, n_ref, e1_ref, r2_ref, e2_ref, g_ref):
    tt = g_ref.shape[-1]
    gdt = g_ref.dtype
    a8 = pl.multiple_of(step * PEER_A + sub * PEER_SUB, SUBLANES)
    for tc in range(tt // LANE):
        lanes = slice(tc * LANE, (tc + 1) * LANE)
        nrows = [n_ref[h, pl.ds(a8, PEER_SUB), lanes].astype(gdt) for h in range(PEER_HEADS)]
        erows = [e1_ref[h, pl.ds(a8, PEER_SUB), lanes].astype(gdt) for h in range(PEER_HEADS)]
        for blk in range(PEER_SUB // PEER_GATE_BLOCK):
            gates = [jnp.zeros((PEER_NKEYS, LANE), gdt) for _ in range(PEER_GATE_BLOCK)]
            for h in range(PEER_HEADS):
                r2 = r2_ref[h, :, lanes]
                e2 = e2_ref[h, :, lanes]
                for k in range(PEER_GATE_BLOCK):
                    al = blk * PEER_GATE_BLOCK + k
                    keep = r2 < nrows[h][al:al + 1, :]
                    gates[k] = gates[k] + jnp.where(keep, e2, 0.0) * erows[h][al:al + 1, :]
            for k in range(PEER_GATE_BLOCK):
                al = blk * PEER_GATE_BLOCK + k
                g_ref[sub, al * PEER_NKEYS:(al + 1) * PEER_NKEYS, lanes] = gates[k]


def _peer_kernel(x_ref, hnt_ref, wqt_ref, sk_ref, u_ref, vt_ref, o_ref,
                 qt_ref, n_ref, e1_ref, r2_ref, e2_ref, g_ref, acc_ref, ht_ref):
    j = pl.program_id(1)
    nj = pl.num_programs(1)
    half = PEER_DKEY // 2
    gdt = r2_ref.dtype
    gate_refs = (n_ref, e1_ref, r2_ref, e2_ref, g_ref)

    @pl.when(j == 0)
    def _():
        qt_ref[...] = _dot(wqt_ref[...], hnt_ref[...]).astype(BF16)

        def body(h, carry):
            r0 = pl.multiple_of(h * PEER_DKEY, PEER_DKEY)
            s1 = _dot(sk_ref[2 * h], qt_ref[pl.ds(r0, half), :])
            s2 = _dot(sk_ref[2 * h + 1], qt_ref[pl.ds(r0 + half, half), :])
            _peer_select(h, s1, s2, n_ref, e1_ref, r2_ref, e2_ref)
            return carry

        lax.fori_loop(0, PEER_HEADS, body, 0)
        _peer_gates(0, 0, *gate_refs)
        acc_ref[...] = jnp.zeros(acc_ref.shape, F32)

    sub_rows = lambda sub: slice(sub * PEER_SUB * PEER_NKEYS, (sub + 1) * PEER_SUB * PEER_NKEYS)

    def act_matmul(sub):
        return _dot(u_ref[sub_rows(sub), :], hnt_ref[...]).astype(gdt)

    def value_matmul(sub):
        acc_ref[...] += _dot(vt_ref[:, sub_rows(sub)], ht_ref[sub_rows(sub), :])

    _peer_gates(j, 1, *gate_refs)
    at_a = act_matmul(0)
    at_b = act_matmul(1)
    ht_ref[sub_rows(0), :] = jax.nn.gelu(at_a) * g_ref[0]
    value_matmul(0)
    ht_ref[sub_rows(1), :] = jax.nn.gelu(at_b) * g_ref[1]
    _peer_gates((j + 1) % nj, 0, *gate_refs)
    value_matmul(1)

    @pl.when(j == nj - 1)
    def _():
        o_ref[...] = x_ref[...] + acc_ref[...].T


def _peer(x, hnt, wqt, sk, u, vt):
    t = x.shape[0]
    tt = PEER_TT
    ne = u.shape[0]
    et = PEER_A * PEER_NKEYS
    hk = PEER_HEADS * PEER_DKEY
    sel = lambda dt: pltpu.VMEM((PEER_HEADS, PEER_NKEYS, tt), dt)
    return pl.pallas_call(
        _peer_kernel,
        grid=(t // tt, ne // et),
        in_specs=[
            pl.BlockSpec((tt, D_MODEL), lambda i, j: (i, 0)),
            pl.BlockSpec((D_MODEL, tt), lambda i, j: (0, i)),
            pl.BlockSpec((hk, D_MODEL), lambda i, j: (0, 0), pipeline_mode=pl.Buffered(1)),
            pl.BlockSpec((2 * PEER_HEADS, PEER_NKEYS, PEER_DKEY // 2), lambda i, j: (0, 0, 0)),
            pl.BlockSpec((et, D_MODEL), lambda i, j: (j, 0)),
            pl.BlockSpec((D_MODEL, et), lambda i, j: (0, j)),
        ],
        out_specs=pl.BlockSpec((tt, D_MODEL), lambda i, j: (i, 0)),
        out_shape=jax.ShapeDtypeStruct((t, D_MODEL), F32),
        scratch_shapes=[
            pltpu.VMEM((hk, tt), BF16),
            sel(F32), sel(F32), sel(PEER_GATE_DTYPE), sel(PEER_GATE_DTYPE),
            pltpu.VMEM((PEER_A // PEER_SUB, PEER_SUB * PEER_NKEYS, tt), PEER_GATE_DTYPE),
            pltpu.VMEM((D_MODEL, tt), F32),
            pltpu.VMEM((et, tt), BF16),
        ],
        compiler_params=_cparams(("parallel", "arbitrary")),
        name="peer",
    )(x, hnt, wqt, sk, u, vt)


def _final_norm_kernel(x_ref, g_ref, o_ref):
    o_ref[...] = _rms(x_ref[...], g_ref[...])


def _final_norm(x, g):
    t = x.shape[0]
    tm = ROW_TILE
    return pl.pallas_call(
        _final_norm_kernel,
        grid=(t // tm,),
        in_specs=[pl.BlockSpec((tm, D_MODEL), lambda i: (i, 0)), pl.BlockSpec((1, D_MODEL), lambda i: (0, 0))],
        out_specs=pl.BlockSpec((tm, D_MODEL), lambda i: (i, 0)),
        out_shape=jax.ShapeDtypeStruct((t, D_MODEL), F32),
        compiler_params=_cparams(("parallel",)),
        name="final_norm",
    )(x, g)


def _rope_table(pos, dim):
    inv = ROPE_THETA ** (-jnp.arange(0, dim, 2, dtype=F32) / dim)
    ang = pos.astype(F32)[:, None] * inv[None, :]
    ang = jnp.concatenate([ang, ang], axis=-1)
    sign = jnp.concatenate([-jnp.ones((dim // 2,), F32), jnp.ones((dim // 2,), F32)])
    return jnp.cos(ang), jnp.sin(ang) * sign


def _rope_tables(seq):
    pos = jnp.arange(seq)
    cos1, sin1 = _rope_table(pos, HEAD_DIM)
    cr, sr = _rope_table(pos // GRID_W, HEAD_DIM // 2)
    cc, sc = _rope_table(pos % GRID_W, HEAD_DIM // 2)
    cos2 = jnp.concatenate([cr, cc], axis=-1)
    sin2 = jnp.concatenate([sr, sc], axis=-1)
    tile = lambda a: jnp.tile(a, (1, N_Q_HEADS))
    return tile(cos1), tile(sin1), tile(cos2), tile(sin2)


def kernel(x, mix_norm_g, ffn_norm_g, final_norm_g, even_w_in, even_w_out, conv_w, conv_b, conv_ln_g, conv_ln_b, sink_logits, odd_w_in, odd_w_out, q_norm_g, k_norm_g, sgu_ln_g, sgu_ln_b, sgu_w, sgu_b, peer_wq, peer_subkeys, peer_u, peer_v):
    batch, seq, d = x.shape
    depth = mix_norm_g.shape[0]
    t = batch * seq
    cos1, sin1, cos2, sin2 = _rope_tables(seq)
    row = lambda a: a.reshape(1, -1).astype(F32)
    xf = x.reshape(t, d)
    for layer in range(depth):
        i = layer // 2
        if layer % 2 == 0:
            a_in, q, k, v = _inproj_even(xf, row(mix_norm_g[layer]), even_w_in[i].astype(BF16), cos1, sin1, seq)
            m1 = _conv_module(a_in, conv_w[i, :, 0, :], row(conv_b[i]), row(conv_ln_g[i]), row(conv_ln_b[i]), batch, seq)
            sink_b = jnp.broadcast_to(sink_logits[i].astype(F32)[:, None], (N_Q_HEADS, LANE))
            m2 = _swa(q, k, v, sink_b, batch, seq)
            w_out = even_w_out[i]
        else:
            q, k, v, d_in = _inproj_odd(xf, row(mix_norm_g[layer]), odd_w_in[i].astype(BF16),
                                        row(jnp.tile(q_norm_g[i], N_Q_HEADS)), row(jnp.tile(k_norm_g[i], N_KV_HEADS)),
                                        cos2, sin2, seq)
            m1 = _dense_attn(q, k, v, batch, seq)
            bs = jnp.broadcast_to(sgu_b[i].astype(F32)[:, :, None], (SGU_GROUPS, CHUNK, MIX_WIDTH // SGU_GROUPS))
            m2 = _sgu(d_in, row(sgu_ln_g[i]), row(sgu_ln_b[i]), sgu_w[i].astype(BF16), bs)
            w_out = odd_w_out[i]
        xn, hnt = _outproj(xf, m1, m2, w_out.astype(BF16), row(ffn_norm_g[layer]))
        sk = peer_subkeys[layer].reshape(2 * PEER_HEADS, PEER_NKEYS, PEER_DKEY // 2).astype(BF16)
        xf = _peer(xn, hnt, peer_wq[layer].T.astype(BF16), sk, peer_u[layer].astype(BF16),
                   peer_v[layer].T.astype(BF16))
    return _final_norm(xf, row(final_norm_g)).reshape(batch, seq, d)
```

```python
import functools
import math

import jax
import jax.numpy as jnp
from jax import lax
from jax.experimental import pallas as pl
from jax.experimental.pallas import tpu as pltpu

F32 = jnp.float32
BF16 = jnp.bfloat16

D_MODEL = 1024
HEAD_DIM = 64
N_Q_HEADS = 8
N_KV_HEADS = 2
GQA_GROUPS = N_Q_HEADS // N_KV_HEADS
ATTN_WIDTH = N_Q_HEADS * HEAD_DIM
KV_WIDTH = N_KV_HEADS * HEAD_DIM
MIX_WIDTH = D_MODEL // 2
CONV_WIDTH = 31
CONV_HALO = 16
SGU_GROUPS = 4
CHUNK = 128
WINDOW = 128
BLOCK_Q = 128
GRID_W = 64
ROPE_THETA = 10000.0
PEER_HEADS = 8
PEER_NKEYS = 128
PEER_DKEY = 128
PEER_TOPK = 16
EPS = 1e-6
NEG = -1e30

VMEM_LIMIT_BYTES = 56 * 1024 * 1024

ROW_TILE = 512
CONV_ROWS = 64
SWA_BLOCKS = 4
ATT_TQ = 512
ATT_CHAINS = 4
PEER_TT = 512
PEER_A = 16
PEER_SUB = 8
PEER_GATE_BLOCK = 4
PEER_GATE_DTYPE = BF16
LANE = 128


def _cparams(sem):
    return pltpu.CompilerParams(dimension_semantics=sem, vmem_limit_bytes=VMEM_LIMIT_BYTES)


def _dot(a, b):
    return jnp.dot(a, b, preferred_element_type=F32)


def _dot_nt(a, b):
    return lax.dot_general(a, b, (((1,), (1,)), ((), ())), preferred_element_type=F32)


def _rms(x, g):
    return x * lax.rsqrt(jnp.mean(x * x, axis=-1, keepdims=True) + EPS) * g


def _layernorm(x, g, b):
    mu = jnp.mean(x, axis=-1, keepdims=True)
    xc = x - mu
    var = jnp.mean(xc * xc, axis=-1, keepdims=True)
    return xc * lax.rsqrt(var + EPS) * g + b


def _rope(x, cos, sin_signed, group, half):
    n = x.shape[-1]
    lane = lax.broadcasted_iota(jnp.int32, x.shape, x.ndim - 1)
    first = (lane % group) < half
    partner = jnp.where(first, pltpu.roll(x, n - half, x.ndim - 1), pltpu.roll(x, half, x.ndim - 1))
    return x * cos + partner * sin_signed


def _inproj_even_kernel(x_ref, g_ref, w_ref, cos_ref, sin_ref, a_ref, q_ref, k_ref, v_ref):
    h = _rms(x_ref[...], g_ref[...]).astype(BF16)
    o1 = 2 * MIX_WIDTH
    o2 = o1 + ATTN_WIDTH
    o3 = o2 + KV_WIDTH
    a_ref[...] = _dot(h, w_ref[:, :o1])
    cos = cos_ref[...]
    sin = sin_ref[...]
    q = _rope(_dot(h, w_ref[:, o1:o2]), cos, sin, HEAD_DIM, HEAD_DIM // 2)
    q_ref[...] = (q * (HEAD_DIM ** -0.5)).astype(BF16)
    k = _rope(_dot(h, w_ref[:, o2:o3]), cos[:, :KV_WIDTH], sin[:, :KV_WIDTH], HEAD_DIM, HEAD_DIM // 2)
    k_ref[...] = k.astype(BF16)
    v_ref[...] = _dot(h, w_ref[:, o3:]).astype(BF16)


def _inproj_even(x, g, w, cos, sin, seq):
    t = x.shape[0]
    tm = ROW_TILE
    nseq = seq // tm
    in_w = w.shape[1]
    row = lambda i: (i, 0)
    fixed = lambda i: (0, 0)
    pos = lambda i: (i % nseq, 0)
    return pl.pallas_call(
        _inproj_even_kernel,
        grid=(t // tm,),
        in_specs=[
            pl.BlockSpec((tm, D_MODEL), row),
            pl.BlockSpec((1, D_MODEL), fixed),
            pl.BlockSpec((D_MODEL, in_w), fixed),
            pl.BlockSpec((tm, ATTN_WIDTH), pos),
            pl.BlockSpec((tm, ATTN_WIDTH), pos),
        ],
        out_specs=[
            pl.BlockSpec((tm, 2 * MIX_WIDTH), row),
            pl.BlockSpec((tm, ATTN_WIDTH), row),
            pl.BlockSpec((tm, KV_WIDTH), row),
            pl.BlockSpec((tm, KV_WIDTH), row),
        ],
        out_shape=[
            jax.ShapeDtypeStruct((t, 2 * MIX_WIDTH), F32),
            jax.ShapeDtypeStruct((t, ATTN_WIDTH), BF16),
            jax.ShapeDtypeStruct((t, KV_WIDTH), BF16),
            jax.ShapeDtypeStruct((t, KV_WIDTH), BF16),
        ],
        compiler_params=_cparams(("parallel",)),
        name="inproj_even",
    )(x, g, w, cos, sin)


def _head_sumsq(x):
    n = x.shape[-1]
    r = lax.broadcasted_iota(jnp.int32, (n, n), 0) // HEAD_DIM
    c = lax.broadcasted_iota(jnp.int32, (n, n), 1) // HEAD_DIM
    ones = jnp.where(r == c, 1.0, 0.0).astype(BF16)
    sq = x * x
    hi = sq.astype(BF16)
    lo = (sq - hi.astype(F32)).astype(BF16)
    return _dot(hi, ones) + _dot(lo, ones)


def _inproj_odd_kernel(x_ref, g_ref, w_ref, qg_ref, kg_ref, cos_ref, sin_ref, q_ref, k_ref, v_ref, d_ref):
    h = _rms(x_ref[...], g_ref[...]).astype(BF16)
    c1 = ATTN_WIDTH
    c2 = c1 + KV_WIDTH
    c3 = c2 + KV_WIDTH
    cos = cos_ref[...]
    sin = sin_ref[...]
    half = HEAD_DIM // 2
    q = _dot(h, w_ref[:, :c1])
    q = q * lax.rsqrt(_head_sumsq(q) * (1.0 / HEAD_DIM) + EPS) * qg_ref[...]
    q = _rope(q, cos, sin, half, half // 2)
    q_ref[...] = (q * (HEAD_DIM ** -0.5)).astype(BF16)
    k = _dot(h, w_ref[:, c1:c2])
    k = k * lax.rsqrt(_head_sumsq(k) * (1.0 / HEAD_DIM) + EPS) * kg_ref[...]
    k = _rope(k, cos[:, :KV_WIDTH], sin[:, :KV_WIDTH], half, half // 2).astype(BF16)
    v = _dot(h, w_ref[:, c2:c3]).astype(BF16)
    for hh in range(N_KV_HEADS):
        k_ref[hh] = k[:, hh * HEAD_DIM:(hh + 1) * HEAD_DIM]
        v_ref[hh] = v[:, hh * HEAD_DIM:(hh + 1) * HEAD_DIM]
    d_ref[...] = _dot(h, w_ref[:, c3:])


def _inproj_odd(x, g, w, qg, kg, cos, sin, seq):
    t = x.shape[0]
    tm = ROW_TILE
    nseq = seq // tm
    in_w = w.shape[1]
    row = lambda i: (i, 0)
    fixed = lambda i: (0, 0)
    pos = lambda i: (i % nseq, 0)
    kv = lambda i: (0, i, 0)
    return pl.pallas_call(
        _inproj_odd_kernel,
        grid=(t // tm,),
        in_specs=[
            pl.BlockSpec((tm, D_MODEL), row),
            pl.BlockSpec((1, D_MODEL), fixed),
            pl.BlockSpec((D_MODEL, in_w), fixed),
            pl.BlockSpec((1, ATTN_WIDTH), fixed),
            pl.BlockSpec((1, KV_WIDTH), fixed),
            pl.BlockSpec((tm, ATTN_WIDTH), pos),
            pl.BlockSpec((tm, ATTN_WIDTH), pos),
        ],
        out_specs=[
            pl.BlockSpec((tm, ATTN_WIDTH), row),
            pl.BlockSpec((N_KV_HEADS, tm, HEAD_DIM), kv),
            pl.BlockSpec((N_KV_HEADS, tm, HEAD_DIM), kv),
            pl.BlockSpec((tm, 2 * MIX_WIDTH), row),
        ],
        out_shape=[
            jax.ShapeDtypeStruct((t, ATTN_WIDTH), BF16),
            jax.ShapeDtypeStruct((N_KV_HEADS, t, HEAD_DIM), BF16),
            jax.ShapeDtypeStruct((N_KV_HEADS, t, HEAD_DIM), BF16),
            jax.ShapeDtypeStruct((t, 2 * MIX_WIDTH), F32),
        ],
        compiler_params=_cparams(("parallel",)),
        name="inproj_odd",
    )(x, g, w, qg, kg, cos, sin)


def _glu(x):
    return x[:, :MIX_WIDTH] * jax.nn.sigmoid(x[:, MIX_WIDTH:])


def _conv_kernel(prev_ref, cur_ref, next_ref, w_ref, b_ref, g_ref, beta_ref, o_ref, buf_ref):
    i = pl.program_id(1)
    ns = pl.num_programs(1)
    ts = cur_ref.shape[0]
    halo = CONV_HALO
    buf_ref[0:halo, :] = jnp.where(i > 0, _glu(prev_ref[...]), 0.0)
    buf_ref[halo:halo + ts, :] = _glu(cur_ref[...])
    buf_ref[halo + ts:, :] = jnp.where(i < ns - 1, _glu(next_ref[...]), 0.0)
    w = w_ref[...]
    bias = b_ref[...]
    g = g_ref[...]
    beta = beta_ref[...]
    first = halo - CONV_WIDTH // 2
    for c in range(ts // CONV_ROWS):
        r0 = c * CONV_ROWS
        acc = jnp.zeros((CONV_ROWS, MIX_WIDTH), F32)
        for k in range(CONV_WIDTH):
            acc = acc + w[k:k + 1, :] * buf_ref[r0 + first + k:r0 + first + k + CONV_ROWS, :]
        y = _layernorm(acc + bias, g, beta)
        o_ref[r0:r0 + CONV_ROWS, :] = (y * jax.nn.sigmoid(y)).astype(o_ref.dtype)


def _conv_module(a_in, w, b, g, beta, batch, seq):
    t = a_in.shape[0]
    ts = ROW_TILE
    ns = seq // ts
    hb = ts // CONV_HALO
    last = t // CONV_HALO - 1
    fixed = lambda bb, i: (0, 0)
    return pl.pallas_call(
        _conv_kernel,
        grid=(batch, ns),
        in_specs=[
            pl.BlockSpec((CONV_HALO, 2 * MIX_WIDTH), lambda bb, i: (jnp.maximum((bb * ns + i) * hb - 1, 0), 0)),
            pl.BlockSpec((ts, 2 * MIX_WIDTH), lambda bb, i: (bb * ns + i, 0)),
            pl.BlockSpec((CONV_HALO, 2 * MIX_WIDTH), lambda bb, i: (jnp.minimum((bb * ns + i + 1) * hb, last), 0)),
            pl.BlockSpec((CONV_WIDTH, MIX_WIDTH), fixed),
            pl.BlockSpec((1, MIX_WIDTH), fixed),
            pl.BlockSpec((1, MIX_WIDTH), fixed),
            pl.BlockSpec((1, MIX_WIDTH), fixed),
        ],
        out_specs=pl.BlockSpec((ts, MIX_WIDTH), lambda bb, i: (bb * ns + i, 0)),
        out_shape=jax.ShapeDtypeStruct((t, MIX_WIDTH), BF16),
        scratch_shapes=[pltpu.VMEM((ts + 2 * CONV_HALO, MIX_WIDTH), F32)],
        compiler_params=_cparams(("parallel", "parallel")),
        name="conv_module",
    )(a_in, a_in, a_in, w, b, g, beta)


def _swa_kernel(q_ref, kp_ref, kc_ref, kn_ref, vp_ref, vc_ref, vn_ref, sink_ref, o_ref):
    n = pl.program_id(1)
    nsteps = pl.num_programs(1)
    bq = BLOCK_Q
    rows = GQA_GROUPS * bq
    ii = lax.broadcasted_iota(jnp.int32, (rows, 3 * bq), 0) % bq
    jj = lax.broadcasted_iota(jnp.int32, (rows, 3 * bq), 1)
    band = (jj >= ii) & (jj <= ii + 2 * WINDOW)
    kcat = jnp.concatenate([kp_ref[...], kc_ref[...], kn_ref[...]], axis=0)
    vcat = jnp.concatenate([vp_ref[...], vc_ref[...], vn_ref[...]], axis=0)
    for qb in range(SWA_BLOCKS):
        mask = band
        if qb == 0:
            mask = mask & (jj >= jnp.where(n > 0, 0, bq))
        if qb == SWA_BLOCKS - 1:
            mask = mask & (jj < jnp.where(n < nsteps - 1, 3 * bq, 2 * bq))
        q = q_ref[qb * bq:(qb + 1) * bq, :]
        outs = []
        for hk in range(N_KV_HEADS):
            heads = range(hk * GQA_GROUPS, (hk + 1) * GQA_GROUPS)
            kh = kcat[qb * bq:(qb + 3) * bq, hk * HEAD_DIM:(hk + 1) * HEAD_DIM]
            vh = vcat[qb * bq:(qb + 3) * bq, hk * HEAD_DIM:(hk + 1) * HEAD_DIM]
            qs = jnp.concatenate([q[:, hd * HEAD_DIM:(hd + 1) * HEAD_DIM] for hd in heads], axis=0)
            sink = jnp.concatenate([jnp.broadcast_to(sink_ref[hd:hd + 1, 0:1], (bq, 1)) for hd in heads], axis=0)
            s = jnp.where(mask, _dot_nt(qs, kh), NEG)
            m = jnp.maximum(jnp.max(s, axis=-1, keepdims=True), sink)
            p = jnp.exp(s - m)
            denom = jnp.sum(p, axis=-1, keepdims=True) + jnp.exp(sink - m)
            o = _dot(p.astype(BF16), vh) / denom
            outs.extend(o[g * bq:(g + 1) * bq, :] for g in range(GQA_GROUPS))
        o_ref[qb * bq:(qb + 1) * bq, :] = jnp.concatenate(outs, axis=-1).astype(o_ref.dtype)


def _swa(q, k, v, sink_b, batch, seq):
    t = q.shape[0]
    bq = BLOCK_Q
    tq = SWA_BLOCKS * bq
    ns = seq // tq
    last = t // bq - 1
    cur = lambda bb, n: (bb * ns + n, 0)
    prev = lambda bb, n: (jnp.maximum((bb * ns + n) * SWA_BLOCKS - 1, 0), 0)
    nxt = lambda bb, n: (jnp.minimum((bb * ns + n + 1) * SWA_BLOCKS, last), 0)
    halo = lambda im: pl.BlockSpec((bq, KV_WIDTH), im)
    body = pl.BlockSpec((tq, KV_WIDTH), cur)
    return pl.pallas_call(
        _swa_kernel,
        grid=(batch, ns),
        in_specs=[pl.BlockSpec((tq, ATTN_WIDTH), cur), halo(prev), body, halo(nxt), halo(prev), body, halo(nxt),
                  pl.BlockSpec((N_Q_HEADS, LANE), lambda bb, n: (0, 0))],
        out_specs=pl.BlockSpec((tq, ATTN_WIDTH), cur),
        out_shape=jax.ShapeDtypeStruct((t, ATTN_WIDTH), BF16),
        compiler_params=_cparams(("parallel", "parallel")),
        name="window_attn",
    )(q, k, k, k, v, v, v, sink_b)


def _dense_attn_kernel(q_ref, k_ref, v_ref, o_ref):
    ts = q_ref.shape[0] // ATT_CHAINS
    k = k_ref[...]
    v = v_ref[...]
    for c in range(ATT_CHAINS):
        q = q_ref[c * ts:(c + 1) * ts, :]
        qs = jnp.concatenate([q[:, g * HEAD_DIM:(g + 1) * HEAD_DIM] for g in range(GQA_GROUPS)], axis=0)
        s = _dot_nt(qs, k)
        p = jnp.exp(s - jnp.max(s, axis=-1, keepdims=True))
        o = _dot(p.astype(BF16), v) / jnp.sum(p, axis=-1, keepdims=True)
        o_ref[c * ts:(c + 1) * ts, :] = jnp.concatenate(
            [o[g * ts:(g + 1) * ts, :] for g in range(GQA_GROUPS)], axis=-1).astype(o_ref.dtype)


def _dense_attn(q, k, v, batch, seq):
    t = q.shape[0]
    tq = min(ATT_TQ, seq)
    nq = seq // tq
    gw = GQA_GROUPS * HEAD_DIM
    qmap = lambda bb, hk, qi: (bb * nq + qi, hk)
    kmap = lambda bb, hk, qi: (hk, bb, 0)
    return pl.pallas_call(
        _dense_attn_kernel,
        grid=(batch, N_KV_HEADS, nq),
        in_specs=[
            pl.BlockSpec((tq, gw), qmap),
            pl.BlockSpec((None, seq, HEAD_DIM), kmap),
            pl.BlockSpec((None, seq, HEAD_DIM), kmap),
        ],
        out_specs=pl.BlockSpec((tq, gw), qmap),
        out_shape=jax.ShapeDtypeStruct((t, ATTN_WIDTH), BF16),
        compiler_params=_cparams(("parallel", "parallel", "parallel")),
        name="dense_attn",
    )(q, k, v)


def _sgu_kernel(d_ref, g_ref, b_ref, w_ref, bs_ref, o_ref):
    z = jax.nn.gelu(d_ref[...])
    u = z[:, :MIX_WIDTH]
    v = _layernorm(z[:, MIX_WIDTH:], g_ref[...], b_ref[...]).astype(BF16)
    gc = MIX_WIDTH // SGU_GROUPS
    for n in range(d_ref.shape[0] // CHUNK):
        rows = slice(n * CHUNK, (n + 1) * CHUNK)
        for g in range(SGU_GROUPS):
            cols = slice(g * gc, (g + 1) * gc)
            mixed = _dot(w_ref[g], v[rows, cols]) + bs_ref[g]
            o_ref[rows, cols] = (u[rows, cols] * mixed).astype(o_ref.dtype)


def _sgu(d_in, g, b, w, bs):
    t = d_in.shape[0]
    ts = ROW_TILE
    gc = MIX_WIDTH // SGU_GROUPS
    row = lambda i: (i, 0)
    fixed = lambda i: (0, 0)
    fixed3 = lambda i: (0, 0, 0)
    return pl.pallas_call(
        _sgu_kernel,
        grid=(t // ts,),
        in_specs=[
            pl.BlockSpec((ts, 2 * MIX_WIDTH), row),
            pl.BlockSpec((1, MIX_WIDTH), fixed),
            pl.BlockSpec((1, MIX_WIDTH), fixed),
            pl.BlockSpec((SGU_GROUPS, CHUNK, CHUNK), fixed3),
            pl.BlockSpec((SGU_GROUPS, CHUNK, gc), fixed3),
        ],
        out_specs=pl.BlockSpec((ts, MIX_WIDTH), row),
        out_shape=jax.ShapeDtypeStruct((t, MIX_WIDTH), BF16),
        compiler_params=_cparams(("parallel",)),
        name="spatial_gating",
    )(d_in, g, b, w, bs)


def _outproj_kernel(x_ref, a_ref, b_ref, w_ref, g_ref, xn_ref, hnt_ref):
    y = _dot(a_ref[...], w_ref[:MIX_WIDTH, :]) + _dot(b_ref[...], w_ref[MIX_WIDTH:, :])
    xn = x_ref[...] + y
    xn_ref[...] = xn
    hnt_ref[...] = _rms(xn, g_ref[...]).T.astype(BF16)


def _outproj(x, a, b, w, g):
    t = x.shape[0]
    tm = ROW_TILE
    row = lambda i: (i, 0)
    fixed = lambda i: (0, 0)
    return pl.pallas_call(
        _outproj_kernel,
        grid=(t // tm,),
        in_specs=[
            pl.BlockSpec((tm, D_MODEL), row),
            pl.BlockSpec((tm, MIX_WIDTH), row),
            pl.BlockSpec((tm, MIX_WIDTH), row),
            pl.BlockSpec((2 * MIX_WIDTH, D_MODEL), fixed),
            pl.BlockSpec((1, D_MODEL), fixed),
        ],
        out_specs=[pl.BlockSpec((tm, D_MODEL), row), pl.BlockSpec((D_MODEL, tm), lambda i: (0, i))],
        out_shape=[jax.ShapeDtypeStruct((t, D_MODEL), F32), jax.ShapeDtypeStruct((D_MODEL, t), BF16)],
        compiler_params=_cparams(("parallel",)),
        name="outproj",
    )(x, a, b, w, g)


_PEER_PAIRS = [(i, j) for i in range(PEER_TOPK) for j in range(PEER_TOPK) if (i + 1) * (j + 1) <= PEER_TOPK]
_PEER_CAND_ROWS = -(-len(_PEER_PAIRS) // 8) * 8


SUBLANES = 8


def _bitonic_merge_desc(v, lo, n):
    step = n // 2
    while step >= 1:
        for i in range(lo, lo + n):
            if (i - lo) & step == 0:
                hi_, lo_ = jnp.maximum(v[i], v[i + step]), jnp.minimum(v[i], v[i + step])
                v[i], v[i + step] = hi_, lo_
        step //= 2


def _sort16_desc(v):
    n = len(v)
    size = 2
    while size <= n:
        for lo in range(0, n, size):
            half = size // 2
            v[lo + half:lo + size] = v[lo + half:lo + size][::-1]
            _bitonic_merge_desc(v, lo, size)
        size *= 2
    return v


def _sublane_all(x, op):
    for shift in (4, 2, 1):
        x = op(x, pltpu.roll(x, shift, 0))
    return x


def _top16_sorted(s):
    v = _sort16_desc([s[k * SUBLANES:(k + 1) * SUBLANES, :] for k in range(PEER_NKEYS // SUBLANES)])
    for shift in (4, 2, 1):
        v = [jnp.maximum(v[k], pltpu.roll(v[PEER_TOPK - 1 - k], shift, 0)) for k in range(PEER_TOPK)]
        _bitonic_merge_desc(v, 0, PEER_TOPK)
    return v


def _peer_select(h, s1_all, s2_all, n_ref, e1_ref, r2_ref, e2_ref):
    tt = s1_all.shape[-1]
    nv = PEER_NKEYS // SUBLANES
    sub = lax.broadcasted_iota(jnp.int32, (SUBLANES, LANE), 0)
    for tc in range(tt // LANE):
        lanes = slice(tc * LANE, (tc + 1) * LANE)
        s1 = s1_all[:, lanes]
        s2 = s2_all[:, lanes]
        v1 = _top16_sorted(s1)
        v2 = _top16_sorted(s2)
        pair_sum = {p: v1[p[0]] + v2[p[1]] for p in _PEER_PAIRS}
        cand = []
        for g in range(_PEER_CAND_ROWS // SUBLANES):
            c = jnp.full((SUBLANES, LANE), -jnp.inf, F32)
            for r, p in enumerate(_PEER_PAIRS[g * SUBLANES:(g + 1) * SUBLANES]):
                c = jnp.where(sub == r, pair_sum[p], c)
            cand.append(c)
        f = []
        for _ in range(PEER_TOPK):
            mx = cand[0]
            for c in cand[1:]:
                mx = jnp.maximum(mx, c)
            mx = _sublane_all(mx, jnp.maximum)
            f.append(mx)
            cand = [jnp.where(c == mx, -jnp.inf, c) for c in cand]
        thr = f[PEER_TOPK - 1]
        z = jnp.ones((SUBLANES, LANE), F32)
        for r in range(1, PEER_TOPK):
            z = z + jnp.exp(f[r] - f[0])
        rz = 1.0 / z
        cnt = []
        for i in range(PEER_TOPK):
            c = jnp.zeros((SUBLANES, LANE), F32)
            for j in range(PEER_TOPK // (i + 1)):
                c = c + jnp.where(pair_sum[(i, j)] >= thr, 1.0, 0.0)
            cnt.append(c)
        ranks = []
        e2s = []
        for k in range(nv):
            rows = slice(k * SUBLANES, (k + 1) * SUBLANES)
            a = s1[rows, :]
            b = s2[rows, :]
            n = jnp.zeros((SUBLANES, LANE), F32)
            rank = jnp.full((SUBLANES, LANE), float(PEER_TOPK), F32)
            for r in range(PEER_TOPK):
                n = jnp.where(a == v1[r], cnt[r], n)
                rank = jnp.where(b == v2[r], float(r), rank)
            n_ref[h, rows, lanes] = n
            e1_ref[h, rows, lanes] = jnp.exp(a - v1[0]) * rz
            ranks.append(rank)
            e2s.append(jnp.exp(b - v2[0]))
        r2_ref[h, :, lanes] = jnp.concatenate(ranks, axis=0).astype(r2_ref.dtype)
        e2_ref[h, :, lanes] = jnp.concatenate(e2s, axis=0).astype(e2_ref.dtype)


def _peer_gates(step, sub, n_ref, e1_ref, r2_ref, e2_ref, g_ref):
    tt = g_ref.shape[-1]
    gdt = g_ref.dtype
    a8 = pl.multiple_of(step * PEER_A + sub * PEER_SUB, SUBLANES)
    for tc in range(tt // LANE):
        lanes = slice(tc * LANE, (tc + 1) * LANE)
        nrows = [n_ref[h, pl.ds(a8, PEER_SUB), lanes].astype(gdt) for h in range(PEER_HEADS)]
        erows = [e1_ref[h, pl.ds(a8, PEER_SUB), lanes].astype(gdt) for h in range(PEER_HEADS)]
        for blk in range(PEER_SUB // PEER_GATE_BLOCK):
            gates = [jnp.zeros((PEER_NKEYS, LANE), gdt) for _ in range(PEER_GATE_BLOCK)]
            for h in range(PEER_HEADS):
                r2 = r2_ref[h, :, lanes]
                e2 = e2_ref[h, :, lanes]
                for k in range(PEER_GATE_BLOCK):
                    al = blk * PEER_GATE_BLOCK + k
                    keep = r2 < nrows[h][al:al + 1, :]
                    gates[k] = gates[k] + jnp.where(keep, e2, 0.0) * erows[h][al:al + 1, :]
            for k in range(PEER_GATE_BLOCK):
                al = blk * PEER_GATE_BLOCK + k
                g_ref[sub, al * PEER_NKEYS:(al + 1) * PEER_NKEYS, lanes] = gates[k]


def _peer_kernel(x_ref, hnt_ref, wqt_ref, sk_ref, u_ref, vt_ref, o_ref,
                 qt_ref, n_ref, e1_ref, r2_ref, e2_ref, g_ref, acc_ref, ht_ref):
    j = pl.program_id(1)
    nj = pl.num_programs(1)
    half = PEER_DKEY // 2
    gdt = r2_ref.dtype
    gate_refs = (n_ref, e1_ref, r2_ref, e2_ref, g_ref)

    @pl.when(j == 0)
    def _():
        qt_ref[...] = _dot(wqt_ref[...], hnt_ref[...]).astype(BF16)

        def body(h, carry):
            r0 = pl.multiple_of(h * PEER_DKEY, PEER_DKEY)
            s1 = _dot(sk_ref[2 * h], qt_ref[pl.ds(r0, half), :])
            s2 = _dot(sk_ref[2 * h + 1], qt_ref[pl.ds(r0 + half, half), :])
            _peer_select(h, s1, s2, n_ref, e1_ref, r2_ref, e2_ref)
            return carry

        lax.fori_loop(0, PEER_HEADS, body, 0)
        acc_ref[...] = jnp.zeros(acc_ref.shape, F32)

    sub_rows = lambda sub: slice(sub * PEER_SUB * PEER_NKEYS, (sub + 1) * PEER_SUB * PEER_NKEYS)
    for sub in range(PEER_A // PEER_SUB):
        _peer_gates(j, sub, *gate_refs)
    for sub in range(PEER_A // PEER_SUB):
        at = _dot(u_ref[sub_rows(sub), :], hnt_ref[...]).astype(gdt)
        ht_ref[sub_rows(sub), :] = jax.nn.gelu(at) * g_ref[sub]
    acc_ref[...] += _dot(vt_ref[...], ht_ref[...])

    @pl.when(j == nj - 1)
    def _():
        o_ref[...] = x_ref[...] + acc_ref[...].T


def _peer(x, hnt, wqt, sk, u, vt):
    t = x.shape[0]
    tt = PEER_TT
    ne = u.shape[0]
    et = PEER_A * PEER_NKEYS
    hk = PEER_HEADS * PEER_DKEY
    sel = lambda dt: pltpu.VMEM((PEER_HEADS, PEER_NKEYS, tt), dt)
    return pl.pallas_call(
        _peer_kernel,
        grid=(t // tt, ne // et),
        in_specs=[
            pl.BlockSpec((tt, D_MODEL), lambda i, j: (i, 0)),
            pl.BlockSpec((D_MODEL, tt), lambda i, j: (0, i)),
            pl.BlockSpec((hk, D_MODEL), lambda i, j: (0, 0), pipeline_mode=pl.Buffered(1)),
            pl.BlockSpec((2 * PEER_HEADS, PEER_NKEYS, PEER_DKEY // 2), lambda i, j: (0, 0, 0)),
            pl.BlockSpec((et, D_MODEL), lambda i, j: (j, 0)),
            pl.BlockSpec((D_MODEL, et), lambda i, j: (0, j)),
        ],
        out_specs=pl.BlockSpec((tt, D_MODEL), lambda i, j: (i, 0)),
        out_shape=jax.ShapeDtypeStruct((t, D_MODEL), F32),
        scratch_shapes=[
            pltpu.VMEM((hk, tt), BF16),
            sel(F32), sel(F32), sel(PEER_GATE_DTYPE), sel(PEER_GATE_DTYPE),
            pltpu.VMEM((PEER_A // PEER_SUB, PEER_SUB * PEER_NKEYS, tt), PEER_GATE_DTYPE),
            pltpu.VMEM((D_MODEL, tt), F32),
            pltpu.VMEM((et, tt), BF16),
        ],
        compiler_params=_cparams(("parallel", "arbitrary")),
        name="peer",
    )(x, hnt, wqt, sk, u, vt)


def _final_norm_kernel(x_ref, g_ref, o_ref):
    o_ref[...] = _rms(x_ref[...], g_ref[...])


def _final_norm(x, g):
    t = x.shape[0]
    tm = ROW_TILE
    return pl.pallas_call(
        _final_norm_kernel,
        grid=(t // tm,),
        in_specs=[pl.BlockSpec((tm, D_MODEL), lambda i: (i, 0)), pl.BlockSpec((1, D_MODEL), lambda i: (0, 0))],
        out_specs=pl.BlockSpec((tm, D_MODEL), lambda i: (i, 0)),
        out_shape=jax.ShapeDtypeStruct((t, D_MODEL), F32),
        compiler_params=_cparams(("parallel",)),
        name="final_norm",
    )(x, g)


def _rope_table(pos, dim):
    inv = ROPE_THETA ** (-jnp.arange(0, dim, 2, dtype=F32) / dim)
    ang = pos.astype(F32)[:, None] * inv[None, :]
    ang = jnp.concatenate([ang, ang], axis=-1)
    sign = jnp.concatenate([-jnp.ones((dim // 2,), F32), jnp.ones((dim // 2,), F32)])
    return jnp.cos(ang), jnp.sin(ang) * sign


def _rope_tables(seq):
    pos = jnp.arange(seq)
    cos1, sin1 = _rope_table(pos, HEAD_DIM)
    cr, sr = _rope_table(pos // GRID_W, HEAD_DIM // 2)
    cc, sc = _rope_table(pos % GRID_W, HEAD_DIM // 2)
    cos2 = jnp.concatenate([cr, cc], axis=-1)
    sin2 = jnp.concatenate([sr, sc], axis=-1)
    tile = lambda a: jnp.tile(a, (1, N_Q_HEADS))
    return tile(cos1), tile(sin1), tile(cos2), tile(sin2)


def kernel(x, mix_norm_g, ffn_norm_g, final_norm_g, even_w_in, even_w_out, conv_w, conv_b, conv_ln_g, conv_ln_b, sink_logits, odd_w_in, odd_w_out, q_norm_g, k_norm_g, sgu_ln_g, sgu_ln_b, sgu_w, sgu_b, peer_wq, peer_subkeys, peer_u, peer_v):
    batch, seq, d = x.shape
    depth = mix_norm_g.shape[0]
    t = batch * seq
    cos1, sin1, cos2, sin2 = _rope_tables(seq)
    row = lambda a: a.reshape(1, -1).astype(F32)
    xf = x.reshape(t, d)
    for layer in range(depth):
        i = layer // 2
        if layer % 2 == 0:
            a_in, q, k, v = _inproj_even(xf, row(mix_norm_g[layer]), even_w_in[i].astype(BF16), cos1, sin1, seq)
            m1 = _conv_module(a_in, conv_w[i, :, 0, :], row(conv_b[i]), row(conv_ln_g[i]), row(conv_ln_b[i]), batch, seq)
            sink_b = jnp.broadcast_to(sink_logits[i].astype(F32)[:, None], (N_Q_HEADS, LANE))
            m2 = _swa(q, k, v, sink_b, batch, seq)
            w_out = even_w_out[i]
        else:
            q, k, v, d_in = _inproj_odd(xf, row(mix_norm_g[layer]), odd_w_in[i].astype(BF16),
                                        row(jnp.tile(q_norm_g[i], N_Q_HEADS)), row(jnp.tile(k_norm_g[i], N_KV_HEADS)),
                                        cos2, sin2, seq)
            m1 = _dense_attn(q, k, v, batch, seq)
            bs = jnp.broadcast_to(sgu_b[i].astype(F32)[:, :, None], (SGU_GROUPS, CHUNK, MIX_WIDTH // SGU_GROUPS))
            m2 = _sgu(d_in, row(sgu_ln_g[i]), row(sgu_ln_b[i]), sgu_w[i].astype(BF16), bs)
            w_out = odd_w_out[i]
        xn, hnt = _outproj(xf, m1, m2, w_out.astype(BF16), row(ffn_norm_g[layer]))
        sk = peer_subkeys[layer].reshape(2 * PEER_HEADS, PEER_NKEYS, PEER_DKEY // 2).astype(BF16)
        xf = _peer(xn, hnt, peer_wq[layer].T.astype(BF16), sk, peer_u[layer].astype(BF16),
                   peer_v[layer].T.astype(BF16))
    return _final_norm(xf, row(final_norm_g)).reshape(batch, seq, d)
```

```python
import functools
import math

import jax
import jax.numpy as jnp
from jax import lax
from jax.experimental import pallas as pl
from jax.experimental.pallas import tpu as pltpu

F32 = jnp.float32
BF16 = jnp.bfloat16

D_MODEL = 1024
HEAD_DIM = 64
N_Q_HEADS = 8
N_KV_HEADS = 2
GQA_GROUPS = N_Q_HEADS // N_KV_HEADS
ATTN_WIDTH = N_Q_HEADS * HEAD_DIM
KV_WIDTH = N_KV_HEADS * HEAD_DIM
MIX_WIDTH = D_MODEL // 2
CONV_WIDTH = 31
CONV_HALO = 16
SGU_GROUPS = 4
CHUNK = 128
WINDOW = 128
BLOCK_Q = 128
GRID_W = 64
ROPE_THETA = 10000.0
PEER_HEADS = 8
PEER_NKEYS = 128
PEER_DKEY = 128
PEER_TOPK = 16
EPS = 1e-6
NEG = -1e30

VMEM_LIMIT_BYTES = 56 * 1024 * 1024

ROW_TILE = 512
CONV_ROWS = 64
SWA_BLOCKS = 4
ATT_TQ = 512
ATT_CHAINS = 4
PEER_TT = 1024
PEER_A = 8
PEER_SUB = 8
PEER_GATE_BLOCK = 4
PEER_GATE_DTYPE = BF16
LANE = 128


def _cparams(sem):
    return pltpu.CompilerParams(dimension_semantics=sem, vmem_limit_bytes=VMEM_LIMIT_BYTES)


def _dot(a, b):
    return jnp.dot(a, b, preferred_element_type=F32)


def _dot_nt(a, b):
    return lax.dot_general(a, b, (((1,), (1,)), ((), ())), preferred_element_type=F32)


def _rms(x, g):
    return x * lax.rsqrt(jnp.mean(x * x, axis=-1, keepdims=True) + EPS) * g


def _layernorm(x, g, b):
    mu = jnp.mean(x, axis=-1, keepdims=True)
    xc = x - mu
    var = jnp.mean(xc * xc, axis=-1, keepdims=True)
    return xc * lax.rsqrt(var + EPS) * g + b


def _rope(x, cos, sin_signed, group, half):
    n = x.shape[-1]
    lane = lax.broadcasted_iota(jnp.int32, x.shape, x.ndim - 1)
    first = (lane % group) < half
    partner = jnp.where(first, pltpu.roll(x, n - half, x.ndim - 1), pltpu.roll(x, half, x.ndim - 1))
    return x * cos + partner * sin_signed


def _inproj_even_kernel(x_ref, g_ref, w_ref, cos_ref, sin_ref, a_ref, q_ref, k_ref, v_ref):
    h = _rms(x_ref[...], g_ref[...]).astype(BF16)
    o1 = 2 * MIX_WIDTH
    o2 = o1 + ATTN_WIDTH
    o3 = o2 + KV_WIDTH
    a_ref[...] = _dot(h, w_ref[:, :o1])
    cos = cos_ref[...]
    sin = sin_ref[...]
    q = _rope(_dot(h, w_ref[:, o1:o2]), cos, sin, HEAD_DIM, HEAD_DIM // 2)
    q_ref[...] = (q * (HEAD_DIM ** -0.5)).astype(BF16)
    k = _rope(_dot(h, w_ref[:, o2:o3]), cos[:, :KV_WIDTH], sin[:, :KV_WIDTH], HEAD_DIM, HEAD_DIM // 2)
    k_ref[...] = k.astype(BF16)
    v_ref[...] = _dot(h, w_ref[:, o3:]).astype(BF16)


def _inproj_even(x, g, w, cos, sin, seq):
    t = x.shape[0]
    tm = ROW_TILE
    nseq = seq // tm
    in_w = w.shape[1]
    row = lambda i: (i, 0)
    fixed = lambda i: (0, 0)
    pos = lambda i: (i % nseq, 0)
    return pl.pallas_call(
        _inproj_even_kernel,
        grid=(t // tm,),
        in_specs=[
            pl.BlockSpec((tm, D_MODEL), row),
            pl.BlockSpec((1, D_MODEL), fixed),
            pl.BlockSpec((D_MODEL, in_w), fixed),
            pl.BlockSpec((tm, ATTN_WIDTH), pos),
            pl.BlockSpec((tm, ATTN_WIDTH), pos),
        ],
        out_specs=[
            pl.BlockSpec((tm, 2 * MIX_WIDTH), row),
            pl.BlockSpec((tm, ATTN_WIDTH), row),
            pl.BlockSpec((tm, KV_WIDTH), row),
            pl.BlockSpec((tm, KV_WIDTH), row),
        ],
        out_shape=[
            jax.ShapeDtypeStruct((t, 2 * MIX_WIDTH), F32),
            jax.ShapeDtypeStruct((t, ATTN_WIDTH), BF16),
            jax.ShapeDtypeStruct((t, KV_WIDTH), BF16),
            jax.ShapeDtypeStruct((t, KV_WIDTH), BF16),
        ],
        compiler_params=_cparams(("parallel",)),
        name="inproj_even",
    )(x, g, w, cos, sin)


def _head_sumsq(x):
    n = x.shape[-1]
    r = lax.broadcasted_iota(jnp.int32, (n, n), 0) // HEAD_DIM
    c = lax.broadcasted_iota(jnp.int32, (n, n), 1) // HEAD_DIM
    ones = jnp.where(r == c, 1.0, 0.0).astype(BF16)
    sq = x * x
    hi = sq.astype(BF16)
    lo = (sq - hi.astype(F32)).astype(BF16)
    return _dot(hi, ones) + _dot(lo, ones)


def _inproj_odd_kernel(x_ref, g_ref, w_ref, qg_ref, kg_ref, cos_ref, sin_ref, q_ref, k_ref, v_ref, d_ref):
    h = _rms(x_ref[...], g_ref[...]).astype(BF16)
    c1 = ATTN_WIDTH
    c2 = c1 + KV_WIDTH
    c3 = c2 + KV_WIDTH
    cos = cos_ref[...]
    sin = sin_ref[...]
    half = HEAD_DIM // 2
    q = _dot(h, w_ref[:, :c1])
    q = q * lax.rsqrt(_head_sumsq(q) * (1.0 / HEAD_DIM) + EPS) * qg_ref[...]
    q = _rope(q, cos, sin, half, half // 2)
    q_ref[...] = (q * (HEAD_DIM ** -0.5)).astype(BF16)
    k = _dot(h, w_ref[:, c1:c2])
    k = k * lax.rsqrt(_head_sumsq(k) * (1.0 / HEAD_DIM) + EPS) * kg_ref[...]
    k = _rope(k, cos[:, :KV_WIDTH], sin[:, :KV_WIDTH], half, half // 2).astype(BF16)
    v = _dot(h, w_ref[:, c2:c3]).astype(BF16)
    for hh in range(N_KV_HEADS):
        k_ref[hh] = k[:, hh * HEAD_DIM:(hh + 1) * HEAD_DIM]
        v_ref[hh] = v[:, hh * HEAD_DIM:(hh + 1) * HEAD_DIM]
    d_ref[...] = _dot(h, w_ref[:, c3:])


def _inproj_odd(x, g, w, qg, kg, cos, sin, seq):
    t = x.shape[0]
    tm = ROW_TILE
    nseq = seq // tm
    in_w = w.shape[1]
    row = lambda i: (i, 0)
    fixed = lambda i: (0, 0)
    pos = lambda i: (i % nseq, 0)
    kv = lambda i: (0, i, 0)
    return pl.pallas_call(
        _inproj_odd_kernel,
        grid=(t // tm,),
        in_specs=[
            pl.BlockSpec((tm, D_MODEL), row),
            pl.BlockSpec((1, D_MODEL), fixed),
            pl.BlockSpec((D_MODEL, in_w), fixed),
            pl.BlockSpec((1, ATTN_WIDTH), fixed),
            pl.BlockSpec((1, KV_WIDTH), fixed),
            pl.BlockSpec((tm, ATTN_WIDTH), pos),
            pl.BlockSpec((tm, ATTN_WIDTH), pos),
        ],
        out_specs=[
            pl.BlockSpec((tm, ATTN_WIDTH), row),
            pl.BlockSpec((N_KV_HEADS, tm, HEAD_DIM), kv),
            pl.BlockSpec((N_KV_HEADS, tm, HEAD_DIM), kv),
            pl.BlockSpec((tm, 2 * MIX_WIDTH), row),
        ],
        out_shape=[
            jax.ShapeDtypeStruct((t, ATTN_WIDTH), BF16),
            jax.ShapeDtypeStruct((N_KV_HEADS, t, HEAD_DIM), BF16),
            jax.ShapeDtypeStruct((N_KV_HEADS, t, HEAD_DIM), BF16),
            jax.ShapeDtypeStruct((t, 2 * MIX_WIDTH), F32),
        ],
        compiler_params=_cparams(("parallel",)),
        name="inproj_odd",
    )(x, g, w, qg, kg, cos, sin)


def _glu(x):
    return x[:, :MIX_WIDTH] * jax.nn.sigmoid(x[:, MIX_WIDTH:])


def _conv_kernel(prev_ref, cur_ref, next_ref, w_ref, b_ref, g_ref, beta_ref, o_ref, buf_ref):
    i = pl.program_id(1)
    ns = pl.num_programs(1)
    ts = cur_ref.shape[0]
    halo = CONV_HALO
    buf_ref[0:halo, :] = jnp.where(i > 0, _glu(prev_ref[...]), 0.0)
    buf_ref[halo:halo + ts, :] = _glu(cur_ref[...])
    buf_ref[halo + ts:, :] = jnp.where(i < ns - 1, _glu(next_ref[...]), 0.0)
    w = w_ref[...]
    bias = b_ref[...]
    g = g_ref[...]
    beta = beta_ref[...]
    first = halo - CONV_WIDTH // 2
    for c in range(ts // CONV_ROWS):
        r0 = c * CONV_ROWS
        acc = jnp.zeros((CONV_ROWS, MIX_WIDTH), F32)
        for k in range(CONV_WIDTH):
            acc = acc + w[k:k + 1, :] * buf_ref[r0 + first + k:r0 + first + k + CONV_ROWS, :]
        y = _layernorm(acc + bias, g, beta)
        o_ref[r0:r0 + CONV_ROWS, :] = (y * jax.nn.sigmoid(y)).astype(o_ref.dtype)


def _conv_module(a_in, w, b, g, beta, batch, seq):
    t = a_in.shape[0]
    ts = ROW_TILE
    ns = seq // ts
    hb = ts // CONV_HALO
    last = t // CONV_HALO - 1
    fixed = lambda bb, i: (0, 0)
    return pl.pallas_call(
        _conv_kernel,
        grid=(batch, ns),
        in_specs=[
            pl.BlockSpec((CONV_HALO, 2 * MIX_WIDTH), lambda bb, i: (jnp.maximum((bb * ns + i) * hb - 1, 0), 0)),
            pl.BlockSpec((ts, 2 * MIX_WIDTH), lambda bb, i: (bb * ns + i, 0)),
            pl.BlockSpec((CONV_HALO, 2 * MIX_WIDTH), lambda bb, i: (jnp.minimum((bb * ns + i + 1) * hb, last), 0)),
            pl.BlockSpec((CONV_WIDTH, MIX_WIDTH), fixed),
            pl.BlockSpec((1, MIX_WIDTH), fixed),
            pl.BlockSpec((1, MIX_WIDTH), fixed),
            pl.BlockSpec((1, MIX_WIDTH), fixed),
        ],
        out_specs=pl.BlockSpec((ts, MIX_WIDTH), lambda bb, i: (bb * ns + i, 0)),
        out_shape=jax.ShapeDtypeStruct((t, MIX_WIDTH), BF16),
        scratch_shapes=[pltpu.VMEM((ts + 2 * CONV_HALO, MIX_WIDTH), F32)],
        compiler_params=_cparams(("parallel", "parallel")),
        name="conv_module",
    )(a_in, a_in, a_in, w, b, g, beta)


def _swa_kernel(q_ref, kp_ref, kc_ref, kn_ref, vp_ref, vc_ref, vn_ref, sink_ref, o_ref):
    n = pl.program_id(1)
    nsteps = pl.num_programs(1)
    bq = BLOCK_Q
    rows = GQA_GROUPS * bq
    ii = lax.broadcasted_iota(jnp.int32, (rows, 3 * bq), 0) % bq
    jj = lax.broadcasted_iota(jnp.int32, (rows, 3 * bq), 1)
    band = (jj >= ii) & (jj <= ii + 2 * WINDOW)
    kcat = jnp.concatenate([kp_ref[...], kc_ref[...], kn_ref[...]], axis=0)
    vcat = jnp.concatenate([vp_ref[...], vc_ref[...], vn_ref[...]], axis=0)
    for qb in range(SWA_BLOCKS):
        mask = band
        if qb == 0:
            mask = mask & (jj >= jnp.where(n > 0, 0, bq))
        if qb == SWA_BLOCKS - 1:
            mask = mask & (jj < jnp.where(n < nsteps - 1, 3 * bq, 2 * bq))
        q = q_ref[qb * bq:(qb + 1) * bq, :]
        outs = []
        for hk in range(N_KV_HEADS):
            heads = range(hk * GQA_GROUPS, (hk + 1) * GQA_GROUPS)
            kh = kcat[qb * bq:(qb + 3) * bq, hk * HEAD_DIM:(hk + 1) * HEAD_DIM]
            vh = vcat[qb * bq:(qb + 3) * bq, hk * HEAD_DIM:(hk + 1) * HEAD_DIM]
            qs = jnp.concatenate([q[:, hd * HEAD_DIM:(hd + 1) * HEAD_DIM] for hd in heads], axis=0)
            sink = jnp.concatenate([jnp.broadcast_to(sink_ref[hd:hd + 1, 0:1], (bq, 1)) for hd in heads], axis=0)
            s = jnp.where(mask, _dot_nt(qs, kh), NEG)
            m = jnp.maximum(jnp.max(s, axis=-1, keepdims=True), sink)
            p = jnp.exp(s - m)
            denom = jnp.sum(p, axis=-1, keepdims=True) + jnp.exp(sink - m)
            o = _dot(p.astype(BF16), vh) / denom
            outs.extend(o[g * bq:(g + 1) * bq, :] for g in range(GQA_GROUPS))
        o_ref[qb * bq:(qb + 1) * bq, :] = jnp.concatenate(outs, axis=-1).astype(o_ref.dtype)


def _swa(q, k, v, sink_b, batch, seq):
    t = q.shape[0]
    bq = BLOCK_Q
    tq = SWA_BLOCKS * bq
    ns = seq // tq
    last = t // bq - 1
    cur = lambda bb, n: (bb * ns + n, 0)
    prev = lambda bb, n: (jnp.maximum((bb * ns + n) * SWA_BLOCKS - 1, 0), 0)
    nxt = lambda bb, n: (jnp.minimum((bb * ns + n + 1) * SWA_BLOCKS, last), 0)
    halo = lambda im: pl.BlockSpec((bq, KV_WIDTH), im)
    body = pl.BlockSpec((tq, KV_WIDTH), cur)
    return pl.pallas_call(
        _swa_kernel,
        grid=(batch, ns),
        in_specs=[pl.BlockSpec((tq, ATTN_WIDTH), cur), halo(prev), body, halo(nxt), halo(prev), body, halo(nxt),
                  pl.BlockSpec((N_Q_HEADS, LANE), lambda bb, n: (0, 0))],
        out_specs=pl.BlockSpec((tq, ATTN_WIDTH), cur),
        out_shape=jax.ShapeDtypeStruct((t, ATTN_WIDTH), BF16),
        compiler_params=_cparams(("parallel", "parallel")),
        name="window_attn",
    )(q, k, k, k, v, v, v, sink_b)


def _dense_attn_kernel(q_ref, k_ref, v_ref, o_ref):
    ts = q_ref.shape[0] // ATT_CHAINS
    k = k_ref[...]
    v = v_ref[...]
    for c in range(ATT_CHAINS):
        q = q_ref[c * ts:(c + 1) * ts, :]
        qs = jnp.concatenate([q[:, g * HEAD_DIM:(g + 1) * HEAD_DIM] for g in range(GQA_GROUPS)], axis=0)
        s = _dot_nt(qs, k)
        p = jnp.exp(s - jnp.max(s, axis=-1, keepdims=True))
        o = _dot(p.astype(BF16), v) / jnp.sum(p, axis=-1, keepdims=True)
        o_ref[c * ts:(c + 1) * ts, :] = jnp.concatenate(
            [o[g * ts:(g + 1) * ts, :] for g in range(GQA_GROUPS)], axis=-1).astype(o_ref.dtype)


def _dense_attn(q, k, v, batch, seq):
    t = q.shape[0]
    tq = min(ATT_TQ, seq)
    nq = seq // tq
    gw = GQA_GROUPS * HEAD_DIM
    qmap = lambda bb, hk, qi: (bb * nq + qi, hk)
    kmap = lambda bb, hk, qi: (hk, bb, 0)
    return pl.pallas_call(
        _dense_attn_kernel,
        grid=(batch, N_KV_HEADS, nq),
        in_specs=[
            pl.BlockSpec((tq, gw), qmap),
            pl.BlockSpec((None, seq, HEAD_DIM), kmap),
            pl.BlockSpec((None, seq, HEAD_DIM), kmap),
        ],
        out_specs=pl.BlockSpec((tq, gw), qmap),
        out_shape=jax.ShapeDtypeStruct((t, ATTN_WIDTH), BF16),
        compiler_params=_cparams(("parallel", "parallel", "parallel")),
        name="dense_attn",
    )(q, k, v)


def _sgu_kernel(d_ref, g_ref, b_ref, w_ref, bs_ref, o_ref):
    z = jax.nn.gelu(d_ref[...])
    u = z[:, :MIX_WIDTH]
    v = _layernorm(z[:, MIX_WIDTH:], g_ref[...], b_ref[...]).astype(BF16)
    gc = MIX_WIDTH // SGU_GROUPS
    for n in range(d_ref.shape[0] // CHUNK):
        rows = slice(n * CHUNK, (n + 1) * CHUNK)
        for g in range(SGU_GROUPS):
            cols = slice(g * gc, (g + 1) * gc)
            mixed = _dot(w_ref[g], v[rows, cols]) + bs_ref[g]
            o_ref[rows, cols] = (u[rows, cols] * mixed).astype(o_ref.dtype)


def _sgu(d_in, g, b, w, bs):
    t = d_in.shape[0]
    ts = ROW_TILE
    gc = MIX_WIDTH // SGU_GROUPS
    row = lambda i: (i, 0)
    fixed = lambda i: (0, 0)
    fixed3 = lambda i: (0, 0, 0)
    return pl.pallas_call(
        _sgu_kernel,
        grid=(t // ts,),
        in_specs=[
            pl.BlockSpec((ts, 2 * MIX_WIDTH), row),
            pl.BlockSpec((1, MIX_WIDTH), fixed),
            pl.BlockSpec((1, MIX_WIDTH), fixed),
            pl.BlockSpec((SGU_GROUPS, CHUNK, CHUNK), fixed3),
            pl.BlockSpec((SGU_GROUPS, CHUNK, gc), fixed3),
        ],
        out_specs=pl.BlockSpec((ts, MIX_WIDTH), row),
        out_shape=jax.ShapeDtypeStruct((t, MIX_WIDTH), BF16),
        compiler_params=_cparams(("parallel",)),
        name="spatial_gating",
    )(d_in, g, b, w, bs)


def _outproj_kernel(x_ref, a_ref, b_ref, w_ref, g_ref, xn_ref, hnt_ref):
    y = _dot(a_ref[...], w_ref[:MIX_WIDTH, :]) + _dot(b_ref[...], w_ref[MIX_WIDTH:, :])
    xn = x_ref[...] + y
    xn_ref[...] = xn
    hnt_ref[...] = _rms(xn, g_ref[...]).T.astype(BF16)


def _outproj(x, a, b, w, g):
    t = x.shape[0]
    tm = ROW_TILE
    row = lambda i: (i, 0)
    fixed = lambda i: (0, 0)
    return pl.pallas_call(
        _outproj_kernel,
        grid=(t // tm,),
        in_specs=[
            pl.BlockSpec((tm, D_MODEL), row),
            pl.BlockSpec((tm, MIX_WIDTH), row),
            pl.BlockSpec((tm, MIX_WIDTH), row),
            pl.BlockSpec((2 * MIX_WIDTH, D_MODEL), fixed),
            pl.BlockSpec((1, D_MODEL), fixed),
        ],
        out_specs=[pl.BlockSpec((tm, D_MODEL), row), pl.BlockSpec((D_MODEL, tm), lambda i: (0, i))],
        out_shape=[jax.ShapeDtypeStruct((t, D_MODEL), F32), jax.ShapeDtypeStruct((D_MODEL, t), BF16)],
        compiler_params=_cparams(("parallel",)),
        name="outproj",
    )(x, a, b, w, g)


_PEER_PAIRS = [(i, j) for i in range(PEER_TOPK) for j in range(PEER_TOPK) if (i + 1) * (j + 1) <= PEER_TOPK]
_PEER_CAND_ROWS = -(-len(_PEER_PAIRS) // 8) * 8


SUBLANES = 8


def _bitonic_merge_desc(v, lo, n):
    step = n // 2
    while step >= 1:
        for i in range(lo, lo + n):
            if (i - lo) & step == 0:
                hi_, lo_ = jnp.maximum(v[i], v[i + step]), jnp.minimum(v[i], v[i + step])
                v[i], v[i + step] = hi_, lo_
        step //= 2


def _sort16_desc(v):
    n = len(v)
    size = 2
    while size <= n:
        for lo in range(0, n, size):
            half = size // 2
            v[lo + half:lo + size] = v[lo + half:lo + size][::-1]
            _bitonic_merge_desc(v, lo, size)
        size *= 2
    return v


def _sublane_all(x, op):
    for shift in (4, 2, 1):
        x = op(x, pltpu.roll(x, shift, 0))
    return x


def _top16_sorted(s):
    v = _sort16_desc([s[k * SUBLANES:(k + 1) * SUBLANES, :] for k in range(PEER_NKEYS // SUBLANES)])
    for shift in (4, 2, 1):
        v = [jnp.maximum(v[k], pltpu.roll(v[PEER_TOPK - 1 - k], shift, 0)) for k in range(PEER_TOPK)]
        _bitonic_merge_desc(v, 0, PEER_TOPK)
    return v


def _peer_select(h, s1_all, s2_all, n_ref, e1_ref, r2_ref, e2_ref):
    tt = s1_all.shape[-1]
    nv = PEER_NKEYS // SUBLANES
    sub = lax.broadcasted_iota(jnp.int32, (SUBLANES, LANE), 0)
    for tc in range(tt // LANE):
        lanes = slice(tc * LANE, (tc + 1) * LANE)
        s1 = s1_all[:, lanes]
        s2 = s2_all[:, lanes]
        v1 = _top16_sorted(s1)
        v2 = _top16_sorted(s2)
        pair_sum = {p: v1[p[0]] + v2[p[1]] for p in _PEER_PAIRS}
        cand = []
        for g in range(_PEER_CAND_ROWS // SUBLANES):
            c = jnp.full((SUBLANES, LANE), -jnp.inf, F32)
            for r, p in enumerate(_PEER_PAIRS[g * SUBLANES:(g + 1) * SUBLANES]):
                c = jnp.where(sub == r, pair_sum[p], c)
            cand.append(c)
        f = []
        for _ in range(PEER_TOPK):
            mx = cand[0]
            for c in cand[1:]:
                mx = jnp.maximum(mx, c)
            mx = _sublane_all(mx, jnp.maximum)
            f.append(mx)
            cand = [jnp.where(c == mx, -jnp.inf, c) for c in cand]
        thr = f[PEER_TOPK - 1]
        z = jnp.ones((SUBLANES, LANE), F32)
        for r in range(1, PEER_TOPK):
            z = z + jnp.exp(f[r] - f[0])
        rz = 1.0 / z
        cnt = []
        for i in range(PEER_TOPK):
            c = jnp.zeros((SUBLANES, LANE), F32)
            for j in range(PEER_TOPK // (i + 1)):
                c = c + jnp.where(pair_sum[(i, j)] >= thr, 1.0, 0.0)
            cnt.append(c)
        ranks = []
        e2s = []
        for k in range(nv):
            rows = slice(k * SUBLANES, (k + 1) * SUBLANES)
            a = s1[rows, :]
            b = s2[rows, :]
            n = jnp.zeros((SUBLANES, LANE), F32)
            rank = jnp.full((SUBLANES, LANE), float(PEER_TOPK), F32)
            for r in range(PEER_TOPK):
                n = jnp.where(a == v1[r], cnt[r], n)
                rank = jnp.where(b == v2[r], float(r), rank)
            n_ref[h, rows, lanes] = n
            e1_ref[h, rows, lanes] = jnp.exp(a - v1[0]) * rz
            ranks.append(rank)
            e2s.append(jnp.exp(b - v2[0]))
        r2_ref[h, :, lanes] = jnp.concatenate(ranks, axis=0).astype(r2_ref.dtype)
        e2_ref[h, :, lanes] = jnp.concatenate(e2s, axis=0).astype(e2_ref.dtype)


def _peer_gates(step, sub, n_ref, e1_ref, r2_ref, e2_ref, g_ref):
    tt = g_ref.shape[-1]
    gdt = g_ref.dtype
    a8 = pl.multiple_of(step * PEER_A + sub * PEER_SUB, SUBLANES)
    for tc in range(tt // LANE):
        lanes = slice(tc * LANE, (tc + 1) * LANE)
        nrows = [n_ref[h, pl.ds(a8, PEER_SUB), lanes].astype(gdt) for h in range(PEER_HEADS)]
        erows = [e1_ref[h, pl.ds(a8, PEER_SUB), lanes].astype(gdt) for h in range(PEER_HEADS)]
        for blk in range(PEER_SUB // PEER_GATE_BLOCK):
            gates = [jnp.zeros((PEER_NKEYS, LANE), gdt) for _ in range(PEER_GATE_BLOCK)]
            for h in range(PEER_HEADS):
                r2 = r2_ref[h, :, lanes]
                e2 = e2_ref[h, :, lanes]
                for k in range(PEER_GATE_BLOCK):
                    al = blk * PEER_GATE_BLOCK + k
                    keep = r2 < nrows[h][al:al + 1, :]
                    gates[k] = gates[k] + jnp.where(keep, e2, 0.0) * erows[h][al:al + 1, :]
            for k in range(PEER_GATE_BLOCK):
                al = blk * PEER_GATE_BLOCK + k
                g_ref[sub, al * PEER_NKEYS:(al + 1) * PEER_NKEYS, lanes] = gates[k]


def _peer_kernel(x_ref, hnt_ref, wqt_ref, sk_ref, u_ref, vt_ref, o_ref,
                 qt_ref, n_ref, e1_ref, r2_ref, e2_ref, g_ref, acc_ref, ht_ref):
    j = pl.program_id(1)
    nj = pl.num_programs(1)
    half = PEER_DKEY // 2
    gdt = r2_ref.dtype
    gate_refs = (n_ref, e1_ref, r2_ref, e2_ref, g_ref)

    @pl.when(j == 0)
    def _():
        qt_ref[...] = _dot(wqt_ref[...], hnt_ref[...]).astype(BF16)

        def body(h, carry):
            r0 = pl.multiple_of(h * PEER_DKEY, PEER_DKEY)
            s1 = _dot(sk_ref[2 * h], qt_ref[pl.ds(r0, half), :])
            s2 = _dot(sk_ref[2 * h + 1], qt_ref[pl.ds(r0 + half, half), :])
            _peer_select(h, s1, s2, n_ref, e1_ref, r2_ref, e2_ref)
            return carry

        lax.fori_loop(0, PEER_HEADS, body, 0)
        acc_ref[...] = jnp.zeros(acc_ref.shape, F32)

    sub_rows = lambda sub: slice(sub * PEER_SUB * PEER_NKEYS, (sub + 1) * PEER_SUB * PEER_NKEYS)
    for sub in range(PEER_A // PEER_SUB):
        _peer_gates(j, sub, *gate_refs)
    for sub in range(PEER_A // PEER_SUB):
        at = _dot(u_ref[sub_rows(sub), :], hnt_ref[...]).astype(gdt)
        ht_ref[sub_rows(sub), :] = jax.nn.gelu(at) * g_ref[sub]
    acc_ref[...] += _dot(vt_ref[...], ht_ref[...])

    @pl.when(j == nj - 1)
    def _():
        o_ref[...] = x_ref[...] + acc_ref[...].T


def _peer(x, hnt, wqt, sk, u, vt):
    t = x.shape[0]
    tt = PEER_TT
    ne = u.shape[0]
    et = PEER_A * PEER_NKEYS
    hk = PEER_HEADS * PEER_DKEY
    sel = lambda dt: pltpu.VMEM((PEER_HEADS, PEER_NKEYS, tt), dt)
    return pl.pallas_call(
        _peer_kernel,
        grid=(t // tt, ne // et),
        in_specs=[
            pl.BlockSpec((tt, D_MODEL), lambda i, j: (i, 0), pipeline_mode=pl.Buffered(1)),
            pl.BlockSpec((D_MODEL, tt), lambda i, j: (0, i), pipeline_mode=pl.Buffered(1)),
            pl.BlockSpec((hk, D_MODEL), lambda i, j: (0, 0), pipeline_mode=pl.Buffered(1)),
            pl.BlockSpec((2 * PEER_HEADS, PEER_NKEYS, PEER_DKEY // 2), lambda i, j: (0, 0, 0),
                         pipeline_mode=pl.Buffered(1)),
            pl.BlockSpec((et, D_MODEL), lambda i, j: (j, 0)),
            pl.BlockSpec((D_MODEL, et), lambda i, j: (0, j)),
        ],
        out_specs=pl.BlockSpec((tt, D_MODEL), lambda i, j: (i, 0)),
        out_shape=jax.ShapeDtypeStruct((t, D_MODEL), F32),
        scratch_shapes=[
            pltpu.VMEM((hk, tt), BF16),
            sel(F32), sel(F32), sel(PEER_GATE_DTYPE), sel(PEER_GATE_DTYPE),
            pltpu.VMEM((PEER_A // PEER_SUB, PEER_SUB * PEER_NKEYS, tt), PEER_GATE_DTYPE),
            pltpu.VMEM((D_MODEL, tt), F32),
            pltpu.VMEM((et, tt), BF16),
        ],
        compiler_params=_cparams(("parallel", "arbitrary")),
        name="peer",
    )(x, hnt, wqt, sk, u, vt)


def _final_norm_kernel(x_ref, g_ref, o_ref):
    o_ref[...] = _rms(x_ref[...], g_ref[...])


def _final_norm(x, g):
    t = x.shape[0]
    tm = ROW_TILE
    return pl.pallas_call(
        _final_norm_kernel,
        grid=(t // tm,),
        in_specs=[pl.BlockSpec((tm, D_MODEL), lambda i: (i, 0)), pl.BlockSpec((1, D_MODEL), lambda i: (0, 0))],
        out_specs=pl.BlockSpec((tm, D_MODEL), lambda i: (i, 0)),
        out_shape=jax.ShapeDtypeStruct((t, D_MODEL), F32),
        compiler_params=_cparams(("parallel",)),
        name="final_norm",
    )(x, g)


def _rope_table(pos, dim):
    inv = ROPE_THETA ** (-jnp.arange(0, dim, 2, dtype=F32) / dim)
    ang = pos.astype(F32)[:, None] * inv[None, :]
    ang = jnp.concatenate([ang, ang], axis=-1)
    sign = jnp.concatenate([-jnp.ones((dim // 2,), F32), jnp.ones((dim // 2,), F32)])
    return jnp.cos(ang), jnp.sin(ang) * sign


def _rope_tables(seq):
    pos = jnp.arange(seq)
    cos1, sin1 = _rope_table(pos, HEAD_DIM)
    cr, sr = _rope_table(pos // GRID_W, HEAD_DIM // 2)
    cc, sc = _rope_table(pos % GRID_W, HEAD_DIM // 2)
    cos2 = jnp.concatenate([cr, cc], axis=-1)
    sin2 = jnp.concatenate([sr, sc], axis=-1)
    tile = lambda a: jnp.tile(a, (1, N_Q_HEADS))
    return tile(cos1), tile(sin1), tile(cos2), tile(sin2)


def kernel(x, mix_norm_g, ffn_norm_g, final_norm_g, even_w_in, even_w_out, conv_w, conv_b, conv_ln_g, conv_ln_b, sink_logits, odd_w_in, odd_w_out, q_norm_g, k_norm_g, sgu_ln_g, sgu_ln_b, sgu_w, sgu_b, peer_wq, peer_subkeys, peer_u, peer_v):
    batch, seq, d = x.shape
    depth = mix_norm_g.shape[0]
    t = batch * seq
    cos1, sin1, cos2, sin2 = _rope_tables(seq)
    row = lambda a: a.reshape(1, -1).astype(F32)
    xf = x.reshape(t, d)
    for layer in range(depth):
        i = layer // 2
        if layer % 2 == 0:
            a_in, q, k, v = _inproj_even(xf, row(mix_norm_g[layer]), even_w_in[i].astype(BF16), cos1, sin1, seq)
            m1 = _conv_module(a_in, conv_w[i, :, 0, :], row(conv_b[i]), row(conv_ln_g[i]), row(conv_ln_b[i]), batch, seq)
            sink_b = jnp.broadcast_to(sink_logits[i].astype(F32)[:, None], (N_Q_HEADS, LANE))
            m2 = _swa(q, k, v, sink_b, batch, seq)
            w_out = even_w_out[i]
        else:
            q, k, v, d_in = _inproj_odd(xf, row(mix_norm_g[layer]), odd_w_in[i].astype(BF16),
                                        row(jnp.tile(q_norm_g[i], N_Q_HEADS)), row(jnp.tile(k_norm_g[i], N_KV_HEADS)),
                                        cos2, sin2, seq)
            m1 = _dense_attn(q, k, v, batch, seq)
            bs = jnp.broadcast_to(sgu_b[i].astype(F32)[:, :, None], (SGU_GROUPS, CHUNK, MIX_WIDTH // SGU_GROUPS))
            m2 = _sgu(d_in, row(sgu_ln_g[i]), row(sgu_ln_b[i]), sgu_w[i].astype(BF16), bs)
            w_out = odd_w_out[i]
        xn, hnt = _outproj(xf, m1, m2, w_out.astype(BF16), row(ffn_norm_g[layer]))
        sk = peer_subkeys[layer].reshape(2 * PEER_HEADS, PEER_NKEYS, PEER_DKEY // 2).astype(BF16)
        xf = _peer(xn, hnt, peer_wq[layer].T.astype(BF16), sk, peer_u[layer].astype(BF16),
                   peer_v[layer].T.astype(BF16))
    return _final_norm(xf, row(final_norm_g)).reshape(batch, seq, d)
```

```python
import functools
import math

import jax
import jax.numpy as jnp
from jax import lax
from jax.experimental import pallas as pl
from jax.experimental.pallas import tpu as pltpu

F32 = jnp.float32
BF16 = jnp.bfloat16

D_MODEL = 1024
HEAD_DIM = 64
N_Q_HEADS = 8
N_KV_HEADS = 2
GQA_GROUPS = N_Q_HEADS // N_KV_HEADS
ATTN_WIDTH = N_Q_HEADS * HEAD_DIM
KV_WIDTH = N_KV_HEADS * HEAD_DIM
MIX_WIDTH = D_MODEL // 2
CONV_WIDTH = 31
CONV_HALO = 16
SGU_GROUPS = 4
CHUNK = 128
WINDOW = 128
BLOCK_Q = 128
GRID_W = 64
ROPE_THETA = 10000.0
PEER_HEADS = 8
PEER_NKEYS = 128
PEER_DKEY = 128
PEER_TOPK = 16
EPS = 1e-6
NEG = -1e30

VMEM_LIMIT_BYTES = 56 * 1024 * 1024

ROW_TILE = 512
CONV_ROWS = 64
SWA_BLOCKS = 4
ATT_TQ = 1024
ATT_CHAINS = 8
PEER_TT = 512
PEER_A = 16
PEER_SUB = 8
PEER_GATE_BLOCK = 4
PEER_GATE_DTYPE = BF16
LANE = 128
BF16_ROWS = 16


def _cparams(sem):
    return pltpu.CompilerParams(dimension_semantics=sem, vmem_limit_bytes=VMEM_LIMIT_BYTES)


def _dot(a, b):
    return jnp.dot(a, b, preferred_element_type=F32)


def _dot_nt(a, b):
    return lax.dot_general(a, b, (((1,), (1,)), ((), ())), preferred_element_type=F32)


def _rms(x, g):
    return x * lax.rsqrt(jnp.mean(x * x, axis=-1, keepdims=True) + EPS) * g


def _layernorm(x, g, b):
    mu = jnp.mean(x, axis=-1, keepdims=True)
    xc = x - mu
    var = jnp.mean(xc * xc, axis=-1, keepdims=True)
    return xc * lax.rsqrt(var + EPS) * g + b


def _rope(x, cos, sin_signed, group, half):
    n = x.shape[-1]
    lane = lax.broadcasted_iota(jnp.int32, x.shape, x.ndim - 1)
    first = (lane % group) < half
    partner = jnp.where(first, pltpu.roll(x, n - half, x.ndim - 1), pltpu.roll(x, half, x.ndim - 1))
    return x * cos + partner * sin_signed


def _inproj_even_kernel(x_ref, g_ref, w_ref, cos_ref, sin_ref, a_ref, q_ref, k_ref, v_ref):
    h = _rms(x_ref[...], g_ref[...]).astype(BF16)
    o1 = 2 * MIX_WIDTH
    o2 = o1 + ATTN_WIDTH
    o3 = o2 + KV_WIDTH
    a_ref[...] = _dot(h, w_ref[:, :o1])
    cos = cos_ref[...]
    sin = sin_ref[...]
    q = _rope(_dot(h, w_ref[:, o1:o2]), cos, sin, HEAD_DIM, HEAD_DIM // 2)
    q_ref[...] = (q * (HEAD_DIM ** -0.5)).astype(BF16)
    k = _rope(_dot(h, w_ref[:, o2:o3]), cos[:, :KV_WIDTH], sin[:, :KV_WIDTH], HEAD_DIM, HEAD_DIM // 2)
    k_ref[...] = k.astype(BF16)
    v_ref[...] = _dot(h, w_ref[:, o3:]).astype(BF16)


def _inproj_even(x, g, w, cos, sin, seq):
    t = x.shape[0]
    tm = ROW_TILE
    nseq = seq // tm
    in_w = w.shape[1]
    row = lambda i: (i, 0)
    fixed = lambda i: (0, 0)
    pos = lambda i: (i % nseq, 0)
    return pl.pallas_call(
        _inproj_even_kernel,
        grid=(t // tm,),
        in_specs=[
            pl.BlockSpec((tm, D_MODEL), row),
            pl.BlockSpec((1, D_MODEL), fixed),
            pl.BlockSpec((D_MODEL, in_w), fixed),
            pl.BlockSpec((tm, ATTN_WIDTH), pos),
            pl.BlockSpec((tm, ATTN_WIDTH), pos),
        ],
        out_specs=[
            pl.BlockSpec((tm, 2 * MIX_WIDTH), row),
            pl.BlockSpec((tm, ATTN_WIDTH), row),
            pl.BlockSpec((tm, KV_WIDTH), row),
            pl.BlockSpec((tm, KV_WIDTH), row),
        ],
        out_shape=[
            jax.ShapeDtypeStruct((t, 2 * MIX_WIDTH), F32),
            jax.ShapeDtypeStruct((t, ATTN_WIDTH), BF16),
            jax.ShapeDtypeStruct((t, KV_WIDTH), BF16),
            jax.ShapeDtypeStruct((t, KV_WIDTH), BF16),
        ],
        compiler_params=_cparams(("parallel",)),
        name="inproj_even",
    )(x, g, w, cos, sin)


def _head_sumsq(x):
    n = x.shape[-1]
    r = lax.broadcasted_iota(jnp.int32, (n, n), 0) // HEAD_DIM
    c = lax.broadcasted_iota(jnp.int32, (n, n), 1) // HEAD_DIM
    ones = jnp.where(r == c, 1.0, 0.0).astype(BF16)
    sq = x * x
    hi = sq.astype(BF16)
    lo = (sq - hi.astype(F32)).astype(BF16)
    return _dot(hi, ones) + _dot(lo, ones)


def _inproj_odd_kernel(x_ref, g_ref, w_ref, qg_ref, kg_ref, cos_ref, sin_ref, q_ref, k_ref, vt_ref, d_ref):
    h = _rms(x_ref[...], g_ref[...]).astype(BF16)
    c1 = ATTN_WIDTH
    c2 = c1 + KV_WIDTH
    c3 = c2 + KV_WIDTH
    cos = cos_ref[...]
    sin = sin_ref[...]
    half = HEAD_DIM // 2
    q = _dot(h, w_ref[:, :c1])
    q = q * lax.rsqrt(_head_sumsq(q) * (1.0 / HEAD_DIM) + EPS) * qg_ref[...]
    q = _rope(q, cos, sin, half, half // 2)
    q_ref[...] = (q * (HEAD_DIM ** -0.5)).astype(BF16)
    k = _dot(h, w_ref[:, c1:c2])
    k = k * lax.rsqrt(_head_sumsq(k) * (1.0 / HEAD_DIM) + EPS) * kg_ref[...]
    k = _rope(k, cos[:, :KV_WIDTH], sin[:, :KV_WIDTH], half, half // 2).astype(BF16)
    v_t = _dot(h, w_ref[:, c2:c3]).T.astype(BF16)
    for hh in range(N_KV_HEADS):
        k_ref[hh] = k[:, hh * HEAD_DIM:(hh + 1) * HEAD_DIM]
        vt_ref[hh] = v_t[hh * HEAD_DIM:(hh + 1) * HEAD_DIM, :]
    d_ref[...] = _dot(h, w_ref[:, c3:])


def _inproj_odd(x, g, w, qg, kg, cos, sin, seq):
    t = x.shape[0]
    tm = ROW_TILE
    nseq = seq // tm
    in_w = w.shape[1]
    row = lambda i: (i, 0)
    fixed = lambda i: (0, 0)
    pos = lambda i: (i % nseq, 0)
    kv = lambda i: (0, i, 0)
    return pl.pallas_call(
        _inproj_odd_kernel,
        grid=(t // tm,),
        in_specs=[
            pl.BlockSpec((tm, D_MODEL), row),
            pl.BlockSpec((1, D_MODEL), fixed),
            pl.BlockSpec((D_MODEL, in_w), fixed),
            pl.BlockSpec((1, ATTN_WIDTH), fixed),
            pl.BlockSpec((1, KV_WIDTH), fixed),
            pl.BlockSpec((tm, ATTN_WIDTH), pos),
            pl.BlockSpec((tm, ATTN_WIDTH), pos),
        ],
        out_specs=[
            pl.BlockSpec((tm, ATTN_WIDTH), row),
            pl.BlockSpec((N_KV_HEADS, tm, HEAD_DIM), kv),
            pl.BlockSpec((N_KV_HEADS, HEAD_DIM, tm), lambda i: (0, 0, i)),
            pl.BlockSpec((tm, 2 * MIX_WIDTH), row),
        ],
        out_shape=[
            jax.ShapeDtypeStruct((t, ATTN_WIDTH), BF16),
            jax.ShapeDtypeStruct((N_KV_HEADS, t, HEAD_DIM), BF16),
            jax.ShapeDtypeStruct((N_KV_HEADS, HEAD_DIM, t), BF16),
            jax.ShapeDtypeStruct((t, 2 * MIX_WIDTH), F32),
        ],
        compiler_params=_cparams(("parallel",)),
        name="inproj_odd",
    )(x, g, w, qg, kg, cos, sin)


def _glu(x):
    return x[:, :MIX_WIDTH] * jax.nn.sigmoid(x[:, MIX_WIDTH:])


def _conv_kernel(prev_ref, cur_ref, next_ref, w_ref, b_ref, g_ref, beta_ref, o_ref, buf_ref):
    i = pl.program_id(1)
    ns = pl.num_programs(1)
    ts = cur_ref.shape[0]
    halo = CONV_HALO
    buf_ref[0:halo, :] = jnp.where(i > 0, _glu(prev_ref[...]), 0.0)
    buf_ref[halo:halo + ts, :] = _glu(cur_ref[...])
    buf_ref[halo + ts:, :] = jnp.where(i < ns - 1, _glu(next_ref[...]), 0.0)
    w = w_ref[...]
    bias = b_ref[...]
    g = g_ref[...]
    beta = beta_ref[...]
    first = halo - CONV_WIDTH // 2
    for c in range(ts // CONV_ROWS):
        r0 = c * CONV_ROWS
        acc = jnp.zeros((CONV_ROWS, MIX_WIDTH), F32)
        for k in range(CONV_WIDTH):
            acc = acc + w[k:k + 1, :] * buf_ref[r0 + first + k:r0 + first + k + CONV_ROWS, :]
        y = _layernorm(acc + bias, g, beta)
        o_ref[r0:r0 + CONV_ROWS, :] = (y * jax.nn.sigmoid(y)).astype(o_ref.dtype)


def _conv_module(a_in, w, b, g, beta, batch, seq):
    t = a_in.shape[0]
    ts = ROW_TILE
    ns = seq // ts
    hb = ts // CONV_HALO
    last = t // CONV_HALO - 1
    fixed = lambda bb, i: (0, 0)
    return pl.pallas_call(
        _conv_kernel,
        grid=(batch, ns),
        in_specs=[
            pl.BlockSpec((CONV_HALO, 2 * MIX_WIDTH), lambda bb, i: (jnp.maximum((bb * ns + i) * hb - 1, 0), 0)),
            pl.BlockSpec((ts, 2 * MIX_WIDTH), lambda bb, i: (bb * ns + i, 0)),
            pl.BlockSpec((CONV_HALO, 2 * MIX_WIDTH), lambda bb, i: (jnp.minimum((bb * ns + i + 1) * hb, last), 0)),
            pl.BlockSpec((CONV_WIDTH, MIX_WIDTH), fixed),
            pl.BlockSpec((1, MIX_WIDTH), fixed),
            pl.BlockSpec((1, MIX_WIDTH), fixed),
            pl.BlockSpec((1, MIX_WIDTH), fixed),
        ],
        out_specs=pl.BlockSpec((ts, MIX_WIDTH), lambda bb, i: (bb * ns + i, 0)),
        out_shape=jax.ShapeDtypeStruct((t, MIX_WIDTH), BF16),
        scratch_shapes=[pltpu.VMEM((ts + 2 * CONV_HALO, MIX_WIDTH), F32)],
        compiler_params=_cparams(("parallel", "parallel")),
        name="conv_module",
    )(a_in, a_in, a_in, w, b, g, beta)


def _swa_kernel(q_ref, kp_ref, kc_ref, kn_ref, vp_ref, vc_ref, vn_ref, sink_ref, o_ref):
    n = pl.program_id(1)
    nsteps = pl.num_programs(1)
    bq = BLOCK_Q
    cols = GQA_GROUPS * bq
    jj = lax.broadcasted_iota(jnp.int32, (3 * bq, cols), 0)
    ii = lax.broadcasted_iota(jnp.int32, (3 * bq, cols), 1) % bq
    band = (jj >= ii) & (jj <= ii + 2 * WINDOW)
    kcat = jnp.concatenate([kp_ref[...], kc_ref[...], kn_ref[...]], axis=0)
    vcat_t = jnp.concatenate([vp_ref[...], vc_ref[...], vn_ref[...]], axis=0).astype(F32).T.astype(BF16)
    for qb in range(SWA_BLOCKS):
        mask = band
        if qb == 0:
            mask = mask & (jj >= jnp.where(n > 0, 0, bq))
        if qb == SWA_BLOCKS - 1:
            mask = mask & (jj < jnp.where(n < nsteps - 1, 3 * bq, 2 * bq))
        q_t = q_ref[qb * bq:(qb + 1) * bq, :].astype(F32).T.astype(BF16)
        outs = []
        for hk in range(N_KV_HEADS):
            heads = range(hk * GQA_GROUPS, (hk + 1) * GQA_GROUPS)
            kh = kcat[qb * bq:(qb + 3) * bq, hk * HEAD_DIM:(hk + 1) * HEAD_DIM]
            vh_t = jnp.concatenate([vcat_t[hk * HEAD_DIM:(hk + 1) * HEAD_DIM, qb * bq:(qb + 3) * bq],
                                    jnp.ones((BF16_ROWS, 3 * bq), BF16)], axis=0)
            qs_t = jnp.concatenate([q_t[hd * HEAD_DIM:(hd + 1) * HEAD_DIM, :] for hd in heads], axis=1)
            sink = jnp.concatenate([jnp.broadcast_to(sink_ref[hd:hd + 1, 0:1], (1, bq)) for hd in heads], axis=1)
            s = jnp.where(mask, _dot(kh, qs_t), NEG)
            m = jnp.maximum(jnp.max(s, axis=0, keepdims=True), sink)
            o_ext = _dot(vh_t, jnp.exp(s - m).astype(BF16))
            denom = o_ext[HEAD_DIM:HEAD_DIM + 1, :] + jnp.exp(sink - m)
            o_t = o_ext[:HEAD_DIM, :] / denom
            outs.extend(o_t[:, g * bq:(g + 1) * bq] for g in range(GQA_GROUPS))
        o_ref[qb * bq:(qb + 1) * bq, :] = jnp.concatenate(outs, axis=0).T.astype(o_ref.dtype)


def _swa(q, k, v, sink_b, batch, seq):
    t = q.shape[0]
    bq = BLOCK_Q
    tq = SWA_BLOCKS * bq
    ns = seq // tq
    last = t // bq - 1
    cur = lambda bb, n: (bb * ns + n, 0)
    prev = lambda bb, n: (jnp.maximum((bb * ns + n) * SWA_BLOCKS - 1, 0), 0)
    nxt = lambda bb, n: (jnp.minimum((bb * ns + n + 1) * SWA_BLOCKS, last), 0)
    halo = lambda im: pl.BlockSpec((bq, KV_WIDTH), im)
    body = pl.BlockSpec((tq, KV_WIDTH), cur)
    return pl.pallas_call(
        _swa_kernel,
        grid=(batch, ns),
        in_specs=[pl.BlockSpec((tq, ATTN_WIDTH), cur), halo(prev), body, halo(nxt), halo(prev), body, halo(nxt),
                  pl.BlockSpec((N_Q_HEADS, LANE), lambda bb, n: (0, 0))],
        out_specs=pl.BlockSpec((tq, ATTN_WIDTH), cur),
        out_shape=jax.ShapeDtypeStruct((t, ATTN_WIDTH), BF16),
        compiler_params=_cparams(("parallel", "parallel")),
        name="window_attn",
    )(q, k, k, k, v, v, v, sink_b)


def _dense_attn_kernel(q_ref, k_ref, vt_ref, o_ref):
    ts = q_ref.shape[0] // ATT_CHAINS
    k = k_ref[...]
    v_ext = jnp.concatenate([vt_ref[...], jnp.ones((BF16_ROWS, vt_ref.shape[1]), BF16)], axis=0)
    q_t = q_ref[...].astype(F32).T.astype(BF16)

    def scores(c):
        cols = slice(c * ts, (c + 1) * ts)
        qs_t = jnp.concatenate([q_t[g * HEAD_DIM:(g + 1) * HEAD_DIM, cols] for g in range(GQA_GROUPS)], axis=1)
        return _dot(k, qs_t)

    s_next = scores(0)
    for c in range(ATT_CHAINS):
        cols = slice(c * ts, (c + 1) * ts)
        s = s_next
        if c + 1 < ATT_CHAINS:
            s_next = scores(c + 1)
        p = jnp.exp(s - jnp.max(s, axis=0, keepdims=True))
        o_ext = _dot(v_ext, p.astype(BF16))
        o_t = o_ext[:HEAD_DIM, :] / o_ext[HEAD_DIM:HEAD_DIM + 1, :]
        o_ref[cols, :] = jnp.concatenate(
            [o_t[:, g * ts:(g + 1) * ts] for g in range(GQA_GROUPS)], axis=0).T.astype(o_ref.dtype)


def _dense_attn(q, k, v, batch, seq):
    t = q.shape[0]
    tq = min(ATT_TQ, seq)
    nq = seq // tq
    gw = GQA_GROUPS * HEAD_DIM
    qmap = lambda bb, hk, qi: (bb * nq + qi, hk)
    return pl.pallas_call(
        _dense_attn_kernel,
        grid=(batch, N_KV_HEADS, nq),
        in_specs=[
            pl.BlockSpec((tq, gw), qmap),
            pl.BlockSpec((None, seq, HEAD_DIM), lambda bb, hk, qi: (hk, bb, 0)),
            pl.BlockSpec((None, HEAD_DIM, seq), lambda bb, hk, qi: (hk, 0, bb)),
        ],
        out_specs=pl.BlockSpec((tq, gw), qmap),
        out_shape=jax.ShapeDtypeStruct((t, ATTN_WIDTH), BF16),
        compiler_params=_cparams(("parallel", "parallel", "parallel")),
        name="dense_attn",
    )(q, k, v)


def _sgu_kernel(d_ref, g_ref, b_ref, w_ref, bs_ref, o_ref):
    z = jax.nn.gelu(d_ref[...])
    u = z[:, :MIX_WIDTH]
    v = _layernorm(z[:, MIX_WIDTH:], g_ref[...], b_ref[...]).astype(BF16)
    gc = MIX_WIDTH // SGU_GROUPS
    for n in range(d_ref.shape[0] // CHUNK):
        rows = slice(n * CHUNK, (n + 1) * CHUNK)
        for g in range(SGU_GROUPS):
            cols = slice(g * gc, (g + 1) * gc)
            mixed = _dot(w_ref[g], v[rows, cols]) + bs_ref[g]
            o_ref[rows, cols] = (u[rows, cols] * mixed).astype(o_ref.dtype)


def _sgu(d_in, g, b, w, bs):
    t = d_in.shape[0]
    ts = ROW_TILE
    gc = MIX_WIDTH // SGU_GROUPS
    row = lambda i: (i, 0)
    fixed = lambda i: (0, 0)
    fixed3 = lambda i: (0, 0, 0)
    return pl.pallas_call(
        _sgu_kernel,
        grid=(t // ts,),
        in_specs=[
            pl.BlockSpec((ts, 2 * MIX_WIDTH), row),
            pl.BlockSpec((1, MIX_WIDTH), fixed),
            pl.BlockSpec((1, MIX_WIDTH), fixed),
            pl.BlockSpec((SGU_GROUPS, CHUNK, CHUNK), fixed3),
            pl.BlockSpec((SGU_GROUPS, CHUNK, gc), fixed3),
        ],
        out_specs=pl.BlockSpec((ts, MIX_WIDTH), row),
        out_shape=jax.ShapeDtypeStruct((t, MIX_WIDTH), BF16),
        compiler_params=_cparams(("parallel",)),
        name="spatial_gating",
    )(d_in, g, b, w, bs)


def _outproj_kernel(x_ref, a_ref, b_ref, w_ref, g_ref, xn_ref, hnt_ref):
    y = _dot(a_ref[...], w_ref[:MIX_WIDTH, :]) + _dot(b_ref[...], w_ref[MIX_WIDTH:, :])
    xn = x_ref[...] + y
    xn_ref[...] = xn
    hnt_ref[...] = _rms(xn, g_ref[...]).T.astype(BF16)


def _outproj(x, a, b, w, g):
    t = x.shape[0]
    tm = ROW_TILE
    row = lambda i: (i, 0)
    fixed = lambda i: (0, 0)
    return pl.pallas_call(
        _outproj_kernel,
        grid=(t // tm,),
        in_specs=[
            pl.BlockSpec((tm, D_MODEL), row),
            pl.BlockSpec((tm, MIX_WIDTH), row),
            pl.BlockSpec((tm, MIX_WIDTH), row),
            pl.BlockSpec((2 * MIX_WIDTH, D_MODEL), fixed),
            pl.BlockSpec((1, D_MODEL), fixed),
        ],
        out_specs=[pl.BlockSpec((tm, D_MODEL), row), pl.BlockSpec((D_MODEL, tm), lambda i: (0, i))],
        out_shape=[jax.ShapeDtypeStruct((t, D_MODEL), F32), jax.ShapeDtypeStruct((D_MODEL, t), BF16)],
        compiler_params=_cparams(("parallel",)),
        name="outproj",
    )(x, a, b, w, g)


_PEER_PAIRS = [(i, j) for i in range(PEER_TOPK) for j in range(PEER_TOPK) if (i + 1) * (j + 1) <= PEER_TOPK]
_PEER_CAND_ROWS = -(-len(_PEER_PAIRS) // 8) * 8


SUBLANES = 8


def _bitonic_merge_desc(v, lo, n):
    step = n // 2
    while step >= 1:
        for i in range(lo, lo + n):
            if (i - lo) & step == 0:
                hi_, lo_ = jnp.maximum(v[i], v[i + step]), jnp.minimum(v[i], v[i + step])
                v[i], v[i + step] = hi_, lo_
        step //= 2


def _sort16_desc(v):
    n = len(v)
    size = 2
    while size <= n:
        for lo in range(0, n, size):
            half = size // 2
            v[lo + half:lo + size] = v[lo + half:lo + size][::-1]
            _bitonic_merge_desc(v, lo, size)
        size *= 2
    return v


def _sublane_all(x, op):
    for shift in (4, 2, 1):
        x = op(x, pltpu.roll(x, shift, 0))
    return x


def _top16_sorted(s):
    v = _sort16_desc([s[k * SUBLANES:(k + 1) * SUBLANES, :] for k in range(PEER_NKEYS // SUBLANES)])
    for shift in (4, 2, 1):
        v = [jnp.maximum(v[k], pltpu.roll(v[PEER_TOPK - 1 - k], shift, 0)) for k in range(PEER_TOPK)]
        _bitonic_merge_desc(v, 0, PEER_TOPK)
    return v


def _peer_select(h, s1_all, s2_all, n_ref, e1_ref, r2_ref, e2_ref):
    tt = s1_all.shape[-1]
    nv = PEER_NKEYS // SUBLANES
    sub = lax.broadcasted_iota(jnp.int32, (SUBLANES, LANE), 0)
    for tc in range(tt // LANE):
        lanes = slice(tc * LANE, (tc + 1) * LANE)
        s1 = s1_all[:, lanes]
        s2 = s2_all[:, lanes]
        v1 = _top16_sorted(s1)
        v2 = _top16_sorted(s2)
        pair_sum = {p: v1[p[0]] + v2[p[1]] for p in _PEER_PAIRS}
        cand = []
        for g in range(_PEER_CAND_ROWS // SUBLANES):
            c = jnp.full((SUBLANES, LANE), -jnp.inf, F32)
            for r, p in enumerate(_PEER_PAIRS[g * SUBLANES:(g + 1) * SUBLANES]):
                c = jnp.where(sub == r, pair_sum[p], c)
            cand.append(c)
        f = []
        for _ in range(PEER_TOPK):
            mx = cand[0]
            for c in cand[1:]:
                mx = jnp.maximum(mx, c)
            mx = _sublane_all(mx, jnp.maximum)
            f.append(mx)
            cand = [jnp.where(c == mx, -jnp.inf, c) for c in cand]
        thr = f[PEER_TOPK - 1]
        z = jnp.ones((SUBLANES, LANE), F32)
        for r in range(1, PEER_TOPK):
            z = z + jnp.exp(f[r] - f[0])
        rz = 1.0 / z
        cnt = []
        for i in range(PEER_TOPK):
            c = jnp.zeros((SUBLANES, LANE), F32)
            for j in range(PEER_TOPK // (i + 1)):
                c = c + jnp.where(pair_sum[(i, j)] >= thr, 1.0, 0.0)
            cnt.append(c)
        ranks = []
        e2s = []
        for k in range(nv):
            rows = slice(k * SUBLANES, (k + 1) * SUBLANES)
            a = s1[rows, :]
            b = s2[rows, :]
            n = jnp.zeros((SUBLANES, LANE), F32)
            rank = jnp.full((SUBLANES, LANE), float(PEER_TOPK), F32)
            for r in range(PEER_TOPK):
                n = jnp.where(a == v1[r], cnt[r], n)
                rank = jnp.where(b == v2[r], float(r), rank)
            n_ref[h, rows, lanes] = n
            e1_ref[h, rows, lanes] = jnp.exp(a - v1[0]) * rz
            ranks.append(rank)
            e2s.append(jnp.exp(b - v2[0]))
        r2_ref[h, :, lanes] = jnp.concatenate(ranks, axis=0).astype(r2_ref.dtype)
        e2_ref[h, :, lanes] = jnp.concatenate(e2s, axis=0).astype(e2_ref.dtype)


def _peer_gates(step, sub, n_ref, e1_ref, r2_ref, e2_ref, g_ref):
    tt = g_ref.shape[-1]
    gdt = g_ref.dtype
    a8 = pl.multiple_of(step * PEER_A + sub * PEER_SUB, SUBLANES)
    for tc in range(tt // LANE):
        lanes = slice(tc * LANE, (tc + 1) * LANE)
        nrows = [n_ref[h, pl.ds(a8, PEER_SUB), lanes].astype(gdt) for h in range(PEER_HEADS)]
        erows = [e1_ref[h, pl.ds(a8, PEER_SUB), lanes].astype(gdt) for h in range(PEER_HEADS)]
        for blk in range(PEER_SUB // PEER_GATE_BLOCK):
            gates = [jnp.zeros((PEER_NKEYS, LANE), gdt) for _ in range(PEER_GATE_BLOCK)]
            for h in range(PEER_HEADS):
                r2 = r2_ref[h, :, lanes]
                e2 = e2_ref[h, :, lanes]
                for k in range(PEER_GATE_BLOCK):
                    al = blk * PEER_GATE_BLOCK + k
                    keep = r2 < nrows[h][al:al + 1, :]
                    gates[k] = gates[k] + jnp.where(keep, e2, 0.0) * erows[h][al:al + 1, :]
            for k in range(PEER_GATE_BLOCK):
                al = blk * PEER_GATE_BLOCK + k
                g_ref[sub, al * PEER_NKEYS:(al + 1) * PEER_NKEYS, lanes] = gates[k]


def _peer_kernel(x_ref, hnt_ref, wqt_ref, sk_ref, u_ref, vt_ref, o_ref,
                 qt_ref, n_ref, e1_ref, r2_ref, e2_ref, g_ref, acc_ref, ht_ref):
    j = pl.program_id(1)
    nj = pl.num_programs(1)
    half = PEER_DKEY // 2
    gdt = r2_ref.dtype
    gate_refs = (n_ref, e1_ref, r2_ref, e2_ref, g_ref)

    @pl.when(j == 0)
    def _():
        qt_ref[...] = _dot(wqt_ref[...], hnt_ref[...]).astype(BF16)

        def body(h, carry):
            r0 = pl.multiple_of(h * PEER_DKEY, PEER_DKEY)
            s1 = _dot(sk_ref[2 * h], qt_ref[pl.ds(r0, half), :])
            s2 = _dot(sk_ref[2 * h + 1], qt_ref[pl.ds(r0 + half, half), :])
            _peer_select(h, s1, s2, n_ref, e1_ref, r2_ref, e2_ref)
            return carry

        lax.fori_loop(0, PEER_HEADS, body, 0)
        acc_ref[...] = jnp.zeros(acc_ref.shape, F32)

    sub_rows = lambda sub: slice(sub * PEER_SUB * PEER_NKEYS, (sub + 1) * PEER_SUB * PEER_NKEYS)
    for sub in range(PEER_A // PEER_SUB):
        _peer_gates(j, sub, *gate_refs)
    for sub in range(PEER_A // PEER_SUB):
        at = _dot(u_ref[sub_rows(sub), :], hnt_ref[...]).astype(gdt)
        ht_ref[sub_rows(sub), :] = jax.nn.gelu(at) * g_ref[sub]
    acc_ref[...] += _dot(vt_ref[...], ht_ref[...])

    @pl.when(j == nj - 1)
    def _():
        o_ref[...] = x_ref[...] + acc_ref[...].T


def _peer(x, hnt, wqt, sk, u, vt):
    t = x.shape[0]
    tt = PEER_TT
    ne = u.shape[0]
    et = PEER_A * PEER_NKEYS
    hk = PEER_HEADS * PEER_DKEY
    sel = lambda dt: pltpu.VMEM((PEER_HEADS, PEER_NKEYS, tt), dt)
    return pl.pallas_call(
        _peer_kernel,
        grid=(t // tt, ne // et),
        in_specs=[
            pl.BlockSpec((tt, D_MODEL), lambda i, j: (i, 0)),
            pl.BlockSpec((D_MODEL, tt), lambda i, j: (0, i)),
            pl.BlockSpec((hk, D_MODEL), lambda i, j: (0, 0), pipeline_mode=pl.Buffered(1)),
            pl.BlockSpec((2 * PEER_HEADS, PEER_NKEYS, PEER_DKEY // 2), lambda i, j: (0, 0, 0)),
            pl.BlockSpec((et, D_MODEL), lambda i, j: (j, 0)),
            pl.BlockSpec((D_MODEL, et), lambda i, j: (0, j)),
        ],
        out_specs=pl.BlockSpec((tt, D_MODEL), lambda i, j: (i, 0)),
        out_shape=jax.ShapeDtypeStruct((t, D_MODEL), F32),
        scratch_shapes=[
            pltpu.VMEM((hk, tt), BF16),
            sel(F32), sel(F32), sel(PEER_GATE_DTYPE), sel(PEER_GATE_DTYPE),
            pltpu.VMEM((PEER_A // PEER_SUB, PEER_SUB * PEER_NKEYS, tt), PEER_GATE_DTYPE),
            pltpu.VMEM((D_MODEL, tt), F32),
            pltpu.VMEM((et, tt), BF16),
        ],
        compiler_params=_cparams(("parallel", "arbitrary")),
        name="peer",
    )(x, hnt, wqt, sk, u, vt)


def _final_norm_kernel(x_ref, g_ref, o_ref):
    o_ref[...] = _rms(x_ref[...], g_ref[...])


def _final_norm(x, g):
    t = x.shape[0]
    tm = ROW_TILE
    return pl.pallas_call(
        _final_norm_kernel,
        grid=(t // tm,),
        in_specs=[pl.BlockSpec((tm, D_MODEL), lambda i: (i, 0)), pl.BlockSpec((1, D_MODEL), lambda i: (0, 0))],
        out_specs=pl.BlockSpec((tm, D_MODEL), lambda i: (i, 0)),
        out_shape=jax.ShapeDtypeStruct((t, D_MODEL), F32),
        compiler_params=_cparams(("parallel",)),
        name="final_norm",
    )(x, g)


def _rope_table(pos, dim):
    inv = ROPE_THETA ** (-jnp.arange(0, dim, 2, dtype=F32) / dim)
    ang = pos.astype(F32)[:, None] * inv[None, :]
    ang = jnp.concatenate([ang, ang], axis=-1)
    sign = jnp.concatenate([-jnp.ones((dim // 2,), F32), jnp.ones((dim // 2,), F32)])
    return jnp.cos(ang), jnp.sin(ang) * sign


def _rope_tables(seq):
    pos = jnp.arange(seq)
    cos1, sin1 = _rope_table(pos, HEAD_DIM)
    cr, sr = _rope_table(pos // GRID_W, HEAD_DIM // 2)
    cc, sc = _rope_table(pos % GRID_W, HEAD_DIM // 2)
    cos2 = jnp.concatenate([cr, cc], axis=-1)
    sin2 = jnp.concatenate([sr, sc], axis=-1)
    tile = lambda a: jnp.tile(a, (1, N_Q_HEADS))
    return tile(cos1), tile(sin1), tile(cos2), tile(sin2)


def kernel(x, mix_norm_g, ffn_norm_g, final_norm_g, even_w_in, even_w_out, conv_w, conv_b, conv_ln_g, conv_ln_b, sink_logits, odd_w_in, odd_w_out, q_norm_g, k_norm_g, sgu_ln_g, sgu_ln_b, sgu_w, sgu_b, peer_wq, peer_subkeys, peer_u, peer_v):
    batch, seq, d = x.shape
    depth = mix_norm_g.shape[0]
    t = batch * seq
    cos1, sin1, cos2, sin2 = _rope_tables(seq)
    row = lambda a: a.reshape(1, -1).astype(F32)
    xf = x.reshape(t, d)
    for layer in range(depth):
        i = layer // 2
        if layer % 2 == 0:
            a_in, q, k, v = _inproj_even(xf, row(mix_norm_g[layer]), even_w_in[i].astype(BF16), cos1, sin1, seq)
            m1 = _conv_module(a_in, conv_w[i, :, 0, :], row(conv_b[i]), row(conv_ln_g[i]), row(conv_ln_b[i]), batch, seq)
            sink_b = jnp.broadcast_to(sink_logits[i].astype(F32)[:, None], (N_Q_HEADS, LANE))
            m2 = _swa(q, k, v, sink_b, batch, seq)
            w_out = even_w_out[i]
        else:
            q, k, v, d_in = _inproj_odd(xf, row(mix_norm_g[layer]), odd_w_in[i].astype(BF16),
                                        row(jnp.tile(q_norm_g[i], N_Q_HEADS)), row(jnp.tile(k_norm_g[i], N_KV_HEADS)),
                                        cos2, sin2, seq)
            m1 = _dense_attn(q, k, v, batch, seq)
            bs = jnp.broadcast_to(sgu_b[i].astype(F32)[:, :, None], (SGU_GROUPS, CHUNK, MIX_WIDTH // SGU_GROUPS))
            m2 = _sgu(d_in, row(sgu_ln_g[i]), row(sgu_ln_b[i]), sgu_w[i].astype(BF16), bs)
            w_out = odd_w_out[i]
        xn, hnt = _outproj(xf, m1, m2, w_out.astype(BF16), row(ffn_norm_g[layer]))
        sk = peer_subkeys[layer].reshape(2 * PEER_HEADS, PEER_NKEYS, PEER_DKEY // 2).astype(BF16)
        xf = _peer(xn, hnt, peer_wq[layer].T.astype(BF16), sk, peer_u[layer].astype(BF16),
                   peer_v[layer].T.astype(BF16))
    return _final_norm(xf, row(final_norm_g)).reshape(batch, seq, d)
```

```python
import functools
import math

import jax
import jax.numpy as jnp
from jax import lax
from jax.experimental import pallas as pl
from jax.experimental.pallas import tpu as pltpu

F32 = jnp.float32
BF16 = jnp.bfloat16

D_MODEL = 1024
HEAD_DIM = 64
N_Q_HEADS = 8
N_KV_HEADS = 2
GQA_GROUPS = N_Q_HEADS // N_KV_HEADS
ATTN_WIDTH = N_Q_HEADS * HEAD_DIM
KV_WIDTH = N_KV_HEADS * HEAD_DIM
MIX_WIDTH = D_MODEL // 2
CONV_WIDTH = 31
CONV_HALO = 16
SGU_GROUPS = 4
CHUNK = 128
WINDOW = 128
BLOCK_Q = 128
GRID_W = 64
ROPE_THETA = 10000.0
PEER_HEADS = 8
PEER_NKEYS = 128
PEER_DKEY = 128
PEER_TOPK = 16
EPS = 1e-6
NEG = -1e30

VMEM_LIMIT_BYTES = 56 * 1024 * 1024

ROW_TILE = 512
CONV_ROWS = 64
SWA_BLOCKS = 4
ATT_TQ = 1024
ATT_CHAINS = 8
PEER_TT = 512
PEER_A = 16
PEER_SUB = 8
PEER_GATE_BLOCK = 4
PEER_GATE_DTYPE = BF16
LANE = 128
BF16_ROWS = 16


def _cparams(sem):
    return pltpu.CompilerParams(dimension_semantics=sem, vmem_limit_bytes=VMEM_LIMIT_BYTES)


def _dot(a, b):
    return jnp.dot(a, b, preferred_element_type=F32)


def _dot_nt(a, b):
    return lax.dot_general(a, b, (((1,), (1,)), ((), ())), preferred_element_type=F32)


def _rms(x, g):
    return x * lax.rsqrt(jnp.mean(x * x, axis=-1, keepdims=True) + EPS) * g


def _layernorm(x, g, b):
    mu = jnp.mean(x, axis=-1, keepdims=True)
    xc = x - mu
    var = jnp.mean(xc * xc, axis=-1, keepdims=True)
    return xc * lax.rsqrt(var + EPS) * g + b


def _rope(x, cos, sin_signed, group, half):
    n = x.shape[-1]
    lane = lax.broadcasted_iota(jnp.int32, x.shape, x.ndim - 1)
    first = (lane % group) < half
    partner = jnp.where(first, pltpu.roll(x, n - half, x.ndim - 1), pltpu.roll(x, half, x.ndim - 1))
    return x * cos + partner * sin_signed


def _inproj_even_kernel(x_ref, g_ref, w_ref, cos_ref, sin_ref, a_ref, q_ref, k_ref, v_ref):
    h = _rms(x_ref[...], g_ref[...]).astype(BF16)
    o1 = 2 * MIX_WIDTH
    o2 = o1 + ATTN_WIDTH
    o3 = o2 + KV_WIDTH
    a_ref[...] = _dot(h, w_ref[:, :o1])
    cos = cos_ref[...]
    sin = sin_ref[...]
    q = _rope(_dot(h, w_ref[:, o1:o2]), cos, sin, HEAD_DIM, HEAD_DIM // 2)
    q_ref[...] = (q * (HEAD_DIM ** -0.5)).astype(BF16)
    k = _rope(_dot(h, w_ref[:, o2:o3]), cos[:, :KV_WIDTH], sin[:, :KV_WIDTH], HEAD_DIM, HEAD_DIM // 2)
    k_ref[...] = k.astype(BF16)
    v_ref[...] = _dot(h, w_ref[:, o3:]).astype(BF16)


def _inproj_even(x, g, w, cos, sin, seq):
    t = x.shape[0]
    tm = ROW_TILE
    nseq = seq // tm
    in_w = w.shape[1]
    row = lambda i: (i, 0)
    fixed = lambda i: (0, 0)
    pos = lambda i: (i % nseq, 0)
    return pl.pallas_call(
        _inproj_even_kernel,
        grid=(t // tm,),
        in_specs=[
            pl.BlockSpec((tm, D_MODEL), row),
            pl.BlockSpec((1, D_MODEL), fixed),
            pl.BlockSpec((D_MODEL, in_w), fixed),
            pl.BlockSpec((tm, ATTN_WIDTH), pos),
            pl.BlockSpec((tm, ATTN_WIDTH), pos),
        ],
        out_specs=[
            pl.BlockSpec((tm, 2 * MIX_WIDTH), row),
            pl.BlockSpec((tm, ATTN_WIDTH), row),
            pl.BlockSpec((tm, KV_WIDTH), row),
            pl.BlockSpec((tm, KV_WIDTH), row),
        ],
        out_shape=[
            jax.ShapeDtypeStruct((t, 2 * MIX_WIDTH), F32),
            jax.ShapeDtypeStruct((t, ATTN_WIDTH), BF16),
            jax.ShapeDtypeStruct((t, KV_WIDTH), BF16),
            jax.ShapeDtypeStruct((t, KV_WIDTH), BF16),
        ],
        compiler_params=_cparams(("parallel",)),
        name="inproj_even",
    )(x, g, w, cos, sin)


def _head_sumsq(x):
    n = x.shape[-1]
    r = lax.broadcasted_iota(jnp.int32, (n, n), 0) // HEAD_DIM
    c = lax.broadcasted_iota(jnp.int32, (n, n), 1) // HEAD_DIM
    ones = jnp.where(r == c, 1.0, 0.0).astype(BF16)
    sq = x * x
    hi = sq.astype(BF16)
    lo = (sq - hi.astype(F32)).astype(BF16)
    return _dot(hi, ones) + _dot(lo, ones)


def _inproj_odd_kernel(x_ref, g_ref, w_ref, qg_ref, kg_ref, cos_ref, sin_ref, q_ref, k_ref, vt_ref, d_ref):
    h = _rms(x_ref[...], g_ref[...]).astype(BF16)
    c1 = ATTN_WIDTH
    c2 = c1 + KV_WIDTH
    c3 = c2 + KV_WIDTH
    cos = cos_ref[...]
    sin = sin_ref[...]
    half = HEAD_DIM // 2
    q = _dot(h, w_ref[:, :c1])
    q = q * lax.rsqrt(_head_sumsq(q) * (1.0 / HEAD_DIM) + EPS) * qg_ref[...]
    q = _rope(q, cos, sin, half, half // 2)
    q_ref[...] = (q * (HEAD_DIM ** -0.5)).astype(BF16)
    k = _dot(h, w_ref[:, c1:c2])
    k = k * lax.rsqrt(_head_sumsq(k) * (1.0 / HEAD_DIM) + EPS) * kg_ref[...]
    k = _rope(k, cos[:, :KV_WIDTH], sin[:, :KV_WIDTH], half, half // 2).astype(BF16)
    v_t = _dot(h, w_ref[:, c2:c3]).T.astype(BF16)
    for hh in range(N_KV_HEADS):
        k_ref[hh] = k[:, hh * HEAD_DIM:(hh + 1) * HEAD_DIM]
        vt_ref[hh] = v_t[hh * HEAD_DIM:(hh + 1) * HEAD_DIM, :]
    d_ref[...] = _dot(h, w_ref[:, c3:])


def _inproj_odd(x, g, w, qg, kg, cos, sin, seq):
    t = x.shape[0]
    tm = ROW_TILE
    nseq = seq // tm
    in_w = w.shape[1]
    row = lambda i: (i, 0)
    fixed = lambda i: (0, 0)
    pos = lambda i: (i % nseq, 0)
    kv = lambda i: (0, i, 0)
    return pl.pallas_call(
        _inproj_odd_kernel,
        grid=(t // tm,),
        in_specs=[
            pl.BlockSpec((tm, D_MODEL), row),
            pl.BlockSpec((1, D_MODEL), fixed),
            pl.BlockSpec((D_MODEL, in_w), fixed),
            pl.BlockSpec((1, ATTN_WIDTH), fixed),
            pl.BlockSpec((1, KV_WIDTH), fixed),
            pl.BlockSpec((tm, ATTN_WIDTH), pos),
            pl.BlockSpec((tm, ATTN_WIDTH), pos),
        ],
        out_specs=[
            pl.BlockSpec((tm, ATTN_WIDTH), row),
            pl.BlockSpec((N_KV_HEADS, tm, HEAD_DIM), kv),
            pl.BlockSpec((N_KV_HEADS, HEAD_DIM, tm), lambda i: (0, 0, i)),
            pl.BlockSpec((tm, 2 * MIX_WIDTH), row),
        ],
        out_shape=[
            jax.ShapeDtypeStruct((t, ATTN_WIDTH), BF16),
            jax.ShapeDtypeStruct((N_KV_HEADS, t, HEAD_DIM), BF16),
            jax.ShapeDtypeStruct((N_KV_HEADS, HEAD_DIM, t), BF16),
            jax.ShapeDtypeStruct((t, 2 * MIX_WIDTH), F32),
        ],
        compiler_params=_cparams(("parallel",)),
        name="inproj_odd",
    )(x, g, w, qg, kg, cos, sin)


def _glu(x):
    return x[:, :MIX_WIDTH] * jax.nn.sigmoid(x[:, MIX_WIDTH:])


def _conv_kernel(prev_ref, cur_ref, next_ref, w_ref, b_ref, g_ref, beta_ref, o_ref, buf_ref):
    i = pl.program_id(1)
    ns = pl.num_programs(1)
    ts = cur_ref.shape[0]
    halo = CONV_HALO
    buf_ref[0, 0:halo, :] = jnp.where(i > 0, _glu(prev_ref[...]), 0.0)
    buf_ref[0, halo:halo + ts, :] = _glu(cur_ref[...])
    buf_ref[0, halo + ts:, :] = jnp.where(i < ns - 1, _glu(next_ref[...]), 0.0)
    span = ts + 2 * halo - SUBLANES
    for s in range(1, SUBLANES):
        buf_ref[s, 0:span, :] = buf_ref[0, s:s + span, :]
    w = w_ref[...]
    bias = b_ref[...]
    g = g_ref[...]
    beta = beta_ref[...]
    first = halo - CONV_WIDTH // 2
    for c in range(ts // CONV_ROWS):
        r0 = c * CONV_ROWS
        acc = jnp.zeros((CONV_ROWS, MIX_WIDTH), F32)
        for k in range(CONV_WIDTH):
            m, s = divmod(first + k, SUBLANES)
            lo = r0 + SUBLANES * m
            acc = acc + w[k:k + 1, :] * buf_ref[s, lo:lo + CONV_ROWS, :]
        y = _layernorm(acc + bias, g, beta)
        o_ref[r0:r0 + CONV_ROWS, :] = (y * jax.nn.sigmoid(y)).astype(o_ref.dtype)


def _conv_module(a_in, w, b, g, beta, batch, seq):
    t = a_in.shape[0]
    ts = ROW_TILE
    ns = seq // ts
    hb = ts // CONV_HALO
    last = t // CONV_HALO - 1
    fixed = lambda bb, i: (0, 0)
    return pl.pallas_call(
        _conv_kernel,
        grid=(batch, ns),
        in_specs=[
            pl.BlockSpec((CONV_HALO, 2 * MIX_WIDTH), lambda bb, i: (jnp.maximum((bb * ns + i) * hb - 1, 0), 0)),
            pl.BlockSpec((ts, 2 * MIX_WIDTH), lambda bb, i: (bb * ns + i, 0)),
            pl.BlockSpec((CONV_HALO, 2 * MIX_WIDTH), lambda bb, i: (jnp.minimum((bb * ns + i + 1) * hb, last), 0)),
            pl.BlockSpec((CONV_WIDTH, MIX_WIDTH), fixed),
            pl.BlockSpec((1, MIX_WIDTH), fixed),
            pl.BlockSpec((1, MIX_WIDTH), fixed),
            pl.BlockSpec((1, MIX_WIDTH), fixed),
        ],
        out_specs=pl.BlockSpec((ts, MIX_WIDTH), lambda bb, i: (bb * ns + i, 0)),
        out_shape=jax.ShapeDtypeStruct((t, MIX_WIDTH), BF16),
        scratch_shapes=[pltpu.VMEM((SUBLANES, ts + 2 * CONV_HALO, MIX_WIDTH), F32)],
        compiler_params=_cparams(("parallel", "parallel")),
        name="conv_module",
    )(a_in, a_in, a_in, w, b, g, beta)


def _swa_kernel(q_ref, kp_ref, kc_ref, kn_ref, vp_ref, vc_ref, vn_ref, sink_ref, o_ref):
    n = pl.program_id(1)
    nsteps = pl.num_programs(1)
    bq = BLOCK_Q
    cols = GQA_GROUPS * bq
    jj = lax.broadcasted_iota(jnp.int32, (3 * bq, cols), 0)
    ii = lax.broadcasted_iota(jnp.int32, (3 * bq, cols), 1) % bq
    band = (jj >= ii) & (jj <= ii + 2 * WINDOW)
    kcat = jnp.concatenate([kp_ref[...], kc_ref[...], kn_ref[...]], axis=0)
    vcat_t = jnp.concatenate([vp_ref[...], vc_ref[...], vn_ref[...]], axis=0).astype(F32).T.astype(BF16)
    for qb in range(SWA_BLOCKS):
        mask = band
        if qb == 0:
            mask = mask & (jj >= jnp.where(n > 0, 0, bq))
        if qb == SWA_BLOCKS - 1:
            mask = mask & (jj < jnp.where(n < nsteps - 1, 3 * bq, 2 * bq))
        q_t = q_ref[qb * bq:(qb + 1) * bq, :].astype(F32).T.astype(BF16)
        outs = []
        for hk in range(N_KV_HEADS):
            heads = range(hk * GQA_GROUPS, (hk + 1) * GQA_GROUPS)
            kh = kcat[qb * bq:(qb + 3) * bq, hk * HEAD_DIM:(hk + 1) * HEAD_DIM]
            vh_t = jnp.concatenate([vcat_t[hk * HEAD_DIM:(hk + 1) * HEAD_DIM, qb * bq:(qb + 3) * bq],
                                    jnp.ones((BF16_ROWS, 3 * bq), BF16)], axis=0)
            qs_t = jnp.concatenate([q_t[hd * HEAD_DIM:(hd + 1) * HEAD_DIM, :] for hd in heads], axis=1)
            sink = jnp.concatenate([jnp.broadcast_to(sink_ref[hd:hd + 1, 0:1], (1, bq)) for hd in heads], axis=1)
            s = jnp.where(mask, _dot(kh, qs_t), NEG)
            m = jnp.maximum(jnp.max(s, axis=0, keepdims=True), sink)
            o_ext = _dot(vh_t, jnp.exp(s - m).astype(BF16))
            denom = o_ext[HEAD_DIM:HEAD_DIM + 1, :] + jnp.exp(sink - m)
            o_t = o_ext[:HEAD_DIM, :] / denom
            outs.extend(o_t[:, g * bq:(g + 1) * bq] for g in range(GQA_GROUPS))
        o_ref[qb * bq:(qb + 1) * bq, :] = jnp.concatenate(outs, axis=0).T.astype(o_ref.dtype)


def _swa(q, k, v, sink_b, batch, seq):
    t = q.shape[0]
    bq = BLOCK_Q
    tq = SWA_BLOCKS * bq
    ns = seq // tq
    last = t // bq - 1
    cur = lambda bb, n: (bb * ns + n, 0)
    prev = lambda bb, n: (jnp.maximum((bb * ns + n) * SWA_BLOCKS - 1, 0), 0)
    nxt = lambda bb, n: (jnp.minimum((bb * ns + n + 1) * SWA_BLOCKS, last), 0)
    halo = lambda im: pl.BlockSpec((bq, KV_WIDTH), im)
    body = pl.BlockSpec((tq, KV_WIDTH), cur)
    return pl.pallas_call(
        _swa_kernel,
        grid=(batch, ns),
        in_specs=[pl.BlockSpec((tq, ATTN_WIDTH), cur), halo(prev), body, halo(nxt), halo(prev), body, halo(nxt),
                  pl.BlockSpec((N_Q_HEADS, LANE), lambda bb, n: (0, 0))],
        out_specs=pl.BlockSpec((tq, ATTN_WIDTH), cur),
        out_shape=jax.ShapeDtypeStruct((t, ATTN_WIDTH), BF16),
        compiler_params=_cparams(("parallel", "parallel")),
        name="window_attn",
    )(q, k, k, k, v, v, v, sink_b)


def _dense_attn_kernel(q_ref, k_ref, vt_ref, o_ref):
    ts = q_ref.shape[0] // ATT_CHAINS
    k = k_ref[...]
    v_ext = jnp.concatenate([vt_ref[...], jnp.ones((BF16_ROWS, vt_ref.shape[1]), BF16)], axis=0)
    q_t = q_ref[...].astype(F32).T.astype(BF16)

    def scores(c):
        cols = slice(c * ts, (c + 1) * ts)
        qs_t = jnp.concatenate([q_t[g * HEAD_DIM:(g + 1) * HEAD_DIM, cols] for g in range(GQA_GROUPS)], axis=1)
        return _dot(k, qs_t)

    s_next = scores(0)
    for c in range(ATT_CHAINS):
        cols = slice(c * ts, (c + 1) * ts)
        s = s_next
        if c + 1 < ATT_CHAINS:
            s_next = scores(c + 1)
        p = jnp.exp(s - jnp.max(s, axis=0, keepdims=True))
        o_ext = _dot(v_ext, p.astype(BF16))
        o_t = o_ext[:HEAD_DIM, :] / o_ext[HEAD_DIM:HEAD_DIM + 1, :]
        o_ref[cols, :] = jnp.concatenate(
            [o_t[:, g * ts:(g + 1) * ts] for g in range(GQA_GROUPS)], axis=0).T.astype(o_ref.dtype)


def _dense_attn(q, k, v, batch, seq):
    t = q.shape[0]
    tq = min(ATT_TQ, seq)
    nq = seq // tq
    gw = GQA_GROUPS * HEAD_DIM
    qmap = lambda bb, hk, qi: (bb * nq + qi, hk)
    return pl.pallas_call(
        _dense_attn_kernel,
        grid=(batch, N_KV_HEADS, nq),
        in_specs=[
            pl.BlockSpec((tq, gw), qmap),
            pl.BlockSpec((None, seq, HEAD_DIM), lambda bb, hk, qi: (hk, bb, 0)),
            pl.BlockSpec((None, HEAD_DIM, seq), lambda bb, hk, qi: (hk, 0, bb)),
        ],
        out_specs=pl.BlockSpec((tq, gw), qmap),
        out_shape=jax.ShapeDtypeStruct((t, ATTN_WIDTH), BF16),
        compiler_params=_cparams(("parallel", "parallel", "parallel")),
        name="dense_attn",
    )(q, k, v)


def _sgu_kernel(d_ref, g_ref, b_ref, w_ref, bs_ref, o_ref):
    z = jax.nn.gelu(d_ref[...])
    u = z[:, :MIX_WIDTH]
    v = _layernorm(z[:, MIX_WIDTH:], g_ref[...], b_ref[...]).astype(BF16)
    gc = MIX_WIDTH // SGU_GROUPS
    for n in range(d_ref.shape[0] // CHUNK):
        rows = slice(n * CHUNK, (n + 1) * CHUNK)
        for g in range(SGU_GROUPS):
            cols = slice(g * gc, (g + 1) * gc)
            mixed = _dot(w_ref[g], v[rows, cols]) + bs_ref[g]
            o_ref[rows, cols] = (u[rows, cols] * mixed).astype(o_ref.dtype)


def _sgu(d_in, g, b, w, bs):
    t = d_in.shape[0]
    ts = ROW_TILE
    gc = MIX_WIDTH // SGU_GROUPS
    row = lambda i: (i, 0)
    fixed = lambda i: (0, 0)
    fixed3 = lambda i: (0, 0, 0)
    return pl.pallas_call(
        _sgu_kernel,
        grid=(t // ts,),
        in_specs=[
            pl.BlockSpec((ts, 2 * MIX_WIDTH), row),
            pl.BlockSpec((1, MIX_WIDTH), fixed),
            pl.BlockSpec((1, MIX_WIDTH), fixed),
            pl.BlockSpec((SGU_GROUPS, CHUNK, CHUNK), fixed3),
            pl.BlockSpec((SGU_GROUPS, CHUNK, gc), fixed3),
        ],
        out_specs=pl.BlockSpec((ts, MIX_WIDTH), row),
        out_shape=jax.ShapeDtypeStruct((t, MIX_WIDTH), BF16),
        compiler_params=_cparams(("parallel",)),
        name="spatial_gating",
    )(d_in, g, b, w, bs)


def _outproj_kernel(x_ref, a_ref, b_ref, w_ref, g_ref, xn_ref, hnt_ref):
    y = _dot(a_ref[...], w_ref[:MIX_WIDTH, :]) + _dot(b_ref[...], w_ref[MIX_WIDTH:, :])
    xn = x_ref[...] + y
    xn_ref[...] = xn
    hnt_ref[...] = _rms(xn, g_ref[...]).T.astype(BF16)


def _outproj(x, a, b, w, g):
    t = x.shape[0]
    tm = ROW_TILE
    row = lambda i: (i, 0)
    fixed = lambda i: (0, 0)
    return pl.pallas_call(
        _outproj_kernel,
        grid=(t // tm,),
        in_specs=[
            pl.BlockSpec((tm, D_MODEL), row),
            pl.BlockSpec((tm, MIX_WIDTH), row),
            pl.BlockSpec((tm, MIX_WIDTH), row),
            pl.BlockSpec((2 * MIX_WIDTH, D_MODEL), fixed),
            pl.BlockSpec((1, D_MODEL), fixed),
        ],
        out_specs=[pl.BlockSpec((tm, D_MODEL), row), pl.BlockSpec((D_MODEL, tm), lambda i: (0, i))],
        out_shape=[jax.ShapeDtypeStruct((t, D_MODEL), F32), jax.ShapeDtypeStruct((D_MODEL, t), BF16)],
        compiler_params=_cparams(("parallel",)),
        name="outproj",
    )(x, a, b, w, g)


_PEER_PAIRS = [(i, j) for i in range(PEER_TOPK) for j in range(PEER_TOPK) if (i + 1) * (j + 1) <= PEER_TOPK]
_PEER_CAND_ROWS = -(-len(_PEER_PAIRS) // 8) * 8


SUBLANES = 8


def _bitonic_merge_desc(v, lo, n):
    step = n // 2
    while step >= 1:
        for i in range(lo, lo + n):
            if (i - lo) & step == 0:
                hi_, lo_ = jnp.maximum(v[i], v[i + step]), jnp.minimum(v[i], v[i + step])
                v[i], v[i + step] = hi_, lo_
        step //= 2


def _sort16_desc(v):
    n = len(v)
    size = 2
    while size <= n:
        for lo in range(0, n, size):
            half = size // 2
            v[lo + half:lo + size] = v[lo + half:lo + size][::-1]
            _bitonic_merge_desc(v, lo, size)
        size *= 2
    return v


def _sublane_all(x, op):
    for shift in (4, 2, 1):
        x = op(x, pltpu.roll(x, shift, 0))
    return x


def _top16_sorted(s):
    v = _sort16_desc([s[k * SUBLANES:(k + 1) * SUBLANES, :] for k in range(PEER_NKEYS // SUBLANES)])
    for shift in (4, 2, 1):
        v = [jnp.maximum(v[k], pltpu.roll(v[PEER_TOPK - 1 - k], shift, 0)) for k in range(PEER_TOPK)]
        _bitonic_merge_desc(v, 0, PEER_TOPK)
    return v


def _peer_select(h, s1_all, s2_all, n_ref, e1_ref, r2_ref, e2_ref):
    tt = s1_all.shape[-1]
    nv = PEER_NKEYS // SUBLANES
    sub = lax.broadcasted_iota(jnp.int32, (SUBLANES, LANE), 0)
    for tc in range(tt // LANE):
        lanes = slice(tc * LANE, (tc + 1) * LANE)
        s1 = s1_all[:, lanes]
        s2 = s2_all[:, lanes]
        v1 = _top16_sorted(s1)
        v2 = _top16_sorted(s2)
        pair_sum = {p: v1[p[0]] + v2[p[1]] for p in _PEER_PAIRS}
        cand = []
        for g in range(_PEER_CAND_ROWS // SUBLANES):
            c = jnp.full((SUBLANES, LANE), -jnp.inf, F32)
            for r, p in enumerate(_PEER_PAIRS[g * SUBLANES:(g + 1) * SUBLANES]):
                c = jnp.where(sub == r, pair_sum[p], c)
            cand.append(c)
        f = []
        for _ in range(PEER_TOPK):
            mx = cand[0]
            for c in cand[1:]:
                mx = jnp.maximum(mx, c)
            mx = _sublane_all(mx, jnp.maximum)
            f.append(mx)
            cand = [jnp.where(c == mx, -jnp.inf, c) for c in cand]
        thr = f[PEER_TOPK - 1]
        z = jnp.ones((SUBLANES, LANE), F32)
        for r in range(1, PEER_TOPK):
            z = z + jnp.exp(f[r] - f[0])
        rz = 1.0 / z
        cnt = []
        for i in range(PEER_TOPK):
            c = jnp.zeros((SUBLANES, LANE), F32)
            for j in range(PEER_TOPK // (i + 1)):
                c = c + jnp.where(pair_sum[(i, j)] >= thr, 1.0, 0.0)
            cnt.append(c)
        ranks = []
        e2s = []
        for k in range(nv):
            rows = slice(k * SUBLANES, (k + 1) * SUBLANES)
            a = s1[rows, :]
            b = s2[rows, :]
            n = jnp.zeros((SUBLANES, LANE), F32)
            rank = jnp.full((SUBLANES, LANE), float(PEER_TOPK), F32)
            for r in range(PEER_TOPK):
                n = jnp.where(a == v1[r], cnt[r], n)
                rank = jnp.where(b == v2[r], float(r), rank)
            n_ref[h, rows, lanes] = n
            e1_ref[h, rows, lanes] = jnp.exp(a - v1[0]) * rz
            ranks.append(rank)
            e2s.append(jnp.exp(b - v2[0]))
        r2_ref[h, :, lanes] = jnp.concatenate(ranks, axis=0).astype(r2_ref.dtype)
        e2_ref[h, :, lanes] = jnp.concatenate(e2s, axis=0).astype(e2_ref.dtype)


def _peer_gates(step, sub, n_ref, e1_ref, r2_ref, e2_ref, g_ref):
    tt = g_ref.shape[-1]
    gdt = g_ref.dtype
    a8 = pl.multiple_of(step * PEER_A + sub * PEER_SUB, SUBLANES)
    for tc in range(tt // LANE):
        lanes = slice(tc * LANE, (tc + 1) * LANE)
        nrows = [n_ref[h, pl.ds(a8, PEER_SUB), lanes].astype(gdt) for h in range(PEER_HEADS)]
        erows = [e1_ref[h, pl.ds(a8, PEER_SUB), lanes].astype(gdt) for h in range(PEER_HEADS)]
        for blk in range(PEER_SUB // PEER_GATE_BLOCK):
            gates = [jnp.zeros((PEER_NKEYS, LANE), gdt) for _ in range(PEER_GATE_BLOCK)]
            for h in range(PEER_HEADS):
                r2 = r2_ref[h, :, lanes]
                e2 = e2_ref[h, :, lanes]
                for k in range(PEER_GATE_BLOCK):
                    al = blk * PEER_GATE_BLOCK + k
                    keep = r2 < nrows[h][al:al + 1, :]
                    gates[k] = gates[k] + jnp.where(keep, e2, 0.0) * erows[h][al:al + 1, :]
            for k in range(PEER_GATE_BLOCK):
                al = blk * PEER_GATE_BLOCK + k
                g_ref[sub, al * PEER_NKEYS:(al + 1) * PEER_NKEYS, lanes] = gates[k]


def _peer_kernel(x_ref, hnt_ref, wqt_ref, sk_ref, u_ref, vt_ref, o_ref,
                 qt_ref, n_ref, e1_ref, r2_ref, e2_ref, g_ref, acc_ref, ht_ref):
    j = pl.program_id(1)
    nj = pl.num_programs(1)
    half = PEER_DKEY // 2
    gdt = r2_ref.dtype
    gate_refs = (n_ref, e1_ref, r2_ref, e2_ref, g_ref)

    @pl.when(j == 0)
    def _():
        qt_ref[...] = _dot(wqt_ref[...], hnt_ref[...]).astype(BF16)

        def body(h, carry):
            r0 = pl.multiple_of(h * PEER_DKEY, PEER_DKEY)
            s1 = _dot(sk_ref[2 * h], qt_ref[pl.ds(r0, half), :])
            s2 = _dot(sk_ref[2 * h + 1], qt_ref[pl.ds(r0 + half, half), :])
            _peer_select(h, s1, s2, n_ref, e1_ref, r2_ref, e2_ref)
            return carry

        lax.fori_loop(0, PEER_HEADS, body, 0)
        acc_ref[...] = jnp.zeros(acc_ref.shape, F32)

    sub_rows = lambda sub: slice(sub * PEER_SUB * PEER_NKEYS, (sub + 1) * PEER_SUB * PEER_NKEYS)
    for sub in range(PEER_A // PEER_SUB):
        _peer_gates(j, sub, *gate_refs)
    for sub in range(PEER_A // PEER_SUB):
        at = _dot(u_ref[sub_rows(sub), :], hnt_ref[...]).astype(gdt)
        ht_ref[sub_rows(sub), :] = jax.nn.gelu(at) * g_ref[sub]
    acc_ref[...] += _dot(vt_ref[...], ht_ref[...])

    @pl.when(j == nj - 1)
    def _():
        o_ref[...] = x_ref[...] + acc_ref[...].T


def _peer(x, hnt, wqt, sk, u, vt, layer):
    t = x.shape[0]
    tt = PEER_TT
    ne = u.shape[1]
    et = PEER_A * PEER_NKEYS
    hk = PEER_HEADS * PEER_DKEY
    sel = lambda dt: pltpu.VMEM((PEER_HEADS, PEER_NKEYS, tt), dt)
    return pl.pallas_call(
        _peer_kernel,
        grid=(t // tt, ne // et),
        in_specs=[
            pl.BlockSpec((tt, D_MODEL), lambda i, j: (i, 0)),
            pl.BlockSpec((D_MODEL, tt), lambda i, j: (0, i)),
            pl.BlockSpec((hk, D_MODEL), lambda i, j: (0, 0), pipeline_mode=pl.Buffered(1)),
            pl.BlockSpec((2 * PEER_HEADS, PEER_NKEYS, PEER_DKEY // 2), lambda i, j: (0, 0, 0)),
            pl.BlockSpec((None, et, D_MODEL), lambda i, j: (layer, j, 0)),
            pl.BlockSpec((None, D_MODEL, et), lambda i, j: (layer, 0, j)),
        ],
        out_specs=pl.BlockSpec((tt, D_MODEL), lambda i, j: (i, 0)),
        out_shape=jax.ShapeDtypeStruct((t, D_MODEL), F32),
        scratch_shapes=[
            pltpu.VMEM((hk, tt), BF16),
            sel(F32), sel(F32), sel(PEER_GATE_DTYPE), sel(PEER_GATE_DTYPE),
            pltpu.VMEM((PEER_A // PEER_SUB, PEER_SUB * PEER_NKEYS, tt), PEER_GATE_DTYPE),
            pltpu.VMEM((D_MODEL, tt), F32),
            pltpu.VMEM((et, tt), BF16),
        ],
        compiler_params=_cparams(("parallel", "arbitrary")),
        name="peer",
    )(x, hnt, wqt, sk, u, vt)


def _final_norm_kernel(x_ref, g_ref, o_ref):
    o_ref[...] = _rms(x_ref[...], g_ref[...])


def _final_norm(x, g):
    t = x.shape[0]
    tm = ROW_TILE
    return pl.pallas_call(
        _final_norm_kernel,
        grid=(t // tm,),
        in_specs=[pl.BlockSpec((tm, D_MODEL), lambda i: (i, 0)), pl.BlockSpec((1, D_MODEL), lambda i: (0, 0))],
        out_specs=pl.BlockSpec((tm, D_MODEL), lambda i: (i, 0)),
        out_shape=jax.ShapeDtypeStruct((t, D_MODEL), F32),
        compiler_params=_cparams(("parallel",)),
        name="final_norm",
    )(x, g)


def _rope_table(pos, dim):
    inv = ROPE_THETA ** (-jnp.arange(0, dim, 2, dtype=F32) / dim)
    ang = pos.astype(F32)[:, None] * inv[None, :]
    ang = jnp.concatenate([ang, ang], axis=-1)
    sign = jnp.concatenate([-jnp.ones((dim // 2,), F32), jnp.ones((dim // 2,), F32)])
    return jnp.cos(ang), jnp.sin(ang) * sign


def _rope_tables(seq):
    pos = jnp.arange(seq)
    cos1, sin1 = _rope_table(pos, HEAD_DIM)
    cr, sr = _rope_table(pos // GRID_W, HEAD_DIM // 2)
    cc, sc = _rope_table(pos % GRID_W, HEAD_DIM // 2)
    cos2 = jnp.concatenate([cr, cc], axis=-1)
    sin2 = jnp.concatenate([sr, sc], axis=-1)
    tile = lambda a: jnp.tile(a, (1, N_Q_HEADS))
    return tile(cos1), tile(sin1), tile(cos2), tile(sin2)


def kernel(x, mix_norm_g, ffn_norm_g, final_norm_g, even_w_in, even_w_out, conv_w, conv_b, conv_ln_g, conv_ln_b, sink_logits, odd_w_in, odd_w_out, q_norm_g, k_norm_g, sgu_ln_g, sgu_ln_b, sgu_w, sgu_b, peer_wq, peer_subkeys, peer_u, peer_v):
    batch, seq, d = x.shape
    depth = mix_norm_g.shape[0]
    t = batch * seq
    cos1, sin1, cos2, sin2 = _rope_tables(seq)
    row = lambda a: a.reshape(1, -1).astype(F32)
    xf = x.reshape(t, d)
    u_all = peer_u.astype(BF16)
    vt_all = jnp.swapaxes(peer_v.astype(BF16), 1, 2)
    for layer in range(depth):
        i = layer // 2
        if layer % 2 == 0:
            a_in, q, k, v = _inproj_even(xf, row(mix_norm_g[layer]), even_w_in[i].astype(BF16), cos1, sin1, seq)
            m1 = _conv_module(a_in, conv_w[i, :, 0, :], row(conv_b[i]), row(conv_ln_g[i]), row(conv_ln_b[i]), batch, seq)
            sink_b = jnp.broadcast_to(sink_logits[i].astype(F32)[:, None], (N_Q_HEADS, LANE))
            m2 = _swa(q, k, v, sink_b, batch, seq)
            w_out = even_w_out[i]
        else:
            q, k, v, d_in = _inproj_odd(xf, row(mix_norm_g[layer]), odd_w_in[i].astype(BF16),
                                        row(jnp.tile(q_norm_g[i], N_Q_HEADS)), row(jnp.tile(k_norm_g[i], N_KV_HEADS)),
                                        cos2, sin2, seq)
            m1 = _dense_attn(q, k, v, batch, seq)
            bs = jnp.broadcast_to(sgu_b[i].astype(F32)[:, :, None], (SGU_GROUPS, CHUNK, MIX_WIDTH // SGU_GROUPS))
            m2 = _sgu(d_in, row(sgu_ln_g[i]), row(sgu_ln_b[i]), sgu_w[i].astype(BF16), bs)
            w_out = odd_w_out[i]
        xn, hnt = _outproj(xf, m1, m2, w_out.astype(BF16), row(ffn_norm_g[layer]))
        sk = peer_subkeys[layer].reshape(2 * PEER_HEADS, PEER_NKEYS, PEER_DKEY // 2).astype(BF16)
        xf = _peer(xn, hnt, peer_wq[layer].astype(BF16).T, sk, u_all, vt_all, layer)
    return _final_norm(xf, row(final_norm_g)).reshape(batch, seq, d)
```

```python
import functools
import math

import jax
import jax.numpy as jnp
from jax import lax
from jax.experimental import pallas as pl
from jax.experimental.pallas import tpu as pltpu

F32 = jnp.float32
BF16 = jnp.bfloat16

D_MODEL = 1024
HEAD_DIM = 64
N_Q_HEADS = 8
N_KV_HEADS = 2
GQA_GROUPS = N_Q_HEADS // N_KV_HEADS
ATTN_WIDTH = N_Q_HEADS * HEAD_DIM
KV_WIDTH = N_KV_HEADS * HEAD_DIM
MIX_WIDTH = D_MODEL // 2
CONV_WIDTH = 31
CONV_HALO = 16
SGU_GROUPS = 4
CHUNK = 128
WINDOW = 128
BLOCK_Q = 128
GRID_W = 64
ROPE_THETA = 10000.0
PEER_HEADS = 8
PEER_NKEYS = 128
PEER_DKEY = 128
PEER_TOPK = 16
EPS = 1e-6
NEG = -1e30

VMEM_LIMIT_BYTES = 56 * 1024 * 1024

ROW_TILE = 512
CONV_ROWS = 64
SWA_BLOCKS = 4
ATT_TQ = 1024
ATT_CHAINS = 8
PEER_TT = 512
PEER_A = 16
PEER_SUB = 8
PEER_GATE_BLOCK = 4
PEER_GATE_DTYPE = BF16
LANE = 128
BF16_ROWS = 16


def _cparams(sem):
    return pltpu.CompilerParams(dimension_semantics=sem, vmem_limit_bytes=VMEM_LIMIT_BYTES)


def _dot(a, b):
    return jnp.dot(a, b, preferred_element_type=F32)


def _dot_nt(a, b):
    return lax.dot_general(a, b, (((1,), (1,)), ((), ())), preferred_element_type=F32)


def _rms(x, g):
    return x * lax.rsqrt(jnp.mean(x * x, axis=-1, keepdims=True) + EPS) * g


def _layernorm(x, g, b):
    mu = jnp.mean(x, axis=-1, keepdims=True)
    xc = x - mu
    var = jnp.mean(xc * xc, axis=-1, keepdims=True)
    return xc * lax.rsqrt(var + EPS) * g + b


def _rope(x, cos, sin_signed, group, half):
    n = x.shape[-1]
    lane = lax.broadcasted_iota(jnp.int32, x.shape, x.ndim - 1)
    first = (lane % group) < half
    partner = jnp.where(first, pltpu.roll(x, n - half, x.ndim - 1), pltpu.roll(x, half, x.ndim - 1))
    return x * cos + partner * sin_signed


def _inproj_even_kernel(x_ref, g_ref, w_ref, cos_ref, sin_ref, a_ref, q_ref, k_ref, v_ref):
    h = _rms(x_ref[...], g_ref[...]).astype(BF16)
    o1 = 2 * MIX_WIDTH
    o2 = o1 + ATTN_WIDTH
    o3 = o2 + KV_WIDTH
    a_ref[...] = _dot(h, w_ref[:, :o1])
    cos = cos_ref[...]
    sin = sin_ref[...]
    q = _rope(_dot(h, w_ref[:, o1:o2]), cos, sin, HEAD_DIM, HEAD_DIM // 2)
    q_ref[...] = (q * (HEAD_DIM ** -0.5)).astype(BF16)
    k = _rope(_dot(h, w_ref[:, o2:o3]), cos[:, :KV_WIDTH], sin[:, :KV_WIDTH], HEAD_DIM, HEAD_DIM // 2)
    k_ref[...] = k.astype(BF16)
    v_ref[...] = _dot(h, w_ref[:, o3:]).astype(BF16)


def _inproj_even(x, g, w, cos, sin, seq):
    t = x.shape[0]
    tm = ROW_TILE
    nseq = seq // tm
    in_w = w.shape[1]
    row = lambda i: (i, 0)
    fixed = lambda i: (0, 0)
    pos = lambda i: (i % nseq, 0)
    return pl.pallas_call(
        _inproj_even_kernel,
        grid=(t // tm,),
        in_specs=[
            pl.BlockSpec((tm, D_MODEL), row),
            pl.BlockSpec((1, D_MODEL), fixed),
            pl.BlockSpec((D_MODEL, in_w), fixed),
            pl.BlockSpec((tm, ATTN_WIDTH), pos),
            pl.BlockSpec((tm, ATTN_WIDTH), pos),
        ],
        out_specs=[
            pl.BlockSpec((tm, 2 * MIX_WIDTH), row),
            pl.BlockSpec((tm, ATTN_WIDTH), row),
            pl.BlockSpec((tm, KV_WIDTH), row),
            pl.BlockSpec((tm, KV_WIDTH), row),
        ],
        out_shape=[
            jax.ShapeDtypeStruct((t, 2 * MIX_WIDTH), F32),
            jax.ShapeDtypeStruct((t, ATTN_WIDTH), BF16),
            jax.ShapeDtypeStruct((t, KV_WIDTH), BF16),
            jax.ShapeDtypeStruct((t, KV_WIDTH), BF16),
        ],
        compiler_params=_cparams(("parallel",)),
        name="inproj_even",
    )(x, g, w, cos, sin)


def _head_sumsq(x):
    n = x.shape[-1]
    r = lax.broadcasted_iota(jnp.int32, (n, n), 0) // HEAD_DIM
    c = lax.broadcasted_iota(jnp.int32, (n, n), 1) // HEAD_DIM
    ones = jnp.where(r == c, 1.0, 0.0).astype(BF16)
    sq = x * x
    hi = sq.astype(BF16)
    lo = (sq - hi.astype(F32)).astype(BF16)
    return _dot(hi, ones) + _dot(lo, ones)


def _inproj_odd_kernel(x_ref, g_ref, w_ref, qg_ref, kg_ref, cos_ref, sin_ref, q_ref, k_ref, vt_ref, d_ref):
    h = _rms(x_ref[...], g_ref[...]).astype(BF16)
    c1 = ATTN_WIDTH
    c2 = c1 + KV_WIDTH
    c3 = c2 + KV_WIDTH
    cos = cos_ref[...]
    sin = sin_ref[...]
    half = HEAD_DIM // 2
    q = _dot(h, w_ref[:, :c1])
    q = q * lax.rsqrt(_head_sumsq(q) * (1.0 / HEAD_DIM) + EPS) * qg_ref[...]
    q = _rope(q, cos, sin, half, half // 2)
    q_ref[...] = (q * (HEAD_DIM ** -0.5)).astype(BF16)
    k = _dot(h, w_ref[:, c1:c2])
    k = k * lax.rsqrt(_head_sumsq(k) * (1.0 / HEAD_DIM) + EPS) * kg_ref[...]
    k = _rope(k, cos[:, :KV_WIDTH], sin[:, :KV_WIDTH], half, half // 2).astype(BF16)
    v_t = _dot(h, w_ref[:, c2:c3]).T.astype(BF16)
    for hh in range(N_KV_HEADS):
        k_ref[hh] = k[:, hh * HEAD_DIM:(hh + 1) * HEAD_DIM]
        vt_ref[hh] = v_t[hh * HEAD_DIM:(hh + 1) * HEAD_DIM, :]
    d_ref[...] = _dot(h, w_ref[:, c3:])


def _inproj_odd(x, g, w, qg, kg, cos, sin, seq):
    t = x.shape[0]
    tm = ROW_TILE
    nseq = seq // tm
    in_w = w.shape[1]
    row = lambda i: (i, 0)
    fixed = lambda i: (0, 0)
    pos = lambda i: (i % nseq, 0)
    kv = lambda i: (0, i, 0)
    return pl.pallas_call(
        _inproj_odd_kernel,
        grid=(t // tm,),
        in_specs=[
            pl.BlockSpec((tm, D_MODEL), row),
            pl.BlockSpec((1, D_MODEL), fixed),
            pl.BlockSpec((D_MODEL, in_w), fixed),
            pl.BlockSpec((1, ATTN_WIDTH), fixed),
            pl.BlockSpec((1, KV_WIDTH), fixed),
            pl.BlockSpec((tm, ATTN_WIDTH), pos),
            pl.BlockSpec((tm, ATTN_WIDTH), pos),
        ],
        out_specs=[
            pl.BlockSpec((tm, ATTN_WIDTH), row),
            pl.BlockSpec((N_KV_HEADS, tm, HEAD_DIM), kv),
            pl.BlockSpec((N_KV_HEADS, HEAD_DIM, tm), lambda i: (0, 0, i)),
            pl.BlockSpec((tm, 2 * MIX_WIDTH), row),
        ],
        out_shape=[
            jax.ShapeDtypeStruct((t, ATTN_WIDTH), BF16),
            jax.ShapeDtypeStruct((N_KV_HEADS, t, HEAD_DIM), BF16),
            jax.ShapeDtypeStruct((N_KV_HEADS, HEAD_DIM, t), BF16),
            jax.ShapeDtypeStruct((t, 2 * MIX_WIDTH), F32),
        ],
        compiler_params=_cparams(("parallel",)),
        name="inproj_odd",
    )(x, g, w, qg, kg, cos, sin)


def _glu(x):
    return x[:, :MIX_WIDTH] * jax.nn.sigmoid(x[:, MIX_WIDTH:])


def _conv_kernel(prev_ref, cur_ref, next_ref, w_ref, b_ref, g_ref, beta_ref, o_ref, buf_ref):
    i = pl.program_id(1)
    ns = pl.num_programs(1)
    ts = cur_ref.shape[0]
    halo = CONV_HALO
    buf_ref[0, 0:halo, :] = jnp.where(i > 0, _glu(prev_ref[...]), 0.0)
    buf_ref[0, halo:halo + ts, :] = _glu(cur_ref[...])
    buf_ref[0, halo + ts:, :] = jnp.where(i < ns - 1, _glu(next_ref[...]), 0.0)
    span = ts + 2 * halo - SUBLANES
    for s in range(1, SUBLANES):
        buf_ref[s, 0:span, :] = buf_ref[0, s:s + span, :]
    w = w_ref[...]
    bias = b_ref[...]
    g = g_ref[...]
    beta = beta_ref[...]
    first = halo - CONV_WIDTH // 2
    for c in range(ts // CONV_ROWS):
        r0 = c * CONV_ROWS
        acc = jnp.zeros((CONV_ROWS, MIX_WIDTH), F32)
        for k in range(CONV_WIDTH):
            m, s = divmod(first + k, SUBLANES)
            lo = r0 + SUBLANES * m
            acc = acc + w[k:k + 1, :] * buf_ref[s, lo:lo + CONV_ROWS, :]
        y = _layernorm(acc + bias, g, beta)
        o_ref[r0:r0 + CONV_ROWS, :] = (y * jax.nn.sigmoid(y)).astype(o_ref.dtype)


def _conv_module(a_in, w, b, g, beta, batch, seq):
    t = a_in.shape[0]
    ts = ROW_TILE
    ns = seq // ts
    hb = ts // CONV_HALO
    last = t // CONV_HALO - 1
    fixed = lambda bb, i: (0, 0)
    return pl.pallas_call(
        _conv_kernel,
        grid=(batch, ns),
        in_specs=[
            pl.BlockSpec((CONV_HALO, 2 * MIX_WIDTH), lambda bb, i: (jnp.maximum((bb * ns + i) * hb - 1, 0), 0)),
            pl.BlockSpec((ts, 2 * MIX_WIDTH), lambda bb, i: (bb * ns + i, 0)),
            pl.BlockSpec((CONV_HALO, 2 * MIX_WIDTH), lambda bb, i: (jnp.minimum((bb * ns + i + 1) * hb, last), 0)),
            pl.BlockSpec((CONV_WIDTH, MIX_WIDTH), fixed),
            pl.BlockSpec((1, MIX_WIDTH), fixed),
            pl.BlockSpec((1, MIX_WIDTH), fixed),
            pl.BlockSpec((1, MIX_WIDTH), fixed),
        ],
        out_specs=pl.BlockSpec((ts, MIX_WIDTH), lambda bb, i: (bb * ns + i, 0)),
        out_shape=jax.ShapeDtypeStruct((t, MIX_WIDTH), BF16),
        scratch_shapes=[pltpu.VMEM((SUBLANES, ts + 2 * CONV_HALO, MIX_WIDTH), F32)],
        compiler_params=_cparams(("parallel", "parallel")),
        name="conv_module",
    )(a_in, a_in, a_in, w, b, g, beta)


def _swa_kernel(q_ref, kp_ref, kc_ref, kn_ref, vp_ref, vc_ref, vn_ref, sink_ref, o_ref):
    n = pl.program_id(1)
    nsteps = pl.num_programs(1)
    bq = BLOCK_Q
    cols = GQA_GROUPS * bq
    jj = lax.broadcasted_iota(jnp.int32, (3 * bq, cols), 0)
    ii = lax.broadcasted_iota(jnp.int32, (3 * bq, cols), 1) % bq
    band = (jj >= ii) & (jj <= ii + 2 * WINDOW)
    kcat = jnp.concatenate([kp_ref[...], kc_ref[...], kn_ref[...]], axis=0)
    vcat_t = jnp.concatenate([vp_ref[...], vc_ref[...], vn_ref[...]], axis=0).astype(F32).T.astype(BF16)
    for qb in range(SWA_BLOCKS):
        mask = band
        if qb == 0:
            mask = mask & (jj >= jnp.where(n > 0, 0, bq))
        if qb == SWA_BLOCKS - 1:
            mask = mask & (jj < jnp.where(n < nsteps - 1, 3 * bq, 2 * bq))
        q_t = q_ref[qb * bq:(qb + 1) * bq, :].astype(F32).T.astype(BF16)
        outs = []
        for hk in range(N_KV_HEADS):
            heads = range(hk * GQA_GROUPS, (hk + 1) * GQA_GROUPS)
            kh = kcat[qb * bq:(qb + 3) * bq, hk * HEAD_DIM:(hk + 1) * HEAD_DIM]
            vh_t = jnp.concatenate([vcat_t[hk * HEAD_DIM:(hk + 1) * HEAD_DIM, qb * bq:(qb + 3) * bq],
                                    jnp.ones((BF16_ROWS, 3 * bq), BF16)], axis=0)
            qs_t = jnp.concatenate([q_t[hd * HEAD_DIM:(hd + 1) * HEAD_DIM, :] for hd in heads], axis=1)
            sink = jnp.concatenate([jnp.broadcast_to(sink_ref[hd:hd + 1, 0:1], (1, bq)) for hd in heads], axis=1)
            s = jnp.where(mask, _dot(kh, qs_t), NEG)
            m = jnp.maximum(jnp.max(s, axis=0, keepdims=True), sink)
            o_ext = _dot(vh_t, jnp.exp(s - m).astype(BF16))
            denom = o_ext[HEAD_DIM:HEAD_DIM + 1, :] + jnp.exp(sink - m)
            o_t = o_ext[:HEAD_DIM, :] / denom
            outs.extend(o_t[:, g * bq:(g + 1) * bq] for g in range(GQA_GROUPS))
        o_ref[qb * bq:(qb + 1) * bq, :] = jnp.concatenate(outs, axis=0).T.astype(o_ref.dtype)


def _swa(q, k, v, sink_b, batch, seq):
    t = q.shape[0]
    bq = BLOCK_Q
    tq = SWA_BLOCKS * bq
    ns = seq // tq
    last = t // bq - 1
    cur = lambda bb, n: (bb * ns + n, 0)
    prev = lambda bb, n: (jnp.maximum((bb * ns + n) * SWA_BLOCKS - 1, 0), 0)
    nxt = lambda bb, n: (jnp.minimum((bb * ns + n + 1) * SWA_BLOCKS, last), 0)
    halo = lambda im: pl.BlockSpec((bq, KV_WIDTH), im)
    body = pl.BlockSpec((tq, KV_WIDTH), cur)
    return pl.pallas_call(
        _swa_kernel,
        grid=(batch, ns),
        in_specs=[pl.BlockSpec((tq, ATTN_WIDTH), cur), halo(prev), body, halo(nxt), halo(prev), body, halo(nxt),
                  pl.BlockSpec((N_Q_HEADS, LANE), lambda bb, n: (0, 0))],
        out_specs=pl.BlockSpec((tq, ATTN_WIDTH), cur),
        out_shape=jax.ShapeDtypeStruct((t, ATTN_WIDTH), BF16),
        compiler_params=_cparams(("parallel", "parallel")),
        name="window_attn",
    )(q, k, k, k, v, v, v, sink_b)


def _dense_attn_kernel(q_ref, k_ref, vt_ref, o_ref):
    ts = q_ref.shape[0] // ATT_CHAINS
    k = k_ref[...]
    v_ext = jnp.concatenate([vt_ref[...], jnp.ones((BF16_ROWS, vt_ref.shape[1]), BF16)], axis=0)
    q_t = q_ref[...].astype(F32).T.astype(BF16)

    def scores(c):
        cols = slice(c * ts, (c + 1) * ts)
        qs_t = jnp.concatenate([q_t[g * HEAD_DIM:(g + 1) * HEAD_DIM, cols] for g in range(GQA_GROUPS)], axis=1)
        return _dot(k, qs_t)

    s_next = scores(0)
    for c in range(ATT_CHAINS):
        cols = slice(c * ts, (c + 1) * ts)
        s = s_next
        if c + 1 < ATT_CHAINS:
            s_next = scores(c + 1)
        p = jnp.exp(s - jnp.max(s, axis=0, keepdims=True))
        o_ext = _dot(v_ext, p.astype(BF16))
        o_t = o_ext[:HEAD_DIM, :] / o_ext[HEAD_DIM:HEAD_DIM + 1, :]
        o_ref[cols, :] = jnp.concatenate(
            [o_t[:, g * ts:(g + 1) * ts] for g in range(GQA_GROUPS)], axis=0).T.astype(o_ref.dtype)


def _dense_attn(q, k, v, batch, seq):
    t = q.shape[0]
    tq = min(ATT_TQ, seq)
    nq = seq // tq
    gw = GQA_GROUPS * HEAD_DIM
    qmap = lambda bb, hk, qi: (bb * nq + qi, hk)
    return pl.pallas_call(
        _dense_attn_kernel,
        grid=(batch, N_KV_HEADS, nq),
        in_specs=[
            pl.BlockSpec((tq, gw), qmap),
            pl.BlockSpec((None, seq, HEAD_DIM), lambda bb, hk, qi: (hk, bb, 0)),
            pl.BlockSpec((None, HEAD_DIM, seq), lambda bb, hk, qi: (hk, 0, bb)),
        ],
        out_specs=pl.BlockSpec((tq, gw), qmap),
        out_shape=jax.ShapeDtypeStruct((t, ATTN_WIDTH), BF16),
        compiler_params=_cparams(("parallel", "parallel", "parallel")),
        name="dense_attn",
    )(q, k, v)


def _sgu_kernel(d_ref, g_ref, b_ref, w_ref, bs_ref, o_ref):
    z = jax.nn.gelu(d_ref[...])
    u = z[:, :MIX_WIDTH]
    v = _layernorm(z[:, MIX_WIDTH:], g_ref[...], b_ref[...]).astype(BF16)
    gc = MIX_WIDTH // SGU_GROUPS
    for n in range(d_ref.shape[0] // CHUNK):
        rows = slice(n * CHUNK, (n + 1) * CHUNK)
        for g in range(SGU_GROUPS):
            cols = slice(g * gc, (g + 1) * gc)
            mixed = _dot(w_ref[g], v[rows, cols]) + bs_ref[g]
            o_ref[rows, cols] = (u[rows, cols] * mixed).astype(o_ref.dtype)


def _sgu(d_in, g, b, w, bs):
    t = d_in.shape[0]
    ts = ROW_TILE
    gc = MIX_WIDTH // SGU_GROUPS
    row = lambda i: (i, 0)
    fixed = lambda i: (0, 0)
    fixed3 = lambda i: (0, 0, 0)
    return pl.pallas_call(
        _sgu_kernel,
        grid=(t // ts,),
        in_specs=[
            pl.BlockSpec((ts, 2 * MIX_WIDTH), row),
            pl.BlockSpec((1, MIX_WIDTH), fixed),
            pl.BlockSpec((1, MIX_WIDTH), fixed),
            pl.BlockSpec((SGU_GROUPS, CHUNK, CHUNK), fixed3),
            pl.BlockSpec((SGU_GROUPS, CHUNK, gc), fixed3),
        ],
        out_specs=pl.BlockSpec((ts, MIX_WIDTH), row),
        out_shape=jax.ShapeDtypeStruct((t, MIX_WIDTH), BF16),
        compiler_params=_cparams(("parallel",)),
        name="spatial_gating",
    )(d_in, g, b, w, bs)


def _outproj_kernel(x_ref, a_ref, b_ref, w_ref, g_ref, xn_ref, hnt_ref):
    y = _dot(a_ref[...], w_ref[:MIX_WIDTH, :]) + _dot(b_ref[...], w_ref[MIX_WIDTH:, :])
    xn = x_ref[...] + y
    xn_ref[...] = xn
    hnt_ref[...] = _rms(xn, g_ref[...]).T.astype(BF16)


def _outproj(x, a, b, w, g):
    t = x.shape[0]
    tm = ROW_TILE
    row = lambda i: (i, 0)
    fixed = lambda i: (0, 0)
    return pl.pallas_call(
        _outproj_kernel,
        grid=(t // tm,),
        in_specs=[
            pl.BlockSpec((tm, D_MODEL), row),
            pl.BlockSpec((tm, MIX_WIDTH), row),
            pl.BlockSpec((tm, MIX_WIDTH), row),
            pl.BlockSpec((2 * MIX_WIDTH, D_MODEL), fixed),
            pl.BlockSpec((1, D_MODEL), fixed),
        ],
        out_specs=[pl.BlockSpec((tm, D_MODEL), row), pl.BlockSpec((D_MODEL, tm), lambda i: (0, i))],
        out_shape=[jax.ShapeDtypeStruct((t, D_MODEL), F32), jax.ShapeDtypeStruct((D_MODEL, t), BF16)],
        compiler_params=_cparams(("parallel",)),
        name="outproj",
    )(x, a, b, w, g)


_PEER_PAIRS = [(i, j) for i in range(PEER_TOPK) for j in range(PEER_TOPK) if (i + 1) * (j + 1) <= PEER_TOPK]
_PEER_CAND_ROWS = -(-len(_PEER_PAIRS) // 8) * 8


SUBLANES = 8


def _bitonic_merge_desc(v, lo, n):
    step = n // 2
    while step >= 1:
        for i in range(lo, lo + n):
            if (i - lo) & step == 0:
                hi_, lo_ = jnp.maximum(v[i], v[i + step]), jnp.minimum(v[i], v[i + step])
                v[i], v[i + step] = hi_, lo_
        step //= 2


def _sort16_desc(v):
    n = len(v)
    size = 2
    while size <= n:
        for lo in range(0, n, size):
            half = size // 2
            v[lo + half:lo + size] = v[lo + half:lo + size][::-1]
            _bitonic_merge_desc(v, lo, size)
        size *= 2
    return v


def _sublane_all(x, op):
    for shift in (4, 2, 1):
        x = op(x, pltpu.roll(x, shift, 0))
    return x


def _top16_sorted(s):
    v = _sort16_desc([s[k * SUBLANES:(k + 1) * SUBLANES, :] for k in range(PEER_NKEYS // SUBLANES)])
    for shift in (4, 2, 1):
        v = [jnp.maximum(v[k], pltpu.roll(v[PEER_TOPK - 1 - k], shift, 0)) for k in range(PEER_TOPK)]
        _bitonic_merge_desc(v, 0, PEER_TOPK)
    return v


def _peer_select(h, s1_all, s2_all, n_ref, e1_ref, r2_ref, e2_ref):
    tt = s1_all.shape[-1]
    nv = PEER_NKEYS // SUBLANES
    sub = lax.broadcasted_iota(jnp.int32, (SUBLANES, LANE), 0)
    for tc in range(tt // LANE):
        lanes = slice(tc * LANE, (tc + 1) * LANE)
        s1 = s1_all[:, lanes]
        s2 = s2_all[:, lanes]
        v1 = _top16_sorted(s1)
        v2 = _top16_sorted(s2)
        pair_sum = {p: v1[p[0]] + v2[p[1]] for p in _PEER_PAIRS}
        cand = []
        for g in range(_PEER_CAND_ROWS // SUBLANES):
            c = jnp.full((SUBLANES, LANE), -jnp.inf, F32)
            for r, p in enumerate(_PEER_PAIRS[g * SUBLANES:(g + 1) * SUBLANES]):
                c = jnp.where(sub == r, pair_sum[p], c)
            cand.append(c)
        f = []
        for _ in range(PEER_TOPK):
            mx = cand[0]
            for c in cand[1:]:
                mx = jnp.maximum(mx, c)
            mx = _sublane_all(mx, jnp.maximum)
            f.append(mx)
            cand = [jnp.where(c == mx, -jnp.inf, c) for c in cand]
        thr = f[PEER_TOPK - 1]
        z = jnp.ones((SUBLANES, LANE), F32)
        for r in range(1, PEER_TOPK):
            z = z + jnp.exp(f[r] - f[0])
        rz = 1.0 / z
        cnt = []
        for i in range(PEER_TOPK):
            c = jnp.zeros((SUBLANES, LANE), F32)
            for j in range(PEER_TOPK // (i + 1)):
                c = c + jnp.where(pair_sum[(i, j)] >= thr, 1.0, 0.0)
            cnt.append(c)
        ranks = []
        e2s = []
        for k in range(nv):
            rows = slice(k * SUBLANES, (k + 1) * SUBLANES)
            a = s1[rows, :]
            b = s2[rows, :]
            n = jnp.zeros((SUBLANES, LANE), F32)
            rank = jnp.full((SUBLANES, LANE), float(PEER_TOPK), F32)
            for r in range(PEER_TOPK):
                n = jnp.where(a == v1[r], cnt[r], n)
                rank = jnp.where(b == v2[r], float(r), rank)
            n_ref[h, rows, lanes] = n
            e1_ref[h, rows, lanes] = jnp.exp(a - v1[0]) * rz
            ranks.append(rank)
            e2s.append(jnp.exp(b - v2[0]))
        r2_ref[h, :, lanes] = jnp.concatenate(ranks, axis=0).astype(r2_ref.dtype)
        e2_ref[h, :, lanes] = jnp.concatenate(e2s, axis=0).astype(e2_ref.dtype)


def _peer_gates(step, sub, n_ref, e1_ref, r2_ref, e2_ref, g_ref):
    tt = g_ref.shape[-1]
    gdt = g_ref.dtype
    first = sub * PEER_SUB
    a8 = pl.multiple_of(step * PEER_A + first // SUBLANES * SUBLANES, SUBLANES)
    nload = -(-(first % SUBLANES + PEER_SUB) // SUBLANES) * SUBLANES
    for tc in range(tt // LANE):
        lanes = slice(tc * LANE, (tc + 1) * LANE)
        nrows = [n_ref[h, pl.ds(a8, nload), lanes].astype(gdt) for h in range(PEER_HEADS)]
        erows = [e1_ref[h, pl.ds(a8, nload), lanes].astype(gdt) for h in range(PEER_HEADS)]
        for blk in range(PEER_SUB // PEER_GATE_BLOCK):
            gates = [jnp.zeros((PEER_NKEYS, LANE), gdt) for _ in range(PEER_GATE_BLOCK)]
            for h in range(PEER_HEADS):
                r2 = r2_ref[h, :, lanes]
                e2 = e2_ref[h, :, lanes]
                for k in range(PEER_GATE_BLOCK):
                    al = first % SUBLANES + blk * PEER_GATE_BLOCK + k
                    keep = r2 < nrows[h][al:al + 1, :]
                    gates[k] = gates[k] + jnp.where(keep, e2, 0.0) * erows[h][al:al + 1, :]
            for k in range(PEER_GATE_BLOCK):
                al = blk * PEER_GATE_BLOCK + k
                g_ref[sub, al * PEER_NKEYS:(al + 1) * PEER_NKEYS, lanes] = gates[k]


def _peer_kernel(x_ref, hnt_ref, wqt_ref, sk_ref, u_ref, vt_ref, o_ref,
                 qt_ref, n_ref, e1_ref, r2_ref, e2_ref, g_ref, acc_ref, ht_ref):
    j = pl.program_id(1)
    nj = pl.num_programs(1)
    half = PEER_DKEY // 2
    gdt = r2_ref.dtype
    gate_refs = (n_ref, e1_ref, r2_ref, e2_ref, g_ref)

    @pl.when(j == 0)
    def _():
        qt_ref[...] = _dot(wqt_ref[...], hnt_ref[...]).astype(BF16)

        def body(h, carry):
            r0 = pl.multiple_of(h * PEER_DKEY, PEER_DKEY)
            s1 = _dot(sk_ref[2 * h], qt_ref[pl.ds(r0, half), :])
            s2 = _dot(sk_ref[2 * h + 1], qt_ref[pl.ds(r0 + half, half), :])
            _peer_select(h, s1, s2, n_ref, e1_ref, r2_ref, e2_ref)
            return carry

        lax.fori_loop(0, PEER_HEADS, body, 0)
        acc_ref[...] = jnp.zeros(acc_ref.shape, F32)

    sub_rows = lambda sub: slice(sub * PEER_SUB * PEER_NKEYS, (sub + 1) * PEER_SUB * PEER_NKEYS)
    for sub in range(PEER_A // PEER_SUB):
        _peer_gates(j, sub, *gate_refs)
    for sub in range(PEER_A // PEER_SUB):
        at = _dot(u_ref[sub_rows(sub), :], hnt_ref[...]).astype(gdt)
        ht_ref[sub_rows(sub), :] = jax.nn.gelu(at) * g_ref[sub]
    acc_ref[...] += lax.dot_general(vt_ref[...], ht_ref[...], (((0,), (0,)), ((), ())), preferred_element_type=F32)

    @pl.when(j == nj - 1)
    def _():
        o_ref[...] = x_ref[...] + acc_ref[...].T


def _peer(x, hnt, wqt, sk, u, vt, layer):
    t = x.shape[0]
    tt = PEER_TT
    ne = u.shape[1]
    et = PEER_A * PEER_NKEYS
    hk = PEER_HEADS * PEER_DKEY
    sel = lambda dt: pltpu.VMEM((PEER_HEADS, PEER_NKEYS, tt), dt)
    return pl.pallas_call(
        _peer_kernel,
        grid=(t // tt, ne // et),
        in_specs=[
            pl.BlockSpec((tt, D_MODEL), lambda i, j: (i, 0)),
            pl.BlockSpec((D_MODEL, tt), lambda i, j: (0, i)),
            pl.BlockSpec((hk, D_MODEL), lambda i, j: (0, 0), pipeline_mode=pl.Buffered(1)),
            pl.BlockSpec((2 * PEER_HEADS, PEER_NKEYS, PEER_DKEY // 2), lambda i, j: (0, 0, 0)),
            pl.BlockSpec((None, et, D_MODEL), lambda i, j: (layer, j, 0)),
            pl.BlockSpec((None, et, D_MODEL), lambda i, j: (layer, j, 0)),
        ],
        out_specs=pl.BlockSpec((tt, D_MODEL), lambda i, j: (i, 0)),
        out_shape=jax.ShapeDtypeStruct((t, D_MODEL), F32),
        scratch_shapes=[
            pltpu.VMEM((hk, tt), BF16),
            sel(F32), sel(F32), sel(PEER_GATE_DTYPE), sel(PEER_GATE_DTYPE),
            pltpu.VMEM((PEER_A // PEER_SUB, PEER_SUB * PEER_NKEYS, tt), PEER_GATE_DTYPE),
            pltpu.VMEM((D_MODEL, tt), F32),
            pltpu.VMEM((et, tt), BF16),
        ],
        compiler_params=_cparams(("parallel", "arbitrary")),
        name="peer",
    )(x, hnt, wqt, sk, u, vt)


def _final_norm_kernel(x_ref, g_ref, o_ref):
    o_ref[...] = _rms(x_ref[...], g_ref[...])


def _final_norm(x, g):
    t = x.shape[0]
    tm = ROW_TILE
    return pl.pallas_call(
        _final_norm_kernel,
        grid=(t // tm,),
        in_specs=[pl.BlockSpec((tm, D_MODEL), lambda i: (i, 0)), pl.BlockSpec((1, D_MODEL), lambda i: (0, 0))],
        out_specs=pl.BlockSpec((tm, D_MODEL), lambda i: (i, 0)),
        out_shape=jax.ShapeDtypeStruct((t, D_MODEL), F32),
        compiler_params=_cparams(("parallel",)),
        name="final_norm",
    )(x, g)


def _rope_table(pos, dim):
    inv = ROPE_THETA ** (-jnp.arange(0, dim, 2, dtype=F32) / dim)
    ang = pos.astype(F32)[:, None] * inv[None, :]
    ang = jnp.concatenate([ang, ang], axis=-1)
    sign = jnp.concatenate([-jnp.ones((dim // 2,), F32), jnp.ones((dim // 2,), F32)])
    return jnp.cos(ang), jnp.sin(ang) * sign


def _rope_tables(seq):
    pos = jnp.arange(seq)
    cos1, sin1 = _rope_table(pos, HEAD_DIM)
    cr, sr = _rope_table(pos // GRID_W, HEAD_DIM // 2)
    cc, sc = _rope_table(pos % GRID_W, HEAD_DIM // 2)
    cos2 = jnp.concatenate([cr, cc], axis=-1)
    sin2 = jnp.concatenate([sr, sc], axis=-1)
    tile = lambda a: jnp.tile(a, (1, N_Q_HEADS))
    return tile(cos1), tile(sin1), tile(cos2), tile(sin2)


def kernel(x, mix_norm_g, ffn_norm_g, final_norm_g, even_w_in, even_w_out, conv_w, conv_b, conv_ln_g, conv_ln_b, sink_logits, odd_w_in, odd_w_out, q_norm_g, k_norm_g, sgu_ln_g, sgu_ln_b, sgu_w, sgu_b, peer_wq, peer_subkeys, peer_u, peer_v):
    batch, seq, d = x.shape
    depth = mix_norm_g.shape[0]
    t = batch * seq
    cos1, sin1, cos2, sin2 = _rope_tables(seq)
    row = lambda a: a.reshape(1, -1).astype(F32)
    xf = x.reshape(t, d)
    u_all = peer_u.astype(BF16)
    vt_all = peer_v.astype(BF16)
    for layer in range(depth):
        i = layer // 2
        if layer % 2 == 0:
            a_in, q, k, v = _inproj_even(xf, row(mix_norm_g[layer]), even_w_in[i].astype(BF16), cos1, sin1, seq)
            m1 = _conv_module(a_in, conv_w[i, :, 0, :], row(conv_b[i]), row(conv_ln_g[i]), row(conv_ln_b[i]), batch, seq)
            sink_b = jnp.broadcast_to(sink_logits[i].astype(F32)[:, None], (N_Q_HEADS, LANE))
            m2 = _swa(q, k, v, sink_b, batch, seq)
            w_out = even_w_out[i]
        else:
            q, k, v, d_in = _inproj_odd(xf, row(mix_norm_g[layer]), odd_w_in[i].astype(BF16),
                                        row(jnp.tile(q_norm_g[i], N_Q_HEADS)), row(jnp.tile(k_norm_g[i], N_KV_HEADS)),
                                        cos2, sin2, seq)
            m1 = _dense_attn(q, k, v, batch, seq)
            bs = jnp.broadcast_to(sgu_b[i].astype(F32)[:, :, None], (SGU_GROUPS, CHUNK, MIX_WIDTH // SGU_GROUPS))
            m2 = _sgu(d_in, row(sgu_ln_g[i]), row(sgu_ln_b[i]), sgu_w[i].astype(BF16), bs)
            w_out = odd_w_out[i]
        xn, hnt = _outproj(xf, m1, m2, w_out.astype(BF16), row(ffn_norm_g[layer]))
        sk = peer_subkeys[layer].reshape(2 * PEER_HEADS, PEER_NKEYS, PEER_DKEY // 2).astype(BF16)
        xf = _peer(xn, hnt, peer_wq[layer].astype(BF16).T, sk, u_all, vt_all, layer)
    return _final_norm(xf, row(final_norm_g)).reshape(batch, seq, d)
```

```python
import jax
import jax.numpy as jnp
from jax import lax
from jax.experimental import pallas as pl
from jax.experimental.pallas import tpu as pltpu

F32 = jnp.float32
BF16 = jnp.bfloat16

D_MODEL = 1024
HEAD_DIM = 64
N_Q_HEADS = 8
N_KV_HEADS = 2
GQA_GROUPS = N_Q_HEADS // N_KV_HEADS
ATTN_WIDTH = N_Q_HEADS * HEAD_DIM
KV_WIDTH = N_KV_HEADS * HEAD_DIM
MIX_WIDTH = D_MODEL // 2
CONV_WIDTH = 31
CONV_HALO = 16
SGU_GROUPS = 4
CHUNK = 128
WINDOW = 128
BLOCK_Q = 128
GRID_W = 64
ROPE_THETA = 10000.0
PEER_HEADS = 8
PEER_NKEYS = 128
PEER_DKEY = 128
PEER_TOPK = 16
EPS = 1e-6
NEG = -1e30

VMEM_LIMIT_BYTES = 56 * 1024 * 1024

ROW_TILE = 512
CONV_ROWS = 64
SWA_BLOCKS = 4
ATT_TQ = 1024
ATT_CHAINS = 8
PEER_TT = 512
PEER_A = 16
PEER_SUB = 8
PEER_GATE_BLOCK = 4
PEER_GATE_DTYPE = BF16
LANE = 128
BF16_ROWS = 16


def _cparams(sem):
    return pltpu.CompilerParams(dimension_semantics=sem, vmem_limit_bytes=VMEM_LIMIT_BYTES)


def _dot(a, b):
    return jnp.dot(a, b, preferred_element_type=F32)


def _rms(x, g):
    return x * lax.rsqrt(jnp.mean(x * x, axis=-1, keepdims=True) + EPS) * g


def _layernorm(x, g, b):
    mu = jnp.mean(x, axis=-1, keepdims=True)
    xc = x - mu
    var = jnp.mean(xc * xc, axis=-1, keepdims=True)
    return xc * lax.rsqrt(var + EPS) * g + b


def _rope(x, cos, sin_signed, group, half):
    n = x.shape[-1]
    lane = lax.broadcasted_iota(jnp.int32, x.shape, x.ndim - 1)
    first = (lane % group) < half
    partner = jnp.where(first, pltpu.roll(x, n - half, x.ndim - 1), pltpu.roll(x, half, x.ndim - 1))
    return x * cos + partner * sin_signed


def _inproj_even_kernel(x_ref, g_ref, w_ref, cos_ref, sin_ref, a_ref, q_ref, k_ref, v_ref):
    h = _rms(x_ref[...], g_ref[...]).astype(BF16)
    o1 = 2 * MIX_WIDTH
    o2 = o1 + ATTN_WIDTH
    o3 = o2 + KV_WIDTH
    a_ref[...] = _dot(h, w_ref[:, :o1])
    cos = cos_ref[...]
    sin = sin_ref[...]
    q = _rope(_dot(h, w_ref[:, o1:o2]), cos, sin, HEAD_DIM, HEAD_DIM // 2)
    q_ref[...] = (q * (HEAD_DIM ** -0.5)).astype(BF16)
    k = _rope(_dot(h, w_ref[:, o2:o3]), cos[:, :KV_WIDTH], sin[:, :KV_WIDTH], HEAD_DIM, HEAD_DIM // 2)
    k_ref[...] = k.astype(BF16)
    v_ref[...] = _dot(h, w_ref[:, o3:]).astype(BF16)


def _inproj_even(x, g, w, cos, sin, seq):
    t = x.shape[0]
    tm = ROW_TILE
    nseq = seq // tm
    in_w = w.shape[1]
    row = lambda i: (i, 0)
    fixed = lambda i: (0, 0)
    pos = lambda i: (i % nseq, 0)
    return pl.pallas_call(
        _inproj_even_kernel,
        grid=(t // tm,),
        in_specs=[
            pl.BlockSpec((tm, D_MODEL), row),
            pl.BlockSpec((1, D_MODEL), fixed),
            pl.BlockSpec((D_MODEL, in_w), fixed),
            pl.BlockSpec((tm, ATTN_WIDTH), pos),
            pl.BlockSpec((tm, ATTN_WIDTH), pos),
        ],
        out_specs=[
            pl.BlockSpec((tm, 2 * MIX_WIDTH), row),
            pl.BlockSpec((tm, ATTN_WIDTH), row),
            pl.BlockSpec((tm, KV_WIDTH), row),
            pl.BlockSpec((tm, KV_WIDTH), row),
        ],
        out_shape=[
            jax.ShapeDtypeStruct((t, 2 * MIX_WIDTH), F32),
            jax.ShapeDtypeStruct((t, ATTN_WIDTH), BF16),
            jax.ShapeDtypeStruct((t, KV_WIDTH), BF16),
            jax.ShapeDtypeStruct((t, KV_WIDTH), BF16),
        ],
        compiler_params=_cparams(("parallel",)),
        name="inproj_even",
    )(x, g, w, cos, sin)


def _head_sumsq(x):
    n = x.shape[-1]
    r = lax.broadcasted_iota(jnp.int32, (n, n), 0) // HEAD_DIM
    c = lax.broadcasted_iota(jnp.int32, (n, n), 1) // HEAD_DIM
    ones = jnp.where(r == c, 1.0, 0.0).astype(BF16)
    sq = x * x
    hi = sq.astype(BF16)
    lo = (sq - hi.astype(F32)).astype(BF16)
    return _dot(hi, ones) + _dot(lo, ones)


def _inproj_odd_kernel(x_ref, g_ref, w_ref, qg_ref, kg_ref, cos_ref, sin_ref, q_ref, k_ref, vt_ref, d_ref):
    h = _rms(x_ref[...], g_ref[...]).astype(BF16)
    c1 = ATTN_WIDTH
    c2 = c1 + KV_WIDTH
    c3 = c2 + KV_WIDTH
    cos = cos_ref[...]
    sin = sin_ref[...]
    half = HEAD_DIM // 2
    q = _dot(h, w_ref[:, :c1])
    q = q * lax.rsqrt(_head_sumsq(q) * (1.0 / HEAD_DIM) + EPS) * qg_ref[...]
    q = _rope(q, cos, sin, half, half // 2)
    q_ref[...] = (q * (HEAD_DIM ** -0.5)).astype(BF16)
    k = _dot(h, w_ref[:, c1:c2])
    k = k * lax.rsqrt(_head_sumsq(k) * (1.0 / HEAD_DIM) + EPS) * kg_ref[...]
    k = _rope(k, cos[:, :KV_WIDTH], sin[:, :KV_WIDTH], half, half // 2).astype(BF16)
    v_t = _dot(h, w_ref[:, c2:c3]).T.astype(BF16)
    for hh in range(N_KV_HEADS):
        k_ref[hh] = k[:, hh * HEAD_DIM:(hh + 1) * HEAD_DIM]
        vt_ref[hh] = v_t[hh * HEAD_DIM:(hh + 1) * HEAD_DIM, :]
    d_ref[...] = _dot(h, w_ref[:, c3:])


def _inproj_odd(x, g, w, qg, kg, cos, sin, seq):
    t = x.shape[0]
    tm = ROW_TILE
    nseq = seq // tm
    in_w = w.shape[1]
    row = lambda i: (i, 0)
    fixed = lambda i: (0, 0)
    pos = lambda i: (i % nseq, 0)
    kv = lambda i: (0, i, 0)
    return pl.pallas_call(
        _inproj_odd_kernel,
        grid=(t // tm,),
        in_specs=[
            pl.BlockSpec((tm, D_MODEL), row),
            pl.BlockSpec((1, D_MODEL), fixed),
            pl.BlockSpec((D_MODEL, in_w), fixed),
            pl.BlockSpec((1, ATTN_WIDTH), fixed),
            pl.BlockSpec((1, KV_WIDTH), fixed),
            pl.BlockSpec((tm, ATTN_WIDTH), pos),
            pl.BlockSpec((tm, ATTN_WIDTH), pos),
        ],
        out_specs=[
            pl.BlockSpec((tm, ATTN_WIDTH), row),
            pl.BlockSpec((N_KV_HEADS, tm, HEAD_DIM), kv),
            pl.BlockSpec((N_KV_HEADS, HEAD_DIM, tm), lambda i: (0, 0, i)),
            pl.BlockSpec((tm, 2 * MIX_WIDTH), row),
        ],
        out_shape=[
            jax.ShapeDtypeStruct((t, ATTN_WIDTH), BF16),
            jax.ShapeDtypeStruct((N_KV_HEADS, t, HEAD_DIM), BF16),
            jax.ShapeDtypeStruct((N_KV_HEADS, HEAD_DIM, t), BF16),
            jax.ShapeDtypeStruct((t, 2 * MIX_WIDTH), F32),
        ],
        compiler_params=_cparams(("parallel",)),
        name="inproj_odd",
    )(x, g, w, qg, kg, cos, sin)


def _glu(x):
    return x[:, :MIX_WIDTH] * jax.nn.sigmoid(x[:, MIX_WIDTH:])


def _conv_kernel(prev_ref, cur_ref, next_ref, w_ref, b_ref, g_ref, beta_ref, o_ref, buf_ref):
    i = pl.program_id(1)
    ns = pl.num_programs(1)
    ts = cur_ref.shape[0]
    halo = CONV_HALO
    buf_ref[0, 0:halo, :] = jnp.where(i > 0, _glu(prev_ref[...]), 0.0)
    buf_ref[0, halo:halo + ts, :] = _glu(cur_ref[...])
    buf_ref[0, halo + ts:, :] = jnp.where(i < ns - 1, _glu(next_ref[...]), 0.0)
    span = ts + 2 * halo - SUBLANES
    for s in range(1, SUBLANES):
        buf_ref[s, 0:span, :] = buf_ref[0, s:s + span, :]
    w = w_ref[...]
    bias = b_ref[...]
    g = g_ref[...]
    beta = beta_ref[...]
    first = halo - CONV_WIDTH // 2
    for c in range(ts // CONV_ROWS):
        r0 = c * CONV_ROWS
        acc = jnp.zeros((CONV_ROWS, MIX_WIDTH), F32)
        for k in range(CONV_WIDTH):
            m, s = divmod(first + k, SUBLANES)
            lo = r0 + SUBLANES * m
            acc = acc + w[k:k + 1, :] * buf_ref[s, lo:lo + CONV_ROWS, :]
        y = _layernorm(acc + bias, g, beta)
        o_ref[r0:r0 + CONV_ROWS, :] = (y * jax.nn.sigmoid(y)).astype(o_ref.dtype)


def _conv_module(a_in, w, b, g, beta, batch, seq):
    t = a_in.shape[0]
    ts = ROW_TILE
    ns = seq // ts
    hb = ts // CONV_HALO
    last = t // CONV_HALO - 1
    fixed = lambda bb, i: (0, 0)
    return pl.pallas_call(
        _conv_kernel,
        grid=(batch, ns),
        in_specs=[
            pl.BlockSpec((CONV_HALO, 2 * MIX_WIDTH), lambda bb, i: (jnp.maximum((bb * ns + i) * hb - 1, 0), 0)),
            pl.BlockSpec((ts, 2 * MIX_WIDTH), lambda bb, i: (bb * ns + i, 0)),
            pl.BlockSpec((CONV_HALO, 2 * MIX_WIDTH), lambda bb, i: (jnp.minimum((bb * ns + i + 1) * hb, last), 0)),
            pl.BlockSpec((CONV_WIDTH, MIX_WIDTH), fixed),
            pl.BlockSpec((1, MIX_WIDTH), fixed),
            pl.BlockSpec((1, MIX_WIDTH), fixed),
            pl.BlockSpec((1, MIX_WIDTH), fixed),
        ],
        out_specs=pl.BlockSpec((ts, MIX_WIDTH), lambda bb, i: (bb * ns + i, 0)),
        out_shape=jax.ShapeDtypeStruct((t, MIX_WIDTH), BF16),
        scratch_shapes=[pltpu.VMEM((SUBLANES, ts + 2 * CONV_HALO, MIX_WIDTH), F32)],
        compiler_params=_cparams(("parallel", "parallel")),
        name="conv_module",
    )(a_in, a_in, a_in, w, b, g, beta)


def _swa_kernel(q_ref, kp_ref, kc_ref, kn_ref, vp_ref, vc_ref, vn_ref, sink_ref, o_ref):
    n = pl.program_id(1)
    nsteps = pl.num_programs(1)
    bq = BLOCK_Q
    cols = GQA_GROUPS * bq
    jj = lax.broadcasted_iota(jnp.int32, (3 * bq, cols), 0)
    ii = lax.broadcasted_iota(jnp.int32, (3 * bq, cols), 1) % bq
    band = (jj >= ii) & (jj <= ii + 2 * WINDOW)
    kcat = jnp.concatenate([kp_ref[...], kc_ref[...], kn_ref[...]], axis=0)
    vcat_t = jnp.concatenate([vp_ref[...], vc_ref[...], vn_ref[...]], axis=0).astype(F32).T.astype(BF16)
    for qb in range(SWA_BLOCKS):
        mask = band
        if qb == 0:
            mask = mask & (jj >= jnp.where(n > 0, 0, bq))
        if qb == SWA_BLOCKS - 1:
            mask = mask & (jj < jnp.where(n < nsteps - 1, 3 * bq, 2 * bq))
        q_t = q_ref[qb * bq:(qb + 1) * bq, :].astype(F32).T.astype(BF16)
        outs = []
        for hk in range(N_KV_HEADS):
            heads = range(hk * GQA_GROUPS, (hk + 1) * GQA_GROUPS)
            kh = kcat[qb * bq:(qb + 3) * bq, hk * HEAD_DIM:(hk + 1) * HEAD_DIM]
            vh_t = jnp.concatenate([vcat_t[hk * HEAD_DIM:(hk + 1) * HEAD_DIM, qb * bq:(qb + 3) * bq],
                                    jnp.ones((BF16_ROWS, 3 * bq), BF16)], axis=0)
            qs_t = jnp.concatenate([q_t[hd * HEAD_DIM:(hd + 1) * HEAD_DIM, :] for hd in heads], axis=1)
            sink = jnp.concatenate([jnp.broadcast_to(sink_ref[hd:hd + 1, 0:1], (1, bq)) for hd in heads], axis=1)
            s = jnp.where(mask, _dot(kh, qs_t), NEG)
            m = jnp.maximum(jnp.max(s, axis=0, keepdims=True), sink)
            o_ext = _dot(vh_t, jnp.exp(s - m).astype(BF16))
            denom = o_ext[HEAD_DIM:HEAD_DIM + 1, :] + jnp.exp(sink - m)
            o_t = o_ext[:HEAD_DIM, :] / denom
            outs.extend(o_t[:, g * bq:(g + 1) * bq] for g in range(GQA_GROUPS))
        o_ref[qb * bq:(qb + 1) * bq, :] = jnp.concatenate(outs, axis=0).T.astype(o_ref.dtype)


def _swa(q, k, v, sink_b, batch, seq):
    t = q.shape[0]
    bq = BLOCK_Q
    tq = SWA_BLOCKS * bq
    ns = seq // tq
    last = t // bq - 1
    cur = lambda bb, n: (bb * ns + n, 0)
    prev = lambda bb, n: (jnp.maximum((bb * ns + n) * SWA_BLOCKS - 1, 0), 0)
    nxt = lambda bb, n: (jnp.minimum((bb * ns + n + 1) * SWA_BLOCKS, last), 0)
    halo = lambda im: pl.BlockSpec((bq, KV_WIDTH), im)
    body = pl.BlockSpec((tq, KV_WIDTH), cur)
    return pl.pallas_call(
        _swa_kernel,
        grid=(batch, ns),
        in_specs=[pl.BlockSpec((tq, ATTN_WIDTH), cur), halo(prev), body, halo(nxt), halo(prev), body, halo(nxt),
                  pl.BlockSpec((N_Q_HEADS, LANE), lambda bb, n: (0, 0))],
        out_specs=pl.BlockSpec((tq, ATTN_WIDTH), cur),
        out_shape=jax.ShapeDtypeStruct((t, ATTN_WIDTH), BF16),
        compiler_params=_cparams(("parallel", "parallel")),
        name="window_attn",
    )(q, k, k, k, v, v, v, sink_b)


def _dense_attn_kernel(q_ref, k_ref, vt_ref, o_ref):
    ts = q_ref.shape[0] // ATT_CHAINS
    k = k_ref[...]
    v_ext = jnp.concatenate([vt_ref[...], jnp.ones((BF16_ROWS, vt_ref.shape[1]), BF16)], axis=0)
    q_t = q_ref[...].astype(F32).T.astype(BF16)

    def scores(c):
        cols = slice(c * ts, (c + 1) * ts)
        qs_t = jnp.concatenate([q_t[g * HEAD_DIM:(g + 1) * HEAD_DIM, cols] for g in range(GQA_GROUPS)], axis=1)
        return _dot(k, qs_t)

    s_next = scores(0)
    for c in range(ATT_CHAINS):
        cols = slice(c * ts, (c + 1) * ts)
        s = s_next
        if c + 1 < ATT_CHAINS:
            s_next = scores(c + 1)
        p = jnp.exp(s - jnp.max(s, axis=0, keepdims=True))
        o_ext = _dot(v_ext, p.astype(BF16))
        o_t = o_ext[:HEAD_DIM, :] / o_ext[HEAD_DIM:HEAD_DIM + 1, :]
        o_ref[cols, :] = jnp.concatenate(
            [o_t[:, g * ts:(g + 1) * ts] for g in range(GQA_GROUPS)], axis=0).T.astype(o_ref.dtype)


def _dense_attn(q, k, v, batch, seq):
    t = q.shape[0]
    tq = min(ATT_TQ, seq)
    nq = seq // tq
    gw = GQA_GROUPS * HEAD_DIM
    qmap = lambda bb, hk, qi: (bb * nq + qi, hk)
    return pl.pallas_call(
        _dense_attn_kernel,
        grid=(batch, N_KV_HEADS, nq),
        in_specs=[
            pl.BlockSpec((tq, gw), qmap),
            pl.BlockSpec((None, seq, HEAD_DIM), lambda bb, hk, qi: (hk, bb, 0)),
            pl.BlockSpec((None, HEAD_DIM, seq), lambda bb, hk, qi: (hk, 0, bb)),
        ],
        out_specs=pl.BlockSpec((tq, gw), qmap),
        out_shape=jax.ShapeDtypeStruct((t, ATTN_WIDTH), BF16),
        compiler_params=_cparams(("parallel", "parallel", "parallel")),
        name="dense_attn",
    )(q, k, v)


def _sgu_kernel(d_ref, g_ref, b_ref, w_ref, bs_ref, o_ref):
    z = jax.nn.gelu(d_ref[...])
    u = z[:, :MIX_WIDTH]
    v = _layernorm(z[:, MIX_WIDTH:], g_ref[...], b_ref[...]).astype(BF16)
    gc = MIX_WIDTH // SGU_GROUPS
    for n in range(d_ref.shape[0] // CHUNK):
        rows = slice(n * CHUNK, (n + 1) * CHUNK)
        for g in range(SGU_GROUPS):
            cols = slice(g * gc, (g + 1) * gc)
            mixed = _dot(w_ref[g], v[rows, cols]) + bs_ref[g]
            o_ref[rows, cols] = (u[rows, cols] * mixed).astype(o_ref.dtype)


def _sgu(d_in, g, b, w, bs):
    t = d_in.shape[0]
    ts = ROW_TILE
    gc = MIX_WIDTH // SGU_GROUPS
    row = lambda i: (i, 0)
    fixed = lambda i: (0, 0)
    fixed3 = lambda i: (0, 0, 0)
    return pl.pallas_call(
        _sgu_kernel,
        grid=(t // ts,),
        in_specs=[
            pl.BlockSpec((ts, 2 * MIX_WIDTH), row),
            pl.BlockSpec((1, MIX_WIDTH), fixed),
            pl.BlockSpec((1, MIX_WIDTH), fixed),
            pl.BlockSpec((SGU_GROUPS, CHUNK, CHUNK), fixed3),
            pl.BlockSpec((SGU_GROUPS, CHUNK, gc), fixed3),
        ],
        out_specs=pl.BlockSpec((ts, MIX_WIDTH), row),
        out_shape=jax.ShapeDtypeStruct((t, MIX_WIDTH), BF16),
        compiler_params=_cparams(("parallel",)),
        name="spatial_gating",
    )(d_in, g, b, w, bs)


def _outproj_kernel(x_ref, a_ref, b_ref, w_ref, g_ref, xn_ref, hnt_ref):
    y = _dot(a_ref[...], w_ref[:MIX_WIDTH, :]) + _dot(b_ref[...], w_ref[MIX_WIDTH:, :])
    xn = x_ref[...] + y
    xn_ref[...] = xn
    hnt_ref[...] = _rms(xn, g_ref[...]).T.astype(BF16)


def _outproj(x, a, b, w, g):
    t = x.shape[0]
    tm = ROW_TILE
    row = lambda i: (i, 0)
    fixed = lambda i: (0, 0)
    return pl.pallas_call(
        _outproj_kernel,
        grid=(t // tm,),
        in_specs=[
            pl.BlockSpec((tm, D_MODEL), row),
            pl.BlockSpec((tm, MIX_WIDTH), row),
            pl.BlockSpec((tm, MIX_WIDTH), row),
            pl.BlockSpec((2 * MIX_WIDTH, D_MODEL), fixed),
            pl.BlockSpec((1, D_MODEL), fixed),
        ],
        out_specs=[pl.BlockSpec((tm, D_MODEL), row), pl.BlockSpec((D_MODEL, tm), lambda i: (0, i))],
        out_shape=[jax.ShapeDtypeStruct((t, D_MODEL), F32), jax.ShapeDtypeStruct((D_MODEL, t), BF16)],
        compiler_params=_cparams(("parallel",)),
        name="outproj",
    )(x, a, b, w, g)


_PEER_PAIRS = [(i, j) for i in range(PEER_TOPK) for j in range(PEER_TOPK) if (i + 1) * (j + 1) <= PEER_TOPK]
_PEER_CAND_ROWS = -(-len(_PEER_PAIRS) // 8) * 8


SUBLANES = 8


def _bitonic_merge_desc(v, lo, n):
    step = n // 2
    while step >= 1:
        for i in range(lo, lo + n):
            if (i - lo) & step == 0:
                hi_, lo_ = jnp.maximum(v[i], v[i + step]), jnp.minimum(v[i], v[i + step])
                v[i], v[i + step] = hi_, lo_
        step //= 2


def _sort16_desc(v):
    n = len(v)
    size = 2
    while size <= n:
        for lo in range(0, n, size):
            half = size // 2
            v[lo + half:lo + size] = v[lo + half:lo + size][::-1]
            _bitonic_merge_desc(v, lo, size)
        size *= 2
    return v


def _sublane_all(x, op):
    for shift in (4, 2, 1):
        x = op(x, pltpu.roll(x, shift, 0))
    return x


def _top16_sorted(s):
    v = _sort16_desc([s[k * SUBLANES:(k + 1) * SUBLANES, :] for k in range(PEER_NKEYS // SUBLANES)])
    for shift in (4, 2, 1):
        v = [jnp.maximum(v[k], pltpu.roll(v[PEER_TOPK - 1 - k], shift, 0)) for k in range(PEER_TOPK)]
        _bitonic_merge_desc(v, 0, PEER_TOPK)
    return v


def _peer_select(h, s1_all, s2_all, n_ref, e1_ref, r2_ref, e2_ref):
    tt = s1_all.shape[-1]
    nv = PEER_NKEYS // SUBLANES
    sub = lax.broadcasted_iota(jnp.int32, (SUBLANES, LANE), 0)
    for tc in range(tt // LANE):
        lanes = slice(tc * LANE, (tc + 1) * LANE)
        s1 = s1_all[:, lanes]
        s2 = s2_all[:, lanes]
        v1 = _top16_sorted(s1)
        v2 = _top16_sorted(s2)
        pair_sum = {p: v1[p[0]] + v2[p[1]] for p in _PEER_PAIRS}
        cand = []
        for g in range(_PEER_CAND_ROWS // SUBLANES):
            c = jnp.full((SUBLANES, LANE), -jnp.inf, F32)
            for r, p in enumerate(_PEER_PAIRS[g * SUBLANES:(g + 1) * SUBLANES]):
                c = jnp.where(sub == r, pair_sum[p], c)
            cand.append(c)
        f = []
        for _ in range(PEER_TOPK):
            mx = cand[0]
            for c in cand[1:]:
                mx = jnp.maximum(mx, c)
            mx = _sublane_all(mx, jnp.maximum)
            f.append(mx)
            cand = [jnp.where(c == mx, -jnp.inf, c) for c in cand]
        thr = f[PEER_TOPK - 1]
        z = jnp.ones((SUBLANES, LANE), F32)
        for r in range(1, PEER_TOPK):
            z = z + jnp.exp(f[r] - f[0])
        rz = 1.0 / z
        cnt = []
        for i in range(PEER_TOPK):
            c = jnp.zeros((SUBLANES, LANE), F32)
            for j in range(PEER_TOPK // (i + 1)):
                c = c + jnp.where(pair_sum[(i, j)] >= thr, 1.0, 0.0)
            cnt.append(c)
        ranks = []
        e2s = []
        for k in range(nv):
            rows = slice(k * SUBLANES, (k + 1) * SUBLANES)
            a = s1[rows, :]
            b = s2[rows, :]
            n = jnp.zeros((SUBLANES, LANE), F32)
            rank = jnp.full((SUBLANES, LANE), float(PEER_TOPK), F32)
            for r in range(PEER_TOPK):
                n = jnp.where(a == v1[r], cnt[r], n)
                rank = jnp.where(b == v2[r], float(r), rank)
            n_ref[h, rows, lanes] = n
            e1_ref[h, rows, lanes] = jnp.exp(a - v1[0]) * rz
            ranks.append(rank)
            e2s.append(jnp.exp(b - v2[0]))
        r2_ref[h, :, lanes] = jnp.concatenate(ranks, axis=0).astype(r2_ref.dtype)
        e2_ref[h, :, lanes] = jnp.concatenate(e2s, axis=0).astype(e2_ref.dtype)


def _peer_gates(step, sub, n_ref, e1_ref, r2_ref, e2_ref, g_ref):
    tt = g_ref.shape[-1]
    gdt = g_ref.dtype
    first = sub * PEER_SUB
    a8 = pl.multiple_of(step * PEER_A + first // SUBLANES * SUBLANES, SUBLANES)
    nload = -(-(first % SUBLANES + PEER_SUB) // SUBLANES) * SUBLANES
    for tc in range(tt // LANE):
        lanes = slice(tc * LANE, (tc + 1) * LANE)
        nrows = [n_ref[h, pl.ds(a8, nload), lanes].astype(gdt) for h in range(PEER_HEADS)]
        erows = [e1_ref[h, pl.ds(a8, nload), lanes].astype(gdt) for h in range(PEER_HEADS)]
        for blk in range(PEER_SUB // PEER_GATE_BLOCK):
            gates = [jnp.zeros((PEER_NKEYS, LANE), gdt) for _ in range(PEER_GATE_BLOCK)]
            for h in range(PEER_HEADS):
                r2 = r2_ref[h, :, lanes]
                e2 = e2_ref[h, :, lanes]
                for k in range(PEER_GATE_BLOCK):
                    al = first % SUBLANES + blk * PEER_GATE_BLOCK + k
                    keep = r2 < nrows[h][al:al + 1, :]
                    gates[k] = gates[k] + jnp.where(keep, e2, 0.0) * erows[h][al:al + 1, :]
            for k in range(PEER_GATE_BLOCK):
                al = blk * PEER_GATE_BLOCK + k
                g_ref[sub, al * PEER_NKEYS:(al + 1) * PEER_NKEYS, lanes] = gates[k]


def _peer_kernel(x_ref, hnt_ref, wqt_ref, sk_ref, u_ref, vt_ref, o_ref,
                 qt_ref, n_ref, e1_ref, r2_ref, e2_ref, g_ref, acc_ref, ht_ref):
    j = pl.program_id(1)
    nj = pl.num_programs(1)
    half = PEER_DKEY // 2
    gdt = r2_ref.dtype
    gate_refs = (n_ref, e1_ref, r2_ref, e2_ref, g_ref)

    @pl.when(j == 0)
    def _():
        qt_ref[...] = _dot(wqt_ref[...], hnt_ref[...]).astype(BF16)

        def body(h, carry):
            r0 = pl.multiple_of(h * PEER_DKEY, PEER_DKEY)
            s1 = _dot(sk_ref[2 * h], qt_ref[pl.ds(r0, half), :])
            s2 = _dot(sk_ref[2 * h + 1], qt_ref[pl.ds(r0 + half, half), :])
            _peer_select(h, s1, s2, n_ref, e1_ref, r2_ref, e2_ref)
            return carry

        lax.fori_loop(0, PEER_HEADS, body, 0)
        acc_ref[...] = jnp.zeros(acc_ref.shape, F32)

    sub_rows = lambda sub: slice(sub * PEER_SUB * PEER_NKEYS, (sub + 1) * PEER_SUB * PEER_NKEYS)
    for sub in range(PEER_A // PEER_SUB):
        _peer_gates(j, sub, *gate_refs)
    for sub in range(PEER_A // PEER_SUB):
        at = _dot(u_ref[sub_rows(sub), :], hnt_ref[...]).astype(gdt)
        ht_ref[sub_rows(sub), :] = jax.nn.gelu(at) * g_ref[sub]
    acc_ref[...] += _dot(vt_ref[...], ht_ref[...])

    @pl.when(j == nj - 1)
    def _():
        o_ref[...] = x_ref[...] + acc_ref[...].T


def _peer(x, hnt, wqt, sk, u, vt, layer):
    t = x.shape[0]
    tt = PEER_TT
    ne = u.shape[1]
    et = PEER_A * PEER_NKEYS
    hk = PEER_HEADS * PEER_DKEY
    sel = lambda dt: pltpu.VMEM((PEER_HEADS, PEER_NKEYS, tt), dt)
    return pl.pallas_call(
        _peer_kernel,
        grid=(t // tt, ne // et),
        in_specs=[
            pl.BlockSpec((tt, D_MODEL), lambda i, j: (i, 0)),
            pl.BlockSpec((D_MODEL, tt), lambda i, j: (0, i)),
            pl.BlockSpec((hk, D_MODEL), lambda i, j: (0, 0), pipeline_mode=pl.Buffered(1)),
            pl.BlockSpec((2 * PEER_HEADS, PEER_NKEYS, PEER_DKEY // 2), lambda i, j: (0, 0, 0)),
            pl.BlockSpec((None, et, D_MODEL), lambda i, j: (layer, j, 0)),
            pl.BlockSpec((None, D_MODEL, et), lambda i, j: (layer, 0, j)),
        ],
        out_specs=pl.BlockSpec((tt, D_MODEL), lambda i, j: (i, 0)),
        out_shape=jax.ShapeDtypeStruct((t, D_MODEL), F32),
        scratch_shapes=[
            pltpu.VMEM((hk, tt), BF16),
            sel(F32), sel(F32), sel(PEER_GATE_DTYPE), sel(PEER_GATE_DTYPE),
            pltpu.VMEM((PEER_A // PEER_SUB, PEER_SUB * PEER_NKEYS, tt), PEER_GATE_DTYPE),
            pltpu.VMEM((D_MODEL, tt), F32),
            pltpu.VMEM((et, tt), BF16),
        ],
        compiler_params=_cparams(("parallel", "arbitrary")),
        name="peer",
    )(x, hnt, wqt, sk, u, vt)


def _final_norm_kernel(x_ref, g_ref, o_ref):
    o_ref[...] = _rms(x_ref[...], g_ref[...])


def _final_norm(x, g):
    t = x.shape[0]
    tm = ROW_TILE
    return pl.pallas_call(
        _final_norm_kernel,
        grid=(t // tm,),
        in_specs=[pl.BlockSpec((tm, D_MODEL), lambda i: (i, 0)), pl.BlockSpec((1, D_MODEL), lambda i: (0, 0))],
        out_specs=pl.BlockSpec((tm, D_MODEL), lambda i: (i, 0)),
        out_shape=jax.ShapeDtypeStruct((t, D_MODEL), F32),
        compiler_params=_cparams(("parallel",)),
        name="final_norm",
    )(x, g)


def _rope_table(pos, dim):
    inv = ROPE_THETA ** (-jnp.arange(0, dim, 2, dtype=F32) / dim)
    ang = pos.astype(F32)[:, None] * inv[None, :]
    ang = jnp.concatenate([ang, ang], axis=-1)
    sign = jnp.concatenate([-jnp.ones((dim // 2,), F32), jnp.ones((dim // 2,), F32)])
    return jnp.cos(ang), jnp.sin(ang) * sign


def _rope_tables(seq):
    pos = jnp.arange(seq)
    cos1, sin1 = _rope_table(pos, HEAD_DIM)
    cr, sr = _rope_table(pos // GRID_W, HEAD_DIM // 2)
    cc, sc = _rope_table(pos % GRID_W, HEAD_DIM // 2)
    cos2 = jnp.concatenate([cr, cc], axis=-1)
    sin2 = jnp.concatenate([sr, sc], axis=-1)
    tile = lambda a: jnp.tile(a, (1, N_Q_HEADS))
    return tile(cos1), tile(sin1), tile(cos2), tile(sin2)


def kernel(x, mix_norm_g, ffn_norm_g, final_norm_g, even_w_in, even_w_out, conv_w, conv_b, conv_ln_g, conv_ln_b, sink_logits, odd_w_in, odd_w_out, q_norm_g, k_norm_g, sgu_ln_g, sgu_ln_b, sgu_w, sgu_b, peer_wq, peer_subkeys, peer_u, peer_v):
    batch, seq, d = x.shape
    depth = mix_norm_g.shape[0]
    t = batch * seq
    cos1, sin1, cos2, sin2 = _rope_tables(seq)
    row = lambda a: a.reshape(1, -1).astype(F32)
    xf = x.reshape(t, d)
    u_all = peer_u.astype(BF16)
    vt_all = jnp.swapaxes(peer_v.astype(BF16), 1, 2)
    for layer in range(depth):
        i = layer // 2
        if layer % 2 == 0:
            a_in, q, k, v = _inproj_even(xf, row(mix_norm_g[layer]), even_w_in[i].astype(BF16), cos1, sin1, seq)
            m1 = _conv_module(a_in, conv_w[i, :, 0, :], row(conv_b[i]), row(conv_ln_g[i]), row(conv_ln_b[i]), batch, seq)
            sink_b = jnp.broadcast_to(sink_logits[i].astype(F32)[:, None], (N_Q_HEADS, LANE))
            m2 = _swa(q, k, v, sink_b, batch, seq)
            w_out = even_w_out[i]
        else:
            q, k, v, d_in = _inproj_odd(xf, row(mix_norm_g[layer]), odd_w_in[i].astype(BF16),
                                        row(jnp.tile(q_norm_g[i], N_Q_HEADS)), row(jnp.tile(k_norm_g[i], N_KV_HEADS)),
                                        cos2, sin2, seq)
            m1 = _dense_attn(q, k, v, batch, seq)
            bs = jnp.broadcast_to(sgu_b[i].astype(F32)[:, :, None], (SGU_GROUPS, CHUNK, MIX_WIDTH // SGU_GROUPS))
            m2 = _sgu(d_in, row(sgu_ln_g[i]), row(sgu_ln_b[i]), sgu_w[i].astype(BF16), bs)
            w_out = odd_w_out[i]
        xn, hnt = _outproj(xf, m1, m2, w_out.astype(BF16), row(ffn_norm_g[layer]))
        sk = peer_subkeys[layer].reshape(2 * PEER_HEADS, PEER_NKEYS, PEER_DKEY // 2).astype(BF16)
        xf = _peer(xn, hnt, peer_wq[layer].astype(BF16).T, sk, u_all, vt_all, layer)
    return _final_norm(xf, row(final_norm_g)).reshape(batch, seq, d)
```

```python
import jax
import jax.numpy as jnp
from jax import lax
from jax.experimental import pallas as pl
from jax.experimental.pallas import tpu as pltpu

F32 = jnp.float32
BF16 = jnp.bfloat16

D_MODEL = 1024
HEAD_DIM = 64
N_Q_HEADS = 8
N_KV_HEADS = 2
GQA_GROUPS = N_Q_HEADS // N_KV_HEADS
ATTN_WIDTH = N_Q_HEADS * HEAD_DIM
KV_WIDTH = N_KV_HEADS * HEAD_DIM
MIX_WIDTH = D_MODEL // 2
CONV_WIDTH = 31
CONV_HALO = 16
SGU_GROUPS = 4
CHUNK = 128
WINDOW = 128
BLOCK_Q = 128
GRID_W = 64
ROPE_THETA = 10000.0
PEER_HEADS = 8
PEER_NKEYS = 128
PEER_DKEY = 128
PEER_TOPK = 16
EPS = 1e-6
NEG = -1e30

VMEM_LIMIT_BYTES = 56 * 1024 * 1024

ROW_TILE = 512
CONV_ROWS = 64
SWA_BLOCKS = 4
ATT_TQ = 1024
ATT_CHAINS = 8
PEER_TT = 512
PEER_A = 16
PEER_SUB = 8
PEER_GATE_BLOCK = 4
PEER_GATE_DTYPE = BF16
LANE = 128
BF16_ROWS = 16


def _cparams(sem):
    return pltpu.CompilerParams(dimension_semantics=sem, vmem_limit_bytes=VMEM_LIMIT_BYTES)


def _dot(a, b):
    return jnp.dot(a, b, preferred_element_type=F32)


def _rms(x, g):
    return x * lax.rsqrt(jnp.mean(x * x, axis=-1, keepdims=True) + EPS) * g


def _layernorm(x, g, b):
    mu = jnp.mean(x, axis=-1, keepdims=True)
    xc = x - mu
    var = jnp.mean(xc * xc, axis=-1, keepdims=True)
    return xc * lax.rsqrt(var + EPS) * g + b


def _rope(x, cos, sin_signed, group, half):
    n = x.shape[-1]
    lane = lax.broadcasted_iota(jnp.int32, x.shape, x.ndim - 1)
    first = (lane % group) < half
    partner = jnp.where(first, pltpu.roll(x, n - half, x.ndim - 1), pltpu.roll(x, half, x.ndim - 1))
    return x * cos + partner * sin_signed


def _inproj_even_kernel(x_ref, g_ref, w_ref, cos_ref, sin_ref, a_ref, q_ref, k_ref, v_ref):
    h = _rms(x_ref[...], g_ref[...]).astype(BF16)
    o1 = 2 * MIX_WIDTH
    o2 = o1 + ATTN_WIDTH
    o3 = o2 + KV_WIDTH
    a_ref[...] = _dot(h, w_ref[:, :o1])
    cos = cos_ref[...]
    sin = sin_ref[...]
    q = _rope(_dot(h, w_ref[:, o1:o2]), cos, sin, HEAD_DIM, HEAD_DIM // 2)
    q_ref[...] = (q * (HEAD_DIM ** -0.5)).astype(BF16)
    k = _rope(_dot(h, w_ref[:, o2:o3]), cos[:, :KV_WIDTH], sin[:, :KV_WIDTH], HEAD_DIM, HEAD_DIM // 2)
    k_ref[...] = k.astype(BF16)
    v_ref[...] = _dot(h, w_ref[:, o3:]).astype(BF16)


def _inproj_even(x, g, w, cos, sin, seq):
    t = x.shape[0]
    tm = ROW_TILE
    nseq = seq // tm
    in_w = w.shape[1]
    row = lambda i: (i, 0)
    fixed = lambda i: (0, 0)
    pos = lambda i: (i % nseq, 0)
    return pl.pallas_call(
        _inproj_even_kernel,
        grid=(t // tm,),
        in_specs=[
            pl.BlockSpec((tm, D_MODEL), row),
            pl.BlockSpec((1, D_MODEL), fixed),
            pl.BlockSpec((D_MODEL, in_w), fixed),
            pl.BlockSpec((tm, ATTN_WIDTH), pos),
            pl.BlockSpec((tm, ATTN_WIDTH), pos),
        ],
        out_specs=[
            pl.BlockSpec((tm, 2 * MIX_WIDTH), row),
            pl.BlockSpec((tm, ATTN_WIDTH), row),
            pl.BlockSpec((tm, KV_WIDTH), row),
            pl.BlockSpec((tm, KV_WIDTH), row),
        ],
        out_shape=[
            jax.ShapeDtypeStruct((t, 2 * MIX_WIDTH), F32),
            jax.ShapeDtypeStruct((t, ATTN_WIDTH), BF16),
            jax.ShapeDtypeStruct((t, KV_WIDTH), BF16),
            jax.ShapeDtypeStruct((t, KV_WIDTH), BF16),
        ],
        compiler_params=_cparams(("parallel",)),
        name="inproj_even",
    )(x, g, w, cos, sin)


def _head_sumsq(x):
    n = x.shape[-1]
    r = lax.broadcasted_iota(jnp.int32, (n, n), 0) // HEAD_DIM
    c = lax.broadcasted_iota(jnp.int32, (n, n), 1) // HEAD_DIM
    ones = jnp.where(r == c, 1.0, 0.0).astype(BF16)
    sq = x * x
    hi = sq.astype(BF16)
    lo = (sq - hi.astype(F32)).astype(BF16)
    return _dot(hi, ones) + _dot(lo, ones)


def _inproj_odd_kernel(x_ref, g_ref, w_ref, qg_ref, kg_ref, cos_ref, sin_ref, q_ref, k_ref, vt_ref, d_ref):
    h = _rms(x_ref[...], g_ref[...]).astype(BF16)
    c1 = ATTN_WIDTH
    c2 = c1 + KV_WIDTH
    c3 = c2 + KV_WIDTH
    cos = cos_ref[...]
    sin = sin_ref[...]
    half = HEAD_DIM // 2
    q = _dot(h, w_ref[:, :c1])
    q = q * lax.rsqrt(_head_sumsq(q) * (1.0 / HEAD_DIM) + EPS) * qg_ref[...]
    q = _rope(q, cos, sin, half, half // 2)
    q_ref[...] = (q * (HEAD_DIM ** -0.5)).astype(BF16)
    k = _dot(h, w_ref[:, c1:c2])
    k = k * lax.rsqrt(_head_sumsq(k) * (1.0 / HEAD_DIM) + EPS) * kg_ref[...]
    k = _rope(k, cos[:, :KV_WIDTH], sin[:, :KV_WIDTH], half, half // 2).astype(BF16)
    v_t = _dot(h, w_ref[:, c2:c3]).T.astype(BF16)
    for hh in range(N_KV_HEADS):
        k_ref[hh] = k[:, hh * HEAD_DIM:(hh + 1) * HEAD_DIM]
        vt_ref[hh] = v_t[hh * HEAD_DIM:(hh + 1) * HEAD_DIM, :]
    d_ref[...] = _dot(h, w_ref[:, c3:])


def _inproj_odd(x, g, w, qg, kg, cos, sin, seq):
    t = x.shape[0]
    tm = ROW_TILE
    nseq = seq // tm
    in_w = w.shape[1]
    row = lambda i: (i, 0)
    fixed = lambda i: (0, 0)
    pos = lambda i: (i % nseq, 0)
    kv = lambda i: (0, i, 0)
    return pl.pallas_call(
        _inproj_odd_kernel,
        grid=(t // tm,),
        in_specs=[
            pl.BlockSpec((tm, D_MODEL), row),
            pl.BlockSpec((1, D_MODEL), fixed),
            pl.BlockSpec((D_MODEL, in_w), fixed),
            pl.BlockSpec((1, ATTN_WIDTH), fixed),
            pl.BlockSpec((1, KV_WIDTH), fixed),
            pl.BlockSpec((tm, ATTN_WIDTH), pos),
            pl.BlockSpec((tm, ATTN_WIDTH), pos),
        ],
        out_specs=[
            pl.BlockSpec((tm, ATTN_WIDTH), row),
            pl.BlockSpec((N_KV_HEADS, tm, HEAD_DIM), kv),
            pl.BlockSpec((N_KV_HEADS, HEAD_DIM, tm), lambda i: (0, 0, i)),
            pl.BlockSpec((tm, 2 * MIX_WIDTH), row),
        ],
        out_shape=[
            jax.ShapeDtypeStruct((t, ATTN_WIDTH), BF16),
            jax.ShapeDtypeStruct((N_KV_HEADS, t, HEAD_DIM), BF16),
            jax.ShapeDtypeStruct((N_KV_HEADS, HEAD_DIM, t), BF16),
            jax.ShapeDtypeStruct((t, 2 * MIX_WIDTH), F32),
        ],
        compiler_params=_cparams(("parallel",)),
        name="inproj_odd",
    )(x, g, w, qg, kg, cos, sin)


def _glu(x):
    return x[:, :MIX_WIDTH] * jax.nn.sigmoid(x[:, MIX_WIDTH:])


def _conv_kernel(prev_ref, cur_ref, next_ref, w_ref, b_ref, g_ref, beta_ref, o_ref, buf_ref):
    i = pl.program_id(1)
    ns = pl.num_programs(1)
    ts = cur_ref.shape[0]
    halo = CONV_HALO
    buf_ref[0, 0:halo, :] = jnp.where(i > 0, _glu(prev_ref[...]), 0.0)
    buf_ref[0, halo:halo + ts, :] = _glu(cur_ref[...])
    buf_ref[0, halo + ts:, :] = jnp.where(i < ns - 1, _glu(next_ref[...]), 0.0)
    span = ts + 2 * halo - SUBLANES
    for s in range(1, SUBLANES):
        buf_ref[s, 0:span, :] = buf_ref[0, s:s + span, :]
    w = w_ref[...]
    bias = b_ref[...]
    g = g_ref[...]
    beta = beta_ref[...]
    first = halo - CONV_WIDTH // 2
    for c in range(ts // CONV_ROWS):
        r0 = c * CONV_ROWS
        acc = jnp.zeros((CONV_ROWS, MIX_WIDTH), F32)
        for k in range(CONV_WIDTH):
            m, s = divmod(first + k, SUBLANES)
            lo = r0 + SUBLANES * m
            acc = acc + w[k:k + 1, :] * buf_ref[s, lo:lo + CONV_ROWS, :]
        y = _layernorm(acc + bias, g, beta)
        o_ref[r0:r0 + CONV_ROWS, :] = (y * jax.nn.sigmoid(y)).astype(o_ref.dtype)


def _conv_module(a_in, w, b, g, beta, batch, seq):
    t = a_in.shape[0]
    ts = ROW_TILE
    ns = seq // ts
    hb = ts // CONV_HALO
    last = t // CONV_HALO - 1
    fixed = lambda bb, i: (0, 0)
    return pl.pallas_call(
        _conv_kernel,
        grid=(batch, ns),
        in_specs=[
            pl.BlockSpec((CONV_HALO, 2 * MIX_WIDTH), lambda bb, i: (jnp.maximum((bb * ns + i) * hb - 1, 0), 0)),
            pl.BlockSpec((ts, 2 * MIX_WIDTH), lambda bb, i: (bb * ns + i, 0)),
            pl.BlockSpec((CONV_HALO, 2 * MIX_WIDTH), lambda bb, i: (jnp.minimum((bb * ns + i + 1) * hb, last), 0)),
            pl.BlockSpec((CONV_WIDTH, MIX_WIDTH), fixed),
            pl.BlockSpec((1, MIX_WIDTH), fixed),
            pl.BlockSpec((1, MIX_WIDTH), fixed),
            pl.BlockSpec((1, MIX_WIDTH), fixed),
        ],
        out_specs=pl.BlockSpec((ts, MIX_WIDTH), lambda bb, i: (bb * ns + i, 0)),
        out_shape=jax.ShapeDtypeStruct((t, MIX_WIDTH), BF16),
        scratch_shapes=[pltpu.VMEM((SUBLANES, ts + 2 * CONV_HALO, MIX_WIDTH), F32)],
        compiler_params=_cparams(("parallel", "parallel")),
        name="conv_module",
    )(a_in, a_in, a_in, w, b, g, beta)


def _swa_kernel(q_ref, kp_ref, kc_ref, kn_ref, vp_ref, vc_ref, vn_ref, sink_ref, o_ref):
    n = pl.program_id(1)
    nsteps = pl.num_programs(1)
    bq = BLOCK_Q
    cols = GQA_GROUPS * bq
    jj = lax.broadcasted_iota(jnp.int32, (3 * bq, cols), 0)
    ii = lax.broadcasted_iota(jnp.int32, (3 * bq, cols), 1) % bq
    band = (jj >= ii) & (jj <= ii + 2 * WINDOW)
    kcat = jnp.concatenate([kp_ref[...], kc_ref[...], kn_ref[...]], axis=0)
    vcat_t = jnp.concatenate([vp_ref[...], vc_ref[...], vn_ref[...]], axis=0).astype(F32).T.astype(BF16)
    for qb in range(SWA_BLOCKS):
        mask = band
        if qb == 0:
            mask = mask & (jj >= jnp.where(n > 0, 0, bq))
        if qb == SWA_BLOCKS - 1:
            mask = mask & (jj < jnp.where(n < nsteps - 1, 3 * bq, 2 * bq))
        q_t = q_ref[qb * bq:(qb + 1) * bq, :].astype(F32).T.astype(BF16)
        outs = []
        for hk in range(N_KV_HEADS):
            heads = range(hk * GQA_GROUPS, (hk + 1) * GQA_GROUPS)
            kh = kcat[qb * bq:(qb + 3) * bq, hk * HEAD_DIM:(hk + 1) * HEAD_DIM]
            vh_t = jnp.concatenate([vcat_t[hk * HEAD_DIM:(hk + 1) * HEAD_DIM, qb * bq:(qb + 3) * bq],
                                    jnp.ones((BF16_ROWS, 3 * bq), BF16)], axis=0)
            qs_t = jnp.concatenate([q_t[hd * HEAD_DIM:(hd + 1) * HEAD_DIM, :] for hd in heads], axis=1)
            sink = jnp.concatenate([jnp.broadcast_to(sink_ref[hd:hd + 1, 0:1], (1, bq)) for hd in heads], axis=1)
            s = jnp.where(mask, _dot(kh, qs_t), NEG)
            m = jnp.maximum(jnp.max(s, axis=0, keepdims=True), sink)
            o_ext = _dot(vh_t, jnp.exp(s - m).astype(BF16))
            denom = o_ext[HEAD_DIM:HEAD_DIM + 1, :] + jnp.exp(sink - m)
            o_t = o_ext[:HEAD_DIM, :] / denom
            outs.extend(o_t[:, g * bq:(g + 1) * bq] for g in range(GQA_GROUPS))
        o_ref[qb * bq:(qb + 1) * bq, :] = jnp.concatenate(outs, axis=0).T.astype(o_ref.dtype)


def _swa(q, k, v, sink_b, batch, seq):
    t = q.shape[0]
    bq = BLOCK_Q
    tq = SWA_BLOCKS * bq
    ns = seq // tq
    last = t // bq - 1
    cur = lambda bb, n: (bb * ns + n, 0)
    prev = lambda bb, n: (jnp.maximum((bb * ns + n) * SWA_BLOCKS - 1, 0), 0)
    nxt = lambda bb, n: (jnp.minimum((bb * ns + n + 1) * SWA_BLOCKS, last), 0)
    halo = lambda im: pl.BlockSpec((bq, KV_WIDTH), im)
    body = pl.BlockSpec((tq, KV_WIDTH), cur)
    return pl.pallas_call(
        _swa_kernel,
        grid=(batch, ns),
        in_specs=[pl.BlockSpec((tq, ATTN_WIDTH), cur), halo(prev), body, halo(nxt), halo(prev), body, halo(nxt),
                  pl.BlockSpec((N_Q_HEADS, LANE), lambda bb, n: (0, 0))],
        out_specs=pl.BlockSpec((tq, ATTN_WIDTH), cur),
        out_shape=jax.ShapeDtypeStruct((t, ATTN_WIDTH), BF16),
        compiler_params=_cparams(("parallel", "parallel")),
        name="window_attn",
    )(q, k, k, k, v, v, v, sink_b)


def _dense_attn_kernel(q_ref, k_ref, vt_ref, o_ref):
    ts = q_ref.shape[0] // ATT_CHAINS
    k = k_ref[...]
    v_ext = jnp.concatenate([vt_ref[...], jnp.ones((BF16_ROWS, vt_ref.shape[1]), BF16)], axis=0)
    q_t = q_ref[...].astype(F32).T.astype(BF16)

    def scores(c):
        cols = slice(c * ts, (c + 1) * ts)
        qs_t = jnp.concatenate([q_t[g * HEAD_DIM:(g + 1) * HEAD_DIM, cols] for g in range(GQA_GROUPS)], axis=1)
        return _dot(k, qs_t)

    s_next = scores(0)
    for c in range(ATT_CHAINS):
        cols = slice(c * ts, (c + 1) * ts)
        s = s_next
        if c + 1 < ATT_CHAINS:
            s_next = scores(c + 1)
        p = jnp.exp(s - jnp.max(s, axis=0, keepdims=True))
        o_ext = _dot(v_ext, p.astype(BF16))
        o_t = o_ext[:HEAD_DIM, :] / o_ext[HEAD_DIM:HEAD_DIM + 1, :]
        o_ref[cols, :] = jnp.concatenate(
            [o_t[:, g * ts:(g + 1) * ts] for g in range(GQA_GROUPS)], axis=0).T.astype(o_ref.dtype)


def _dense_attn(q, k, v, batch, seq):
    t = q.shape[0]
    tq = min(ATT_TQ, seq)
    nq = seq // tq
    gw = GQA_GROUPS * HEAD_DIM
    qmap = lambda bb, hk, qi: (bb * nq + qi, hk)
    return pl.pallas_call(
        _dense_attn_kernel,
        grid=(batch, N_KV_HEADS, nq),
        in_specs=[
            pl.BlockSpec((tq, gw), qmap),
            pl.BlockSpec((None, seq, HEAD_DIM), lambda bb, hk, qi: (hk, bb, 0)),
            pl.BlockSpec((None, HEAD_DIM, seq), lambda bb, hk, qi: (hk, 0, bb)),
        ],
        out_specs=pl.BlockSpec((tq, gw), qmap),
        out_shape=jax.ShapeDtypeStruct((t, ATTN_WIDTH), BF16),
        compiler_params=_cparams(("parallel", "parallel", "parallel")),
        name="dense_attn",
    )(q, k, v)


def _sgu_kernel(d_ref, g_ref, b_ref, w_ref, bs_ref, o_ref):
    z = jax.nn.gelu(d_ref[...])
    u = z[:, :MIX_WIDTH]
    v = _layernorm(z[:, MIX_WIDTH:], g_ref[...], b_ref[...]).astype(BF16)
    gc = MIX_WIDTH // SGU_GROUPS
    for n in range(d_ref.shape[0] // CHUNK):
        rows = slice(n * CHUNK, (n + 1) * CHUNK)
        for g in range(SGU_GROUPS):
            cols = slice(g * gc, (g + 1) * gc)
            mixed = _dot(w_ref[g], v[rows, cols]) + bs_ref[g]
            o_ref[rows, cols] = (u[rows, cols] * mixed).astype(o_ref.dtype)


def _sgu(d_in, g, b, w, bs):
    t = d_in.shape[0]
    ts = ROW_TILE
    gc = MIX_WIDTH // SGU_GROUPS
    row = lambda i: (i, 0)
    fixed = lambda i: (0, 0)
    fixed3 = lambda i: (0, 0, 0)
    return pl.pallas_call(
        _sgu_kernel,
        grid=(t // ts,),
        in_specs=[
            pl.BlockSpec((ts, 2 * MIX_WIDTH), row),
            pl.BlockSpec((1, MIX_WIDTH), fixed),
            pl.BlockSpec((1, MIX_WIDTH), fixed),
            pl.BlockSpec((SGU_GROUPS, CHUNK, CHUNK), fixed3),
            pl.BlockSpec((SGU_GROUPS, CHUNK, gc), fixed3),
        ],
        out_specs=pl.BlockSpec((ts, MIX_WIDTH), row),
        out_shape=jax.ShapeDtypeStruct((t, MIX_WIDTH), BF16),
        compiler_params=_cparams(("parallel",)),
        name="spatial_gating",
    )(d_in, g, b, w, bs)


def _outproj_kernel(x_ref, a_ref, b_ref, w_ref, g_ref, xn_ref, hnt_ref):
    y = _dot(a_ref[...], w_ref[:MIX_WIDTH, :]) + _dot(b_ref[...], w_ref[MIX_WIDTH:, :])
    xn = x_ref[...] + y
    xn_ref[...] = xn
    hnt_ref[...] = _rms(xn, g_ref[...]).T.astype(BF16)


def _outproj(x, a, b, w, g):
    t = x.shape[0]
    tm = ROW_TILE
    row = lambda i: (i, 0)
    fixed = lambda i: (0, 0)
    return pl.pallas_call(
        _outproj_kernel,
        grid=(t // tm,),
        in_specs=[
            pl.BlockSpec((tm, D_MODEL), row),
            pl.BlockSpec((tm, MIX_WIDTH), row),
            pl.BlockSpec((tm, MIX_WIDTH), row),
            pl.BlockSpec((2 * MIX_WIDTH, D_MODEL), fixed),
            pl.BlockSpec((1, D_MODEL), fixed),
        ],
        out_specs=[pl.BlockSpec((tm, D_MODEL), row), pl.BlockSpec((D_MODEL, tm), lambda i: (0, i))],
        out_shape=[jax.ShapeDtypeStruct((t, D_MODEL), F32), jax.ShapeDtypeStruct((D_MODEL, t), BF16)],
        compiler_params=_cparams(("parallel",)),
        name="outproj",
    )(x, a, b, w, g)


_PEER_PAIRS = [(i, j) for i in range(PEER_TOPK) for j in range(PEER_TOPK) if (i + 1) * (j + 1) <= PEER_TOPK]
_PEER_CAND_ROWS = -(-len(_PEER_PAIRS) // 8) * 8


SUBLANES = 8


def _bitonic_merge_desc(v, lo, n):
    step = n // 2
    while step >= 1:
        for i in range(lo, lo + n):
            if (i - lo) & step == 0:
                hi_, lo_ = jnp.maximum(v[i], v[i + step]), jnp.minimum(v[i], v[i + step])
                v[i], v[i + step] = hi_, lo_
        step //= 2


def _sort16_desc(v):
    n = len(v)
    size = 2
    while size <= n:
        for lo in range(0, n, size):
            half = size // 2
            v[lo + half:lo + size] = v[lo + half:lo + size][::-1]
            _bitonic_merge_desc(v, lo, size)
        size *= 2
    return v


def _sublane_all(x, op):
    for shift in (4, 2, 1):
        x = op(x, pltpu.roll(x, shift, 0))
    return x


def _top16_sorted(s):
    v = _sort16_desc([s[k * SUBLANES:(k + 1) * SUBLANES, :] for k in range(PEER_NKEYS // SUBLANES)])
    for shift in (4, 2, 1):
        v = [jnp.maximum(v[k], pltpu.roll(v[PEER_TOPK - 1 - k], shift, 0)) for k in range(PEER_TOPK)]
        _bitonic_merge_desc(v, 0, PEER_TOPK)
    return v


def _peer_select(h, s1_all, s2_all, n_ref, e1_ref, r2_ref, e2_ref):
    tt = s1_all.shape[-1]
    nv = PEER_NKEYS // SUBLANES
    sub = lax.broadcasted_iota(jnp.int32, (SUBLANES, LANE), 0)
    for tc in range(tt // LANE):
        lanes = slice(tc * LANE, (tc + 1) * LANE)
        s1 = s1_all[:, lanes]
        s2 = s2_all[:, lanes]
        v1 = _top16_sorted(s1)
        v2 = _top16_sorted(s2)
        pair_sum = {p: v1[p[0]] + v2[p[1]] for p in _PEER_PAIRS}
        cand = []
        for g in range(_PEER_CAND_ROWS // SUBLANES):
            c = jnp.full((SUBLANES, LANE), -jnp.inf, F32)
            for r, p in enumerate(_PEER_PAIRS[g * SUBLANES:(g + 1) * SUBLANES]):
                c = jnp.where(sub == r, pair_sum[p], c)
            cand.append(c)
        f = []
        for _ in range(PEER_TOPK):
            mx = cand[0]
            for c in cand[1:]:
                mx = jnp.maximum(mx, c)
            mx = _sublane_all(mx, jnp.maximum)
            f.append(mx)
            cand = [jnp.where(c == mx, -jnp.inf, c) for c in cand]
        thr = f[PEER_TOPK - 1]
        z = jnp.ones((SUBLANES, LANE), F32)
        for r in range(1, PEER_TOPK):
            z = z + jnp.exp(f[r] - f[0])
        rz = 1.0 / z
        cnt = []
        for i in range(PEER_TOPK):
            c = jnp.zeros((SUBLANES, LANE), F32)
            for j in range(PEER_TOPK // (i + 1)):
                c = c + jnp.where(pair_sum[(i, j)] >= thr, 1.0, 0.0)
            cnt.append(c)
        ranks = []
        e2s = []
        for k in range(nv):
            rows = slice(k * SUBLANES, (k + 1) * SUBLANES)
            a = s1[rows, :]
            b = s2[rows, :]
            n = jnp.zeros((SUBLANES, LANE), F32)
            rank = jnp.full((SUBLANES, LANE), float(PEER_TOPK), F32)
            for r in range(PEER_TOPK):
                n = jnp.where(a == v1[r], cnt[r], n)
                rank = jnp.where(b == v2[r], float(r), rank)
            n_ref[h, rows, lanes] = n
            e1_ref[h, rows, lanes] = jnp.exp(a - v1[0]) * rz
            ranks.append(rank)
            e2s.append(jnp.exp(b - v2[0]))
        r2_ref[h, :, lanes] = jnp.concatenate(ranks, axis=0).astype(r2_ref.dtype)
        e2_ref[h, :, lanes] = jnp.concatenate(e2s, axis=0).astype(e2_ref.dtype)


def _peer_gates(step, sub, n_ref, e1_ref, r2_ref, e2_ref, g_ref):
    tt = g_ref.shape[-1]
    gdt = g_ref.dtype
    first = sub * PEER_SUB
    a8 = pl.multiple_of(step * PEER_A + first // SUBLANES * SUBLANES, SUBLANES)
    nload = -(-(first % SUBLANES + PEER_SUB) // SUBLANES) * SUBLANES
    for tc in range(tt // LANE):
        lanes = slice(tc * LANE, (tc + 1) * LANE)
        nrows = [n_ref[h, pl.ds(a8, nload), lanes].astype(gdt) for h in range(PEER_HEADS)]
        erows = [e1_ref[h, pl.ds(a8, nload), lanes].astype(gdt) for h in range(PEER_HEADS)]
        for blk in range(PEER_SUB // PEER_GATE_BLOCK):
            gates = [jnp.zeros((PEER_NKEYS, LANE), gdt) for _ in range(PEER_GATE_BLOCK)]
            for h in range(PEER_HEADS):
                r2 = r2_ref[h, :, lanes]
                e2 = e2_ref[h, :, lanes]
                for k in range(PEER_GATE_BLOCK):
                    al = first % SUBLANES + blk * PEER_GATE_BLOCK + k
                    keep = r2 < nrows[h][al:al + 1, :]
                    gates[k] = gates[k] + jnp.where(keep, e2, 0.0) * erows[h][al:al + 1, :]
            for k in range(PEER_GATE_BLOCK):
                al = blk * PEER_GATE_BLOCK + k
                g_ref[sub, al * PEER_NKEYS:(al + 1) * PEER_NKEYS, lanes] = gates[k]


def _peer_kernel(x_ref, hnt_ref, wqt_ref, sk_ref, u_ref, vt_ref, o_ref,
                 qt_ref, n_ref, e1_ref, r2_ref, e2_ref, g_ref, acc_ref, ht_ref):
    j = pl.program_id(1)
    nj = pl.num_programs(1)
    half = PEER_DKEY // 2
    gdt = r2_ref.dtype
    gate_refs = (n_ref, e1_ref, r2_ref, e2_ref, g_ref)

    @pl.when(j == 0)
    def _():
        qt_ref[...] = _dot(wqt_ref[...], hnt_ref[...]).astype(BF16)

        def body(h, carry):
            r0 = pl.multiple_of(h * PEER_DKEY, PEER_DKEY)
            s1 = _dot(sk_ref[2 * h], qt_ref[pl.ds(r0, half), :])
            s2 = _dot(sk_ref[2 * h + 1], qt_ref[pl.ds(r0 + half, half), :])
            _peer_select(h, s1, s2, n_ref, e1_ref, r2_ref, e2_ref)
            return carry

        lax.fori_loop(0, PEER_HEADS, body, 0)
        acc_ref[...] = jnp.zeros(acc_ref.shape, F32)

    sub_rows = lambda sub: slice(sub * PEER_SUB * PEER_NKEYS, (sub + 1) * PEER_SUB * PEER_NKEYS)
    for sub in range(PEER_A // PEER_SUB):
        _peer_gates(j, sub, *gate_refs)
    for sub in range(PEER_A // PEER_SUB):
        at = _dot(u_ref[sub_rows(sub), :], hnt_ref[...]).astype(gdt)
        ht_ref[sub_rows(sub), :] = jax.nn.gelu(at) * g_ref[sub]
    acc_ref[...] += _dot(vt_ref[...], ht_ref[...])

    @pl.when(j == nj - 1)
    def _():
        o_ref[...] = x_ref[...] + acc_ref[...].T


def _peer(x, hnt, wqt, sk, u, vt, layer):
    t = x.shape[0]
    tt = PEER_TT
    ne = u.shape[1]
    et = PEER_A * PEER_NKEYS
    hk = PEER_HEADS * PEER_DKEY
    sel = lambda dt: pltpu.VMEM((PEER_HEADS, PEER_NKEYS, tt), dt)
    return pl.pallas_call(
        _peer_kernel,
        grid=(t // tt, ne // et),
        in_specs=[
            pl.BlockSpec((tt, D_MODEL), lambda i, j: (i, 0)),
            pl.BlockSpec((D_MODEL, tt), lambda i, j: (0, i)),
            pl.BlockSpec((hk, D_MODEL), lambda i, j: (0, 0), pipeline_mode=pl.Buffered(1)),
            pl.BlockSpec((2 * PEER_HEADS, PEER_NKEYS, PEER_DKEY // 2), lambda i, j: (0, 0, 0)),
            pl.BlockSpec((None, et, D_MODEL), lambda i, j: (layer, j, 0)),
            pl.BlockSpec((None, D_MODEL, et), lambda i, j: (layer, 0, j)),
        ],
        out_specs=pl.BlockSpec((tt, D_MODEL), lambda i, j: (i, 0)),
        out_shape=jax.ShapeDtypeStruct((t, D_MODEL), F32),
        scratch_shapes=[
            pltpu.VMEM((hk, tt), BF16),
            sel(F32), sel(F32), sel(PEER_GATE_DTYPE), sel(PEER_GATE_DTYPE),
            pltpu.VMEM((PEER_A // PEER_SUB, PEER_SUB * PEER_NKEYS, tt), PEER_GATE_DTYPE),
            pltpu.VMEM((D_MODEL, tt), F32),
            pltpu.VMEM((et, tt), BF16),
        ],
        compiler_params=_cparams(("parallel", "arbitrary")),
        name="peer",
    )(x, hnt, wqt, sk, u, vt)


def _final_norm_kernel(x_ref, g_ref, o_ref):
    o_ref[...] = _rms(x_ref[...], g_ref[...])


def _final_norm(x, g):
    t = x.shape[0]
    tm = ROW_TILE
    return pl.pallas_call(
        _final_norm_kernel,
        grid=(t // tm,),
        in_specs=[pl.BlockSpec((tm, D_MODEL), lambda i: (i, 0)), pl.BlockSpec((1, D_MODEL), lambda i: (0, 0))],
        out_specs=pl.BlockSpec((tm, D_MODEL), lambda i: (i, 0)),
        out_shape=jax.ShapeDtypeStruct((t, D_MODEL), F32),
        compiler_params=_cparams(("parallel",)),
        name="final_norm",
    )(x, g)


def _rope_table(pos, dim):
    inv = ROPE_THETA ** (-jnp.arange(0, dim, 2, dtype=F32) / dim)
    ang = pos.astype(F32)[:, None] * inv[None, :]
    ang = jnp.concatenate([ang, ang], axis=-1)
    sign = jnp.concatenate([-jnp.ones((dim // 2,), F32), jnp.ones((dim // 2,), F32)])
    return jnp.cos(ang), jnp.sin(ang) * sign


def _rope_tables(seq):
    pos = jnp.arange(seq)
    cos1, sin1 = _rope_table(pos, HEAD_DIM)
    cr, sr = _rope_table(pos // GRID_W, HEAD_DIM // 2)
    cc, sc = _rope_table(pos % GRID_W, HEAD_DIM // 2)
    cos2 = jnp.concatenate([cr, cc], axis=-1)
    sin2 = jnp.concatenate([sr, sc], axis=-1)
    tile = lambda a: jnp.tile(a, (1, N_Q_HEADS))
    return tile(cos1), tile(sin1), tile(cos2), tile(sin2)


def kernel(x, mix_norm_g, ffn_norm_g, final_norm_g, even_w_in, even_w_out, conv_w, conv_b, conv_ln_g, conv_ln_b, sink_logits, odd_w_in, odd_w_out, q_norm_g, k_norm_g, sgu_ln_g, sgu_ln_b, sgu_w, sgu_b, peer_wq, peer_subkeys, peer_u, peer_v):
    batch, seq, d = x.shape
    depth = mix_norm_g.shape[0]
    t = batch * seq
    cos1, sin1, cos2, sin2 = _rope_tables(seq)
    row = lambda a: a.reshape(1, -1).astype(F32)
    xf = x.reshape(t, d)
    u_all = peer_u.astype(BF16)
    vt_all = jnp.swapaxes(peer_v, 1, 2).astype(BF16)
    for layer in range(depth):
        i = layer // 2
        if layer % 2 == 0:
            a_in, q, k, v = _inproj_even(xf, row(mix_norm_g[layer]), even_w_in[i].astype(BF16), cos1, sin1, seq)
            m1 = _conv_module(a_in, conv_w[i, :, 0, :], row(conv_b[i]), row(conv_ln_g[i]), row(conv_ln_b[i]), batch, seq)
            sink_b = jnp.broadcast_to(sink_logits[i].astype(F32)[:, None], (N_Q_HEADS, LANE))
            m2 = _swa(q, k, v, sink_b, batch, seq)
            w_out = even_w_out[i]
        else:
            q, k, v, d_in = _inproj_odd(xf, row(mix_norm_g[layer]), odd_w_in[i].astype(BF16),
                                        row(jnp.tile(q_norm_g[i], N_Q_HEADS)), row(jnp.tile(k_norm_g[i], N_KV_HEADS)),
                                        cos2, sin2, seq)
            m1 = _dense_attn(q, k, v, batch, seq)
            bs = jnp.broadcast_to(sgu_b[i].astype(F32)[:, :, None], (SGU_GROUPS, CHUNK, MIX_WIDTH // SGU_GROUPS))
            m2 = _sgu(d_in, row(sgu_ln_g[i]), row(sgu_ln_b[i]), sgu_w[i].astype(BF16), bs)
            w_out = odd_w_out[i]
        xn, hnt = _outproj(xf, m1, m2, w_out.astype(BF16), row(ffn_norm_g[layer]))
        sk = peer_subkeys[layer].reshape(2 * PEER_HEADS, PEER_NKEYS, PEER_DKEY // 2).astype(BF16)
        xf = _peer(xn, hnt, peer_wq[layer].astype(BF16).T, sk, u_all, vt_all, layer)
    return _final_norm(xf, row(final_norm_g)).reshape(batch, seq, d)
```

```python
import jax
import jax.numpy as jnp
from jax import lax
from jax.experimental import pallas as pl
from jax.experimental.pallas import tpu as pltpu

F32 = jnp.float32
BF16 = jnp.bfloat16

D_MODEL = 1024
HEAD_DIM = 64
N_Q_HEADS = 8
N_KV_HEADS = 2
GQA_GROUPS = N_Q_HEADS // N_KV_HEADS
ATTN_WIDTH = N_Q_HEADS * HEAD_DIM
KV_WIDTH = N_KV_HEADS * HEAD_DIM
MIX_WIDTH = D_MODEL // 2
CONV_WIDTH = 31
CONV_HALO = 16
SGU_GROUPS = 4
CHUNK = 128
WINDOW = 128
BLOCK_Q = 128
GRID_W = 64
ROPE_THETA = 10000.0
PEER_HEADS = 8
PEER_NKEYS = 128
PEER_DKEY = 128
PEER_TOPK = 16
EPS = 1e-6
NEG = -1e30

VMEM_LIMIT_BYTES = 56 * 1024 * 1024

ROW_TILE = 1024
CONV_ROWS = 64
SWA_BLOCKS = 4
ATT_TQ = 1024
ATT_CHAINS = 8
PEER_TT = 512
PEER_A = 16
PEER_SUB = 8
PEER_GATE_BLOCK = 4
PEER_GATE_DTYPE = BF16
LANE = 128
BF16_ROWS = 16


def _cparams(sem):
    return pltpu.CompilerParams(dimension_semantics=sem, vmem_limit_bytes=VMEM_LIMIT_BYTES)


def _dot(a, b):
    return jnp.dot(a, b, preferred_element_type=F32)


def _rms(x, g):
    return x * lax.rsqrt(jnp.mean(x * x, axis=-1, keepdims=True) + EPS) * g


def _layernorm(x, g, b):
    mu = jnp.mean(x, axis=-1, keepdims=True)
    xc = x - mu
    var = jnp.mean(xc * xc, axis=-1, keepdims=True)
    return xc * lax.rsqrt(var + EPS) * g + b


def _rope(x, cos, sin_signed, group, half):
    n = x.shape[-1]
    lane = lax.broadcasted_iota(jnp.int32, x.shape, x.ndim - 1)
    first = (lane % group) < half
    partner = jnp.where(first, pltpu.roll(x, n - half, x.ndim - 1), pltpu.roll(x, half, x.ndim - 1))
    return x * cos + partner * sin_signed


def _inproj_even_kernel(x_ref, g_ref, w_ref, cos_ref, sin_ref, a_ref, q_ref, k_ref, v_ref):
    h = _rms(x_ref[...], g_ref[...]).astype(BF16)
    o1 = 2 * MIX_WIDTH
    o2 = o1 + ATTN_WIDTH
    o3 = o2 + KV_WIDTH
    a_ref[...] = _dot(h, w_ref[:, :o1])
    cos = cos_ref[...]
    sin = sin_ref[...]
    q = _rope(_dot(h, w_ref[:, o1:o2]), cos, sin, HEAD_DIM, HEAD_DIM // 2)
    q_ref[...] = (q * (HEAD_DIM ** -0.5)).astype(BF16)
    k = _rope(_dot(h, w_ref[:, o2:o3]), cos[:, :KV_WIDTH], sin[:, :KV_WIDTH], HEAD_DIM, HEAD_DIM // 2)
    k_ref[...] = k.astype(BF16)
    v_ref[...] = _dot(h, w_ref[:, o3:]).astype(BF16)


def _inproj_even(x, g, w, cos, sin, seq):
    t = x.shape[0]
    tm = ROW_TILE
    nseq = seq // tm
    in_w = w.shape[1]
    row = lambda i: (i, 0)
    fixed = lambda i: (0, 0)
    pos = lambda i: (i % nseq, 0)
    return pl.pallas_call(
        _inproj_even_kernel,
        grid=(t // tm,),
        in_specs=[
            pl.BlockSpec((tm, D_MODEL), row),
            pl.BlockSpec((1, D_MODEL), fixed),
            pl.BlockSpec((D_MODEL, in_w), fixed),
            pl.BlockSpec((tm, ATTN_WIDTH), pos),
            pl.BlockSpec((tm, ATTN_WIDTH), pos),
        ],
        out_specs=[
            pl.BlockSpec((tm, 2 * MIX_WIDTH), row),
            pl.BlockSpec((tm, ATTN_WIDTH), row),
            pl.BlockSpec((tm, KV_WIDTH), row),
            pl.BlockSpec((tm, KV_WIDTH), row),
        ],
        out_shape=[
            jax.ShapeDtypeStruct((t, 2 * MIX_WIDTH), F32),
            jax.ShapeDtypeStruct((t, ATTN_WIDTH), BF16),
            jax.ShapeDtypeStruct((t, KV_WIDTH), BF16),
            jax.ShapeDtypeStruct((t, KV_WIDTH), BF16),
        ],
        compiler_params=_cparams(("parallel",)),
        name="inproj_even",
    )(x, g, w, cos, sin)


def _head_sumsq(x):
    n = x.shape[-1]
    r = lax.broadcasted_iota(jnp.int32, (n, n), 0) // HEAD_DIM
    c = lax.broadcasted_iota(jnp.int32, (n, n), 1) // HEAD_DIM
    ones = jnp.where(r == c, 1.0, 0.0).astype(BF16)
    sq = x * x
    hi = sq.astype(BF16)
    lo = (sq - hi.astype(F32)).astype(BF16)
    return _dot(hi, ones) + _dot(lo, ones)


def _inproj_odd_kernel(x_ref, g_ref, w_ref, qg_ref, kg_ref, cos_ref, sin_ref, q_ref, k_ref, vt_ref, d_ref):
    h = _rms(x_ref[...], g_ref[...]).astype(BF16)
    c1 = ATTN_WIDTH
    c2 = c1 + KV_WIDTH
    c3 = c2 + KV_WIDTH
    cos = cos_ref[...]
    sin = sin_ref[...]
    half = HEAD_DIM // 2
    q = _dot(h, w_ref[:, :c1])
    q = q * lax.rsqrt(_head_sumsq(q) * (1.0 / HEAD_DIM) + EPS) * qg_ref[...]
    q = _rope(q, cos, sin, half, half // 2)
    q_ref[...] = (q * (HEAD_DIM ** -0.5)).astype(BF16)
    k = _dot(h, w_ref[:, c1:c2])
    k = k * lax.rsqrt(_head_sumsq(k) * (1.0 / HEAD_DIM) + EPS) * kg_ref[...]
    k = _rope(k, cos[:, :KV_WIDTH], sin[:, :KV_WIDTH], half, half // 2).astype(BF16)
    v_t = _dot(h, w_ref[:, c2:c3]).T.astype(BF16)
    for hh in range(N_KV_HEADS):
        k_ref[hh] = k[:, hh * HEAD_DIM:(hh + 1) * HEAD_DIM]
        vt_ref[hh] = v_t[hh * HEAD_DIM:(hh + 1) * HEAD_DIM, :]
    d_ref[...] = _dot(h, w_ref[:, c3:])


def _inproj_odd(x, g, w, qg, kg, cos, sin, seq):
    t = x.shape[0]
    tm = ROW_TILE
    nseq = seq // tm
    in_w = w.shape[1]
    row = lambda i: (i, 0)
    fixed = lambda i: (0, 0)
    pos = lambda i: (i % nseq, 0)
    kv = lambda i: (0, i, 0)
    return pl.pallas_call(
        _inproj_odd_kernel,
        grid=(t // tm,),
        in_specs=[
            pl.BlockSpec((tm, D_MODEL), row),
            pl.BlockSpec((1, D_MODEL), fixed),
            pl.BlockSpec((D_MODEL, in_w), fixed),
            pl.BlockSpec((1, ATTN_WIDTH), fixed),
            pl.BlockSpec((1, KV_WIDTH), fixed),
            pl.BlockSpec((tm, ATTN_WIDTH), pos),
            pl.BlockSpec((tm, ATTN_WIDTH), pos),
        ],
        out_specs=[
            pl.BlockSpec((tm, ATTN_WIDTH), row),
            pl.BlockSpec((N_KV_HEADS, tm, HEAD_DIM), kv),
            pl.BlockSpec((N_KV_HEADS, HEAD_DIM, tm), lambda i: (0, 0, i)),
            pl.BlockSpec((tm, 2 * MIX_WIDTH), row),
        ],
        out_shape=[
            jax.ShapeDtypeStruct((t, ATTN_WIDTH), BF16),
            jax.ShapeDtypeStruct((N_KV_HEADS, t, HEAD_DIM), BF16),
            jax.ShapeDtypeStruct((N_KV_HEADS, HEAD_DIM, t), BF16),
            jax.ShapeDtypeStruct((t, 2 * MIX_WIDTH), F32),
        ],
        compiler_params=_cparams(("parallel",)),
        name="inproj_odd",
    )(x, g, w, qg, kg, cos, sin)


def _glu(x):
    return x[:, :MIX_WIDTH] * jax.nn.sigmoid(x[:, MIX_WIDTH:])


def _conv_kernel(prev_ref, cur_ref, next_ref, w_ref, b_ref, g_ref, beta_ref, o_ref, buf_ref):
    i = pl.program_id(1)
    ns = pl.num_programs(1)
    ts = cur_ref.shape[0]
    halo = CONV_HALO
    buf_ref[0, 0:halo, :] = jnp.where(i > 0, _glu(prev_ref[...]), 0.0)
    buf_ref[0, halo:halo + ts, :] = _glu(cur_ref[...])
    buf_ref[0, halo + ts:, :] = jnp.where(i < ns - 1, _glu(next_ref[...]), 0.0)
    span = ts + 2 * halo - SUBLANES
    for s in range(1, SUBLANES):
        buf_ref[s, 0:span, :] = buf_ref[0, s:s + span, :]
    w = w_ref[...]
    bias = b_ref[...]
    g = g_ref[...]
    beta = beta_ref[...]
    first = halo - CONV_WIDTH // 2
    for c in range(ts // CONV_ROWS):
        r0 = c * CONV_ROWS
        acc = jnp.zeros((CONV_ROWS, MIX_WIDTH), F32)
        for k in range(CONV_WIDTH):
            m, s = divmod(first + k, SUBLANES)
            lo = r0 + SUBLANES * m
            acc = acc + w[k:k + 1, :] * buf_ref[s, lo:lo + CONV_ROWS, :]
        y = _layernorm(acc + bias, g, beta)
        o_ref[r0:r0 + CONV_ROWS, :] = (y * jax.nn.sigmoid(y)).astype(o_ref.dtype)


def _conv_module(a_in, w, b, g, beta, batch, seq):
    t = a_in.shape[0]
    ts = ROW_TILE
    ns = seq // ts
    hb = ts // CONV_HALO
    last = t // CONV_HALO - 1
    fixed = lambda bb, i: (0, 0)
    return pl.pallas_call(
        _conv_kernel,
        grid=(batch, ns),
        in_specs=[
            pl.BlockSpec((CONV_HALO, 2 * MIX_WIDTH), lambda bb, i: (jnp.maximum((bb * ns + i) * hb - 1, 0), 0)),
            pl.BlockSpec((ts, 2 * MIX_WIDTH), lambda bb, i: (bb * ns + i, 0)),
            pl.BlockSpec((CONV_HALO, 2 * MIX_WIDTH), lambda bb, i: (jnp.minimum((bb * ns + i + 1) * hb, last), 0)),
            pl.BlockSpec((CONV_WIDTH, MIX_WIDTH), fixed),
            pl.BlockSpec((1, MIX_WIDTH), fixed),
            pl.BlockSpec((1, MIX_WIDTH), fixed),
            pl.BlockSpec((1, MIX_WIDTH), fixed),
        ],
        out_specs=pl.BlockSpec((ts, MIX_WIDTH), lambda bb, i: (bb * ns + i, 0)),
        out_shape=jax.ShapeDtypeStruct((t, MIX_WIDTH), BF16),
        scratch_shapes=[pltpu.VMEM((SUBLANES, ts + 2 * CONV_HALO, MIX_WIDTH), F32)],
        compiler_params=_cparams(("parallel", "parallel")),
        name="conv_module",
    )(a_in, a_in, a_in, w, b, g, beta)


def _swa_kernel(q_ref, kp_ref, kc_ref, kn_ref, vp_ref, vc_ref, vn_ref, sink_ref, o_ref):
    n = pl.program_id(1)
    nsteps = pl.num_programs(1)
    bq = BLOCK_Q
    cols = GQA_GROUPS * bq
    jj = lax.broadcasted_iota(jnp.int32, (3 * bq, cols), 0)
    ii = lax.broadcasted_iota(jnp.int32, (3 * bq, cols), 1) % bq
    band = (jj >= ii) & (jj <= ii + 2 * WINDOW)
    kcat = jnp.concatenate([kp_ref[...], kc_ref[...], kn_ref[...]], axis=0)
    vcat_t = jnp.concatenate([vp_ref[...], vc_ref[...], vn_ref[...]], axis=0).astype(F32).T.astype(BF16)
    for qb in range(SWA_BLOCKS):
        mask = band
        if qb == 0:
            mask = mask & (jj >= jnp.where(n > 0, 0, bq))
        if qb == SWA_BLOCKS - 1:
            mask = mask & (jj < jnp.where(n < nsteps - 1, 3 * bq, 2 * bq))
        q_t = q_ref[qb * bq:(qb + 1) * bq, :].astype(F32).T.astype(BF16)
        outs = []
        for hk in range(N_KV_HEADS):
            heads = range(hk * GQA_GROUPS, (hk + 1) * GQA_GROUPS)
            kh = kcat[qb * bq:(qb + 3) * bq, hk * HEAD_DIM:(hk + 1) * HEAD_DIM]
            vh_t = jnp.concatenate([vcat_t[hk * HEAD_DIM:(hk + 1) * HEAD_DIM, qb * bq:(qb + 3) * bq],
                                    jnp.ones((BF16_ROWS, 3 * bq), BF16)], axis=0)
            qs_t = jnp.concatenate([q_t[hd * HEAD_DIM:(hd + 1) * HEAD_DIM, :] for hd in heads], axis=1)
            sink = jnp.concatenate([jnp.broadcast_to(sink_ref[hd:hd + 1, 0:1], (1, bq)) for hd in heads], axis=1)
            s = jnp.where(mask, _dot(kh, qs_t), NEG)
            m = jnp.maximum(jnp.max(s, axis=0, keepdims=True), sink)
            o_ext = _dot(vh_t, jnp.exp(s - m).astype(BF16))
            denom = o_ext[HEAD_DIM:HEAD_DIM + 1, :] + jnp.exp(sink - m)
            o_t = o_ext[:HEAD_DIM, :] / denom
            outs.extend(o_t[:, g * bq:(g + 1) * bq] for g in range(GQA_GROUPS))
        o_ref[qb * bq:(qb + 1) * bq, :] = jnp.concatenate(outs, axis=0).T.astype(o_ref.dtype)


def _swa(q, k, v, sink_b, batch, seq):
    t = q.shape[0]
    bq = BLOCK_Q
    tq = SWA_BLOCKS * bq
    ns = seq // tq
    last = t // bq - 1
    cur = lambda bb, n: (bb * ns + n, 0)
    prev = lambda bb, n: (jnp.maximum((bb * ns + n) * SWA_BLOCKS - 1, 0), 0)
    nxt = lambda bb, n: (jnp.minimum((bb * ns + n + 1) * SWA_BLOCKS, last), 0)
    halo = lambda im: pl.BlockSpec((bq, KV_WIDTH), im)
    body = pl.BlockSpec((tq, KV_WIDTH), cur)
    return pl.pallas_call(
        _swa_kernel,
        grid=(batch, ns),
        in_specs=[pl.BlockSpec((tq, ATTN_WIDTH), cur), halo(prev), body, halo(nxt), halo(prev), body, halo(nxt),
                  pl.BlockSpec((N_Q_HEADS, LANE), lambda bb, n: (0, 0))],
        out_specs=pl.BlockSpec((tq, ATTN_WIDTH), cur),
        out_shape=jax.ShapeDtypeStruct((t, ATTN_WIDTH), BF16),
        compiler_params=_cparams(("parallel", "parallel")),
        name="window_attn",
    )(q, k, k, k, v, v, v, sink_b)


def _dense_attn_kernel(q_ref, k_ref, vt_ref, o_ref):
    ts = q_ref.shape[0] // ATT_CHAINS
    k = k_ref[...]
    v_ext = jnp.concatenate([vt_ref[...], jnp.ones((BF16_ROWS, vt_ref.shape[1]), BF16)], axis=0)
    q_t = q_ref[...].astype(F32).T.astype(BF16)

    def scores(c):
        cols = slice(c * ts, (c + 1) * ts)
        qs_t = jnp.concatenate([q_t[g * HEAD_DIM:(g + 1) * HEAD_DIM, cols] for g in range(GQA_GROUPS)], axis=1)
        return _dot(k, qs_t)

    s_next = scores(0)
    for c in range(ATT_CHAINS):
        cols = slice(c * ts, (c + 1) * ts)
        s = s_next
        if c + 1 < ATT_CHAINS:
            s_next = scores(c + 1)
        p = jnp.exp(s - jnp.max(s, axis=0, keepdims=True))
        o_ext = _dot(v_ext, p.astype(BF16))
        o_t = o_ext[:HEAD_DIM, :] / o_ext[HEAD_DIM:HEAD_DIM + 1, :]
        o_ref[cols, :] = jnp.concatenate(
            [o_t[:, g * ts:(g + 1) * ts] for g in range(GQA_GROUPS)], axis=0).T.astype(o_ref.dtype)


def _dense_attn(q, k, v, batch, seq):
    t = q.shape[0]
    tq = min(ATT_TQ, seq)
    nq = seq // tq
    gw = GQA_GROUPS * HEAD_DIM
    qmap = lambda bb, hk, qi: (bb * nq + qi, hk)
    return pl.pallas_call(
        _dense_attn_kernel,
        grid=(batch, N_KV_HEADS, nq),
        in_specs=[
            pl.BlockSpec((tq, gw), qmap),
            pl.BlockSpec((None, seq, HEAD_DIM), lambda bb, hk, qi: (hk, bb, 0)),
            pl.BlockSpec((None, HEAD_DIM, seq), lambda bb, hk, qi: (hk, 0, bb)),
        ],
        out_specs=pl.BlockSpec((tq, gw), qmap),
        out_shape=jax.ShapeDtypeStruct((t, ATTN_WIDTH), BF16),
        compiler_params=_cparams(("parallel", "parallel", "parallel")),
        name="dense_attn",
    )(q, k, v)


def _sgu_kernel(d_ref, g_ref, b_ref, w_ref, bs_ref, o_ref):
    z = jax.nn.gelu(d_ref[...])
    u = z[:, :MIX_WIDTH]
    v = _layernorm(z[:, MIX_WIDTH:], g_ref[...], b_ref[...]).astype(BF16)
    gc = MIX_WIDTH // SGU_GROUPS
    for n in range(d_ref.shape[0] // CHUNK):
        rows = slice(n * CHUNK, (n + 1) * CHUNK)
        for g in range(SGU_GROUPS):
            cols = slice(g * gc, (g + 1) * gc)
            mixed = _dot(w_ref[g], v[rows, cols]) + bs_ref[g]
            o_ref[rows, cols] = (u[rows, cols] * mixed).astype(o_ref.dtype)


def _sgu(d_in, g, b, w, bs):
    t = d_in.shape[0]
    ts = ROW_TILE
    gc = MIX_WIDTH // SGU_GROUPS
    row = lambda i: (i, 0)
    fixed = lambda i: (0, 0)
    fixed3 = lambda i: (0, 0, 0)
    return pl.pallas_call(
        _sgu_kernel,
        grid=(t // ts,),
        in_specs=[
            pl.BlockSpec((ts, 2 * MIX_WIDTH), row),
            pl.BlockSpec((1, MIX_WIDTH), fixed),
            pl.BlockSpec((1, MIX_WIDTH), fixed),
            pl.BlockSpec((SGU_GROUPS, CHUNK, CHUNK), fixed3),
            pl.BlockSpec((SGU_GROUPS, CHUNK, gc), fixed3),
        ],
        out_specs=pl.BlockSpec((ts, MIX_WIDTH), row),
        out_shape=jax.ShapeDtypeStruct((t, MIX_WIDTH), BF16),
        compiler_params=_cparams(("parallel",)),
        name="spatial_gating",
    )(d_in, g, b, w, bs)


def _outproj_kernel(x_ref, a_ref, b_ref, w_ref, g_ref, xn_ref, hnt_ref):
    y = _dot(a_ref[...], w_ref[:MIX_WIDTH, :]) + _dot(b_ref[...], w_ref[MIX_WIDTH:, :])
    xn = x_ref[...] + y
    xn_ref[...] = xn
    hnt_ref[...] = _rms(xn, g_ref[...]).T.astype(BF16)


def _outproj(x, a, b, w, g):
    t = x.shape[0]
    tm = ROW_TILE
    row = lambda i: (i, 0)
    fixed = lambda i: (0, 0)
    return pl.pallas_call(
        _outproj_kernel,
        grid=(t // tm,),
        in_specs=[
            pl.BlockSpec((tm, D_MODEL), row),
            pl.BlockSpec((tm, MIX_WIDTH), row),
            pl.BlockSpec((tm, MIX_WIDTH), row),
            pl.BlockSpec((2 * MIX_WIDTH, D_MODEL), fixed),
            pl.BlockSpec((1, D_MODEL), fixed),
        ],
        out_specs=[pl.BlockSpec((tm, D_MODEL), row), pl.BlockSpec((D_MODEL, tm), lambda i: (0, i))],
        out_shape=[jax.ShapeDtypeStruct((t, D_MODEL), F32), jax.ShapeDtypeStruct((D_MODEL, t), BF16)],
        compiler_params=_cparams(("parallel",)),
        name="outproj",
    )(x, a, b, w, g)


_PEER_PAIRS = [(i, j) for i in range(PEER_TOPK) for j in range(PEER_TOPK) if (i + 1) * (j + 1) <= PEER_TOPK]
_PEER_CAND_ROWS = -(-len(_PEER_PAIRS) // 8) * 8


SUBLANES = 8


def _bitonic_merge_desc(v, lo, n):
    step = n // 2
    while step >= 1:
        for i in range(lo, lo + n):
            if (i - lo) & step == 0:
                hi_, lo_ = jnp.maximum(v[i], v[i + step]), jnp.minimum(v[i], v[i + step])
                v[i], v[i + step] = hi_, lo_
        step //= 2


def _sort16_desc(v):
    n = len(v)
    size = 2
    while size <= n:
        for lo in range(0, n, size):
            half = size // 2
            v[lo + half:lo + size] = v[lo + half:lo + size][::-1]
            _bitonic_merge_desc(v, lo, size)
        size *= 2
    return v


def _sublane_all(x, op):
    for shift in (4, 2, 1):
        x = op(x, pltpu.roll(x, shift, 0))
    return x


def _top16_sorted(s):
    v = _sort16_desc([s[k * SUBLANES:(k + 1) * SUBLANES, :] for k in range(PEER_NKEYS // SUBLANES)])
    for shift in (4, 2, 1):
        v = [jnp.maximum(v[k], pltpu.roll(v[PEER_TOPK - 1 - k], shift, 0)) for k in range(PEER_TOPK)]
        _bitonic_merge_desc(v, 0, PEER_TOPK)
    return v


def _peer_select(h, s1_all, s2_all, n_ref, e1_ref, r2_ref, e2_ref):
    tt = s1_all.shape[-1]
    nv = PEER_NKEYS // SUBLANES
    sub = lax.broadcasted_iota(jnp.int32, (SUBLANES, LANE), 0)
    for tc in range(tt // LANE):
        lanes = slice(tc * LANE, (tc + 1) * LANE)
        s1 = s1_all[:, lanes]
        s2 = s2_all[:, lanes]
        v1 = _top16_sorted(s1)
        v2 = _top16_sorted(s2)
        pair_sum = {p: v1[p[0]] + v2[p[1]] for p in _PEER_PAIRS}
        cand = []
        for g in range(_PEER_CAND_ROWS // SUBLANES):
            c = jnp.full((SUBLANES, LANE), -jnp.inf, F32)
            for r, p in enumerate(_PEER_PAIRS[g * SUBLANES:(g + 1) * SUBLANES]):
                c = jnp.where(sub == r, pair_sum[p], c)
            cand.append(c)
        f = []
        for _ in range(PEER_TOPK):
            mx = cand[0]
            for c in cand[1:]:
                mx = jnp.maximum(mx, c)
            mx = _sublane_all(mx, jnp.maximum)
            f.append(mx)
            cand = [jnp.where(c == mx, -jnp.inf, c) for c in cand]
        thr = f[PEER_TOPK - 1]
        z = jnp.ones((SUBLANES, LANE), F32)
        for r in range(1, PEER_TOPK):
            z = z + jnp.exp(f[r] - f[0])
        rz = 1.0 / z
        cnt = []
        for i in range(PEER_TOPK):
            c = jnp.zeros((SUBLANES, LANE), F32)
            for j in range(PEER_TOPK // (i + 1)):
                c = c + jnp.where(pair_sum[(i, j)] >= thr, 1.0, 0.0)
            cnt.append(c)
        ranks = []
        e2s = []
        for k in range(nv):
            rows = slice(k * SUBLANES, (k + 1) * SUBLANES)
            a = s1[rows, :]
            b = s2[rows, :]
            n = jnp.zeros((SUBLANES, LANE), F32)
            rank = jnp.full((SUBLANES, LANE), float(PEER_TOPK), F32)
            for r in range(PEER_TOPK):
                n = jnp.where(a == v1[r], cnt[r], n)
                rank = jnp.where(b == v2[r], float(r), rank)
            n_ref[h, rows, lanes] = n
            e1_ref[h, rows, lanes] = jnp.exp(a - v1[0]) * rz
            ranks.append(rank)
            e2s.append(jnp.exp(b - v2[0]))
        r2_ref[h, :, lanes] = jnp.concatenate(ranks, axis=0).astype(r2_ref.dtype)
        e2_ref[h, :, lanes] = jnp.concatenate(e2s, axis=0).astype(e2_ref.dtype)


def _peer_gates(step, sub, n_ref, e1_ref, r2_ref, e2_ref, g_ref):
    tt = g_ref.shape[-1]
    gdt = g_ref.dtype
    first = sub * PEER_SUB
    a8 = pl.multiple_of(step * PEER_A + first // SUBLANES * SUBLANES, SUBLANES)
    nload = -(-(first % SUBLANES + PEER_SUB) // SUBLANES) * SUBLANES
    for tc in range(tt // LANE):
        lanes = slice(tc * LANE, (tc + 1) * LANE)
        nrows = [n_ref[h, pl.ds(a8, nload), lanes].astype(gdt) for h in range(PEER_HEADS)]
        erows = [e1_ref[h, pl.ds(a8, nload), lanes].astype(gdt) for h in range(PEER_HEADS)]
        for blk in range(PEER_SUB // PEER_GATE_BLOCK):
            gates = [jnp.zeros((PEER_NKEYS, LANE), gdt) for _ in range(PEER_GATE_BLOCK)]
            for h in range(PEER_HEADS):
                r2 = r2_ref[h, :, lanes]
                e2 = e2_ref[h, :, lanes]
                for k in range(PEER_GATE_BLOCK):
                    al = first % SUBLANES + blk * PEER_GATE_BLOCK + k
                    keep = r2 < nrows[h][al:al + 1, :]
                    gates[k] = gates[k] + jnp.where(keep, e2, 0.0) * erows[h][al:al + 1, :]
            for k in range(PEER_GATE_BLOCK):
                al = blk * PEER_GATE_BLOCK + k
                g_ref[sub, al * PEER_NKEYS:(al + 1) * PEER_NKEYS, lanes] = gates[k]


def _peer_kernel(x_ref, hnt_ref, wqt_ref, sk_ref, u_ref, vt_ref, o_ref,
                 qt_ref, n_ref, e1_ref, r2_ref, e2_ref, g_ref, acc_ref, ht_ref):
    j = pl.program_id(1)
    nj = pl.num_programs(1)
    half = PEER_DKEY // 2
    gdt = r2_ref.dtype
    gate_refs = (n_ref, e1_ref, r2_ref, e2_ref, g_ref)

    @pl.when(j == 0)
    def _():
        qt_ref[...] = _dot(wqt_ref[...], hnt_ref[...]).astype(BF16)

        def body(h, carry):
            r0 = pl.multiple_of(h * PEER_DKEY, PEER_DKEY)
            s1 = _dot(sk_ref[2 * h], qt_ref[pl.ds(r0, half), :])
            s2 = _dot(sk_ref[2 * h + 1], qt_ref[pl.ds(r0 + half, half), :])
            _peer_select(h, s1, s2, n_ref, e1_ref, r2_ref, e2_ref)
            return carry

        lax.fori_loop(0, PEER_HEADS, body, 0)
        acc_ref[...] = jnp.zeros(acc_ref.shape, F32)

    sub_rows = lambda sub: slice(sub * PEER_SUB * PEER_NKEYS, (sub + 1) * PEER_SUB * PEER_NKEYS)
    for sub in range(PEER_A // PEER_SUB):
        _peer_gates(j, sub, *gate_refs)
    for sub in range(PEER_A // PEER_SUB):
        at = _dot(u_ref[sub_rows(sub), :], hnt_ref[...]).astype(gdt)
        ht_ref[sub_rows(sub), :] = jax.nn.gelu(at) * g_ref[sub]
    acc_ref[...] += _dot(vt_ref[...], ht_ref[...])

    @pl.when(j == nj - 1)
    def _():
        o_ref[...] = x_ref[...] + acc_ref[...].T


def _peer(x, hnt, wqt, sk, u, vt, layer):
    t = x.shape[0]
    tt = PEER_TT
    ne = u.shape[1]
    et = PEER_A * PEER_NKEYS
    hk = PEER_HEADS * PEER_DKEY
    sel = lambda dt: pltpu.VMEM((PEER_HEADS, PEER_NKEYS, tt), dt)
    return pl.pallas_call(
        _peer_kernel,
        grid=(t // tt, ne // et),
        in_specs=[
            pl.BlockSpec((tt, D_MODEL), lambda i, j: (i, 0)),
            pl.BlockSpec((D_MODEL, tt), lambda i, j: (0, i)),
            pl.BlockSpec((hk, D_MODEL), lambda i, j: (0, 0), pipeline_mode=pl.Buffered(1)),
            pl.BlockSpec((2 * PEER_HEADS, PEER_NKEYS, PEER_DKEY // 2), lambda i, j: (0, 0, 0)),
            pl.BlockSpec((None, et, D_MODEL), lambda i, j: (layer, j, 0)),
            pl.BlockSpec((None, D_MODEL, et), lambda i, j: (layer, 0, j)),
        ],
        out_specs=pl.BlockSpec((tt, D_MODEL), lambda i, j: (i, 0)),
        out_shape=jax.ShapeDtypeStruct((t, D_MODEL), F32),
        scratch_shapes=[
            pltpu.VMEM((hk, tt), BF16),
            sel(F32), sel(F32), sel(PEER_GATE_DTYPE), sel(PEER_GATE_DTYPE),
            pltpu.VMEM((PEER_A // PEER_SUB, PEER_SUB * PEER_NKEYS, tt), PEER_GATE_DTYPE),
            pltpu.VMEM((D_MODEL, tt), F32),
            pltpu.VMEM((et, tt), BF16),
        ],
        compiler_params=_cparams(("parallel", "arbitrary")),
        name="peer",
    )(x, hnt, wqt, sk, u, vt)


def _final_norm_kernel(x_ref, g_ref, o_ref):
    o_ref[...] = _rms(x_ref[...], g_ref[...])


def _final_norm(x, g):
    t = x.shape[0]
    tm = ROW_TILE
    return pl.pallas_call(
        _final_norm_kernel,
        grid=(t // tm,),
        in_specs=[pl.BlockSpec((tm, D_MODEL), lambda i: (i, 0)), pl.BlockSpec((1, D_MODEL), lambda i: (0, 0))],
        out_specs=pl.BlockSpec((tm, D_MODEL), lambda i: (i, 0)),
        out_shape=jax.ShapeDtypeStruct((t, D_MODEL), F32),
        compiler_params=_cparams(("parallel",)),
        name="final_norm",
    )(x, g)


def _rope_table(pos, dim):
    inv = ROPE_THETA ** (-jnp.arange(0, dim, 2, dtype=F32) / dim)
    ang = pos.astype(F32)[:, None] * inv[None, :]
    ang = jnp.concatenate([ang, ang], axis=-1)
    sign = jnp.concatenate([-jnp.ones((dim // 2,), F32), jnp.ones((dim // 2,), F32)])
    return jnp.cos(ang), jnp.sin(ang) * sign


def _rope_tables(seq):
    pos = jnp.arange(seq)
    cos1, sin1 = _rope_table(pos, HEAD_DIM)
    cr, sr = _rope_table(pos // GRID_W, HEAD_DIM // 2)
    cc, sc = _rope_table(pos % GRID_W, HEAD_DIM // 2)
    cos2 = jnp.concatenate([cr, cc], axis=-1)
    sin2 = jnp.concatenate([sr, sc], axis=-1)
    tile = lambda a: jnp.tile(a, (1, N_Q_HEADS))
    return tile(cos1), tile(sin1), tile(cos2), tile(sin2)


def kernel(x, mix_norm_g, ffn_norm_g, final_norm_g, even_w_in, even_w_out, conv_w, conv_b, conv_ln_g, conv_ln_b, sink_logits, odd_w_in, odd_w_out, q_norm_g, k_norm_g, sgu_ln_g, sgu_ln_b, sgu_w, sgu_b, peer_wq, peer_subkeys, peer_u, peer_v):
    batch, seq, d = x.shape
    depth = mix_norm_g.shape[0]
    t = batch * seq
    cos1, sin1, cos2, sin2 = _rope_tables(seq)
    row = lambda a: a.reshape(1, -1).astype(F32)
    xf = x.reshape(t, d)
    u_all = peer_u.astype(BF16)
    vt_all = jnp.swapaxes(peer_v, 1, 2).astype(BF16)
    for layer in range(depth):
        i = layer // 2
        if layer % 2 == 0:
            a_in, q, k, v = _inproj_even(xf, row(mix_norm_g[layer]), even_w_in[i].astype(BF16), cos1, sin1, seq)
            m1 = _conv_module(a_in, conv_w[i, :, 0, :], row(conv_b[i]), row(conv_ln_g[i]), row(conv_ln_b[i]), batch, seq)
            sink_b = jnp.broadcast_to(sink_logits[i].astype(F32)[:, None], (N_Q_HEADS, LANE))
            m2 = _swa(q, k, v, sink_b, batch, seq)
            w_out = even_w_out[i]
        else:
            q, k, v, d_in = _inproj_odd(xf, row(mix_norm_g[layer]), odd_w_in[i].astype(BF16),
                                        row(jnp.tile(q_norm_g[i], N_Q_HEADS)), row(jnp.tile(k_norm_g[i], N_KV_HEADS)),
                                        cos2, sin2, seq)
            m1 = _dense_attn(q, k, v, batch, seq)
            bs = jnp.broadcast_to(sgu_b[i].astype(F32)[:, :, None], (SGU_GROUPS, CHUNK, MIX_WIDTH // SGU_GROUPS))
            m2 = _sgu(d_in, row(sgu_ln_g[i]), row(sgu_ln_b[i]), sgu_w[i].astype(BF16), bs)
            w_out = odd_w_out[i]
        xn, hnt = _outproj(xf, m1, m2, w_out.astype(BF16), row(ffn_norm_g[layer]))
        sk = peer_subkeys[layer].reshape(2 * PEER_HEADS, PEER_NKEYS, PEER_DKEY // 2).astype(BF16)
        xf = _peer(xn, hnt, peer_wq[layer].astype(BF16).T, sk, u_all, vt_all, layer)
    return _final_norm(xf, row(final_norm_g)).reshape(batch, seq, d)
```

```python
import jax
import jax.numpy as jnp
from jax import lax
from jax.experimental import pallas as pl
from jax.experimental.pallas import tpu as pltpu

F32 = jnp.float32
BF16 = jnp.bfloat16

D_MODEL = 1024
HEAD_DIM = 64
N_Q_HEADS = 8
N_KV_HEADS = 2
GQA_GROUPS = N_Q_HEADS // N_KV_HEADS
ATTN_WIDTH = N_Q_HEADS * HEAD_DIM
KV_WIDTH = N_KV_HEADS * HEAD_DIM
MIX_WIDTH = D_MODEL // 2
CONV_WIDTH = 31
CONV_HALO = 16
SGU_GROUPS = 4
CHUNK = 128
WINDOW = 128
BLOCK_Q = 128
GRID_W = 64
ROPE_THETA = 10000.0
PEER_HEADS = 8
PEER_NKEYS = 128
PEER_DKEY = 128
PEER_TOPK = 16
EPS = 1e-6
NEG = -1e30

VMEM_LIMIT_BYTES = 56 * 1024 * 1024

ROW_TILE = 1024
CONV_ROWS = 64
SWA_BLOCKS = 4
ATT_TQ = 1024
ATT_CHAINS = 8
PEER_TT = 512
PEER_A = 16
PEER_SUB = 8
PEER_GATE_BLOCK = 4
PEER_GATE_DTYPE = BF16
LANE = 128
BF16_ROWS = 16


def _cparams(sem):
    return pltpu.CompilerParams(dimension_semantics=sem, vmem_limit_bytes=VMEM_LIMIT_BYTES)


def _dot(a, b):
    return jnp.dot(a, b, preferred_element_type=F32)


def _rms(x, g):
    return x * lax.rsqrt(jnp.mean(x * x, axis=-1, keepdims=True) + EPS) * g


def _layernorm(x, g, b):
    mu = jnp.mean(x, axis=-1, keepdims=True)
    xc = x - mu
    var = jnp.mean(xc * xc, axis=-1, keepdims=True)
    return xc * lax.rsqrt(var + EPS) * g + b


def _rope(x, cos, sin_signed, group, half):
    n = x.shape[-1]
    lane = lax.broadcasted_iota(jnp.int32, x.shape, x.ndim - 1)
    first = (lane % group) < half
    partner = jnp.where(first, pltpu.roll(x, n - half, x.ndim - 1), pltpu.roll(x, half, x.ndim - 1))
    return x * cos + partner * sin_signed


def _inproj_even_kernel(x_ref, g_ref, w_ref, cos_ref, sin_ref, a_ref, q_ref, k_ref, v_ref):
    h = _rms(x_ref[...], g_ref[...]).astype(BF16)
    o1 = 2 * MIX_WIDTH
    o2 = o1 + ATTN_WIDTH
    o3 = o2 + KV_WIDTH
    a_ref[...] = _dot(h, w_ref[:, :o1])
    cos = cos_ref[...]
    sin = sin_ref[...]
    q = _rope(_dot(h, w_ref[:, o1:o2]), cos, sin, HEAD_DIM, HEAD_DIM // 2)
    q_ref[...] = (q * (HEAD_DIM ** -0.5)).astype(BF16)
    k = _rope(_dot(h, w_ref[:, o2:o3]), cos[:, :KV_WIDTH], sin[:, :KV_WIDTH], HEAD_DIM, HEAD_DIM // 2)
    k_ref[...] = k.astype(BF16)
    v_ref[...] = _dot(h, w_ref[:, o3:]).astype(BF16)


def _inproj_even(x, g, w, cos, sin, seq):
    t = x.shape[0]
    tm = ROW_TILE
    nseq = seq // tm
    in_w = w.shape[1]
    row = lambda i: (i, 0)
    fixed = lambda i: (0, 0)
    pos = lambda i: (i % nseq, 0)
    return pl.pallas_call(
        _inproj_even_kernel,
        grid=(t // tm,),
        in_specs=[
            pl.BlockSpec((tm, D_MODEL), row),
            pl.BlockSpec((1, D_MODEL), fixed),
            pl.BlockSpec((D_MODEL, in_w), fixed),
            pl.BlockSpec((tm, ATTN_WIDTH), pos),
            pl.BlockSpec((tm, ATTN_WIDTH), pos),
        ],
        out_specs=[
            pl.BlockSpec((tm, 2 * MIX_WIDTH), row),
            pl.BlockSpec((tm, ATTN_WIDTH), row),
            pl.BlockSpec((tm, KV_WIDTH), row),
            pl.BlockSpec((tm, KV_WIDTH), row),
        ],
        out_shape=[
            jax.ShapeDtypeStruct((t, 2 * MIX_WIDTH), F32),
            jax.ShapeDtypeStruct((t, ATTN_WIDTH), BF16),
            jax.ShapeDtypeStruct((t, KV_WIDTH), BF16),
            jax.ShapeDtypeStruct((t, KV_WIDTH), BF16),
        ],
        compiler_params=_cparams(("parallel",)),
        name="inproj_even",
    )(x, g, w, cos, sin)


def _head_sumsq(x):
    n = x.shape[-1]
    r = lax.broadcasted_iota(jnp.int32, (n, n), 0) // HEAD_DIM
    c = lax.broadcasted_iota(jnp.int32, (n, n), 1) // HEAD_DIM
    ones = jnp.where(r == c, 1.0, 0.0).astype(BF16)
    sq = x * x
    hi = sq.astype(BF16)
    lo = (sq - hi.astype(F32)).astype(BF16)
    return _dot(hi, ones) + _dot(lo, ones)


def _inproj_odd_kernel(x_ref, g_ref, w_ref, qg_ref, kg_ref, cos_ref, sin_ref, q_ref, k_ref, vt_ref, d_ref):
    h = _rms(x_ref[...], g_ref[...]).astype(BF16)
    c1 = ATTN_WIDTH
    c2 = c1 + KV_WIDTH
    c3 = c2 + KV_WIDTH
    cos = cos_ref[...]
    sin = sin_ref[...]
    half = HEAD_DIM // 2
    q = _dot(h, w_ref[:, :c1])
    q = q * lax.rsqrt(_head_sumsq(q) * (1.0 / HEAD_DIM) + EPS) * qg_ref[...]
    q = _rope(q, cos, sin, half, half // 2)
    q_ref[...] = (q * (HEAD_DIM ** -0.5)).astype(BF16)
    k = _dot(h, w_ref[:, c1:c2])
    k = k * lax.rsqrt(_head_sumsq(k) * (1.0 / HEAD_DIM) + EPS) * kg_ref[...]
    k = _rope(k, cos[:, :KV_WIDTH], sin[:, :KV_WIDTH], half, half // 2).astype(BF16)
    v_t = _dot(h, w_ref[:, c2:c3]).T.astype(BF16)
    for hh in range(N_KV_HEADS):
        k_ref[hh] = k[:, hh * HEAD_DIM:(hh + 1) * HEAD_DIM]
        vt_ref[hh] = v_t[hh * HEAD_DIM:(hh + 1) * HEAD_DIM, :]
    d_ref[...] = _dot(h, w_ref[:, c3:])


def _inproj_odd(x, g, w, qg, kg, cos, sin, seq):
    t = x.shape[0]
    tm = ROW_TILE
    nseq = seq // tm
    in_w = w.shape[1]
    row = lambda i: (i, 0)
    fixed = lambda i: (0, 0)
    pos = lambda i: (i % nseq, 0)
    kv = lambda i: (0, i, 0)
    return pl.pallas_call(
        _inproj_odd_kernel,
        grid=(t // tm,),
        in_specs=[
            pl.BlockSpec((tm, D_MODEL), row),
            pl.BlockSpec((1, D_MODEL), fixed),
            pl.BlockSpec((D_MODEL, in_w), fixed),
            pl.BlockSpec((1, ATTN_WIDTH), fixed),
            pl.BlockSpec((1, KV_WIDTH), fixed),
            pl.BlockSpec((tm, ATTN_WIDTH), pos),
            pl.BlockSpec((tm, ATTN_WIDTH), pos),
        ],
        out_specs=[
            pl.BlockSpec((tm, ATTN_WIDTH), row),
            pl.BlockSpec((N_KV_HEADS, tm, HEAD_DIM), kv),
            pl.BlockSpec((N_KV_HEADS, HEAD_DIM, tm), lambda i: (0, 0, i)),
            pl.BlockSpec((tm, 2 * MIX_WIDTH), row),
        ],
        out_shape=[
            jax.ShapeDtypeStruct((t, ATTN_WIDTH), BF16),
            jax.ShapeDtypeStruct((N_KV_HEADS, t, HEAD_DIM), BF16),
            jax.ShapeDtypeStruct((N_KV_HEADS, HEAD_DIM, t), BF16),
            jax.ShapeDtypeStruct((t, 2 * MIX_WIDTH), F32),
        ],
        compiler_params=_cparams(("parallel",)),
        name="inproj_odd",
    )(x, g, w, qg, kg, cos, sin)


def _glu(x):
    return x[:, :MIX_WIDTH] * jax.nn.sigmoid(x[:, MIX_WIDTH:])


def _conv_kernel(prev_ref, cur_ref, next_ref, w_ref, b_ref, g_ref, beta_ref, o_ref, buf_ref):
    i = pl.program_id(1)
    ns = pl.num_programs(1)
    ts = cur_ref.shape[0]
    halo = CONV_HALO
    buf_ref[0, 0:halo, :] = jnp.where(i > 0, _glu(prev_ref[...]), 0.0)
    buf_ref[0, halo:halo + ts, :] = _glu(cur_ref[...])
    buf_ref[0, halo + ts:, :] = jnp.where(i < ns - 1, _glu(next_ref[...]), 0.0)
    span = ts + 2 * halo - SUBLANES
    for s in range(1, SUBLANES):
        buf_ref[s, 0:span, :] = buf_ref[0, s:s + span, :]
    w = w_ref[...]
    bias = b_ref[...]
    g = g_ref[...]
    beta = beta_ref[...]
    first = halo - CONV_WIDTH // 2
    for c in range(ts // CONV_ROWS):
        r0 = c * CONV_ROWS
        acc = jnp.zeros((CONV_ROWS, MIX_WIDTH), F32)
        for k in range(CONV_WIDTH):
            m, s = divmod(first + k, SUBLANES)
            lo = r0 + SUBLANES * m
            acc = acc + w[k:k + 1, :] * buf_ref[s, lo:lo + CONV_ROWS, :]
        y = _layernorm(acc + bias, g, beta)
        o_ref[r0:r0 + CONV_ROWS, :] = (y * jax.nn.sigmoid(y)).astype(o_ref.dtype)


def _conv_module(a_in, w, b, g, beta, batch, seq):
    t = a_in.shape[0]
    ts = ROW_TILE
    ns = seq // ts
    hb = ts // CONV_HALO
    last = t // CONV_HALO - 1
    fixed = lambda bb, i: (0, 0)
    return pl.pallas_call(
        _conv_kernel,
        grid=(batch, ns),
        in_specs=[
            pl.BlockSpec((CONV_HALO, 2 * MIX_WIDTH), lambda bb, i: (jnp.maximum((bb * ns + i) * hb - 1, 0), 0)),
            pl.BlockSpec((ts, 2 * MIX_WIDTH), lambda bb, i: (bb * ns + i, 0)),
            pl.BlockSpec((CONV_HALO, 2 * MIX_WIDTH), lambda bb, i: (jnp.minimum((bb * ns + i + 1) * hb, last), 0)),
            pl.BlockSpec((CONV_WIDTH, MIX_WIDTH), fixed),
            pl.BlockSpec((1, MIX_WIDTH), fixed),
            pl.BlockSpec((1, MIX_WIDTH), fixed),
            pl.BlockSpec((1, MIX_WIDTH), fixed),
        ],
        out_specs=pl.BlockSpec((ts, MIX_WIDTH), lambda bb, i: (bb * ns + i, 0)),
        out_shape=jax.ShapeDtypeStruct((t, MIX_WIDTH), BF16),
        scratch_shapes=[pltpu.VMEM((SUBLANES, ts + 2 * CONV_HALO, MIX_WIDTH), F32)],
        compiler_params=_cparams(("parallel", "parallel")),
        name="conv_module",
    )(a_in, a_in, a_in, w, b, g, beta)


def _swa_kernel(q_ref, kp_ref, kc_ref, kn_ref, vp_ref, vc_ref, vn_ref, sink_ref, o_ref):
    n = pl.program_id(1)
    nsteps = pl.num_programs(1)
    bq = BLOCK_Q
    cols = GQA_GROUPS * bq
    jj = lax.broadcasted_iota(jnp.int32, (3 * bq, cols), 0)
    ii = lax.broadcasted_iota(jnp.int32, (3 * bq, cols), 1) % bq
    band = (jj >= ii) & (jj <= ii + 2 * WINDOW)
    kcat = jnp.concatenate([kp_ref[...], kc_ref[...], kn_ref[...]], axis=0)
    vcat_t = jnp.concatenate([vp_ref[...], vc_ref[...], vn_ref[...]], axis=0).astype(F32).T.astype(BF16)
    for qb in range(SWA_BLOCKS):
        mask = band
        if qb == 0:
            mask = mask & (jj >= jnp.where(n > 0, 0, bq))
        if qb == SWA_BLOCKS - 1:
            mask = mask & (jj < jnp.where(n < nsteps - 1, 3 * bq, 2 * bq))
        q_t = q_ref[qb * bq:(qb + 1) * bq, :].astype(F32).T.astype(BF16)
        outs = []
        for hk in range(N_KV_HEADS):
            heads = range(hk * GQA_GROUPS, (hk + 1) * GQA_GROUPS)
            kh = kcat[qb * bq:(qb + 3) * bq, hk * HEAD_DIM:(hk + 1) * HEAD_DIM]
            vh_t = jnp.concatenate([vcat_t[hk * HEAD_DIM:(hk + 1) * HEAD_DIM, qb * bq:(qb + 3) * bq],
                                    jnp.ones((BF16_ROWS, 3 * bq), BF16)], axis=0)
            qs_t = jnp.concatenate([q_t[hd * HEAD_DIM:(hd + 1) * HEAD_DIM, :] for hd in heads], axis=1)
            sink = jnp.concatenate([jnp.broadcast_to(sink_ref[hd:hd + 1, 0:1], (1, bq)) for hd in heads], axis=1)
            s = jnp.where(mask, _dot(kh, qs_t), NEG)
            m = jnp.maximum(jnp.max(s, axis=0, keepdims=True), sink)
            o_ext = _dot(vh_t, jnp.exp(s - m).astype(BF16))
            denom = o_ext[HEAD_DIM:HEAD_DIM + 1, :] + jnp.exp(sink - m)
            o_t = o_ext[:HEAD_DIM, :] / denom
            outs.extend(o_t[:, g * bq:(g + 1) * bq] for g in range(GQA_GROUPS))
        o_ref[qb * bq:(qb + 1) * bq, :] = jnp.concatenate(outs, axis=0).T.astype(o_ref.dtype)


def _swa(q, k, v, sink_b, batch, seq):
    t = q.shape[0]
    bq = BLOCK_Q
    tq = SWA_BLOCKS * bq
    ns = seq // tq
    last = t // bq - 1
    cur = lambda bb, n: (bb * ns + n, 0)
    prev = lambda bb, n: (jnp.maximum((bb * ns + n) * SWA_BLOCKS - 1, 0), 0)
    nxt = lambda bb, n: (jnp.minimum((bb * ns + n + 1) * SWA_BLOCKS, last), 0)
    halo = lambda im: pl.BlockSpec((bq, KV_WIDTH), im)
    body = pl.BlockSpec((tq, KV_WIDTH), cur)
    return pl.pallas_call(
        _swa_kernel,
        grid=(batch, ns),
        in_specs=[pl.BlockSpec((tq, ATTN_WIDTH), cur), halo(prev), body, halo(nxt), halo(prev), body, halo(nxt),
                  pl.BlockSpec((N_Q_HEADS, LANE), lambda bb, n: (0, 0))],
        out_specs=pl.BlockSpec((tq, ATTN_WIDTH), cur),
        out_shape=jax.ShapeDtypeStruct((t, ATTN_WIDTH), BF16),
        compiler_params=_cparams(("parallel", "parallel")),
        name="window_attn",
    )(q, k, k, k, v, v, v, sink_b)


def _dense_attn_kernel(q_ref, k_ref, vt_ref, o_ref):
    ts = q_ref.shape[0] // ATT_CHAINS
    k = k_ref[...]
    v_ext = jnp.concatenate([vt_ref[...], jnp.ones((BF16_ROWS, vt_ref.shape[1]), BF16)], axis=0)
    q_t = q_ref[...].astype(F32).T.astype(BF16)

    def scores(c):
        cols = slice(c * ts, (c + 1) * ts)
        qs_t = jnp.concatenate([q_t[g * HEAD_DIM:(g + 1) * HEAD_DIM, cols] for g in range(GQA_GROUPS)], axis=1)
        return _dot(k, qs_t)

    s_next = scores(0)
    for c in range(ATT_CHAINS):
        cols = slice(c * ts, (c + 1) * ts)
        s = s_next
        if c + 1 < ATT_CHAINS:
            s_next = scores(c + 1)
        p = jnp.exp(s - jnp.max(s, axis=0, keepdims=True))
        o_ext = _dot(v_ext, p.astype(BF16))
        o_t = o_ext[:HEAD_DIM, :] / o_ext[HEAD_DIM:HEAD_DIM + 1, :]
        o_ref[cols, :] = jnp.concatenate(
            [o_t[:, g * ts:(g + 1) * ts] for g in range(GQA_GROUPS)], axis=0).T.astype(o_ref.dtype)


def _dense_attn(q, k, v, batch, seq):
    t = q.shape[0]
    tq = min(ATT_TQ, seq)
    nq = seq // tq
    gw = GQA_GROUPS * HEAD_DIM
    qmap = lambda bb, hk, qi: (bb * nq + qi, hk)
    return pl.pallas_call(
        _dense_attn_kernel,
        grid=(batch, N_KV_HEADS, nq),
        in_specs=[
            pl.BlockSpec((tq, gw), qmap),
            pl.BlockSpec((None, seq, HEAD_DIM), lambda bb, hk, qi: (hk, bb, 0)),
            pl.BlockSpec((None, HEAD_DIM, seq), lambda bb, hk, qi: (hk, 0, bb)),
        ],
        out_specs=pl.BlockSpec((tq, gw), qmap),
        out_shape=jax.ShapeDtypeStruct((t, ATTN_WIDTH), BF16),
        compiler_params=_cparams(("parallel", "parallel", "parallel")),
        name="dense_attn",
    )(q, k, v)


def _sgu_kernel(d_ref, g_ref, b_ref, w_ref, bs_ref, o_ref):
    z = jax.nn.gelu(d_ref[...])
    u = z[:, :MIX_WIDTH]
    v = _layernorm(z[:, MIX_WIDTH:], g_ref[...], b_ref[...]).astype(BF16)
    gc = MIX_WIDTH // SGU_GROUPS
    for n in range(d_ref.shape[0] // CHUNK):
        rows = slice(n * CHUNK, (n + 1) * CHUNK)
        for g in range(SGU_GROUPS):
            cols = slice(g * gc, (g + 1) * gc)
            mixed = _dot(w_ref[g], v[rows, cols]) + bs_ref[g]
            o_ref[rows, cols] = (u[rows, cols] * mixed).astype(o_ref.dtype)


def _sgu(d_in, g, b, w, bs):
    t = d_in.shape[0]
    ts = ROW_TILE
    gc = MIX_WIDTH // SGU_GROUPS
    row = lambda i: (i, 0)
    fixed = lambda i: (0, 0)
    fixed3 = lambda i: (0, 0, 0)
    return pl.pallas_call(
        _sgu_kernel,
        grid=(t // ts,),
        in_specs=[
            pl.BlockSpec((ts, 2 * MIX_WIDTH), row),
            pl.BlockSpec((1, MIX_WIDTH), fixed),
            pl.BlockSpec((1, MIX_WIDTH), fixed),
            pl.BlockSpec((SGU_GROUPS, CHUNK, CHUNK), fixed3),
            pl.BlockSpec((SGU_GROUPS, CHUNK, gc), fixed3),
        ],
        out_specs=pl.BlockSpec((ts, MIX_WIDTH), row),
        out_shape=jax.ShapeDtypeStruct((t, MIX_WIDTH), BF16),
        compiler_params=_cparams(("parallel",)),
        name="spatial_gating",
    )(d_in, g, b, w, bs)


def _outproj_kernel(x_ref, a_ref, b_ref, w_ref, g_ref, xn_ref, hnt_ref):
    y = _dot(a_ref[...], w_ref[:MIX_WIDTH, :]) + _dot(b_ref[...], w_ref[MIX_WIDTH:, :])
    xn = x_ref[...] + y
    xn_ref[...] = xn
    hnt_ref[...] = _rms(xn, g_ref[...]).T.astype(BF16)


def _outproj(x, a, b, w, g):
    t = x.shape[0]
    tm = ROW_TILE
    row = lambda i: (i, 0)
    fixed = lambda i: (0, 0)
    return pl.pallas_call(
        _outproj_kernel,
        grid=(t // tm,),
        in_specs=[
            pl.BlockSpec((tm, D_MODEL), row),
            pl.BlockSpec((tm, MIX_WIDTH), row),
            pl.BlockSpec((tm, MIX_WIDTH), row),
            pl.BlockSpec((2 * MIX_WIDTH, D_MODEL), fixed),
            pl.BlockSpec((1, D_MODEL), fixed),
        ],
        out_specs=[pl.BlockSpec((tm, D_MODEL), row), pl.BlockSpec((D_MODEL, tm), lambda i: (0, i))],
        out_shape=[jax.ShapeDtypeStruct((t, D_MODEL), F32), jax.ShapeDtypeStruct((D_MODEL, t), BF16)],
        compiler_params=_cparams(("parallel",)),
        name="outproj",
    )(x, a, b, w, g)


_PEER_PAIRS = [(i, j) for i in range(PEER_TOPK) for j in range(PEER_TOPK) if (i + 1) * (j + 1) <= PEER_TOPK]
_PEER_CAND_ROWS = -(-len(_PEER_PAIRS) // 8) * 8


SUBLANES = 8


def _bitonic_merge_desc(v, lo, n):
    step = n // 2
    while step >= 1:
        for i in range(lo, lo + n):
            if (i - lo) & step == 0:
                hi_, lo_ = jnp.maximum(v[i], v[i + step]), jnp.minimum(v[i], v[i + step])
                v[i], v[i + step] = hi_, lo_
        step //= 2


def _oddeven_merge(lo, hi, r):
    step = r * 2
    if step < hi - lo:
        yield from _oddeven_merge(lo, hi, step)
        yield from _oddeven_merge(lo + r, hi, step)
        yield from [(i, i + r) for i in range(lo + r, hi - r, step)]
    else:
        yield (lo, lo + r)


def _oddeven_merge_sort(lo, hi):
    if hi - lo >= 1:
        mid = lo + (hi - lo) // 2
        yield from _oddeven_merge_sort(lo, mid)
        yield from _oddeven_merge_sort(mid + 1, hi)
        yield from _oddeven_merge(lo, hi, 1)


def _sort16_desc(v):
    for i, j in _oddeven_merge_sort(0, len(v) - 1):
        v[i], v[j] = jnp.maximum(v[i], v[j]), jnp.minimum(v[i], v[j])
    return v


def _rank_among(b, v):
    lt = lambda t: t > b
    c1 = lt(v[7])
    c2 = lt(jnp.where(c1, v[11], v[3]))
    c3 = lt(jnp.where(c1, jnp.where(c2, v[13], v[9]), jnp.where(c2, v[5], v[1])))
    hi4 = jnp.where(c2, jnp.where(c3, v[14], v[12]), jnp.where(c3, v[10], v[8]))
    lo4 = jnp.where(c2, jnp.where(c3, v[6], v[4]), jnp.where(c3, v[2], v[0]))
    c4 = lt(jnp.where(c1, hi4, lo4))
    one = lambda c, x: jnp.where(c, x, 0.0)
    r = one(c1, 8.0) + one(c2, 4.0) + one(c3, 2.0) + one(c4, 1.0)
    return jnp.where(lt(v[15]), float(PEER_TOPK), r)


def _sublane_all(x, op):
    for shift in (4, 2, 1):
        x = op(x, pltpu.roll(x, shift, 0))
    return x


def _top16_sorted(s):
    v = _sort16_desc([s[k * SUBLANES:(k + 1) * SUBLANES, :] for k in range(PEER_NKEYS // SUBLANES)])
    for shift in (4, 2, 1):
        v = [jnp.maximum(v[k], pltpu.roll(v[PEER_TOPK - 1 - k], shift, 0)) for k in range(PEER_TOPK)]
        _bitonic_merge_desc(v, 0, PEER_TOPK)
    return v


def _peer_select(h, s1_all, s2_all, n_ref, e1_ref, r2_ref, e2_ref):
    tt = s1_all.shape[-1]
    nv = PEER_NKEYS // SUBLANES
    sub = lax.broadcasted_iota(jnp.int32, (SUBLANES, LANE), 0)
    for tc in range(tt // LANE):
        lanes = slice(tc * LANE, (tc + 1) * LANE)
        s1 = s1_all[:, lanes]
        s2 = s2_all[:, lanes]
        v1 = _top16_sorted(s1)
        v2 = _top16_sorted(s2)
        pair_sum = {p: v1[p[0]] + v2[p[1]] for p in _PEER_PAIRS}
        cand = []
        for g in range(_PEER_CAND_ROWS // SUBLANES):
            c = jnp.full((SUBLANES, LANE), -jnp.inf, F32)
            for r, p in enumerate(_PEER_PAIRS[g * SUBLANES:(g + 1) * SUBLANES]):
                c = jnp.where(sub == r, pair_sum[p], c)
            cand.append(c)
        f = []
        for _ in range(PEER_TOPK):
            mx = cand[0]
            for c in cand[1:]:
                mx = jnp.maximum(mx, c)
            mx = _sublane_all(mx, jnp.maximum)
            f.append(mx)
            cand = [jnp.where(c == mx, -jnp.inf, c) for c in cand]
        thr = f[PEER_TOPK - 1]
        z = jnp.ones((SUBLANES, LANE), F32)
        for r in range(1, PEER_TOPK):
            z = z + jnp.exp(f[r] - f[0])
        rz = 1.0 / z
        cnt = []
        for i in range(PEER_TOPK):
            c = jnp.zeros((SUBLANES, LANE), F32)
            for j in range(PEER_TOPK // (i + 1)):
                c = c + jnp.where(pair_sum[(i, j)] >= thr, 1.0, 0.0)
            cnt.append(c)
        ranks = []
        e2s = []
        for k in range(nv):
            rows = slice(k * SUBLANES, (k + 1) * SUBLANES)
            a = s1[rows, :]
            b = s2[rows, :]
            n = jnp.zeros((SUBLANES, LANE), F32)
            for r in range(PEER_TOPK):
                n = jnp.where(a == v1[r], cnt[r], n)
            n_ref[h, rows, lanes] = n
            e1_ref[h, rows, lanes] = jnp.exp(a - v1[0]) * rz
            ranks.append(_rank_among(b, v2))
            e2s.append(jnp.exp(b - v2[0]))
        r2_ref[h, :, lanes] = jnp.concatenate(ranks, axis=0).astype(r2_ref.dtype)
        e2_ref[h, :, lanes] = jnp.concatenate(e2s, axis=0).astype(e2_ref.dtype)


def _peer_gates(step, sub, n_ref, e1_ref, r2_ref, e2_ref, g_ref):
    tt = g_ref.shape[-1]
    gdt = g_ref.dtype
    first = sub * PEER_SUB
    a8 = pl.multiple_of(step * PEER_A + first // SUBLANES * SUBLANES, SUBLANES)
    nload = -(-(first % SUBLANES + PEER_SUB) // SUBLANES) * SUBLANES
    for tc in range(tt // LANE):
        lanes = slice(tc * LANE, (tc + 1) * LANE)
        nrows = [n_ref[h, pl.ds(a8, nload), lanes].astype(gdt) for h in range(PEER_HEADS)]
        erows = [e1_ref[h, pl.ds(a8, nload), lanes].astype(gdt) for h in range(PEER_HEADS)]
        for blk in range(PEER_SUB // PEER_GATE_BLOCK):
            gates = [jnp.zeros((PEER_NKEYS, LANE), gdt) for _ in range(PEER_GATE_BLOCK)]
            for h in range(PEER_HEADS):
                r2 = r2_ref[h, :, lanes]
                e2 = e2_ref[h, :, lanes]
                for k in range(PEER_GATE_BLOCK):
                    al = first % SUBLANES + blk * PEER_GATE_BLOCK + k
                    keep = r2 < nrows[h][al:al + 1, :]
                    gates[k] = gates[k] + jnp.where(keep, e2, 0.0) * erows[h][al:al + 1, :]
            for k in range(PEER_GATE_BLOCK):
                al = blk * PEER_GATE_BLOCK + k
                g_ref[sub, al * PEER_NKEYS:(al + 1) * PEER_NKEYS, lanes] = gates[k]


def _peer_kernel(x_ref, hnt_ref, wqt_ref, sk_ref, u_ref, vt_ref, o_ref,
                 qt_ref, n_ref, e1_ref, r2_ref, e2_ref, g_ref, acc_ref, ht_ref):
    j = pl.program_id(1)
    nj = pl.num_programs(1)
    half = PEER_DKEY // 2
    gdt = r2_ref.dtype
    gate_refs = (n_ref, e1_ref, r2_ref, e2_ref, g_ref)

    @pl.when(j == 0)
    def _():
        qt_ref[...] = _dot(wqt_ref[...], hnt_ref[...]).astype(BF16)

        def body(h, carry):
            r0 = pl.multiple_of(h * PEER_DKEY, PEER_DKEY)
            s1 = _dot(sk_ref[2 * h], qt_ref[pl.ds(r0, half), :])
            s2 = _dot(sk_ref[2 * h + 1], qt_ref[pl.ds(r0 + half, half), :])
            _peer_select(h, s1, s2, n_ref, e1_ref, r2_ref, e2_ref)
            return carry

        lax.fori_loop(0, PEER_HEADS, body, 0)
        acc_ref[...] = jnp.zeros(acc_ref.shape, F32)

    sub_rows = lambda sub: slice(sub * PEER_SUB * PEER_NKEYS, (sub + 1) * PEER_SUB * PEER_NKEYS)
    for sub in range(PEER_A // PEER_SUB):
        _peer_gates(j, sub, *gate_refs)
    for sub in range(PEER_A // PEER_SUB):
        at = _dot(u_ref[sub_rows(sub), :], hnt_ref[...]).astype(gdt)
        ht_ref[sub_rows(sub), :] = jax.nn.gelu(at) * g_ref[sub]
    acc_ref[...] += _dot(vt_ref[...], ht_ref[...])

    @pl.when(j == nj - 1)
    def _():
        o_ref[...] = x_ref[...] + acc_ref[...].T


def _peer(x, hnt, wqt, sk, u, vt, layer):
    t = x.shape[0]
    tt = PEER_TT
    ne = u.shape[1]
    et = PEER_A * PEER_NKEYS
    hk = PEER_HEADS * PEER_DKEY
    sel = lambda dt: pltpu.VMEM((PEER_HEADS, PEER_NKEYS, tt), dt)
    return pl.pallas_call(
        _peer_kernel,
        grid=(t // tt, ne // et),
        in_specs=[
            pl.BlockSpec((tt, D_MODEL), lambda i, j: (i, 0)),
            pl.BlockSpec((D_MODEL, tt), lambda i, j: (0, i)),
            pl.BlockSpec((hk, D_MODEL), lambda i, j: (0, 0), pipeline_mode=pl.Buffered(1)),
            pl.BlockSpec((2 * PEER_HEADS, PEER_NKEYS, PEER_DKEY // 2), lambda i, j: (0, 0, 0)),
            pl.BlockSpec((None, et, D_MODEL), lambda i, j: (layer, j, 0)),
            pl.BlockSpec((None, D_MODEL, et), lambda i, j: (layer, 0, j)),
        ],
        out_specs=pl.BlockSpec((tt, D_MODEL), lambda i, j: (i, 0)),
        out_shape=jax.ShapeDtypeStruct((t, D_MODEL), F32),
        scratch_shapes=[
            pltpu.VMEM((hk, tt), BF16),
            sel(F32), sel(F32), sel(PEER_GATE_DTYPE), sel(PEER_GATE_DTYPE),
            pltpu.VMEM((PEER_A // PEER_SUB, PEER_SUB * PEER_NKEYS, tt), PEER_GATE_DTYPE),
            pltpu.VMEM((D_MODEL, tt), F32),
            pltpu.VMEM((et, tt), BF16),
        ],
        compiler_params=_cparams(("parallel", "arbitrary")),
        name="peer",
    )(x, hnt, wqt, sk, u, vt)


def _final_norm_kernel(x_ref, g_ref, o_ref):
    o_ref[...] = _rms(x_ref[...], g_ref[...])


def _final_norm(x, g):
    t = x.shape[0]
    tm = ROW_TILE
    return pl.pallas_call(
        _final_norm_kernel,
        grid=(t // tm,),
        in_specs=[pl.BlockSpec((tm, D_MODEL), lambda i: (i, 0)), pl.BlockSpec((1, D_MODEL), lambda i: (0, 0))],
        out_specs=pl.BlockSpec((tm, D_MODEL), lambda i: (i, 0)),
        out_shape=jax.ShapeDtypeStruct((t, D_MODEL), F32),
        compiler_params=_cparams(("parallel",)),
        name="final_norm",
    )(x, g)


def _rope_table(pos, dim):
    inv = ROPE_THETA ** (-jnp.arange(0, dim, 2, dtype=F32) / dim)
    ang = pos.astype(F32)[:, None] * inv[None, :]
    ang = jnp.concatenate([ang, ang], axis=-1)
    sign = jnp.concatenate([-jnp.ones((dim // 2,), F32), jnp.ones((dim // 2,), F32)])
    return jnp.cos(ang), jnp.sin(ang) * sign


def _rope_tables(seq):
    pos = jnp.arange(seq)
    cos1, sin1 = _rope_table(pos, HEAD_DIM)
    cr, sr = _rope_table(pos // GRID_W, HEAD_DIM // 2)
    cc, sc = _rope_table(pos % GRID_W, HEAD_DIM // 2)
    cos2 = jnp.concatenate([cr, cc], axis=-1)
    sin2 = jnp.concatenate([sr, sc], axis=-1)
    tile = lambda a: jnp.tile(a, (1, N_Q_HEADS))
    return tile(cos1), tile(sin1), tile(cos2), tile(sin2)


def kernel(x, mix_norm_g, ffn_norm_g, final_norm_g, even_w_in, even_w_out, conv_w, conv_b, conv_ln_g, conv_ln_b, sink_logits, odd_w_in, odd_w_out, q_norm_g, k_norm_g, sgu_ln_g, sgu_ln_b, sgu_w, sgu_b, peer_wq, peer_subkeys, peer_u, peer_v):
    batch, seq, d = x.shape
    depth = mix_norm_g.shape[0]
    t = batch * seq
    cos1, sin1, cos2, sin2 = _rope_tables(seq)
    row = lambda a: a.reshape(1, -1).astype(F32)
    xf = x.reshape(t, d)
    u_all = peer_u.astype(BF16)
    vt_all = jnp.swapaxes(peer_v, 1, 2).astype(BF16)
    for layer in range(depth):
        i = layer // 2
        if layer % 2 == 0:
            a_in, q, k, v = _inproj_even(xf, row(mix_norm_g[layer]), even_w_in[i].astype(BF16), cos1, sin1, seq)
            m1 = _conv_module(a_in, conv_w[i, :, 0, :], row(conv_b[i]), row(conv_ln_g[i]), row(conv_ln_b[i]), batch, seq)
            sink_b = jnp.broadcast_to(sink_logits[i].astype(F32)[:, None], (N_Q_HEADS, LANE))
            m2 = _swa(q, k, v, sink_b, batch, seq)
            w_out = even_w_out[i]
        else:
            q, k, v, d_in = _inproj_odd(xf, row(mix_norm_g[layer]), odd_w_in[i].astype(BF16),
                                        row(jnp.tile(q_norm_g[i], N_Q_HEADS)), row(jnp.tile(k_norm_g[i], N_KV_HEADS)),
                                        cos2, sin2, seq)
            m1 = _dense_attn(q, k, v, batch, seq)
            bs = jnp.broadcast_to(sgu_b[i].astype(F32)[:, :, None], (SGU_GROUPS, CHUNK, MIX_WIDTH // SGU_GROUPS))
            m2 = _sgu(d_in, row(sgu_ln_g[i]), row(sgu_ln_b[i]), sgu_w[i].astype(BF16), bs)
            w_out = odd_w_out[i]
        xn, hnt = _outproj(xf, m1, m2, w_out.astype(BF16), row(ffn_norm_g[layer]))
        sk = peer_subkeys[layer].reshape(2 * PEER_HEADS, PEER_NKEYS, PEER_DKEY // 2).astype(BF16)
        xf = _peer(xn, hnt, peer_wq[layer].astype(BF16).T, sk, u_all, vt_all, layer)
    return _final_norm(xf, row(final_norm_g)).reshape(batch, seq, d)
```

```python
import jax
import jax.numpy as jnp
from jax import lax
from jax.experimental import pallas as pl
from jax.experimental.pallas import tpu as pltpu

F32 = jnp.float32
BF16 = jnp.bfloat16

D_MODEL = 1024
HEAD_DIM = 64
N_Q_HEADS = 8
N_KV_HEADS = 2
GQA_GROUPS = N_Q_HEADS // N_KV_HEADS
ATTN_WIDTH = N_Q_HEADS * HEAD_DIM
KV_WIDTH = N_KV_HEADS * HEAD_DIM
MIX_WIDTH = D_MODEL // 2
CONV_WIDTH = 31
CONV_HALO = 16
SGU_GROUPS = 4
CHUNK = 128
WINDOW = 128
BLOCK_Q = 128
GRID_W = 64
ROPE_THETA = 10000.0
PEER_HEADS = 8
PEER_NKEYS = 128
PEER_DKEY = 128
PEER_TOPK = 16
EPS = 1e-6
NEG = -1e30

VMEM_LIMIT_BYTES = 56 * 1024 * 1024

ROW_TILE = 1024
CONV_ROWS = 64
SWA_BLOCKS = 4
ATT_TQ = 1024
ATT_CHAINS = 8
PEER_TT = 512
PEER_A = 16
PEER_SUB = 8
PEER_GATE_BLOCK = 4
PEER_GATE_DTYPE = BF16
LANE = 128
BF16_ROWS = 16


def _cparams(sem):
    return pltpu.CompilerParams(dimension_semantics=sem, vmem_limit_bytes=VMEM_LIMIT_BYTES)


def _dot(a, b):
    return jnp.dot(a, b, preferred_element_type=F32)


def _rms(x, g):
    return x * lax.rsqrt(jnp.mean(x * x, axis=-1, keepdims=True) + EPS) * g


def _layernorm(x, g, b):
    mu = jnp.mean(x, axis=-1, keepdims=True)
    xc = x - mu
    var = jnp.mean(xc * xc, axis=-1, keepdims=True)
    return xc * lax.rsqrt(var + EPS) * g + b


def _rope(x, cos, sin_signed, group, half):
    n = x.shape[-1]
    lane = lax.broadcasted_iota(jnp.int32, x.shape, x.ndim - 1)
    first = (lane % group) < half
    partner = jnp.where(first, pltpu.roll(x, n - half, x.ndim - 1), pltpu.roll(x, half, x.ndim - 1))
    return x * cos + partner * sin_signed


def _inproj_even_kernel(x_ref, g_ref, w_ref, cos_ref, sin_ref, a_ref, q_ref, k_ref, v_ref):
    h = _rms(x_ref[...], g_ref[...]).astype(BF16)
    o1 = 2 * MIX_WIDTH
    o2 = o1 + ATTN_WIDTH
    o3 = o2 + KV_WIDTH
    a_ref[...] = _dot(h, w_ref[:, :o1])
    cos = cos_ref[...]
    sin = sin_ref[...]
    q = _rope(_dot(h, w_ref[:, o1:o2]), cos, sin, HEAD_DIM, HEAD_DIM // 2)
    q_ref[...] = (q * (HEAD_DIM ** -0.5)).astype(BF16)
    k = _rope(_dot(h, w_ref[:, o2:o3]), cos[:, :KV_WIDTH], sin[:, :KV_WIDTH], HEAD_DIM, HEAD_DIM // 2)
    k_ref[...] = k.astype(BF16)
    v_ref[...] = _dot(h, w_ref[:, o3:]).astype(BF16)


def _inproj_even(x, g, w, cos, sin, seq):
    t = x.shape[0]
    tm = ROW_TILE
    nseq = seq // tm
    in_w = w.shape[1]
    row = lambda i: (i, 0)
    fixed = lambda i: (0, 0)
    pos = lambda i: (i % nseq, 0)
    return pl.pallas_call(
        _inproj_even_kernel,
        grid=(t // tm,),
        in_specs=[
            pl.BlockSpec((tm, D_MODEL), row),
            pl.BlockSpec((1, D_MODEL), fixed),
            pl.BlockSpec((D_MODEL, in_w), fixed),
            pl.BlockSpec((tm, ATTN_WIDTH), pos),
            pl.BlockSpec((tm, ATTN_WIDTH), pos),
        ],
        out_specs=[
            pl.BlockSpec((tm, 2 * MIX_WIDTH), row),
            pl.BlockSpec((tm, ATTN_WIDTH), row),
            pl.BlockSpec((tm, KV_WIDTH), row),
            pl.BlockSpec((tm, KV_WIDTH), row),
        ],
        out_shape=[
            jax.ShapeDtypeStruct((t, 2 * MIX_WIDTH), F32),
            jax.ShapeDtypeStruct((t, ATTN_WIDTH), BF16),
            jax.ShapeDtypeStruct((t, KV_WIDTH), BF16),
            jax.ShapeDtypeStruct((t, KV_WIDTH), BF16),
        ],
        compiler_params=_cparams(("parallel",)),
        name="inproj_even",
    )(x, g, w, cos, sin)


def _head_sumsq(x):
    n = x.shape[-1]
    r = lax.broadcasted_iota(jnp.int32, (n, n), 0) // HEAD_DIM
    c = lax.broadcasted_iota(jnp.int32, (n, n), 1) // HEAD_DIM
    ones = jnp.where(r == c, 1.0, 0.0).astype(BF16)
    sq = x * x
    hi = sq.astype(BF16)
    lo = (sq - hi.astype(F32)).astype(BF16)
    return _dot(hi, ones) + _dot(lo, ones)


def _inproj_odd_kernel(x_ref, g_ref, w_ref, qg_ref, kg_ref, cos_ref, sin_ref, q_ref, k_ref, vt_ref, d_ref):
    h = _rms(x_ref[...], g_ref[...]).astype(BF16)
    c1 = ATTN_WIDTH
    c2 = c1 + KV_WIDTH
    c3 = c2 + KV_WIDTH
    cos = cos_ref[...]
    sin = sin_ref[...]
    half = HEAD_DIM // 2
    q = _dot(h, w_ref[:, :c1])
    q = q * lax.rsqrt(_head_sumsq(q) * (1.0 / HEAD_DIM) + EPS) * qg_ref[...]
    q = _rope(q, cos, sin, half, half // 2)
    q_ref[...] = (q * (HEAD_DIM ** -0.5)).astype(BF16)
    k = _dot(h, w_ref[:, c1:c2])
    k = k * lax.rsqrt(_head_sumsq(k) * (1.0 / HEAD_DIM) + EPS) * kg_ref[...]
    k = _rope(k, cos[:, :KV_WIDTH], sin[:, :KV_WIDTH], half, half // 2).astype(BF16)
    v_t = _dot(h, w_ref[:, c2:c3]).T.astype(BF16)
    for hh in range(N_KV_HEADS):
        k_ref[hh] = k[:, hh * HEAD_DIM:(hh + 1) * HEAD_DIM]
        vt_ref[hh] = v_t[hh * HEAD_DIM:(hh + 1) * HEAD_DIM, :]
    d_ref[...] = _dot(h, w_ref[:, c3:])


def _inproj_odd(x, g, w, qg, kg, cos, sin, seq):
    t = x.shape[0]
    tm = ROW_TILE
    nseq = seq // tm
    in_w = w.shape[1]
    row = lambda i: (i, 0)
    fixed = lambda i: (0, 0)
    pos = lambda i: (i % nseq, 0)
    kv = lambda i: (0, i, 0)
    return pl.pallas_call(
        _inproj_odd_kernel,
        grid=(t // tm,),
        in_specs=[
            pl.BlockSpec((tm, D_MODEL), row),
            pl.BlockSpec((1, D_MODEL), fixed),
            pl.BlockSpec((D_MODEL, in_w), fixed),
            pl.BlockSpec((1, ATTN_WIDTH), fixed),
            pl.BlockSpec((1, KV_WIDTH), fixed),
            pl.BlockSpec((tm, ATTN_WIDTH), pos),
            pl.BlockSpec((tm, ATTN_WIDTH), pos),
        ],
        out_specs=[
            pl.BlockSpec((tm, ATTN_WIDTH), row),
            pl.BlockSpec((N_KV_HEADS, tm, HEAD_DIM), kv),
            pl.BlockSpec((N_KV_HEADS, HEAD_DIM, tm), lambda i: (0, 0, i)),
            pl.BlockSpec((tm, 2 * MIX_WIDTH), row),
        ],
        out_shape=[
            jax.ShapeDtypeStruct((t, ATTN_WIDTH), BF16),
            jax.ShapeDtypeStruct((N_KV_HEADS, t, HEAD_DIM), BF16),
            jax.ShapeDtypeStruct((N_KV_HEADS, HEAD_DIM, t), BF16),
            jax.ShapeDtypeStruct((t, 2 * MIX_WIDTH), F32),
        ],
        compiler_params=_cparams(("parallel",)),
        name="inproj_odd",
    )(x, g, w, qg, kg, cos, sin)


def _glu(x):
    return x[:, :MIX_WIDTH] * jax.nn.sigmoid(x[:, MIX_WIDTH:])


def _conv_kernel(prev_ref, cur_ref, next_ref, w_ref, b_ref, g_ref, beta_ref, o_ref, buf_ref):
    i = pl.program_id(1)
    ns = pl.num_programs(1)
    ts = cur_ref.shape[0]
    halo = CONV_HALO
    buf_ref[0, 0:halo, :] = jnp.where(i > 0, _glu(prev_ref[...]), 0.0)
    buf_ref[0, halo:halo + ts, :] = _glu(cur_ref[...])
    buf_ref[0, halo + ts:, :] = jnp.where(i < ns - 1, _glu(next_ref[...]), 0.0)
    span = ts + 2 * halo - SUBLANES
    for s in range(1, SUBLANES):
        buf_ref[s, 0:span, :] = buf_ref[0, s:s + span, :]
    w = w_ref[...]
    bias = b_ref[...]
    g = g_ref[...]
    beta = beta_ref[...]
    first = halo - CONV_WIDTH // 2
    for c in range(ts // CONV_ROWS):
        r0 = c * CONV_ROWS
        acc = jnp.zeros((CONV_ROWS, MIX_WIDTH), F32)
        for k in range(CONV_WIDTH):
            m, s = divmod(first + k, SUBLANES)
            lo = r0 + SUBLANES * m
            acc = acc + w[k:k + 1, :] * buf_ref[s, lo:lo + CONV_ROWS, :]
        y = _layernorm(acc + bias, g, beta)
        o_ref[r0:r0 + CONV_ROWS, :] = (y * jax.nn.sigmoid(y)).astype(o_ref.dtype)


def _conv_module(a_in, w, b, g, beta, batch, seq):
    t = a_in.shape[0]
    ts = ROW_TILE
    ns = seq // ts
    hb = ts // CONV_HALO
    last = t // CONV_HALO - 1
    fixed = lambda bb, i: (0, 0)
    return pl.pallas_call(
        _conv_kernel,
        grid=(batch, ns),
        in_specs=[
            pl.BlockSpec((CONV_HALO, 2 * MIX_WIDTH), lambda bb, i: (jnp.maximum((bb * ns + i) * hb - 1, 0), 0)),
            pl.BlockSpec((ts, 2 * MIX_WIDTH), lambda bb, i: (bb * ns + i, 0)),
            pl.BlockSpec((CONV_HALO, 2 * MIX_WIDTH), lambda bb, i: (jnp.minimum((bb * ns + i + 1) * hb, last), 0)),
            pl.BlockSpec((CONV_WIDTH, MIX_WIDTH), fixed),
            pl.BlockSpec((1, MIX_WIDTH), fixed),
            pl.BlockSpec((1, MIX_WIDTH), fixed),
            pl.BlockSpec((1, MIX_WIDTH), fixed),
        ],
        out_specs=pl.BlockSpec((ts, MIX_WIDTH), lambda bb, i: (bb * ns + i, 0)),
        out_shape=jax.ShapeDtypeStruct((t, MIX_WIDTH), BF16),
        scratch_shapes=[pltpu.VMEM((SUBLANES, ts + 2 * CONV_HALO, MIX_WIDTH), F32)],
        compiler_params=_cparams(("parallel", "parallel")),
        name="conv_module",
    )(a_in, a_in, a_in, w, b, g, beta)


def _swa_kernel(q_ref, kp_ref, kc_ref, kn_ref, vp_ref, vc_ref, vn_ref, sink_ref, o_ref):
    n = pl.program_id(1)
    nsteps = pl.num_programs(1)
    bq = BLOCK_Q
    cols = GQA_GROUPS * bq
    jj = lax.broadcasted_iota(jnp.int32, (3 * bq, cols), 0)
    ii = lax.broadcasted_iota(jnp.int32, (3 * bq, cols), 1) % bq
    band = (jj >= ii) & (jj <= ii + 2 * WINDOW)
    kcat = jnp.concatenate([kp_ref[...], kc_ref[...], kn_ref[...]], axis=0)
    vcat_t = jnp.concatenate([vp_ref[...], vc_ref[...], vn_ref[...]], axis=0).astype(F32).T.astype(BF16)
    for qb in range(SWA_BLOCKS):
        mask = band
        if qb == 0:
            mask = mask & (jj >= jnp.where(n > 0, 0, bq))
        if qb == SWA_BLOCKS - 1:
            mask = mask & (jj < jnp.where(n < nsteps - 1, 3 * bq, 2 * bq))
        q_t = q_ref[qb * bq:(qb + 1) * bq, :].astype(F32).T.astype(BF16)
        outs = []
        for hk in range(N_KV_HEADS):
            heads = range(hk * GQA_GROUPS, (hk + 1) * GQA_GROUPS)
            kh = kcat[qb * bq:(qb + 3) * bq, hk * HEAD_DIM:(hk + 1) * HEAD_DIM]
            vh_t = jnp.concatenate([vcat_t[hk * HEAD_DIM:(hk + 1) * HEAD_DIM, qb * bq:(qb + 3) * bq],
                                    jnp.ones((BF16_ROWS, 3 * bq), BF16)], axis=0)
            qs_t = jnp.concatenate([q_t[hd * HEAD_DIM:(hd + 1) * HEAD_DIM, :] for hd in heads], axis=1)
            sink = jnp.concatenate([jnp.broadcast_to(sink_ref[hd:hd + 1, 0:1], (1, bq)) for hd in heads], axis=1)
            s = jnp.where(mask, _dot(kh, qs_t), NEG)
            m = jnp.maximum(jnp.max(s, axis=0, keepdims=True), sink)
            o_ext = _dot(vh_t, jnp.exp(s - m).astype(BF16))
            denom = o_ext[HEAD_DIM:HEAD_DIM + 1, :] + jnp.exp(sink - m)
            o_t = o_ext[:HEAD_DIM, :] / denom
            outs.extend(o_t[:, g * bq:(g + 1) * bq] for g in range(GQA_GROUPS))
        o_ref[qb * bq:(qb + 1) * bq, :] = jnp.concatenate(outs, axis=0).T.astype(o_ref.dtype)


def _swa(q, k, v, sink_b, batch, seq):
    t = q.shape[0]
    bq = BLOCK_Q
    tq = SWA_BLOCKS * bq
    ns = seq // tq
    last = t // bq - 1
    cur = lambda bb, n: (bb * ns + n, 0)
    prev = lambda bb, n: (jnp.maximum((bb * ns + n) * SWA_BLOCKS - 1, 0), 0)
    nxt = lambda bb, n: (jnp.minimum((bb * ns + n + 1) * SWA_BLOCKS, last), 0)
    halo = lambda im: pl.BlockSpec((bq, KV_WIDTH), im)
    body = pl.BlockSpec((tq, KV_WIDTH), cur)
    return pl.pallas_call(
        _swa_kernel,
        grid=(batch, ns),
        in_specs=[pl.BlockSpec((tq, ATTN_WIDTH), cur), halo(prev), body, halo(nxt), halo(prev), body, halo(nxt),
                  pl.BlockSpec((N_Q_HEADS, LANE), lambda bb, n: (0, 0))],
        out_specs=pl.BlockSpec((tq, ATTN_WIDTH), cur),
        out_shape=jax.ShapeDtypeStruct((t, ATTN_WIDTH), BF16),
        compiler_params=_cparams(("parallel", "parallel")),
        name="window_attn",
    )(q, k, k, k, v, v, v, sink_b)


def _dense_attn_kernel(q_ref, k_ref, vt_ref, o_ref):
    ts = q_ref.shape[0] // ATT_CHAINS
    k = k_ref[...]
    v_ext = jnp.concatenate([vt_ref[...], jnp.ones((BF16_ROWS, vt_ref.shape[1]), BF16)], axis=0)
    q_t = q_ref[...].astype(F32).T.astype(BF16)

    def scores(c):
        cols = slice(c * ts, (c + 1) * ts)
        qs_t = jnp.concatenate([q_t[g * HEAD_DIM:(g + 1) * HEAD_DIM, cols] for g in range(GQA_GROUPS)], axis=1)
        return _dot(k, qs_t)

    s_next = scores(0)
    for c in range(ATT_CHAINS):
        cols = slice(c * ts, (c + 1) * ts)
        s = s_next
        if c + 1 < ATT_CHAINS:
            s_next = scores(c + 1)
        p = jnp.exp(s - jnp.max(s, axis=0, keepdims=True))
        o_ext = _dot(v_ext, p.astype(BF16))
        o_t = o_ext[:HEAD_DIM, :] / o_ext[HEAD_DIM:HEAD_DIM + 1, :]
        o_ref[cols, :] = jnp.concatenate(
            [o_t[:, g * ts:(g + 1) * ts] for g in range(GQA_GROUPS)], axis=0).T.astype(o_ref.dtype)


def _dense_attn(q, k, v, batch, seq):
    t = q.shape[0]
    tq = min(ATT_TQ, seq)
    nq = seq // tq
    gw = GQA_GROUPS * HEAD_DIM
    qmap = lambda bb, hk, qi: (bb * nq + qi, hk)
    return pl.pallas_call(
        _dense_attn_kernel,
        grid=(batch, N_KV_HEADS, nq),
        in_specs=[
            pl.BlockSpec((tq, gw), qmap),
            pl.BlockSpec((None, seq, HEAD_DIM), lambda bb, hk, qi: (hk, bb, 0)),
            pl.BlockSpec((None, HEAD_DIM, seq), lambda bb, hk, qi: (hk, 0, bb)),
        ],
        out_specs=pl.BlockSpec((tq, gw), qmap),
        out_shape=jax.ShapeDtypeStruct((t, ATTN_WIDTH), BF16),
        compiler_params=_cparams(("parallel", "parallel", "parallel")),
        name="dense_attn",
    )(q, k, v)


def _sgu_kernel(d_ref, g_ref, b_ref, w_ref, bs_ref, o_ref):
    z = jax.nn.gelu(d_ref[...])
    u = z[:, :MIX_WIDTH]
    v = _layernorm(z[:, MIX_WIDTH:], g_ref[...], b_ref[...]).astype(BF16)
    gc = MIX_WIDTH // SGU_GROUPS
    for n in range(d_ref.shape[0] // CHUNK):
        rows = slice(n * CHUNK, (n + 1) * CHUNK)
        for g in range(SGU_GROUPS):
            cols = slice(g * gc, (g + 1) * gc)
            mixed = _dot(w_ref[g], v[rows, cols]) + bs_ref[g]
            o_ref[rows, cols] = (u[rows, cols] * mixed).astype(o_ref.dtype)


def _sgu(d_in, g, b, w, bs):
    t = d_in.shape[0]
    ts = ROW_TILE
    gc = MIX_WIDTH // SGU_GROUPS
    row = lambda i: (i, 0)
    fixed = lambda i: (0, 0)
    fixed3 = lambda i: (0, 0, 0)
    return pl.pallas_call(
        _sgu_kernel,
        grid=(t // ts,),
        in_specs=[
            pl.BlockSpec((ts, 2 * MIX_WIDTH), row),
            pl.BlockSpec((1, MIX_WIDTH), fixed),
            pl.BlockSpec((1, MIX_WIDTH), fixed),
            pl.BlockSpec((SGU_GROUPS, CHUNK, CHUNK), fixed3),
            pl.BlockSpec((SGU_GROUPS, CHUNK, gc), fixed3),
        ],
        out_specs=pl.BlockSpec((ts, MIX_WIDTH), row),
        out_shape=jax.ShapeDtypeStruct((t, MIX_WIDTH), BF16),
        compiler_params=_cparams(("parallel",)),
        name="spatial_gating",
    )(d_in, g, b, w, bs)


def _outproj_kernel(x_ref, a_ref, b_ref, w_ref, g_ref, xn_ref, hnt_ref):
    y = _dot(a_ref[...], w_ref[:MIX_WIDTH, :]) + _dot(b_ref[...], w_ref[MIX_WIDTH:, :])
    xn = x_ref[...] + y
    xn_ref[...] = xn
    hnt_ref[...] = _rms(xn, g_ref[...]).T.astype(BF16)


def _outproj(x, a, b, w, g):
    t = x.shape[0]
    tm = ROW_TILE
    row = lambda i: (i, 0)
    fixed = lambda i: (0, 0)
    return pl.pallas_call(
        _outproj_kernel,
        grid=(t // tm,),
        in_specs=[
            pl.BlockSpec((tm, D_MODEL), row),
            pl.BlockSpec((tm, MIX_WIDTH), row),
            pl.BlockSpec((tm, MIX_WIDTH), row),
            pl.BlockSpec((2 * MIX_WIDTH, D_MODEL), fixed),
            pl.BlockSpec((1, D_MODEL), fixed),
        ],
        out_specs=[pl.BlockSpec((tm, D_MODEL), row), pl.BlockSpec((D_MODEL, tm), lambda i: (0, i))],
        out_shape=[jax.ShapeDtypeStruct((t, D_MODEL), F32), jax.ShapeDtypeStruct((D_MODEL, t), BF16)],
        compiler_params=_cparams(("parallel",)),
        name="outproj",
    )(x, a, b, w, g)


_PEER_PAIRS = [(i, j) for i in range(PEER_TOPK) for j in range(PEER_TOPK) if (i + 1) * (j + 1) <= PEER_TOPK]
_PEER_CAND_ROWS = -(-len(_PEER_PAIRS) // 8) * 8


SUBLANES = 8


def _bitonic_merge_desc(v, lo, n):
    step = n // 2
    while step >= 1:
        for i in range(lo, lo + n):
            if (i - lo) & step == 0:
                hi_, lo_ = jnp.maximum(v[i], v[i + step]), jnp.minimum(v[i], v[i + step])
                v[i], v[i + step] = hi_, lo_
        step //= 2


def _oddeven_merge(lo, hi, r):
    step = r * 2
    if step < hi - lo:
        yield from _oddeven_merge(lo, hi, step)
        yield from _oddeven_merge(lo + r, hi, step)
        yield from [(i, i + r) for i in range(lo + r, hi - r, step)]
    else:
        yield (lo, lo + r)


def _oddeven_merge_sort(lo, hi):
    if hi - lo >= 1:
        mid = lo + (hi - lo) // 2
        yield from _oddeven_merge_sort(lo, mid)
        yield from _oddeven_merge_sort(mid + 1, hi)
        yield from _oddeven_merge(lo, hi, 1)


def _sort16_desc(v):
    for i, j in _oddeven_merge_sort(0, len(v) - 1):
        v[i], v[j] = jnp.maximum(v[i], v[j]), jnp.minimum(v[i], v[j])
    return v


def _rank_among(b, v):
    lt = lambda t: t > b
    c1 = lt(v[7])
    c2 = lt(jnp.where(c1, v[11], v[3]))
    c3 = lt(jnp.where(c1, jnp.where(c2, v[13], v[9]), jnp.where(c2, v[5], v[1])))
    hi4 = jnp.where(c2, jnp.where(c3, v[14], v[12]), jnp.where(c3, v[10], v[8]))
    lo4 = jnp.where(c2, jnp.where(c3, v[6], v[4]), jnp.where(c3, v[2], v[0]))
    c4 = lt(jnp.where(c1, hi4, lo4))
    one = lambda c, x: jnp.where(c, x, 0.0)
    r = one(c1, 8.0) + one(c2, 4.0) + one(c3, 2.0) + one(c4, 1.0)
    return jnp.where(lt(v[15]), float(PEER_TOPK), r)


def _top16_sorted(s):
    v = _sort16_desc([s[k * SUBLANES:(k + 1) * SUBLANES, :] for k in range(PEER_NKEYS // SUBLANES)])
    for shift in (4, 2, 1):
        v = [jnp.maximum(v[k], pltpu.roll(v[PEER_TOPK - 1 - k], shift, 0)) for k in range(PEER_TOPK)]
        _bitonic_merge_desc(v, 0, PEER_TOPK)
    return v


def _peer_select(h, s1_all, s2_all, n_ref, e1_ref, r2_ref, e2_ref):
    tt = s1_all.shape[-1]
    nv = PEER_NKEYS // SUBLANES
    sub = lax.broadcasted_iota(jnp.int32, (SUBLANES, LANE), 0)
    for tc in range(tt // LANE):
        lanes = slice(tc * LANE, (tc + 1) * LANE)
        s1 = s1_all[:, lanes]
        s2 = s2_all[:, lanes]
        v1 = _top16_sorted(s1)
        v2 = _top16_sorted(s2)
        pair_sum = {p: v1[p[0]] + v2[p[1]] for p in _PEER_PAIRS}
        cand = []
        for g in range(_PEER_CAND_ROWS // SUBLANES):
            c = jnp.full((SUBLANES, LANE), -jnp.inf, F32)
            for r, p in enumerate(_PEER_PAIRS[g * SUBLANES:(g + 1) * SUBLANES]):
                c = jnp.where(sub == r, pair_sum[p], c)
            cand.append(c)
        cand.append(jnp.full((SUBLANES, LANE), -jnp.inf, F32))
        f = _sort16_desc(cand)
        f = f + [pltpu.roll(c, 4, 0) for c in reversed(f)]
        _bitonic_merge_desc(f, 0, PEER_TOPK)
        for shift in (2, 1):
            f = [jnp.maximum(f[k], pltpu.roll(f[PEER_TOPK - 1 - k], shift, 0)) for k in range(PEER_TOPK)]
            _bitonic_merge_desc(f, 0, PEER_TOPK)
        thr = f[PEER_TOPK - 1]
        z = jnp.ones((SUBLANES, LANE), F32)
        for r in range(1, PEER_TOPK):
            z = z + jnp.exp(f[r] - f[0])
        rz = 1.0 / z
        cnt = []
        for i in range(PEER_TOPK):
            c = jnp.zeros((SUBLANES, LANE), F32)
            for j in range(PEER_TOPK // (i + 1)):
                c = c + jnp.where(pair_sum[(i, j)] >= thr, 1.0, 0.0)
            cnt.append(c)
        ranks = []
        e2s = []
        for k in range(nv):
            rows = slice(k * SUBLANES, (k + 1) * SUBLANES)
            a = s1[rows, :]
            b = s2[rows, :]
            n = jnp.zeros((SUBLANES, LANE), F32)
            for r in range(PEER_TOPK):
                n = jnp.where(a == v1[r], cnt[r], n)
            n_ref[h, rows, lanes] = n
            e1_ref[h, rows, lanes] = jnp.exp(a - v1[0]) * rz
            ranks.append(_rank_among(b, v2))
            e2s.append(jnp.exp(b - v2[0]))
        r2_ref[h, :, lanes] = jnp.concatenate(ranks, axis=0).astype(r2_ref.dtype)
        e2_ref[h, :, lanes] = jnp.concatenate(e2s, axis=0).astype(e2_ref.dtype)


def _peer_gates(step, sub, n_ref, e1_ref, r2_ref, e2_ref, g_ref):
    tt = g_ref.shape[-1]
    gdt = g_ref.dtype
    first = sub * PEER_SUB
    a8 = pl.multiple_of(step * PEER_A + first // SUBLANES * SUBLANES, SUBLANES)
    nload = -(-(first % SUBLANES + PEER_SUB) // SUBLANES) * SUBLANES
    for tc in range(tt // LANE):
        lanes = slice(tc * LANE, (tc + 1) * LANE)
        nrows = [n_ref[h, pl.ds(a8, nload), lanes].astype(gdt) for h in range(PEER_HEADS)]
        erows = [e1_ref[h, pl.ds(a8, nload), lanes].astype(gdt) for h in range(PEER_HEADS)]
        for blk in range(PEER_SUB // PEER_GATE_BLOCK):
            gates = [jnp.zeros((PEER_NKEYS, LANE), gdt) for _ in range(PEER_GATE_BLOCK)]
            for h in range(PEER_HEADS):
                r2 = r2_ref[h, :, lanes]
                e2 = e2_ref[h, :, lanes]
                for k in range(PEER_GATE_BLOCK):
                    al = first % SUBLANES + blk * PEER_GATE_BLOCK + k
                    keep = r2 < nrows[h][al:al + 1, :]
                    gates[k] = gates[k] + jnp.where(keep, e2, 0.0) * erows[h][al:al + 1, :]
            for k in range(PEER_GATE_BLOCK):
                al = blk * PEER_GATE_BLOCK + k
                g_ref[sub, al * PEER_NKEYS:(al + 1) * PEER_NKEYS, lanes] = gates[k]


def _peer_kernel(x_ref, hnt_ref, wqt_ref, sk_ref, u_ref, vt_ref, o_ref,
                 qt_ref, n_ref, e1_ref, r2_ref, e2_ref, g_ref, acc_ref, ht_ref):
    j = pl.program_id(1)
    nj = pl.num_programs(1)
    half = PEER_DKEY // 2
    gdt = r2_ref.dtype
    gate_refs = (n_ref, e1_ref, r2_ref, e2_ref, g_ref)

    @pl.when(j == 0)
    def _():
        qt_ref[...] = _dot(wqt_ref[...], hnt_ref[...]).astype(BF16)

        def body(h, carry):
            r0 = pl.multiple_of(h * PEER_DKEY, PEER_DKEY)
            s1 = _dot(sk_ref[2 * h], qt_ref[pl.ds(r0, half), :])
            s2 = _dot(sk_ref[2 * h + 1], qt_ref[pl.ds(r0 + half, half), :])
            _peer_select(h, s1, s2, n_ref, e1_ref, r2_ref, e2_ref)
            return carry

        lax.fori_loop(0, PEER_HEADS, body, 0)
        acc_ref[...] = jnp.zeros(acc_ref.shape, F32)

    sub_rows = lambda sub: slice(sub * PEER_SUB * PEER_NKEYS, (sub + 1) * PEER_SUB * PEER_NKEYS)
    for sub in range(PEER_A // PEER_SUB):
        _peer_gates(j, sub, *gate_refs)
    for sub in range(PEER_A // PEER_SUB):
        at = _dot(u_ref[sub_rows(sub), :], hnt_ref[...]).astype(gdt)
        ht_ref[sub_rows(sub), :] = jax.nn.gelu(at) * g_ref[sub]
    acc_ref[...] += _dot(vt_ref[...], ht_ref[...])

    @pl.when(j == nj - 1)
    def _():
        o_ref[...] = x_ref[...] + acc_ref[...].T


def _peer(x, hnt, wqt, sk, u, vt, layer):
    t = x.shape[0]
    tt = PEER_TT
    ne = u.shape[1]
    et = PEER_A * PEER_NKEYS
    hk = PEER_HEADS * PEER_DKEY
    sel = lambda dt: pltpu.VMEM((PEER_HEADS, PEER_NKEYS, tt), dt)
    return pl.pallas_call(
        _peer_kernel,
        grid=(t // tt, ne // et),
        in_specs=[
            pl.BlockSpec((tt, D_MODEL), lambda i, j: (i, 0)),
            pl.BlockSpec((D_MODEL, tt), lambda i, j: (0, i)),
            pl.BlockSpec((hk, D_MODEL), lambda i, j: (0, 0), pipeline_mode=pl.Buffered(1)),
            pl.BlockSpec((2 * PEER_HEADS, PEER_NKEYS, PEER_DKEY // 2), lambda i, j: (0, 0, 0)),
            pl.BlockSpec((None, et, D_MODEL), lambda i, j: (layer, j, 0)),
            pl.BlockSpec((None, D_MODEL, et), lambda i, j: (layer, 0, j)),
        ],
        out_specs=pl.BlockSpec((tt, D_MODEL), lambda i, j: (i, 0)),
        out_shape=jax.ShapeDtypeStruct((t, D_MODEL), F32),
        scratch_shapes=[
            pltpu.VMEM((hk, tt), BF16),
            sel(F32), sel(F32), sel(PEER_GATE_DTYPE), sel(PEER_GATE_DTYPE),
            pltpu.VMEM((PEER_A // PEER_SUB, PEER_SUB * PEER_NKEYS, tt), PEER_GATE_DTYPE),
            pltpu.VMEM((D_MODEL, tt), F32),
            pltpu.VMEM((et, tt), BF16),
        ],
        compiler_params=_cparams(("parallel", "arbitrary")),
        name="peer",
    )(x, hnt, wqt, sk, u, vt)


def _final_norm_kernel(x_ref, g_ref, o_ref):
    o_ref[...] = _rms(x_ref[...], g_ref[...])


def _final_norm(x, g):
    t = x.shape[0]
    tm = ROW_TILE
    return pl.pallas_call(
        _final_norm_kernel,
        grid=(t // tm,),
        in_specs=[pl.BlockSpec((tm, D_MODEL), lambda i: (i, 0)), pl.BlockSpec((1, D_MODEL), lambda i: (0, 0))],
        out_specs=pl.BlockSpec((tm, D_MODEL), lambda i: (i, 0)),
        out_shape=jax.ShapeDtypeStruct((t, D_MODEL), F32),
        compiler_params=_cparams(("parallel",)),
        name="final_norm",
    )(x, g)


def _rope_table(pos, dim):
    inv = ROPE_THETA ** (-jnp.arange(0, dim, 2, dtype=F32) / dim)
    ang = pos.astype(F32)[:, None] * inv[None, :]
    ang = jnp.concatenate([ang, ang], axis=-1)
    sign = jnp.concatenate([-jnp.ones((dim // 2,), F32), jnp.ones((dim // 2,), F32)])
    return jnp.cos(ang), jnp.sin(ang) * sign


def _rope_tables(seq):
    pos = jnp.arange(seq)
    cos1, sin1 = _rope_table(pos, HEAD_DIM)
    cr, sr = _rope_table(pos // GRID_W, HEAD_DIM // 2)
    cc, sc = _rope_table(pos % GRID_W, HEAD_DIM // 2)
    cos2 = jnp.concatenate([cr, cc], axis=-1)
    sin2 = jnp.concatenate([sr, sc], axis=-1)
    tile = lambda a: jnp.tile(a, (1, N_Q_HEADS))
    return tile(cos1), tile(sin1), tile(cos2), tile(sin2)


def kernel(x, mix_norm_g, ffn_norm_g, final_norm_g, even_w_in, even_w_out, conv_w, conv_b, conv_ln_g, conv_ln_b, sink_logits, odd_w_in, odd_w_out, q_norm_g, k_norm_g, sgu_ln_g, sgu_ln_b, sgu_w, sgu_b, peer_wq, peer_subkeys, peer_u, peer_v):
    batch, seq, d = x.shape
    depth = mix_norm_g.shape[0]
    t = batch * seq
    cos1, sin1, cos2, sin2 = _rope_tables(seq)
    row = lambda a: a.reshape(1, -1).astype(F32)
    xf = x.reshape(t, d)
    u_all = peer_u.astype(BF16)
    vt_all = jnp.swapaxes(peer_v, 1, 2).astype(BF16)
    for layer in range(depth):
        i = layer // 2
        if layer % 2 == 0:
            a_in, q, k, v = _inproj_even(xf, row(mix_norm_g[layer]), even_w_in[i].astype(BF16), cos1, sin1, seq)
            m1 = _conv_module(a_in, conv_w[i, :, 0, :], row(conv_b[i]), row(conv_ln_g[i]), row(conv_ln_b[i]), batch, seq)
            sink_b = jnp.broadcast_to(sink_logits[i].astype(F32)[:, None], (N_Q_HEADS, LANE))
            m2 = _swa(q, k, v, sink_b, batch, seq)
            w_out = even_w_out[i]
        else:
            q, k, v, d_in = _inproj_odd(xf, row(mix_norm_g[layer]), odd_w_in[i].astype(BF16),
                                        row(jnp.tile(q_norm_g[i], N_Q_HEADS)), row(jnp.tile(k_norm_g[i], N_KV_HEADS)),
                                        cos2, sin2, seq)
            m1 = _dense_attn(q, k, v, batch, seq)
            bs = jnp.broadcast_to(sgu_b[i].astype(F32)[:, :, None], (SGU_GROUPS, CHUNK, MIX_WIDTH // SGU_GROUPS))
            m2 = _sgu(d_in, row(sgu_ln_g[i]), row(sgu_ln_b[i]), sgu_w[i].astype(BF16), bs)
            w_out = odd_w_out[i]
        xn, hnt = _outproj(xf, m1, m2, w_out.astype(BF16), row(ffn_norm_g[layer]))
        sk = peer_subkeys[layer].reshape(2 * PEER_HEADS, PEER_NKEYS, PEER_DKEY // 2).astype(BF16)
        xf = _peer(xn, hnt, peer_wq[layer].astype(BF16).T, sk, u_all, vt_all, layer)
    return _final_norm(xf, row(final_norm_g)).reshape(batch, seq, d)
```

```python
import jax
import jax.numpy as jnp
from jax import lax
from jax.experimental import pallas as pl
from jax.experimental.pallas import tpu as pltpu

F32 = jnp.float32
BF16 = jnp.bfloat16

D_MODEL = 1024
HEAD_DIM = 64
N_Q_HEADS = 8
N_KV_HEADS = 2
GQA_GROUPS = N_Q_HEADS // N_KV_HEADS
ATTN_WIDTH = N_Q_HEADS * HEAD_DIM
KV_WIDTH = N_KV_HEADS * HEAD_DIM
MIX_WIDTH = D_MODEL // 2
CONV_WIDTH = 31
CONV_HALO = 16
SGU_GROUPS = 4
CHUNK = 128
WINDOW = 128
BLOCK_Q = 128
GRID_W = 64
ROPE_THETA = 10000.0
PEER_HEADS = 8
PEER_NKEYS = 128
PEER_DKEY = 128
PEER_TOPK = 16
EPS = 1e-6
NEG = -1e30

VMEM_LIMIT_BYTES = 56 * 1024 * 1024

ROW_TILE = 1024
CONV_ROWS = 64
SWA_BLOCKS = 4
ATT_TQ = 1024
ATT_CHAINS = 8
PEER_TT = 512
PEER_A = 16
PEER_SUB = 8
PEER_GATE_BLOCK = 4
PEER_GATE_DTYPE = BF16
LANE = 128
BF16_ROWS = 16


def _cparams(sem):
    return pltpu.CompilerParams(dimension_semantics=sem, vmem_limit_bytes=VMEM_LIMIT_BYTES)


def _dot(a, b):
    return jnp.dot(a, b, preferred_element_type=F32)


def _rms(x, g):
    return x * lax.rsqrt(jnp.mean(x * x, axis=-1, keepdims=True) + EPS) * g


def _layernorm(x, g, b):
    mu = jnp.mean(x, axis=-1, keepdims=True)
    xc = x - mu
    var = jnp.mean(xc * xc, axis=-1, keepdims=True)
    return xc * lax.rsqrt(var + EPS) * g + b


def _rope(x, cos, sin_signed, group, half):
    n = x.shape[-1]
    lane = lax.broadcasted_iota(jnp.int32, x.shape, x.ndim - 1)
    first = (lane % group) < half
    partner = jnp.where(first, pltpu.roll(x, n - half, x.ndim - 1), pltpu.roll(x, half, x.ndim - 1))
    return x * cos + partner * sin_signed


def _inproj_even_kernel(x_ref, g_ref, w_ref, cos_ref, sin_ref, a_ref, q_ref, k_ref, v_ref):
    h = _rms(x_ref[...], g_ref[...]).astype(BF16)
    o1 = 2 * MIX_WIDTH
    o2 = o1 + ATTN_WIDTH
    o3 = o2 + KV_WIDTH
    a_ref[...] = _dot(h, w_ref[:, :o1])
    cos = cos_ref[...]
    sin = sin_ref[...]
    q = _rope(_dot(h, w_ref[:, o1:o2]), cos, sin, HEAD_DIM, HEAD_DIM // 2)
    q_ref[...] = (q * (HEAD_DIM ** -0.5)).astype(BF16)
    k = _rope(_dot(h, w_ref[:, o2:o3]), cos[:, :KV_WIDTH], sin[:, :KV_WIDTH], HEAD_DIM, HEAD_DIM // 2)
    k_ref[...] = k.astype(BF16)
    v_ref[...] = _dot(h, w_ref[:, o3:]).astype(BF16)


def _inproj_even(x, g, w, cos, sin, seq):
    t = x.shape[0]
    tm = ROW_TILE
    nseq = seq // tm
    in_w = w.shape[1]
    row = lambda i: (i, 0)
    fixed = lambda i: (0, 0)
    pos = lambda i: (i % nseq, 0)
    return pl.pallas_call(
        _inproj_even_kernel,
        grid=(t // tm,),
        in_specs=[
            pl.BlockSpec((tm, D_MODEL), row),
            pl.BlockSpec((1, D_MODEL), fixed),
            pl.BlockSpec((D_MODEL, in_w), fixed),
            pl.BlockSpec((tm, ATTN_WIDTH), pos),
            pl.BlockSpec((tm, ATTN_WIDTH), pos),
        ],
        out_specs=[
            pl.BlockSpec((tm, 2 * MIX_WIDTH), row),
            pl.BlockSpec((tm, ATTN_WIDTH), row),
            pl.BlockSpec((tm, KV_WIDTH), row),
            pl.BlockSpec((tm, KV_WIDTH), row),
        ],
        out_shape=[
            jax.ShapeDtypeStruct((t, 2 * MIX_WIDTH), F32),
            jax.ShapeDtypeStruct((t, ATTN_WIDTH), BF16),
            jax.ShapeDtypeStruct((t, KV_WIDTH), BF16),
            jax.ShapeDtypeStruct((t, KV_WIDTH), BF16),
        ],
        compiler_params=_cparams(("parallel",)),
        name="inproj_even",
    )(x, g, w, cos, sin)


def _head_sumsq(x):
    n = x.shape[-1]
    r = lax.broadcasted_iota(jnp.int32, (n, n), 0) // HEAD_DIM
    c = lax.broadcasted_iota(jnp.int32, (n, n), 1) // HEAD_DIM
    ones = jnp.where(r == c, 1.0, 0.0).astype(BF16)
    sq = x * x
    hi = sq.astype(BF16)
    lo = (sq - hi.astype(F32)).astype(BF16)
    return _dot(hi, ones) + _dot(lo, ones)


def _inproj_odd_kernel(x_ref, g_ref, w_ref, qg_ref, kg_ref, cos_ref, sin_ref, q_ref, k_ref, vt_ref, d_ref):
    h = _rms(x_ref[...], g_ref[...]).astype(BF16)
    c1 = ATTN_WIDTH
    c2 = c1 + KV_WIDTH
    c3 = c2 + KV_WIDTH
    cos = cos_ref[...]
    sin = sin_ref[...]
    half = HEAD_DIM // 2
    q = _dot(h, w_ref[:, :c1])
    q = q * lax.rsqrt(_head_sumsq(q) * (1.0 / HEAD_DIM) + EPS) * qg_ref[...]
    q = _rope(q, cos, sin, half, half // 2)
    q_ref[...] = (q * (HEAD_DIM ** -0.5)).astype(BF16)
    k = _dot(h, w_ref[:, c1:c2])
    k = k * lax.rsqrt(_head_sumsq(k) * (1.0 / HEAD_DIM) + EPS) * kg_ref[...]
    k = _rope(k, cos[:, :KV_WIDTH], sin[:, :KV_WIDTH], half, half // 2).astype(BF16)
    v_t = _dot(h, w_ref[:, c2:c3]).T.astype(BF16)
    for hh in range(N_KV_HEADS):
        k_ref[hh] = k[:, hh * HEAD_DIM:(hh + 1) * HEAD_DIM]
        vt_ref[hh] = v_t[hh * HEAD_DIM:(hh + 1) * HEAD_DIM, :]
    d_ref[...] = _dot(h, w_ref[:, c3:])


def _inproj_odd(x, g, w, qg, kg, cos, sin, seq):
    t = x.shape[0]
    tm = ROW_TILE
    nseq = seq // tm
    in_w = w.shape[1]
    row = lambda i: (i, 0)
    fixed = lambda i: (0, 0)
    pos = lambda i: (i % nseq, 0)
    kv = lambda i: (0, i, 0)
    return pl.pallas_call(
        _inproj_odd_kernel,
        grid=(t // tm,),
        in_specs=[
            pl.BlockSpec((tm, D_MODEL), row),
            pl.BlockSpec((1, D_MODEL), fixed),
            pl.BlockSpec((D_MODEL, in_w), fixed),
            pl.BlockSpec((1, ATTN_WIDTH), fixed),
            pl.BlockSpec((1, KV_WIDTH), fixed),
            pl.BlockSpec((tm, ATTN_WIDTH), pos),
            pl.BlockSpec((tm, ATTN_WIDTH), pos),
        ],
        out_specs=[
            pl.BlockSpec((tm, ATTN_WIDTH), row),
            pl.BlockSpec((N_KV_HEADS, tm, HEAD_DIM), kv),
            pl.BlockSpec((N_KV_HEADS, HEAD_DIM, tm), lambda i: (0, 0, i)),
            pl.BlockSpec((tm, 2 * MIX_WIDTH), row),
        ],
        out_shape=[
            jax.ShapeDtypeStruct((t, ATTN_WIDTH), BF16),
            jax.ShapeDtypeStruct((N_KV_HEADS, t, HEAD_DIM), BF16),
            jax.ShapeDtypeStruct((N_KV_HEADS, HEAD_DIM, t), BF16),
            jax.ShapeDtypeStruct((t, 2 * MIX_WIDTH), F32),
        ],
        compiler_params=_cparams(("parallel",)),
        name="inproj_odd",
    )(x, g, w, qg, kg, cos, sin)


def _glu(x):
    return x[:, :MIX_WIDTH] * jax.nn.sigmoid(x[:, MIX_WIDTH:])


def _conv_kernel(prev_ref, cur_ref, next_ref, w_ref, b_ref, g_ref, beta_ref, o_ref, buf_ref):
    i = pl.program_id(1)
    ns = pl.num_programs(1)
    ts = cur_ref.shape[0]
    halo = CONV_HALO
    buf_ref[0, 0:halo, :] = jnp.where(i > 0, _glu(prev_ref[...]), 0.0)
    buf_ref[0, halo:halo + ts, :] = _glu(cur_ref[...])
    buf_ref[0, halo + ts:, :] = jnp.where(i < ns - 1, _glu(next_ref[...]), 0.0)
    span = ts + 2 * halo - SUBLANES
    for s in range(1, SUBLANES):
        buf_ref[s, 0:span, :] = buf_ref[0, s:s + span, :]
    w = w_ref[...]
    bias = b_ref[...]
    g = g_ref[...]
    beta = beta_ref[...]
    first = halo - CONV_WIDTH // 2
    for c in range(ts // CONV_ROWS):
        r0 = c * CONV_ROWS
        acc = jnp.zeros((CONV_ROWS, MIX_WIDTH), F32)
        for k in range(CONV_WIDTH):
            m, s = divmod(first + k, SUBLANES)
            lo = r0 + SUBLANES * m
            acc = acc + w[k:k + 1, :] * buf_ref[s, lo:lo + CONV_ROWS, :]
        y = _layernorm(acc + bias, g, beta)
        o_ref[r0:r0 + CONV_ROWS, :] = (y * jax.nn.sigmoid(y)).astype(o_ref.dtype)


def _conv_module(a_in, w, b, g, beta, batch, seq):
    t = a_in.shape[0]
    ts = ROW_TILE
    ns = seq // ts
    hb = ts // CONV_HALO
    last = t // CONV_HALO - 1
    fixed = lambda bb, i: (0, 0)
    return pl.pallas_call(
        _conv_kernel,
        grid=(batch, ns),
        in_specs=[
            pl.BlockSpec((CONV_HALO, 2 * MIX_WIDTH), lambda bb, i: (jnp.maximum((bb * ns + i) * hb - 1, 0), 0)),
            pl.BlockSpec((ts, 2 * MIX_WIDTH), lambda bb, i: (bb * ns + i, 0)),
            pl.BlockSpec((CONV_HALO, 2 * MIX_WIDTH), lambda bb, i: (jnp.minimum((bb * ns + i + 1) * hb, last), 0)),
            pl.BlockSpec((CONV_WIDTH, MIX_WIDTH), fixed),
            pl.BlockSpec((1, MIX_WIDTH), fixed),
            pl.BlockSpec((1, MIX_WIDTH), fixed),
            pl.BlockSpec((1, MIX_WIDTH), fixed),
        ],
        out_specs=pl.BlockSpec((ts, MIX_WIDTH), lambda bb, i: (bb * ns + i, 0)),
        out_shape=jax.ShapeDtypeStruct((t, MIX_WIDTH), BF16),
        scratch_shapes=[pltpu.VMEM((SUBLANES, ts + 2 * CONV_HALO, MIX_WIDTH), F32)],
        compiler_params=_cparams(("parallel", "parallel")),
        name="conv_module",
    )(a_in, a_in, a_in, w, b, g, beta)


def _swa_kernel(q_ref, kp_ref, kc_ref, kn_ref, vp_ref, vc_ref, vn_ref, sink_ref, o_ref):
    n = pl.program_id(1)
    nsteps = pl.num_programs(1)
    bq = BLOCK_Q
    cols = GQA_GROUPS * bq
    jj = lax.broadcasted_iota(jnp.int32, (3 * bq, cols), 0)
    ii = lax.broadcasted_iota(jnp.int32, (3 * bq, cols), 1) % bq
    band = (jj >= ii) & (jj <= ii + 2 * WINDOW)
    kcat = jnp.concatenate([kp_ref[...], kc_ref[...], kn_ref[...]], axis=0)
    vcat_t = jnp.concatenate([vp_ref[...], vc_ref[...], vn_ref[...]], axis=0).astype(F32).T.astype(BF16)
    for qb in range(SWA_BLOCKS):
        mask = band
        if qb == 0:
            mask = mask & (jj >= jnp.where(n > 0, 0, bq))
        if qb == SWA_BLOCKS - 1:
            mask = mask & (jj < jnp.where(n < nsteps - 1, 3 * bq, 2 * bq))
        q_t = q_ref[qb * bq:(qb + 1) * bq, :].astype(F32).T.astype(BF16)
        outs = []
        for hk in range(N_KV_HEADS):
            heads = range(hk * GQA_GROUPS, (hk + 1) * GQA_GROUPS)
            kh = kcat[qb * bq:(qb + 3) * bq, hk * HEAD_DIM:(hk + 1) * HEAD_DIM]
            vh_t = jnp.concatenate([vcat_t[hk * HEAD_DIM:(hk + 1) * HEAD_DIM, qb * bq:(qb + 3) * bq],
                                    jnp.ones((BF16_ROWS, 3 * bq), BF16)], axis=0)
            qs_t = jnp.concatenate([q_t[hd * HEAD_DIM:(hd + 1) * HEAD_DIM, :] for hd in heads], axis=1)
            sink = jnp.concatenate([jnp.broadcast_to(sink_ref[hd:hd + 1, 0:1], (1, bq)) for hd in heads], axis=1)
            s = jnp.where(mask, _dot(kh, qs_t), NEG)
            m = jnp.maximum(jnp.max(s, axis=0, keepdims=True), sink)
            o_ext = _dot(vh_t, jnp.exp(s - m).astype(BF16))
            denom = o_ext[HEAD_DIM:HEAD_DIM + 1, :] + jnp.exp(sink - m)
            o_t = o_ext[:HEAD_DIM, :] / denom
            outs.extend(o_t[:, g * bq:(g + 1) * bq] for g in range(GQA_GROUPS))
        o_ref[qb * bq:(qb + 1) * bq, :] = jnp.concatenate(outs, axis=0).T.astype(o_ref.dtype)


def _swa(q, k, v, sink_b, batch, seq):
    t = q.shape[0]
    bq = BLOCK_Q
    tq = SWA_BLOCKS * bq
    ns = seq // tq
    last = t // bq - 1
    cur = lambda bb, n: (bb * ns + n, 0)
    prev = lambda bb, n: (jnp.maximum((bb * ns + n) * SWA_BLOCKS - 1, 0), 0)
    nxt = lambda bb, n: (jnp.minimum((bb * ns + n + 1) * SWA_BLOCKS, last), 0)
    halo = lambda im: pl.BlockSpec((bq, KV_WIDTH), im)
    body = pl.BlockSpec((tq, KV_WIDTH), cur)
    return pl.pallas_call(
        _swa_kernel,
        grid=(batch, ns),
        in_specs=[pl.BlockSpec((tq, ATTN_WIDTH), cur), halo(prev), body, halo(nxt), halo(prev), body, halo(nxt),
                  pl.BlockSpec((N_Q_HEADS, LANE), lambda bb, n: (0, 0))],
        out_specs=pl.BlockSpec((tq, ATTN_WIDTH), cur),
        out_shape=jax.ShapeDtypeStruct((t, ATTN_WIDTH), BF16),
        compiler_params=_cparams(("parallel", "parallel")),
        name="window_attn",
    )(q, k, k, k, v, v, v, sink_b)


def _dense_attn_kernel(q_ref, k_ref, vt_ref, o_ref):
    ts = q_ref.shape[0] // ATT_CHAINS
    k = k_ref[...]
    v_ext = jnp.concatenate([vt_ref[...], jnp.ones((BF16_ROWS, vt_ref.shape[1]), BF16)], axis=0)
    q_t = q_ref[...].astype(F32).T.astype(BF16)

    def scores(c):
        cols = slice(c * ts, (c + 1) * ts)
        qs_t = jnp.concatenate([q_t[g * HEAD_DIM:(g + 1) * HEAD_DIM, cols] for g in range(GQA_GROUPS)], axis=1)
        return _dot(k, qs_t)

    s_next = scores(0)
    for c in range(ATT_CHAINS):
        cols = slice(c * ts, (c + 1) * ts)
        s = s_next
        if c + 1 < ATT_CHAINS:
            s_next = scores(c + 1)
        p = jnp.exp(s - jnp.max(s, axis=0, keepdims=True))
        o_ext = _dot(v_ext, p.astype(BF16))
        o_t = o_ext[:HEAD_DIM, :] / o_ext[HEAD_DIM:HEAD_DIM + 1, :]
        o_ref[cols, :] = jnp.concatenate(
            [o_t[:, g * ts:(g + 1) * ts] for g in range(GQA_GROUPS)], axis=0).T.astype(o_ref.dtype)


def _dense_attn(q, k, v, batch, seq):
    t = q.shape[0]
    tq = min(ATT_TQ, seq)
    nq = seq // tq
    gw = GQA_GROUPS * HEAD_DIM
    qmap = lambda bb, hk, qi: (bb * nq + qi, hk)
    return pl.pallas_call(
        _dense_attn_kernel,
        grid=(batch, N_KV_HEADS, nq),
        in_specs=[
            pl.BlockSpec((tq, gw), qmap),
            pl.BlockSpec((None, seq, HEAD_DIM), lambda bb, hk, qi: (hk, bb, 0)),
            pl.BlockSpec((None, HEAD_DIM, seq), lambda bb, hk, qi: (hk, 0, bb)),
        ],
        out_specs=pl.BlockSpec((tq, gw), qmap),
        out_shape=jax.ShapeDtypeStruct((t, ATTN_WIDTH), BF16),
        compiler_params=_cparams(("parallel", "parallel", "parallel")),
        name="dense_attn",
    )(q, k, v)


def _sgu_kernel(d_ref, g_ref, b_ref, w_ref, bs_ref, o_ref):
    z = jax.nn.gelu(d_ref[...])
    u = z[:, :MIX_WIDTH]
    v = _layernorm(z[:, MIX_WIDTH:], g_ref[...], b_ref[...]).astype(BF16)
    gc = MIX_WIDTH // SGU_GROUPS
    for n in range(d_ref.shape[0] // CHUNK):
        rows = slice(n * CHUNK, (n + 1) * CHUNK)
        for g in range(SGU_GROUPS):
            cols = slice(g * gc, (g + 1) * gc)
            mixed = _dot(w_ref[g], v[rows, cols]) + bs_ref[g]
            o_ref[rows, cols] = (u[rows, cols] * mixed).astype(o_ref.dtype)


def _sgu(d_in, g, b, w, bs):
    t = d_in.shape[0]
    ts = ROW_TILE
    gc = MIX_WIDTH // SGU_GROUPS
    row = lambda i: (i, 0)
    fixed = lambda i: (0, 0)
    fixed3 = lambda i: (0, 0, 0)
    return pl.pallas_call(
        _sgu_kernel,
        grid=(t // ts,),
        in_specs=[
            pl.BlockSpec((ts, 2 * MIX_WIDTH), row),
            pl.BlockSpec((1, MIX_WIDTH), fixed),
            pl.BlockSpec((1, MIX_WIDTH), fixed),
            pl.BlockSpec((SGU_GROUPS, CHUNK, CHUNK), fixed3),
            pl.BlockSpec((SGU_GROUPS, CHUNK, gc), fixed3),
        ],
        out_specs=pl.BlockSpec((ts, MIX_WIDTH), row),
        out_shape=jax.ShapeDtypeStruct((t, MIX_WIDTH), BF16),
        compiler_params=_cparams(("parallel",)),
        name="spatial_gating",
    )(d_in, g, b, w, bs)


def _outproj_kernel(x_ref, a_ref, b_ref, w_ref, g_ref, xn_ref, hnt_ref):
    y = _dot(a_ref[...], w_ref[:MIX_WIDTH, :]) + _dot(b_ref[...], w_ref[MIX_WIDTH:, :])
    xn = x_ref[...] + y
    xn_ref[...] = xn
    hnt_ref[...] = _rms(xn, g_ref[...]).T.astype(BF16)


def _outproj(x, a, b, w, g):
    t = x.shape[0]
    tm = ROW_TILE
    row = lambda i: (i, 0)
    fixed = lambda i: (0, 0)
    return pl.pallas_call(
        _outproj_kernel,
        grid=(t // tm,),
        in_specs=[
            pl.BlockSpec((tm, D_MODEL), row),
            pl.BlockSpec((tm, MIX_WIDTH), row),
            pl.BlockSpec((tm, MIX_WIDTH), row),
            pl.BlockSpec((2 * MIX_WIDTH, D_MODEL), fixed),
            pl.BlockSpec((1, D_MODEL), fixed),
        ],
        out_specs=[pl.BlockSpec((tm, D_MODEL), row), pl.BlockSpec((D_MODEL, tm), lambda i: (0, i))],
        out_shape=[jax.ShapeDtypeStruct((t, D_MODEL), F32), jax.ShapeDtypeStruct((D_MODEL, t), BF16)],
        compiler_params=_cparams(("parallel",)),
        name="outproj",
    )(x, a, b, w, g)


_PEER_PAIRS = [(i, j) for i in range(PEER_TOPK) for j in range(PEER_TOPK) if (i + 1) * (j + 1) <= PEER_TOPK]
_PEER_CAND_ROWS = -(-len(_PEER_PAIRS) // 8) * 8


SUBLANES = 8


def _bitonic_merge_desc(v, lo, n):
    step = n // 2
    while step >= 1:
        for i in range(lo, lo + n):
            if (i - lo) & step == 0:
                hi_, lo_ = jnp.maximum(v[i], v[i + step]), jnp.minimum(v[i], v[i + step])
                v[i], v[i + step] = hi_, lo_
        step //= 2


def _oddeven_merge(lo, hi, r):
    step = r * 2
    if step < hi - lo:
        yield from _oddeven_merge(lo, hi, step)
        yield from _oddeven_merge(lo + r, hi, step)
        yield from [(i, i + r) for i in range(lo + r, hi - r, step)]
    else:
        yield (lo, lo + r)


def _oddeven_merge_sort(lo, hi):
    if hi - lo >= 1:
        mid = lo + (hi - lo) // 2
        yield from _oddeven_merge_sort(lo, mid)
        yield from _oddeven_merge_sort(mid + 1, hi)
        yield from _oddeven_merge(lo, hi, 1)


def _sort16_desc(v):
    for i, j in _oddeven_merge_sort(0, len(v) - 1):
        v[i], v[j] = jnp.maximum(v[i], v[j]), jnp.minimum(v[i], v[j])
    return v


def _rank_among(b, v):
    lt = lambda t: t > b
    c1 = lt(v[7])
    c2 = lt(jnp.where(c1, v[11], v[3]))
    c3 = lt(jnp.where(c1, jnp.where(c2, v[13], v[9]), jnp.where(c2, v[5], v[1])))
    hi4 = jnp.where(c2, jnp.where(c3, v[14], v[12]), jnp.where(c3, v[10], v[8]))
    lo4 = jnp.where(c2, jnp.where(c3, v[6], v[4]), jnp.where(c3, v[2], v[0]))
    c4 = lt(jnp.where(c1, hi4, lo4))
    one = lambda c, x: jnp.where(c, x, 0.0)
    r = one(c1, 8.0) + one(c2, 4.0) + one(c3, 2.0) + one(c4, 1.0)
    return jnp.where(lt(v[15]), float(PEER_TOPK), r)


def _top16_sorted(s):
    v = _sort16_desc([s[k * SUBLANES:(k + 1) * SUBLANES, :] for k in range(PEER_NKEYS // SUBLANES)])
    for shift in (4, 2, 1):
        v = [jnp.maximum(v[k], pltpu.roll(v[PEER_TOPK - 1 - k], shift, 0)) for k in range(PEER_TOPK)]
        _bitonic_merge_desc(v, 0, PEER_TOPK)
    return v


def _peer_select(h, s1_all, s2_all, n_ref, e1_ref, r2_ref, e2_ref):
    tt = s1_all.shape[-1]
    nv = PEER_NKEYS // SUBLANES
    sub = lax.broadcasted_iota(jnp.int32, (SUBLANES, LANE), 0)
    for tc in range(tt // LANE):
        lanes = slice(tc * LANE, (tc + 1) * LANE)
        s1 = s1_all[:, lanes]
        s2 = s2_all[:, lanes]
        v1 = _top16_sorted(s1)
        v2 = _top16_sorted(s2)
        pair_sum = {p: v1[p[0]] + v2[p[1]] for p in _PEER_PAIRS}
        cand = []
        for g in range(_PEER_CAND_ROWS // SUBLANES):
            c = jnp.full((SUBLANES, LANE), -jnp.inf, F32)
            for r, p in enumerate(_PEER_PAIRS[g * SUBLANES:(g + 1) * SUBLANES]):
                c = jnp.where(sub == r, pair_sum[p], c)
            cand.append(c)
        cand.append(jnp.full((SUBLANES, LANE), -jnp.inf, F32))
        f = _sort16_desc(cand)
        f = f + [pltpu.roll(c, 4, 0) for c in reversed(f)]
        _bitonic_merge_desc(f, 0, PEER_TOPK)
        for shift in (2, 1):
            f = [jnp.maximum(f[k], pltpu.roll(f[PEER_TOPK - 1 - k], shift, 0)) for k in range(PEER_TOPK)]
            _bitonic_merge_desc(f, 0, PEER_TOPK)
        thr = f[PEER_TOPK - 1]
        z = jnp.ones((SUBLANES, LANE), F32)
        for r in range(1, PEER_TOPK):
            z = z + jnp.exp(f[r] - f[0])
        rz = 1.0 / z
        cnt = []
        for i in range(PEER_TOPK):
            c = jnp.zeros((SUBLANES, LANE), F32)
            for j in range(PEER_TOPK // (i + 1)):
                c = c + jnp.where(pair_sum[(i, j)] >= thr, 1.0, 0.0)
            cnt.append(c)
        ranks = []
        e2s = []
        for k in range(nv):
            rows = slice(k * SUBLANES, (k + 1) * SUBLANES)
            a = s1[rows, :]
            b = s2[rows, :]
            n = jnp.zeros((SUBLANES, LANE), F32)
            for r in range(PEER_TOPK):
                n = jnp.where(a == v1[r], cnt[r], n)
            n_ref[h, rows, lanes] = n
            e1_ref[h, rows, lanes] = jnp.exp(a - v1[0]) * rz
            ranks.append(_rank_among(b, v2))
            e2s.append(jnp.exp(b - v2[0]))
        r2_ref[h, :, lanes] = jnp.concatenate(ranks, axis=0).astype(r2_ref.dtype)
        e2_ref[h, :, lanes] = jnp.concatenate(e2s, axis=0).astype(e2_ref.dtype)


def _peer_gates(step, sub, n_ref, e1_ref, r2_ref, e2_ref, g_ref):
    tt = g_ref.shape[-1]
    gdt = g_ref.dtype
    first = sub * PEER_SUB
    a8 = pl.multiple_of(step * PEER_A + first // SUBLANES * SUBLANES, SUBLANES)
    nload = -(-(first % SUBLANES + PEER_SUB) // SUBLANES) * SUBLANES
    for tc in range(tt // LANE):
        lanes = slice(tc * LANE, (tc + 1) * LANE)
        nrows = [n_ref[h, pl.ds(a8, nload), lanes].astype(gdt) for h in range(PEER_HEADS)]
        erows = [e1_ref[h, pl.ds(a8, nload), lanes].astype(gdt) for h in range(PEER_HEADS)]
        for blk in range(PEER_SUB // PEER_GATE_BLOCK):
            gates = [jnp.zeros((PEER_NKEYS, LANE), gdt) for _ in range(PEER_GATE_BLOCK)]
            for h in range(PEER_HEADS):
                r2 = r2_ref[h, :, lanes]
                e2 = e2_ref[h, :, lanes]
                for k in range(PEER_GATE_BLOCK):
                    al = first % SUBLANES + blk * PEER_GATE_BLOCK + k
                    keep = r2 < nrows[h][al:al + 1, :]
                    gates[k] = gates[k] + jnp.where(keep, e2, 0.0) * erows[h][al:al + 1, :]
            for k in range(PEER_GATE_BLOCK):
                al = blk * PEER_GATE_BLOCK + k
                g_ref[sub, al * PEER_NKEYS:(al + 1) * PEER_NKEYS, lanes] = gates[k]


def _peer_kernel(x_ref, hnt_ref, wqt_ref, sk_ref, *refs):
    nsub = PEER_A // PEER_SUB
    u_refs, vt_refs = refs[:nsub], refs[nsub:2 * nsub]
    o_ref, qt_ref, n_ref, e1_ref, r2_ref, e2_ref, g_ref, acc_ref, ht_ref = refs[2 * nsub:]
    j = pl.program_id(1)
    nj = pl.num_programs(1)
    half = PEER_DKEY // 2
    gdt = r2_ref.dtype
    gate_refs = (n_ref, e1_ref, r2_ref, e2_ref, g_ref)

    @pl.when(j == 0)
    def _():
        qt_ref[...] = _dot(wqt_ref[...], hnt_ref[...]).astype(BF16)

        def body(h, carry):
            r0 = pl.multiple_of(h * PEER_DKEY, PEER_DKEY)
            s1 = _dot(sk_ref[2 * h], qt_ref[pl.ds(r0, half), :])
            s2 = _dot(sk_ref[2 * h + 1], qt_ref[pl.ds(r0 + half, half), :])
            _peer_select(h, s1, s2, n_ref, e1_ref, r2_ref, e2_ref)
            return carry

        lax.fori_loop(0, PEER_HEADS, body, 0)
        acc_ref[...] = jnp.zeros(acc_ref.shape, F32)

    sub_rows = lambda sub: slice(sub * PEER_SUB * PEER_NKEYS, (sub + 1) * PEER_SUB * PEER_NKEYS)
    for sub in range(nsub):
        _peer_gates(j, sub, *gate_refs)
    for sub in range(nsub):
        at = _dot(u_refs[sub][...], hnt_ref[...]).astype(gdt)
        ht_ref[sub_rows(sub), :] = jax.nn.gelu(at) * g_ref[sub]
    out = _dot(vt_refs[0][...], ht_ref[sub_rows(0), :])
    for sub in range(1, nsub):
        out = out + _dot(vt_refs[sub][...], ht_ref[sub_rows(sub), :])
    acc_ref[...] += out

    @pl.when(j == nj - 1)
    def _():
        o_ref[...] = x_ref[...] + acc_ref[...].T


def _peer(x, hnt, wqt, sk, u, vt, layer):
    t = x.shape[0]
    tt = PEER_TT
    ne = u.shape[1]
    et = PEER_A * PEER_NKEYS
    nsub = PEER_A // PEER_SUB
    es = PEER_SUB * PEER_NKEYS
    hk = PEER_HEADS * PEER_DKEY
    sel = lambda dt: pltpu.VMEM((PEER_HEADS, PEER_NKEYS, tt), dt)
    u_spec = lambda sub: pl.BlockSpec((None, es, D_MODEL), lambda i, j: (layer, j * nsub + sub, 0))
    vt_spec = lambda sub: pl.BlockSpec((None, D_MODEL, es), lambda i, j: (layer, 0, j * nsub + sub))
    return pl.pallas_call(
        _peer_kernel,
        grid=(t // tt, ne // et),
        in_specs=[
            pl.BlockSpec((tt, D_MODEL), lambda i, j: (i, 0)),
            pl.BlockSpec((D_MODEL, tt), lambda i, j: (0, i)),
            pl.BlockSpec((hk, D_MODEL), lambda i, j: (0, 0), pipeline_mode=pl.Buffered(1)),
            pl.BlockSpec((2 * PEER_HEADS, PEER_NKEYS, PEER_DKEY // 2), lambda i, j: (0, 0, 0)),
            *[u_spec(sub) for sub in range(nsub)],
            *[vt_spec(sub) for sub in range(nsub)],
        ],
        out_specs=pl.BlockSpec((tt, D_MODEL), lambda i, j: (i, 0)),
        out_shape=jax.ShapeDtypeStruct((t, D_MODEL), F32),
        scratch_shapes=[
            pltpu.VMEM((hk, tt), BF16),
            sel(F32), sel(F32), sel(PEER_GATE_DTYPE), sel(PEER_GATE_DTYPE),
            pltpu.VMEM((PEER_A // PEER_SUB, PEER_SUB * PEER_NKEYS, tt), PEER_GATE_DTYPE),
            pltpu.VMEM((D_MODEL, tt), F32),
            pltpu.VMEM((et, tt), BF16),
        ],
        compiler_params=_cparams(("parallel", "arbitrary")),
        name="peer",
    )(x, hnt, wqt, sk, *([u] * nsub), *([vt] * nsub))


def _final_norm_kernel(x_ref, g_ref, o_ref):
    o_ref[...] = _rms(x_ref[...], g_ref[...])


def _final_norm(x, g):
    t = x.shape[0]
    tm = ROW_TILE
    return pl.pallas_call(
        _final_norm_kernel,
        grid=(t // tm,),
        in_specs=[pl.BlockSpec((tm, D_MODEL), lambda i: (i, 0)), pl.BlockSpec((1, D_MODEL), lambda i: (0, 0))],
        out_specs=pl.BlockSpec((tm, D_MODEL), lambda i: (i, 0)),
        out_shape=jax.ShapeDtypeStruct((t, D_MODEL), F32),
        compiler_params=_cparams(("parallel",)),
        name="final_norm",
    )(x, g)


def _rope_table(pos, dim):
    inv = ROPE_THETA ** (-jnp.arange(0, dim, 2, dtype=F32) / dim)
    ang = pos.astype(F32)[:, None] * inv[None, :]
    ang = jnp.concatenate([ang, ang], axis=-1)
    sign = jnp.concatenate([-jnp.ones((dim // 2,), F32), jnp.ones((dim // 2,), F32)])
    return jnp.cos(ang), jnp.sin(ang) * sign


def _rope_tables(seq):
    pos = jnp.arange(seq)
    cos1, sin1 = _rope_table(pos, HEAD_DIM)
    cr, sr = _rope_table(pos // GRID_W, HEAD_DIM // 2)
    cc, sc = _rope_table(pos % GRID_W, HEAD_DIM // 2)
    cos2 = jnp.concatenate([cr, cc], axis=-1)
    sin2 = jnp.concatenate([sr, sc], axis=-1)
    tile = lambda a: jnp.tile(a, (1, N_Q_HEADS))
    return tile(cos1), tile(sin1), tile(cos2), tile(sin2)


def kernel(x, mix_norm_g, ffn_norm_g, final_norm_g, even_w_in, even_w_out, conv_w, conv_b, conv_ln_g, conv_ln_b, sink_logits, odd_w_in, odd_w_out, q_norm_g, k_norm_g, sgu_ln_g, sgu_ln_b, sgu_w, sgu_b, peer_wq, peer_subkeys, peer_u, peer_v):
    batch, seq, d = x.shape
    depth = mix_norm_g.shape[0]
    t = batch * seq
    cos1, sin1, cos2, sin2 = _rope_tables(seq)
    row = lambda a: a.reshape(1, -1).astype(F32)
    xf = x.reshape(t, d)
    u_all = peer_u.astype(BF16)
    vt_all = jnp.swapaxes(peer_v, 1, 2).astype(BF16)
    for layer in range(depth):
        i = layer // 2
        if layer % 2 == 0:
            a_in, q, k, v = _inproj_even(xf, row(mix_norm_g[layer]), even_w_in[i].astype(BF16), cos1, sin1, seq)
            m1 = _conv_module(a_in, conv_w[i, :, 0, :], row(conv_b[i]), row(conv_ln_g[i]), row(conv_ln_b[i]), batch, seq)
            sink_b = jnp.broadcast_to(sink_logits[i].astype(F32)[:, None], (N_Q_HEADS, LANE))
            m2 = _swa(q, k, v, sink_b, batch, seq)
            w_out = even_w_out[i]
        else:
            q, k, v, d_in = _inproj_odd(xf, row(mix_norm_g[layer]), odd_w_in[i].astype(BF16),
                                        row(jnp.tile(q_norm_g[i], N_Q_HEADS)), row(jnp.tile(k_norm_g[i], N_KV_HEADS)),
                                        cos2, sin2, seq)
            m1 = _dense_attn(q, k, v, batch, seq)
            bs = jnp.broadcast_to(sgu_b[i].astype(F32)[:, :, None], (SGU_GROUPS, CHUNK, MIX_WIDTH // SGU_GROUPS))
            m2 = _sgu(d_in, row(sgu_ln_g[i]), row(sgu_ln_b[i]), sgu_w[i].astype(BF16), bs)
            w_out = odd_w_out[i]
        xn, hnt = _outproj(xf, m1, m2, w_out.astype(BF16), row(ffn_norm_g[layer]))
        sk = peer_subkeys[layer].reshape(2 * PEER_HEADS, PEER_NKEYS, PEER_DKEY // 2).astype(BF16)
        xf = _peer(xn, hnt, peer_wq[layer].astype(BF16).T, sk, u_all, vt_all, layer)
    return _final_norm(xf, row(final_norm_g)).reshape(batch, seq, d)
```

```python
import jax
import jax.numpy as jnp
from jax import lax
from jax.experimental import pallas as pl
from jax.experimental.pallas import tpu as pltpu

F32 = jnp.float32
BF16 = jnp.bfloat16

D_MODEL = 1024
HEAD_DIM = 64
N_Q_HEADS = 8
N_KV_HEADS = 2
GQA_GROUPS = N_Q_HEADS // N_KV_HEADS
ATTN_WIDTH = N_Q_HEADS * HEAD_DIM
KV_WIDTH = N_KV_HEADS * HEAD_DIM
MIX_WIDTH = D_MODEL // 2
CONV_WIDTH = 31
CONV_HALO = 16
SGU_GROUPS = 4
CHUNK = 128
WINDOW = 128
BLOCK_Q = 128
GRID_W = 64
ROPE_THETA = 10000.0
PEER_HEADS = 8
PEER_NKEYS = 128
PEER_DKEY = 128
PEER_TOPK = 16
EPS = 1e-6
NEG = -1e30

VMEM_LIMIT_BYTES = 56 * 1024 * 1024

ROW_TILE = 1024
CONV_ROWS = 64
SWA_BLOCKS = 4
ATT_TQ = 1024
ATT_CHAINS = 8
PEER_TT = 512
PEER_A = 16
PEER_SUB = 8
PEER_GATE_BLOCK = 4
PEER_GATE_DTYPE = BF16
LANE = 128
BF16_ROWS = 16


def _cparams(sem):
    return pltpu.CompilerParams(dimension_semantics=sem, vmem_limit_bytes=VMEM_LIMIT_BYTES)


def _dot(a, b):
    return jnp.dot(a, b, preferred_element_type=F32)


def _rms(x, g):
    return x * lax.rsqrt(jnp.mean(x * x, axis=-1, keepdims=True) + EPS) * g


def _layernorm(x, g, b):
    mu = jnp.mean(x, axis=-1, keepdims=True)
    xc = x - mu
    var = jnp.mean(xc * xc, axis=-1, keepdims=True)
    return xc * lax.rsqrt(var + EPS) * g + b


def _rope(x, cos, sin_signed, group, half):
    n = x.shape[-1]
    lane = lax.broadcasted_iota(jnp.int32, x.shape, x.ndim - 1)
    first = (lane % group) < half
    partner = jnp.where(first, pltpu.roll(x, n - half, x.ndim - 1), pltpu.roll(x, half, x.ndim - 1))
    return x * cos + partner * sin_signed


def _inproj_even_kernel(x_ref, g_ref, w_ref, cos_ref, sin_ref, a_ref, q_ref, k_ref, v_ref):
    h = _rms(x_ref[...], g_ref[...]).astype(BF16)
    o1 = 2 * MIX_WIDTH
    o2 = o1 + ATTN_WIDTH
    o3 = o2 + KV_WIDTH
    a_ref[...] = _dot(h, w_ref[:, :o1])
    cos = cos_ref[...]
    sin = sin_ref[...]
    q = _rope(_dot(h, w_ref[:, o1:o2]), cos, sin, HEAD_DIM, HEAD_DIM // 2)
    q_ref[...] = (q * (HEAD_DIM ** -0.5)).astype(BF16)
    k = _rope(_dot(h, w_ref[:, o2:o3]), cos[:, :KV_WIDTH], sin[:, :KV_WIDTH], HEAD_DIM, HEAD_DIM // 2)
    k_ref[...] = k.astype(BF16)
    v_ref[...] = _dot(h, w_ref[:, o3:]).astype(BF16)


def _inproj_even(x, g, w, cos, sin, seq):
    t = x.shape[0]
    tm = ROW_TILE
    nseq = seq // tm
    in_w = w.shape[1]
    row = lambda i: (i, 0)
    fixed = lambda i: (0, 0)
    pos = lambda i: (i % nseq, 0)
    return pl.pallas_call(
        _inproj_even_kernel,
        grid=(t // tm,),
        in_specs=[
            pl.BlockSpec((tm, D_MODEL), row),
            pl.BlockSpec((1, D_MODEL), fixed),
            pl.BlockSpec((D_MODEL, in_w), fixed),
            pl.BlockSpec((tm, ATTN_WIDTH), pos),
            pl.BlockSpec((tm, ATTN_WIDTH), pos),
        ],
        out_specs=[
            pl.BlockSpec((tm, 2 * MIX_WIDTH), row),
            pl.BlockSpec((tm, ATTN_WIDTH), row),
            pl.BlockSpec((tm, KV_WIDTH), row),
            pl.BlockSpec((tm, KV_WIDTH), row),
        ],
        out_shape=[
            jax.ShapeDtypeStruct((t, 2 * MIX_WIDTH), F32),
            jax.ShapeDtypeStruct((t, ATTN_WIDTH), BF16),
            jax.ShapeDtypeStruct((t, KV_WIDTH), BF16),
            jax.ShapeDtypeStruct((t, KV_WIDTH), BF16),
        ],
        compiler_params=_cparams(("parallel",)),
        name="inproj_even",
    )(x, g, w, cos, sin)


def _head_sumsq(x):
    n = x.shape[-1]
    r = lax.broadcasted_iota(jnp.int32, (n, n), 0) // HEAD_DIM
    c = lax.broadcasted_iota(jnp.int32, (n, n), 1) // HEAD_DIM
    ones = jnp.where(r == c, 1.0, 0.0).astype(BF16)
    sq = x * x
    hi = sq.astype(BF16)
    lo = (sq - hi.astype(F32)).astype(BF16)
    return _dot(hi, ones) + _dot(lo, ones)


def _inproj_odd_kernel(x_ref, g_ref, w_ref, qg_ref, kg_ref, cos_ref, sin_ref, q_ref, k_ref, vt_ref, d_ref):
    h = _rms(x_ref[...], g_ref[...]).astype(BF16)
    c1 = ATTN_WIDTH
    c2 = c1 + KV_WIDTH
    c3 = c2 + KV_WIDTH
    cos = cos_ref[...]
    sin = sin_ref[...]
    half = HEAD_DIM // 2
    q = _dot(h, w_ref[:, :c1])
    q = q * lax.rsqrt(_head_sumsq(q) * (1.0 / HEAD_DIM) + EPS) * qg_ref[...]
    q = _rope(q, cos, sin, half, half // 2)
    q_ref[...] = (q * (HEAD_DIM ** -0.5)).astype(BF16)
    k = _dot(h, w_ref[:, c1:c2])
    k = k * lax.rsqrt(_head_sumsq(k) * (1.0 / HEAD_DIM) + EPS) * kg_ref[...]
    k = _rope(k, cos[:, :KV_WIDTH], sin[:, :KV_WIDTH], half, half // 2).astype(BF16)
    v_t = _dot(h, w_ref[:, c2:c3]).T.astype(BF16)
    for hh in range(N_KV_HEADS):
        k_ref[hh] = k[:, hh * HEAD_DIM:(hh + 1) * HEAD_DIM]
        vt_ref[hh] = v_t[hh * HEAD_DIM:(hh + 1) * HEAD_DIM, :]
    d_ref[...] = _dot(h, w_ref[:, c3:])


def _inproj_odd(x, g, w, qg, kg, cos, sin, seq):
    t = x.shape[0]
    tm = ROW_TILE
    nseq = seq // tm
    in_w = w.shape[1]
    row = lambda i: (i, 0)
    fixed = lambda i: (0, 0)
    pos = lambda i: (i % nseq, 0)
    kv = lambda i: (0, i, 0)
    return pl.pallas_call(
        _inproj_odd_kernel,
        grid=(t // tm,),
        in_specs=[
            pl.BlockSpec((tm, D_MODEL), row),
            pl.BlockSpec((1, D_MODEL), fixed),
            pl.BlockSpec((D_MODEL, in_w), fixed),
            pl.BlockSpec((1, ATTN_WIDTH), fixed),
            pl.BlockSpec((1, KV_WIDTH), fixed),
            pl.BlockSpec((tm, ATTN_WIDTH), pos),
            pl.BlockSpec((tm, ATTN_WIDTH), pos),
        ],
        out_specs=[
            pl.BlockSpec((tm, ATTN_WIDTH), row),
            pl.BlockSpec((N_KV_HEADS, tm, HEAD_DIM), kv),
            pl.BlockSpec((N_KV_HEADS, HEAD_DIM, tm), lambda i: (0, 0, i)),
            pl.BlockSpec((tm, 2 * MIX_WIDTH), row),
        ],
        out_shape=[
            jax.ShapeDtypeStruct((t, ATTN_WIDTH), BF16),
            jax.ShapeDtypeStruct((N_KV_HEADS, t, HEAD_DIM), BF16),
            jax.ShapeDtypeStruct((N_KV_HEADS, HEAD_DIM, t), BF16),
            jax.ShapeDtypeStruct((t, 2 * MIX_WIDTH), F32),
        ],
        compiler_params=_cparams(("parallel",)),
        name="inproj_odd",
    )(x, g, w, qg, kg, cos, sin)


def _glu(x):
    return x[:, :MIX_WIDTH] * jax.nn.sigmoid(x[:, MIX_WIDTH:])


def _conv_kernel(prev_ref, cur_ref, next_ref, w_ref, b_ref, g_ref, beta_ref, o_ref, buf_ref):
    i = pl.program_id(1)
    ns = pl.num_programs(1)
    ts = cur_ref.shape[0]
    halo = CONV_HALO
    buf_ref[0, 0:halo, :] = jnp.where(i > 0, _glu(prev_ref[...]), 0.0)
    buf_ref[0, halo:halo + ts, :] = _glu(cur_ref[...])
    buf_ref[0, halo + ts:, :] = jnp.where(i < ns - 1, _glu(next_ref[...]), 0.0)
    span = ts + 2 * halo - SUBLANES
    for s in range(1, SUBLANES):
        buf_ref[s, 0:span, :] = buf_ref[0, s:s + span, :]
    w = w_ref[...]
    bias = b_ref[...]
    g = g_ref[...]
    beta = beta_ref[...]
    first = halo - CONV_WIDTH // 2
    for c in range(ts // CONV_ROWS):
        r0 = c * CONV_ROWS
        acc = jnp.zeros((CONV_ROWS, MIX_WIDTH), F32)
        for k in range(CONV_WIDTH):
            m, s = divmod(first + k, SUBLANES)
            lo = r0 + SUBLANES * m
            acc = acc + w[k:k + 1, :] * buf_ref[s, lo:lo + CONV_ROWS, :]
        y = _layernorm(acc + bias, g, beta)
        o_ref[r0:r0 + CONV_ROWS, :] = (y * jax.nn.sigmoid(y)).astype(o_ref.dtype)


def _conv_module(a_in, w, b, g, beta, batch, seq):
    t = a_in.shape[0]
    ts = ROW_TILE
    ns = seq // ts
    hb = ts // CONV_HALO
    last = t // CONV_HALO - 1
    fixed = lambda bb, i: (0, 0)
    return pl.pallas_call(
        _conv_kernel,
        grid=(batch, ns),
        in_specs=[
            pl.BlockSpec((CONV_HALO, 2 * MIX_WIDTH), lambda bb, i: (jnp.maximum((bb * ns + i) * hb - 1, 0), 0)),
            pl.BlockSpec((ts, 2 * MIX_WIDTH), lambda bb, i: (bb * ns + i, 0)),
            pl.BlockSpec((CONV_HALO, 2 * MIX_WIDTH), lambda bb, i: (jnp.minimum((bb * ns + i + 1) * hb, last), 0)),
            pl.BlockSpec((CONV_WIDTH, MIX_WIDTH), fixed),
            pl.BlockSpec((1, MIX_WIDTH), fixed),
            pl.BlockSpec((1, MIX_WIDTH), fixed),
            pl.BlockSpec((1, MIX_WIDTH), fixed),
        ],
        out_specs=pl.BlockSpec((ts, MIX_WIDTH), lambda bb, i: (bb * ns + i, 0)),
        out_shape=jax.ShapeDtypeStruct((t, MIX_WIDTH), BF16),
        scratch_shapes=[pltpu.VMEM((SUBLANES, ts + 2 * CONV_HALO, MIX_WIDTH), F32)],
        compiler_params=_cparams(("parallel", "parallel")),
        name="conv_module",
    )(a_in, a_in, a_in, w, b, g, beta)


def _swa_kernel(q_ref, kp_ref, kc_ref, kn_ref, vp_ref, vc_ref, vn_ref, sink_ref, o_ref):
    n = pl.program_id(1)
    nsteps = pl.num_programs(1)
    bq = BLOCK_Q
    cols = GQA_GROUPS * bq
    jj = lax.broadcasted_iota(jnp.int32, (3 * bq, cols), 0)
    ii = lax.broadcasted_iota(jnp.int32, (3 * bq, cols), 1) % bq
    band = (jj >= ii) & (jj <= ii + 2 * WINDOW)
    kcat = jnp.concatenate([kp_ref[...], kc_ref[...], kn_ref[...]], axis=0)
    vcat_t = jnp.concatenate([vp_ref[...], vc_ref[...], vn_ref[...]], axis=0).astype(F32).T.astype(BF16)
    for qb in range(SWA_BLOCKS):
        mask = band
        if qb == 0:
            mask = mask & (jj >= jnp.where(n > 0, 0, bq))
        if qb == SWA_BLOCKS - 1:
            mask = mask & (jj < jnp.where(n < nsteps - 1, 3 * bq, 2 * bq))
        q_t = q_ref[qb * bq:(qb + 1) * bq, :].astype(F32).T.astype(BF16)
        outs = []
        for hk in range(N_KV_HEADS):
            heads = range(hk * GQA_GROUPS, (hk + 1) * GQA_GROUPS)
            kh = kcat[qb * bq:(qb + 3) * bq, hk * HEAD_DIM:(hk + 1) * HEAD_DIM]
            vh_t = jnp.concatenate([vcat_t[hk * HEAD_DIM:(hk + 1) * HEAD_DIM, qb * bq:(qb + 3) * bq],
                                    jnp.ones((BF16_ROWS, 3 * bq), BF16)], axis=0)
            qs_t = jnp.concatenate([q_t[hd * HEAD_DIM:(hd + 1) * HEAD_DIM, :] for hd in heads], axis=1)
            sink = jnp.concatenate([jnp.broadcast_to(sink_ref[hd:hd + 1, 0:1], (1, bq)) for hd in heads], axis=1)
            s = jnp.where(mask, _dot(kh, qs_t), NEG)
            m = jnp.maximum(jnp.max(s, axis=0, keepdims=True), sink)
            o_ext = _dot(vh_t, jnp.exp(s - m).astype(BF16))
            denom = o_ext[HEAD_DIM:HEAD_DIM + 1, :] + jnp.exp(sink - m)
            o_t = o_ext[:HEAD_DIM, :] / denom
            outs.extend(o_t[:, g * bq:(g + 1) * bq] for g in range(GQA_GROUPS))
        o_ref[qb * bq:(qb + 1) * bq, :] = jnp.concatenate(outs, axis=0).T.astype(o_ref.dtype)


def _swa(q, k, v, sink_b, batch, seq):
    t = q.shape[0]
    bq = BLOCK_Q
    tq = SWA_BLOCKS * bq
    ns = seq // tq
    last = t // bq - 1
    cur = lambda bb, n: (bb * ns + n, 0)
    prev = lambda bb, n: (jnp.maximum((bb * ns + n) * SWA_BLOCKS - 1, 0), 0)
    nxt = lambda bb, n: (jnp.minimum((bb * ns + n + 1) * SWA_BLOCKS, last), 0)
    halo = lambda im: pl.BlockSpec((bq, KV_WIDTH), im)
    body = pl.BlockSpec((tq, KV_WIDTH), cur)
    return pl.pallas_call(
        _swa_kernel,
        grid=(batch, ns),
        in_specs=[pl.BlockSpec((tq, ATTN_WIDTH), cur), halo(prev), body, halo(nxt), halo(prev), body, halo(nxt),
                  pl.BlockSpec((N_Q_HEADS, LANE), lambda bb, n: (0, 0))],
        out_specs=pl.BlockSpec((tq, ATTN_WIDTH), cur),
        out_shape=jax.ShapeDtypeStruct((t, ATTN_WIDTH), BF16),
        compiler_params=_cparams(("parallel", "parallel")),
        name="window_attn",
    )(q, k, k, k, v, v, v, sink_b)


def _dense_attn_kernel(q_ref, k_ref, vt_ref, o_ref):
    ts = q_ref.shape[0] // ATT_CHAINS
    k = k_ref[...]
    v_ext = jnp.concatenate([vt_ref[...], jnp.ones((BF16_ROWS, vt_ref.shape[1]), BF16)], axis=0)
    q_t = q_ref[...].astype(F32).T.astype(BF16)

    def scores(c):
        cols = slice(c * ts, (c + 1) * ts)
        qs_t = jnp.concatenate([q_t[g * HEAD_DIM:(g + 1) * HEAD_DIM, cols] for g in range(GQA_GROUPS)], axis=1)
        return _dot(k, qs_t)

    s_next = scores(0)
    for c in range(ATT_CHAINS):
        cols = slice(c * ts, (c + 1) * ts)
        s = s_next
        if c + 1 < ATT_CHAINS:
            s_next = scores(c + 1)
        p = jnp.exp(s - jnp.max(s, axis=0, keepdims=True))
        o_ext = _dot(v_ext, p.astype(BF16))
        o_t = o_ext[:HEAD_DIM, :] / o_ext[HEAD_DIM:HEAD_DIM + 1, :]
        o_ref[cols, :] = jnp.concatenate(
            [o_t[:, g * ts:(g + 1) * ts] for g in range(GQA_GROUPS)], axis=0).T.astype(o_ref.dtype)


def _dense_attn(q, k, v, batch, seq):
    t = q.shape[0]
    tq = min(ATT_TQ, seq)
    nq = seq // tq
    gw = GQA_GROUPS * HEAD_DIM
    qmap = lambda bb, hk, qi: (bb * nq + qi, hk)
    return pl.pallas_call(
        _dense_attn_kernel,
        grid=(batch, N_KV_HEADS, nq),
        in_specs=[
            pl.BlockSpec((tq, gw), qmap),
            pl.BlockSpec((None, seq, HEAD_DIM), lambda bb, hk, qi: (hk, bb, 0)),
            pl.BlockSpec((None, HEAD_DIM, seq), lambda bb, hk, qi: (hk, 0, bb)),
        ],
        out_specs=pl.BlockSpec((tq, gw), qmap),
        out_shape=jax.ShapeDtypeStruct((t, ATTN_WIDTH), BF16),
        compiler_params=_cparams(("parallel", "parallel", "parallel")),
        name="dense_attn",
    )(q, k, v)


def _sgu_kernel(d_ref, g_ref, b_ref, w_ref, bs_ref, o_ref):
    z = jax.nn.gelu(d_ref[...])
    u = z[:, :MIX_WIDTH]
    v = _layernorm(z[:, MIX_WIDTH:], g_ref[...], b_ref[...]).astype(BF16)
    gc = MIX_WIDTH // SGU_GROUPS
    for n in range(d_ref.shape[0] // CHUNK):
        rows = slice(n * CHUNK, (n + 1) * CHUNK)
        for g in range(SGU_GROUPS):
            cols = slice(g * gc, (g + 1) * gc)
            mixed = _dot(w_ref[g], v[rows, cols]) + bs_ref[g]
            o_ref[rows, cols] = (u[rows, cols] * mixed).astype(o_ref.dtype)


def _sgu(d_in, g, b, w, bs):
    t = d_in.shape[0]
    ts = ROW_TILE
    gc = MIX_WIDTH // SGU_GROUPS
    row = lambda i: (i, 0)
    fixed = lambda i: (0, 0)
    fixed3 = lambda i: (0, 0, 0)
    return pl.pallas_call(
        _sgu_kernel,
        grid=(t // ts,),
        in_specs=[
            pl.BlockSpec((ts, 2 * MIX_WIDTH), row),
            pl.BlockSpec((1, MIX_WIDTH), fixed),
            pl.BlockSpec((1, MIX_WIDTH), fixed),
            pl.BlockSpec((SGU_GROUPS, CHUNK, CHUNK), fixed3),
            pl.BlockSpec((SGU_GROUPS, CHUNK, gc), fixed3),
        ],
        out_specs=pl.BlockSpec((ts, MIX_WIDTH), row),
        out_shape=jax.ShapeDtypeStruct((t, MIX_WIDTH), BF16),
        compiler_params=_cparams(("parallel",)),
        name="spatial_gating",
    )(d_in, g, b, w, bs)


def _outproj_kernel(x_ref, a_ref, b_ref, w_ref, g_ref, xn_ref, hnt_ref):
    y = _dot(a_ref[...], w_ref[:MIX_WIDTH, :]) + _dot(b_ref[...], w_ref[MIX_WIDTH:, :])
    xn = x_ref[...] + y
    xn_ref[...] = xn
    hnt_ref[...] = _rms(xn, g_ref[...]).T.astype(BF16)


def _outproj(x, a, b, w, g):
    t = x.shape[0]
    tm = ROW_TILE
    row = lambda i: (i, 0)
    fixed = lambda i: (0, 0)
    return pl.pallas_call(
        _outproj_kernel,
        grid=(t // tm,),
        in_specs=[
            pl.BlockSpec((tm, D_MODEL), row),
            pl.BlockSpec((tm, MIX_WIDTH), row),
            pl.BlockSpec((tm, MIX_WIDTH), row),
            pl.BlockSpec((2 * MIX_WIDTH, D_MODEL), fixed),
            pl.BlockSpec((1, D_MODEL), fixed),
        ],
        out_specs=[pl.BlockSpec((tm, D_MODEL), row), pl.BlockSpec((D_MODEL, tm), lambda i: (0, i))],
        out_shape=[jax.ShapeDtypeStruct((t, D_MODEL), F32), jax.ShapeDtypeStruct((D_MODEL, t), BF16)],
        compiler_params=_cparams(("parallel",)),
        name="outproj",
    )(x, a, b, w, g)


_PEER_PAIRS = [(i, j) for i in range(PEER_TOPK) for j in range(PEER_TOPK) if (i + 1) * (j + 1) <= PEER_TOPK]
_PEER_CAND_ROWS = -(-len(_PEER_PAIRS) // 8) * 8


SUBLANES = 8


def _bitonic_merge_desc(v, lo, n):
    step = n // 2
    while step >= 1:
        for i in range(lo, lo + n):
            if (i - lo) & step == 0:
                hi_, lo_ = jnp.maximum(v[i], v[i + step]), jnp.minimum(v[i], v[i + step])
                v[i], v[i + step] = hi_, lo_
        step //= 2


def _oddeven_merge(lo, hi, r):
    step = r * 2
    if step < hi - lo:
        yield from _oddeven_merge(lo, hi, step)
        yield from _oddeven_merge(lo + r, hi, step)
        yield from [(i, i + r) for i in range(lo + r, hi - r, step)]
    else:
        yield (lo, lo + r)


def _oddeven_merge_sort(lo, hi):
    if hi - lo >= 1:
        mid = lo + (hi - lo) // 2
        yield from _oddeven_merge_sort(lo, mid)
        yield from _oddeven_merge_sort(mid + 1, hi)
        yield from _oddeven_merge(lo, hi, 1)


def _sort16_desc(v):
    for i, j in _oddeven_merge_sort(0, len(v) - 1):
        v[i], v[j] = jnp.maximum(v[i], v[j]), jnp.minimum(v[i], v[j])
    return v


def _rank_among(b, v):
    lt = lambda t: t > b
    c1 = lt(v[7])
    c2 = lt(jnp.where(c1, v[11], v[3]))
    c3 = lt(jnp.where(c1, jnp.where(c2, v[13], v[9]), jnp.where(c2, v[5], v[1])))
    hi4 = jnp.where(c2, jnp.where(c3, v[14], v[12]), jnp.where(c3, v[10], v[8]))
    lo4 = jnp.where(c2, jnp.where(c3, v[6], v[4]), jnp.where(c3, v[2], v[0]))
    c4 = lt(jnp.where(c1, hi4, lo4))
    one = lambda c, x: jnp.where(c, x, 0.0)
    r = one(c1, 8.0) + one(c2, 4.0) + one(c3, 2.0) + one(c4, 1.0)
    return jnp.where(lt(v[15]), float(PEER_TOPK), r)


def _top16_sorted(s):
    v = _sort16_desc([s[k * SUBLANES:(k + 1) * SUBLANES, :] for k in range(PEER_NKEYS // SUBLANES)])
    for shift in (4, 2, 1):
        v = [jnp.maximum(v[k], pltpu.roll(v[PEER_TOPK - 1 - k], shift, 0)) for k in range(PEER_TOPK)]
        _bitonic_merge_desc(v, 0, PEER_TOPK)
    return v


def _peer_select(h, s1_all, s2_all, n_ref, e1_ref, r2_ref, e2_ref):
    tt = s1_all.shape[-1]
    nv = PEER_NKEYS // SUBLANES
    sub = lax.broadcasted_iota(jnp.int32, (SUBLANES, LANE), 0)
    for tc in range(tt // LANE):
        lanes = slice(tc * LANE, (tc + 1) * LANE)
        s1 = s1_all[:, lanes]
        s2 = s2_all[:, lanes]
        v1 = _top16_sorted(s1)
        v2 = _top16_sorted(s2)
        pair_sum = {p: v1[p[0]] + v2[p[1]] for p in _PEER_PAIRS}
        cand = []
        for g in range(_PEER_CAND_ROWS // SUBLANES):
            c = jnp.full((SUBLANES, LANE), -jnp.inf, F32)
            for r, p in enumerate(_PEER_PAIRS[g * SUBLANES:(g + 1) * SUBLANES]):
                c = jnp.where(sub == r, pair_sum[p], c)
            cand.append(c)
        cand.append(jnp.full((SUBLANES, LANE), -jnp.inf, F32))
        f = _sort16_desc(cand)
        f = f + [pltpu.roll(c, 4, 0) for c in reversed(f)]
        _bitonic_merge_desc(f, 0, PEER_TOPK)
        for shift in (2, 1):
            f = [jnp.maximum(f[k], pltpu.roll(f[PEER_TOPK - 1 - k], shift, 0)) for k in range(PEER_TOPK)]
            _bitonic_merge_desc(f, 0, PEER_TOPK)
        thr = f[PEER_TOPK - 1]
        z = jnp.ones((SUBLANES, LANE), F32)
        for r in range(1, PEER_TOPK):
            z = z + jnp.exp(f[r] - f[0])
        rz = 1.0 / z
        cnt = []
        for i in range(PEER_TOPK):
            c = jnp.zeros((SUBLANES, LANE), F32)
            for j in range(PEER_TOPK // (i + 1)):
                c = c + jnp.where(pair_sum[(i, j)] >= thr, 1.0, 0.0)
            cnt.append(c)
        ranks = []
        e2s = []
        for k in range(nv):
            rows = slice(k * SUBLANES, (k + 1) * SUBLANES)
            a = s1[rows, :]
            b = s2[rows, :]
            n = jnp.zeros((SUBLANES, LANE), F32)
            for r in range(PEER_TOPK):
                n = jnp.where(a == v1[r], cnt[r], n)
            n_ref[h, rows, lanes] = n
            e1_ref[h, rows, lanes] = jnp.exp(a - v1[0]) * rz
            ranks.append(_rank_among(b, v2))
            e2s.append(jnp.exp(b - v2[0]))
        r2_ref[h, :, lanes] = jnp.concatenate(ranks, axis=0).astype(r2_ref.dtype)
        e2_ref[h, :, lanes] = jnp.concatenate(e2s, axis=0).astype(e2_ref.dtype)


def _peer_gates(step, sub, n_ref, e1_ref, r2_ref, e2_ref, g_ref):
    tt = g_ref.shape[-1]
    gdt = g_ref.dtype
    first = sub * PEER_SUB
    a8 = pl.multiple_of(step * PEER_A + first // SUBLANES * SUBLANES, SUBLANES)
    nload = -(-(first % SUBLANES + PEER_SUB) // SUBLANES) * SUBLANES
    for tc in range(tt // LANE):
        lanes = slice(tc * LANE, (tc + 1) * LANE)
        nrows = [n_ref[h, pl.ds(a8, nload), lanes].astype(gdt) for h in range(PEER_HEADS)]
        erows = [e1_ref[h, pl.ds(a8, nload), lanes].astype(gdt) for h in range(PEER_HEADS)]
        for blk in range(PEER_SUB // PEER_GATE_BLOCK):
            gates = [jnp.zeros((PEER_NKEYS, LANE), gdt) for _ in range(PEER_GATE_BLOCK)]
            for h in range(PEER_HEADS):
                r2 = r2_ref[h, :, lanes]
                e2 = e2_ref[h, :, lanes]
                for k in range(PEER_GATE_BLOCK):
                    al = first % SUBLANES + blk * PEER_GATE_BLOCK + k
                    keep = r2 < nrows[h][al:al + 1, :]
                    gates[k] = gates[k] + jnp.where(keep, e2, 0.0) * erows[h][al:al + 1, :]
            for k in range(PEER_GATE_BLOCK):
                al = blk * PEER_GATE_BLOCK + k
                g_ref[sub, al * PEER_NKEYS:(al + 1) * PEER_NKEYS, lanes] = gates[k]


def _peer_kernel(x_ref, hnt_ref, wqt_ref, sk_ref, u_ref, vt_ref, o_ref,
                 qt_ref, n_ref, e1_ref, r2_ref, e2_ref, g_ref, acc_ref, ht_ref):
    j = pl.program_id(1)
    nj = pl.num_programs(1)
    half = PEER_DKEY // 2
    gdt = r2_ref.dtype
    gate_refs = (n_ref, e1_ref, r2_ref, e2_ref, g_ref)

    @pl.when(j == 0)
    def _():
        qt_ref[...] = _dot(wqt_ref[...], hnt_ref[...]).astype(BF16)

        def body(h, carry):
            r0 = pl.multiple_of(h * PEER_DKEY, PEER_DKEY)
            s1 = _dot(sk_ref[2 * h], qt_ref[pl.ds(r0, half), :])
            s2 = _dot(sk_ref[2 * h + 1], qt_ref[pl.ds(r0 + half, half), :])
            _peer_select(h, s1, s2, n_ref, e1_ref, r2_ref, e2_ref)
            return carry

        lax.fori_loop(0, PEER_HEADS, body, 0, unroll=True)
        acc_ref[...] = jnp.zeros(acc_ref.shape, F32)

    sub_rows = lambda sub: slice(sub * PEER_SUB * PEER_NKEYS, (sub + 1) * PEER_SUB * PEER_NKEYS)
    for sub in range(PEER_A // PEER_SUB):
        _peer_gates(j, sub, *gate_refs)
    for sub in range(PEER_A // PEER_SUB):
        at = _dot(u_ref[sub_rows(sub), :], hnt_ref[...]).astype(gdt)
        ht_ref[sub_rows(sub), :] = jax.nn.gelu(at) * g_ref[sub]
    acc_ref[...] += _dot(vt_ref[...], ht_ref[...])

    @pl.when(j == nj - 1)
    def _():
        o_ref[...] = x_ref[...] + acc_ref[...].T


def _peer(x, hnt, wqt, sk, u, vt, layer):
    t = x.shape[0]
    tt = PEER_TT
    ne = u.shape[1]
    et = PEER_A * PEER_NKEYS
    hk = PEER_HEADS * PEER_DKEY
    sel = lambda dt: pltpu.VMEM((PEER_HEADS, PEER_NKEYS, tt), dt)
    return pl.pallas_call(
        _peer_kernel,
        grid=(t // tt, ne // et),
        in_specs=[
            pl.BlockSpec((tt, D_MODEL), lambda i, j: (i, 0)),
            pl.BlockSpec((D_MODEL, tt), lambda i, j: (0, i)),
            pl.BlockSpec((hk, D_MODEL), lambda i, j: (0, 0), pipeline_mode=pl.Buffered(1)),
            pl.BlockSpec((2 * PEER_HEADS, PEER_NKEYS, PEER_DKEY // 2), lambda i, j: (0, 0, 0)),
            pl.BlockSpec((None, et, D_MODEL), lambda i, j: (layer, j, 0)),
            pl.BlockSpec((None, D_MODEL, et), lambda i, j: (layer, 0, j)),
        ],
        out_specs=pl.BlockSpec((tt, D_MODEL), lambda i, j: (i, 0)),
        out_shape=jax.ShapeDtypeStruct((t, D_MODEL), F32),
        scratch_shapes=[
            pltpu.VMEM((hk, tt), BF16),
            sel(F32), sel(F32), sel(PEER_GATE_DTYPE), sel(PEER_GATE_DTYPE),
            pltpu.VMEM((PEER_A // PEER_SUB, PEER_SUB * PEER_NKEYS, tt), PEER_GATE_DTYPE),
            pltpu.VMEM((D_MODEL, tt), F32),
            pltpu.VMEM((et, tt), BF16),
        ],
        compiler_params=_cparams(("parallel", "arbitrary")),
        name="peer",
    )(x, hnt, wqt, sk, u, vt)


def _final_norm_kernel(x_ref, g_ref, o_ref):
    o_ref[...] = _rms(x_ref[...], g_ref[...])


def _final_norm(x, g):
    t = x.shape[0]
    tm = ROW_TILE
    return pl.pallas_call(
        _final_norm_kernel,
        grid=(t // tm,),
        in_specs=[pl.BlockSpec((tm, D_MODEL), lambda i: (i, 0)), pl.BlockSpec((1, D_MODEL), lambda i: (0, 0))],
        out_specs=pl.BlockSpec((tm, D_MODEL), lambda i: (i, 0)),
        out_shape=jax.ShapeDtypeStruct((t, D_MODEL), F32),
        compiler_params=_cparams(("parallel",)),
        name="final_norm",
    )(x, g)


def _rope_table(pos, dim):
    inv = ROPE_THETA ** (-jnp.arange(0, dim, 2, dtype=F32) / dim)
    ang = pos.astype(F32)[:, None] * inv[None, :]
    ang = jnp.concatenate([ang, ang], axis=-1)
    sign = jnp.concatenate([-jnp.ones((dim // 2,), F32), jnp.ones((dim // 2,), F32)])
    return jnp.cos(ang), jnp.sin(ang) * sign


def _rope_tables(seq):
    pos = jnp.arange(seq)
    cos1, sin1 = _rope_table(pos, HEAD_DIM)
    cr, sr = _rope_table(pos // GRID_W, HEAD_DIM // 2)
    cc, sc = _rope_table(pos % GRID_W, HEAD_DIM // 2)
    cos2 = jnp.concatenate([cr, cc], axis=-1)
    sin2 = jnp.concatenate([sr, sc], axis=-1)
    tile = lambda a: jnp.tile(a, (1, N_Q_HEADS))
    return tile(cos1), tile(sin1), tile(cos2), tile(sin2)


def kernel(x, mix_norm_g, ffn_norm_g, final_norm_g, even_w_in, even_w_out, conv_w, conv_b, conv_ln_g, conv_ln_b, sink_logits, odd_w_in, odd_w_out, q_norm_g, k_norm_g, sgu_ln_g, sgu_ln_b, sgu_w, sgu_b, peer_wq, peer_subkeys, peer_u, peer_v):
    batch, seq, d = x.shape
    depth = mix_norm_g.shape[0]
    t = batch * seq
    cos1, sin1, cos2, sin2 = _rope_tables(seq)
    row = lambda a: a.reshape(1, -1).astype(F32)
    xf = x.reshape(t, d)
    u_all = peer_u.astype(BF16)
    vt_all = jnp.swapaxes(peer_v, 1, 2).astype(BF16)
    for layer in range(depth):
        i = layer // 2
        if layer % 2 == 0:
            a_in, q, k, v = _inproj_even(xf, row(mix_norm_g[layer]), even_w_in[i].astype(BF16), cos1, sin1, seq)
            m1 = _conv_module(a_in, conv_w[i, :, 0, :], row(conv_b[i]), row(conv_ln_g[i]), row(conv_ln_b[i]), batch, seq)
            sink_b = jnp.broadcast_to(sink_logits[i].astype(F32)[:, None], (N_Q_HEADS, LANE))
            m2 = _swa(q, k, v, sink_b, batch, seq)
            w_out = even_w_out[i]
        else:
            q, k, v, d_in = _inproj_odd(xf, row(mix_norm_g[layer]), odd_w_in[i].astype(BF16),
                                        row(jnp.tile(q_norm_g[i], N_Q_HEADS)), row(jnp.tile(k_norm_g[i], N_KV_HEADS)),
                                        cos2, sin2, seq)
            m1 = _dense_attn(q, k, v, batch, seq)
            bs = jnp.broadcast_to(sgu_b[i].astype(F32)[:, :, None], (SGU_GROUPS, CHUNK, MIX_WIDTH // SGU_GROUPS))
            m2 = _sgu(d_in, row(sgu_ln_g[i]), row(sgu_ln_b[i]), sgu_w[i].astype(BF16), bs)
            w_out = odd_w_out[i]
        xn, hnt = _outproj(xf, m1, m2, w_out.astype(BF16), row(ffn_norm_g[layer]))
        sk = peer_subkeys[layer].reshape(2 * PEER_HEADS, PEER_NKEYS, PEER_DKEY // 2).astype(BF16)
        xf = _peer(xn, hnt, peer_wq[layer].astype(BF16).T, sk, u_all, vt_all, layer)
    return _final_norm(xf, row(final_norm_g)).reshape(batch, seq, d)
```

```python
import jax
import jax.numpy as jnp
from jax import lax
from jax.experimental import pallas as pl
from jax.experimental.pallas import tpu as pltpu

F32 = jnp.float32
BF16 = jnp.bfloat16

D_MODEL = 1024
HEAD_DIM = 64
N_Q_HEADS = 8
N_KV_HEADS = 2
GQA_GROUPS = N_Q_HEADS // N_KV_HEADS
ATTN_WIDTH = N_Q_HEADS * HEAD_DIM
KV_WIDTH = N_KV_HEADS * HEAD_DIM
MIX_WIDTH = D_MODEL // 2
CONV_WIDTH = 31
CONV_HALO = 16
SGU_GROUPS = 4
CHUNK = 128
WINDOW = 128
BLOCK_Q = 128
GRID_W = 64
ROPE_THETA = 10000.0
PEER_HEADS = 8
PEER_NKEYS = 128
PEER_DKEY = 128
PEER_TOPK = 16
EPS = 1e-6
NEG = -1e30

VMEM_LIMIT_BYTES = 56 * 1024 * 1024

ROW_TILE = 1024
CONV_ROWS = 64
SWA_BLOCKS = 4
ATT_TQ = 1024
ATT_CHAINS = 8
PEER_TT = 512
PEER_A = 16
PEER_SUB = 8
PEER_GATE_BLOCK = 4
PEER_GATE_DTYPE = BF16
LANE = 128
BF16_ROWS = 16


def _cparams(sem):
    return pltpu.CompilerParams(dimension_semantics=sem, vmem_limit_bytes=VMEM_LIMIT_BYTES)


def _dot(a, b):
    return jnp.dot(a, b, preferred_element_type=F32)


def _rms(x, g):
    return x * lax.rsqrt(jnp.mean(x * x, axis=-1, keepdims=True) + EPS) * g


def _layernorm(x, g, b):
    mu = jnp.mean(x, axis=-1, keepdims=True)
    xc = x - mu
    var = jnp.mean(xc * xc, axis=-1, keepdims=True)
    return xc * lax.rsqrt(var + EPS) * g + b


def _rope(x, cos, sin_signed, group, half):
    n = x.shape[-1]
    lane = lax.broadcasted_iota(jnp.int32, x.shape, x.ndim - 1)
    first = (lane % group) < half
    partner = jnp.where(first, pltpu.roll(x, n - half, x.ndim - 1), pltpu.roll(x, half, x.ndim - 1))
    return x * cos + partner * sin_signed


def _inproj_even_kernel(x_ref, g_ref, w_ref, cos_ref, sin_ref, a_ref, q_ref, k_ref, v_ref):
    h = _rms(x_ref[...], g_ref[...]).astype(BF16)
    o1 = 2 * MIX_WIDTH
    o2 = o1 + ATTN_WIDTH
    o3 = o2 + KV_WIDTH
    a_ref[...] = _dot(h, w_ref[:, :o1])
    cos = cos_ref[...]
    sin = sin_ref[...]
    q = _rope(_dot(h, w_ref[:, o1:o2]), cos, sin, HEAD_DIM, HEAD_DIM // 2)
    q_ref[...] = (q * (HEAD_DIM ** -0.5)).astype(BF16)
    k = _rope(_dot(h, w_ref[:, o2:o3]), cos[:, :KV_WIDTH], sin[:, :KV_WIDTH], HEAD_DIM, HEAD_DIM // 2)
    k_ref[...] = k.astype(BF16)
    v_ref[...] = _dot(h, w_ref[:, o3:]).astype(BF16)


def _inproj_even(x, g, w, cos, sin, seq):
    t = x.shape[0]
    tm = ROW_TILE
    nseq = seq // tm
    in_w = w.shape[1]
    row = lambda i: (i, 0)
    fixed = lambda i: (0, 0)
    pos = lambda i: (i % nseq, 0)
    return pl.pallas_call(
        _inproj_even_kernel,
        grid=(t // tm,),
        in_specs=[
            pl.BlockSpec((tm, D_MODEL), row),
            pl.BlockSpec((1, D_MODEL), fixed),
            pl.BlockSpec((D_MODEL, in_w), fixed),
            pl.BlockSpec((tm, ATTN_WIDTH), pos),
            pl.BlockSpec((tm, ATTN_WIDTH), pos),
        ],
        out_specs=[
            pl.BlockSpec((tm, 2 * MIX_WIDTH), row),
            pl.BlockSpec((tm, ATTN_WIDTH), row),
            pl.BlockSpec((tm, KV_WIDTH), row),
            pl.BlockSpec((tm, KV_WIDTH), row),
        ],
        out_shape=[
            jax.ShapeDtypeStruct((t, 2 * MIX_WIDTH), F32),
            jax.ShapeDtypeStruct((t, ATTN_WIDTH), BF16),
            jax.ShapeDtypeStruct((t, KV_WIDTH), BF16),
            jax.ShapeDtypeStruct((t, KV_WIDTH), BF16),
        ],
        compiler_params=_cparams(("parallel",)),
        name="inproj_even",
    )(x, g, w, cos, sin)


def _head_sumsq(x):
    n = x.shape[-1]
    r = lax.broadcasted_iota(jnp.int32, (n, n), 0) // HEAD_DIM
    c = lax.broadcasted_iota(jnp.int32, (n, n), 1) // HEAD_DIM
    ones = jnp.where(r == c, 1.0, 0.0).astype(BF16)
    sq = x * x
    hi = sq.astype(BF16)
    lo = (sq - hi.astype(F32)).astype(BF16)
    return _dot(hi, ones) + _dot(lo, ones)


def _inproj_odd_kernel(x_ref, g_ref, w_ref, qg_ref, kg_ref, cos_ref, sin_ref, q_ref, k_ref, vt_ref, d_ref):
    h = _rms(x_ref[...], g_ref[...]).astype(BF16)
    c1 = ATTN_WIDTH
    c2 = c1 + KV_WIDTH
    c3 = c2 + KV_WIDTH
    cos = cos_ref[...]
    sin = sin_ref[...]
    half = HEAD_DIM // 2
    q = _dot(h, w_ref[:, :c1])
    q = q * lax.rsqrt(_head_sumsq(q) * (1.0 / HEAD_DIM) + EPS) * qg_ref[...]
    q = _rope(q, cos, sin, half, half // 2)
    q_ref[...] = (q * (HEAD_DIM ** -0.5)).astype(BF16)
    k = _dot(h, w_ref[:, c1:c2])
    k = k * lax.rsqrt(_head_sumsq(k) * (1.0 / HEAD_DIM) + EPS) * kg_ref[...]
    k = _rope(k, cos[:, :KV_WIDTH], sin[:, :KV_WIDTH], half, half // 2).astype(BF16)
    v_t = _dot(h, w_ref[:, c2:c3]).T.astype(BF16)
    for hh in range(N_KV_HEADS):
        k_ref[hh] = k[:, hh * HEAD_DIM:(hh + 1) * HEAD_DIM]
        vt_ref[hh] = v_t[hh * HEAD_DIM:(hh + 1) * HEAD_DIM, :]
    d_ref[...] = _dot(h, w_ref[:, c3:])


def _inproj_odd(x, g, w, qg, kg, cos, sin, seq):
    t = x.shape[0]
    tm = ROW_TILE
    nseq = seq // tm
    in_w = w.shape[1]
    row = lambda i: (i, 0)
    fixed = lambda i: (0, 0)
    pos = lambda i: (i % nseq, 0)
    kv = lambda i: (0, i, 0)
    return pl.pallas_call(
        _inproj_odd_kernel,
        grid=(t // tm,),
        in_specs=[
            pl.BlockSpec((tm, D_MODEL), row),
            pl.BlockSpec((1, D_MODEL), fixed),
            pl.BlockSpec((D_MODEL, in_w), fixed),
            pl.BlockSpec((1, ATTN_WIDTH), fixed),
            pl.BlockSpec((1, KV_WIDTH), fixed),
            pl.BlockSpec((tm, ATTN_WIDTH), pos),
            pl.BlockSpec((tm, ATTN_WIDTH), pos),
        ],
        out_specs=[
            pl.BlockSpec((tm, ATTN_WIDTH), row),
            pl.BlockSpec((N_KV_HEADS, tm, HEAD_DIM), kv),
            pl.BlockSpec((N_KV_HEADS, HEAD_DIM, tm), lambda i: (0, 0, i)),
            pl.BlockSpec((tm, 2 * MIX_WIDTH), row),
        ],
        out_shape=[
            jax.ShapeDtypeStruct((t, ATTN_WIDTH), BF16),
            jax.ShapeDtypeStruct((N_KV_HEADS, t, HEAD_DIM), BF16),
            jax.ShapeDtypeStruct((N_KV_HEADS, HEAD_DIM, t), BF16),
            jax.ShapeDtypeStruct((t, 2 * MIX_WIDTH), F32),
        ],
        compiler_params=_cparams(("parallel",)),
        name="inproj_odd",
    )(x, g, w, qg, kg, cos, sin)


def _glu(x):
    return x[:, :MIX_WIDTH] * jax.nn.sigmoid(x[:, MIX_WIDTH:])


def _conv_kernel(prev_ref, cur_ref, next_ref, w_ref, b_ref, g_ref, beta_ref, o_ref, buf_ref):
    i = pl.program_id(1)
    ns = pl.num_programs(1)
    ts = cur_ref.shape[0]
    halo = CONV_HALO
    buf_ref[0, 0:halo, :] = jnp.where(i > 0, _glu(prev_ref[...]), 0.0)
    buf_ref[0, halo:halo + ts, :] = _glu(cur_ref[...])
    buf_ref[0, halo + ts:, :] = jnp.where(i < ns - 1, _glu(next_ref[...]), 0.0)
    span = ts + 2 * halo - SUBLANES
    for s in range(1, SUBLANES):
        buf_ref[s, 0:span, :] = buf_ref[0, s:s + span, :]
    w = w_ref[...]
    bias = b_ref[...]
    g = g_ref[...]
    beta = beta_ref[...]
    first = halo - CONV_WIDTH // 2
    for c in range(ts // CONV_ROWS):
        r0 = c * CONV_ROWS
        acc = jnp.zeros((CONV_ROWS, MIX_WIDTH), F32)
        for k in range(CONV_WIDTH):
            m, s = divmod(first + k, SUBLANES)
            lo = r0 + SUBLANES * m
            acc = acc + w[k:k + 1, :] * buf_ref[s, lo:lo + CONV_ROWS, :]
        y = _layernorm(acc + bias, g, beta)
        o_ref[r0:r0 + CONV_ROWS, :] = (y * jax.nn.sigmoid(y)).astype(o_ref.dtype)


def _conv_module(a_in, w, b, g, beta, batch, seq):
    t = a_in.shape[0]
    ts = ROW_TILE
    ns = seq // ts
    hb = ts // CONV_HALO
    last = t // CONV_HALO - 1
    fixed = lambda bb, i: (0, 0)
    return pl.pallas_call(
        _conv_kernel,
        grid=(batch, ns),
        in_specs=[
            pl.BlockSpec((CONV_HALO, 2 * MIX_WIDTH), lambda bb, i: (jnp.maximum((bb * ns + i) * hb - 1, 0), 0)),
            pl.BlockSpec((ts, 2 * MIX_WIDTH), lambda bb, i: (bb * ns + i, 0)),
            pl.BlockSpec((CONV_HALO, 2 * MIX_WIDTH), lambda bb, i: (jnp.minimum((bb * ns + i + 1) * hb, last), 0)),
            pl.BlockSpec((CONV_WIDTH, MIX_WIDTH), fixed),
            pl.BlockSpec((1, MIX_WIDTH), fixed),
            pl.BlockSpec((1, MIX_WIDTH), fixed),
            pl.BlockSpec((1, MIX_WIDTH), fixed),
        ],
        out_specs=pl.BlockSpec((ts, MIX_WIDTH), lambda bb, i: (bb * ns + i, 0)),
        out_shape=jax.ShapeDtypeStruct((t, MIX_WIDTH), BF16),
        scratch_shapes=[pltpu.VMEM((SUBLANES, ts + 2 * CONV_HALO, MIX_WIDTH), F32)],
        compiler_params=_cparams(("parallel", "parallel")),
        name="conv_module",
    )(a_in, a_in, a_in, w, b, g, beta)


def _swa_kernel(q_ref, kp_ref, kc_ref, kn_ref, vp_ref, vc_ref, vn_ref, sink_ref, o_ref):
    n = pl.program_id(1)
    nsteps = pl.num_programs(1)
    bq = BLOCK_Q
    cols = GQA_GROUPS * bq
    jj = lax.broadcasted_iota(jnp.int32, (3 * bq, cols), 0)
    ii = lax.broadcasted_iota(jnp.int32, (3 * bq, cols), 1) % bq
    band = (jj >= ii) & (jj <= ii + 2 * WINDOW)
    kcat = jnp.concatenate([kp_ref[...], kc_ref[...], kn_ref[...]], axis=0)
    vcat_t = jnp.concatenate([vp_ref[...], vc_ref[...], vn_ref[...]], axis=0).astype(F32).T.astype(BF16)
    for qb in range(SWA_BLOCKS):
        mask = band
        if qb == 0:
            mask = mask & (jj >= jnp.where(n > 0, 0, bq))
        if qb == SWA_BLOCKS - 1:
            mask = mask & (jj < jnp.where(n < nsteps - 1, 3 * bq, 2 * bq))
        q_t = q_ref[qb * bq:(qb + 1) * bq, :].astype(F32).T.astype(BF16)
        outs = []
        for hk in range(N_KV_HEADS):
            heads = range(hk * GQA_GROUPS, (hk + 1) * GQA_GROUPS)
            kh = kcat[qb * bq:(qb + 3) * bq, hk * HEAD_DIM:(hk + 1) * HEAD_DIM]
            vh_t = jnp.concatenate([vcat_t[hk * HEAD_DIM:(hk + 1) * HEAD_DIM, qb * bq:(qb + 3) * bq],
                                    jnp.ones((BF16_ROWS, 3 * bq), BF16)], axis=0)
            qs_t = jnp.concatenate([q_t[hd * HEAD_DIM:(hd + 1) * HEAD_DIM, :] for hd in heads], axis=1)
            sink = jnp.concatenate([jnp.broadcast_to(sink_ref[hd:hd + 1, 0:1], (1, bq)) for hd in heads], axis=1)
            s = jnp.where(mask, _dot(kh, qs_t), NEG)
            m = jnp.maximum(jnp.max(s, axis=0, keepdims=True), sink)
            o_ext = _dot(vh_t, jnp.exp(s - m).astype(BF16))
            denom = o_ext[HEAD_DIM:HEAD_DIM + 1, :] + jnp.exp(sink - m)
            o_t = o_ext[:HEAD_DIM, :] / denom
            outs.extend(o_t[:, g * bq:(g + 1) * bq] for g in range(GQA_GROUPS))
        o_ref[qb * bq:(qb + 1) * bq, :] = jnp.concatenate(outs, axis=0).T.astype(o_ref.dtype)


def _swa(q, k, v, sink_b, batch, seq):
    t = q.shape[0]
    bq = BLOCK_Q
    tq = SWA_BLOCKS * bq
    ns = seq // tq
    last = t // bq - 1
    cur = lambda bb, n: (bb * ns + n, 0)
    prev = lambda bb, n: (jnp.maximum((bb * ns + n) * SWA_BLOCKS - 1, 0), 0)
    nxt = lambda bb, n: (jnp.minimum((bb * ns + n + 1) * SWA_BLOCKS, last), 0)
    halo = lambda im: pl.BlockSpec((bq, KV_WIDTH), im)
    body = pl.BlockSpec((tq, KV_WIDTH), cur)
    return pl.pallas_call(
        _swa_kernel,
        grid=(batch, ns),
        in_specs=[pl.BlockSpec((tq, ATTN_WIDTH), cur), halo(prev), body, halo(nxt), halo(prev), body, halo(nxt),
                  pl.BlockSpec((N_Q_HEADS, LANE), lambda bb, n: (0, 0))],
        out_specs=pl.BlockSpec((tq, ATTN_WIDTH), cur),
        out_shape=jax.ShapeDtypeStruct((t, ATTN_WIDTH), BF16),
        compiler_params=_cparams(("parallel", "parallel")),
        name="window_attn",
    )(q, k, k, k, v, v, v, sink_b)


def _dense_attn_kernel(q_ref, k_ref, vt_ref, o_ref):
    ts = q_ref.shape[0] // ATT_CHAINS
    k = k_ref[...]
    v_ext = jnp.concatenate([vt_ref[...], jnp.ones((BF16_ROWS, vt_ref.shape[1]), BF16)], axis=0)
    q_t = q_ref[...].astype(F32).T.astype(BF16)

    def scores(c):
        cols = slice(c * ts, (c + 1) * ts)
        qs_t = jnp.concatenate([q_t[g * HEAD_DIM:(g + 1) * HEAD_DIM, cols] for g in range(GQA_GROUPS)], axis=1)
        return _dot(k, qs_t)

    s_next = scores(0)
    for c in range(ATT_CHAINS):
        cols = slice(c * ts, (c + 1) * ts)
        s = s_next
        if c + 1 < ATT_CHAINS:
            s_next = scores(c + 1)
        p = jnp.exp(s - jnp.max(s, axis=0, keepdims=True))
        o_ext = _dot(v_ext, p.astype(BF16))
        o_t = o_ext[:HEAD_DIM, :] / o_ext[HEAD_DIM:HEAD_DIM + 1, :]
        o_ref[cols, :] = jnp.concatenate(
            [o_t[:, g * ts:(g + 1) * ts] for g in range(GQA_GROUPS)], axis=0).T.astype(o_ref.dtype)


def _dense_attn(q, k, v, batch, seq):
    t = q.shape[0]
    tq = min(ATT_TQ, seq)
    nq = seq // tq
    gw = GQA_GROUPS * HEAD_DIM
    qmap = lambda bb, hk, qi: (bb * nq + qi, hk)
    return pl.pallas_call(
        _dense_attn_kernel,
        grid=(batch, N_KV_HEADS, nq),
        in_specs=[
            pl.BlockSpec((tq, gw), qmap),
            pl.BlockSpec((None, seq, HEAD_DIM), lambda bb, hk, qi: (hk, bb, 0)),
            pl.BlockSpec((None, HEAD_DIM, seq), lambda bb, hk, qi: (hk, 0, bb)),
        ],
        out_specs=pl.BlockSpec((tq, gw), qmap),
        out_shape=jax.ShapeDtypeStruct((t, ATTN_WIDTH), BF16),
        compiler_params=_cparams(("parallel", "parallel", "parallel")),
        name="dense_attn",
    )(q, k, v)


def _sgu_kernel(d_ref, g_ref, b_ref, w_ref, bs_ref, o_ref):
    z = jax.nn.gelu(d_ref[...])
    u = z[:, :MIX_WIDTH]
    v = _layernorm(z[:, MIX_WIDTH:], g_ref[...], b_ref[...]).astype(BF16)
    gc = MIX_WIDTH // SGU_GROUPS
    for n in range(d_ref.shape[0] // CHUNK):
        rows = slice(n * CHUNK, (n + 1) * CHUNK)
        for g in range(SGU_GROUPS):
            cols = slice(g * gc, (g + 1) * gc)
            mixed = _dot(w_ref[g], v[rows, cols]) + bs_ref[g]
            o_ref[rows, cols] = (u[rows, cols] * mixed).astype(o_ref.dtype)


def _sgu(d_in, g, b, w, bs):
    t = d_in.shape[0]
    ts = ROW_TILE
    gc = MIX_WIDTH // SGU_GROUPS
    row = lambda i: (i, 0)
    fixed = lambda i: (0, 0)
    fixed3 = lambda i: (0, 0, 0)
    return pl.pallas_call(
        _sgu_kernel,
        grid=(t // ts,),
        in_specs=[
            pl.BlockSpec((ts, 2 * MIX_WIDTH), row),
            pl.BlockSpec((1, MIX_WIDTH), fixed),
            pl.BlockSpec((1, MIX_WIDTH), fixed),
            pl.BlockSpec((SGU_GROUPS, CHUNK, CHUNK), fixed3),
            pl.BlockSpec((SGU_GROUPS, CHUNK, gc), fixed3),
        ],
        out_specs=pl.BlockSpec((ts, MIX_WIDTH), row),
        out_shape=jax.ShapeDtypeStruct((t, MIX_WIDTH), BF16),
        compiler_params=_cparams(("parallel",)),
        name="spatial_gating",
    )(d_in, g, b, w, bs)


def _outproj_kernel(x_ref, a_ref, b_ref, w_ref, g_ref, xn_ref, hnt_ref):
    y = _dot(a_ref[...], w_ref[:MIX_WIDTH, :]) + _dot(b_ref[...], w_ref[MIX_WIDTH:, :])
    xn = x_ref[...] + y
    xn_ref[...] = xn
    hnt_ref[...] = _rms(xn, g_ref[...]).T.astype(BF16)


def _outproj(x, a, b, w, g):
    t = x.shape[0]
    tm = ROW_TILE
    row = lambda i: (i, 0)
    fixed = lambda i: (0, 0)
    return pl.pallas_call(
        _outproj_kernel,
        grid=(t // tm,),
        in_specs=[
            pl.BlockSpec((tm, D_MODEL), row),
            pl.BlockSpec((tm, MIX_WIDTH), row),
            pl.BlockSpec((tm, MIX_WIDTH), row),
            pl.BlockSpec((2 * MIX_WIDTH, D_MODEL), fixed),
            pl.BlockSpec((1, D_MODEL), fixed),
        ],
        out_specs=[pl.BlockSpec((tm, D_MODEL), row), pl.BlockSpec((D_MODEL, tm), lambda i: (0, i))],
        out_shape=[jax.ShapeDtypeStruct((t, D_MODEL), F32), jax.ShapeDtypeStruct((D_MODEL, t), BF16)],
        compiler_params=_cparams(("parallel",)),
        name="outproj",
    )(x, a, b, w, g)


_PEER_PAIRS = [(i, j) for i in range(PEER_TOPK) for j in range(PEER_TOPK) if (i + 1) * (j + 1) <= PEER_TOPK]
_PEER_CAND_ROWS = -(-len(_PEER_PAIRS) // 8) * 8


SUBLANES = 8


def _bitonic_merge_desc(v, lo, n):
    step = n // 2
    while step >= 1:
        for i in range(lo, lo + n):
            if (i - lo) & step == 0:
                hi_, lo_ = jnp.maximum(v[i], v[i + step]), jnp.minimum(v[i], v[i + step])
                v[i], v[i + step] = hi_, lo_
        step //= 2


def _oddeven_merge(lo, hi, r):
    step = r * 2
    if step < hi - lo:
        yield from _oddeven_merge(lo, hi, step)
        yield from _oddeven_merge(lo + r, hi, step)
        yield from [(i, i + r) for i in range(lo + r, hi - r, step)]
    else:
        yield (lo, lo + r)


def _oddeven_merge_sort(lo, hi):
    if hi - lo >= 1:
        mid = lo + (hi - lo) // 2
        yield from _oddeven_merge_sort(lo, mid)
        yield from _oddeven_merge_sort(mid + 1, hi)
        yield from _oddeven_merge(lo, hi, 1)


def _sort16_desc(v):
    for i, j in _oddeven_merge_sort(0, len(v) - 1):
        v[i], v[j] = jnp.maximum(v[i], v[j]), jnp.minimum(v[i], v[j])
    return v


def _rank_among(b, v):
    lt = lambda t: t > b
    c1 = lt(v[7])
    c2 = lt(jnp.where(c1, v[11], v[3]))
    c3 = lt(jnp.where(c1, jnp.where(c2, v[13], v[9]), jnp.where(c2, v[5], v[1])))
    hi4 = jnp.where(c2, jnp.where(c3, v[14], v[12]), jnp.where(c3, v[10], v[8]))
    lo4 = jnp.where(c2, jnp.where(c3, v[6], v[4]), jnp.where(c3, v[2], v[0]))
    c4 = lt(jnp.where(c1, hi4, lo4))
    one = lambda c, x: jnp.where(c, x, 0.0)
    r = one(c1, 8.0) + one(c2, 4.0) + one(c3, 2.0) + one(c4, 1.0)
    return jnp.where(lt(v[15]), float(PEER_TOPK), r)


def _top16_sorted(s):
    v = _sort16_desc([s[k * SUBLANES:(k + 1) * SUBLANES, :] for k in range(PEER_NKEYS // SUBLANES)])
    for shift in (4, 2, 1):
        v = [jnp.maximum(v[k], pltpu.roll(v[PEER_TOPK - 1 - k], shift, 0)) for k in range(PEER_TOPK)]
        _bitonic_merge_desc(v, 0, PEER_TOPK)
    return v


def _peer_select(h, s1_all, s2_all, n_ref, e1_ref, r2_ref, e2_ref):
    tt = s1_all.shape[-1]
    nv = PEER_NKEYS // SUBLANES
    sub = lax.broadcasted_iota(jnp.int32, (SUBLANES, LANE), 0)
    for tc in range(tt // LANE):
        lanes = slice(tc * LANE, (tc + 1) * LANE)
        s1 = s1_all[:, lanes]
        s2 = s2_all[:, lanes]
        v1 = _top16_sorted(s1)
        v2 = _top16_sorted(s2)
        pair_sum = {p: v1[p[0]] + v2[p[1]] for p in _PEER_PAIRS}
        cand = []
        for g in range(_PEER_CAND_ROWS // SUBLANES):
            c = jnp.full((SUBLANES, LANE), -jnp.inf, F32)
            for r, p in enumerate(_PEER_PAIRS[g * SUBLANES:(g + 1) * SUBLANES]):
                c = jnp.where(sub == r, pair_sum[p], c)
            cand.append(c)
        cand.append(jnp.full((SUBLANES, LANE), -jnp.inf, F32))
        f = _sort16_desc(cand)
        f = f + [pltpu.roll(c, 4, 0) for c in reversed(f)]
        _bitonic_merge_desc(f, 0, PEER_TOPK)
        for shift in (2, 1):
            f = [jnp.maximum(f[k], pltpu.roll(f[PEER_TOPK - 1 - k], shift, 0)) for k in range(PEER_TOPK)]
            _bitonic_merge_desc(f, 0, PEER_TOPK)
        thr = f[PEER_TOPK - 1]
        z = jnp.ones((SUBLANES, LANE), F32)
        for r in range(1, PEER_TOPK):
            z = z + jnp.exp(f[r] - f[0])
        rz = 1.0 / z
        cnt = []
        for i in range(PEER_TOPK):
            c = jnp.zeros((SUBLANES, LANE), F32)
            for j in range(PEER_TOPK // (i + 1)):
                c = c + jnp.where(pair_sum[(i, j)] >= thr, 1.0, 0.0)
            cnt.append(c)
        ranks = []
        e2s = []
        for k in range(nv):
            rows = slice(k * SUBLANES, (k + 1) * SUBLANES)
            a = s1[rows, :]
            b = s2[rows, :]
            n = jnp.zeros((SUBLANES, LANE), F32)
            for r in range(PEER_TOPK):
                n = jnp.where(a == v1[r], cnt[r], n)
            n_ref[h, rows, lanes] = n
            e1_ref[h, rows, lanes] = jnp.exp(a - v1[0]) * rz
            ranks.append(_rank_among(b, v2))
            e2s.append(jnp.exp(b - v2[0]))
        r2_ref[h, :, lanes] = jnp.concatenate(ranks, axis=0).astype(r2_ref.dtype)
        e2_ref[h, :, lanes] = jnp.concatenate(e2s, axis=0).astype(e2_ref.dtype)


def _peer_gates(step, sub, n_ref, e1_ref, r2_ref, e2_ref, g_ref):
    tt = g_ref.shape[-1]
    gdt = g_ref.dtype
    first = sub * PEER_SUB
    a8 = pl.multiple_of(step * PEER_A + first // SUBLANES * SUBLANES, SUBLANES)
    nload = -(-(first % SUBLANES + PEER_SUB) // SUBLANES) * SUBLANES
    for tc in range(tt // LANE):
        lanes = slice(tc * LANE, (tc + 1) * LANE)
        nrows = [n_ref[h, pl.ds(a8, nload), lanes].astype(gdt) for h in range(PEER_HEADS)]
        erows = [e1_ref[h, pl.ds(a8, nload), lanes].astype(gdt) for h in range(PEER_HEADS)]
        for blk in range(PEER_SUB // PEER_GATE_BLOCK):
            gates = [jnp.zeros((PEER_NKEYS, LANE), gdt) for _ in range(PEER_GATE_BLOCK)]
            for h in range(PEER_HEADS):
                r2 = r2_ref[h, :, lanes]
                e2 = e2_ref[h, :, lanes]
                for k in range(PEER_GATE_BLOCK):
                    al = first % SUBLANES + blk * PEER_GATE_BLOCK + k
                    keep = r2 < nrows[h][al:al + 1, :]
                    gates[k] = gates[k] + jnp.where(keep, e2, 0.0) * erows[h][al:al + 1, :]
            for k in range(PEER_GATE_BLOCK):
                al = blk * PEER_GATE_BLOCK + k
                g_ref[sub, al * PEER_NKEYS:(al + 1) * PEER_NKEYS, lanes] = gates[k]


def _peer_kernel(x_ref, hnt_ref, wqt_ref, sk_ref, u_ref, vt_ref, *rest):
    out_g_ref = rest[0] if len(rest) == 10 else None
    o_ref, qt_ref, n_ref, e1_ref, r2_ref, e2_ref, g_ref, acc_ref, ht_ref = rest[-9:]
    j = pl.program_id(1)
    nj = pl.num_programs(1)
    half = PEER_DKEY // 2
    gdt = r2_ref.dtype
    gate_refs = (n_ref, e1_ref, r2_ref, e2_ref, g_ref)

    @pl.when(j == 0)
    def _():
        qt_ref[...] = _dot(wqt_ref[...], hnt_ref[...]).astype(BF16)

        def body(h, carry):
            r0 = pl.multiple_of(h * PEER_DKEY, PEER_DKEY)
            s1 = _dot(sk_ref[2 * h], qt_ref[pl.ds(r0, half), :])
            s2 = _dot(sk_ref[2 * h + 1], qt_ref[pl.ds(r0 + half, half), :])
            _peer_select(h, s1, s2, n_ref, e1_ref, r2_ref, e2_ref)
            return carry

        lax.fori_loop(0, PEER_HEADS, body, 0, unroll=True)
        acc_ref[...] = jnp.zeros(acc_ref.shape, F32)

    sub_rows = lambda sub: slice(sub * PEER_SUB * PEER_NKEYS, (sub + 1) * PEER_SUB * PEER_NKEYS)
    for sub in range(PEER_A // PEER_SUB):
        _peer_gates(j, sub, *gate_refs)
    for sub in range(PEER_A // PEER_SUB):
        at = _dot(u_ref[sub_rows(sub), :], hnt_ref[...]).astype(gdt)
        ht_ref[sub_rows(sub), :] = jax.nn.gelu(at) * g_ref[sub]
    acc_ref[...] += _dot(vt_ref[...], ht_ref[...])

    @pl.when(j == nj - 1)
    def _():
        y = x_ref[...] + acc_ref[...].T
        o_ref[...] = y if out_g_ref is None else _rms(y, out_g_ref[...])


def _peer(x, hnt, wqt, sk, u, vt, layer, out_g=None):
    t = x.shape[0]
    extra = [] if out_g is None else [out_g]
    extra_specs = [] if out_g is None else [pl.BlockSpec((1, D_MODEL), lambda i, j: (0, 0))]
    tt = PEER_TT
    ne = u.shape[1]
    et = PEER_A * PEER_NKEYS
    hk = PEER_HEADS * PEER_DKEY
    sel = lambda dt: pltpu.VMEM((PEER_HEADS, PEER_NKEYS, tt), dt)
    return pl.pallas_call(
        _peer_kernel,
        grid=(t // tt, ne // et),
        in_specs=[
            pl.BlockSpec((tt, D_MODEL), lambda i, j: (i, 0)),
            pl.BlockSpec((D_MODEL, tt), lambda i, j: (0, i)),
            pl.BlockSpec((hk, D_MODEL), lambda i, j: (0, 0), pipeline_mode=pl.Buffered(1)),
            pl.BlockSpec((2 * PEER_HEADS, PEER_NKEYS, PEER_DKEY // 2), lambda i, j: (0, 0, 0)),
            pl.BlockSpec((None, et, D_MODEL), lambda i, j: (layer, j, 0)),
            pl.BlockSpec((None, D_MODEL, et), lambda i, j: (layer, 0, j)),
            *extra_specs,
        ],
        out_specs=pl.BlockSpec((tt, D_MODEL), lambda i, j: (i, 0)),
        out_shape=jax.ShapeDtypeStruct((t, D_MODEL), F32),
        scratch_shapes=[
            pltpu.VMEM((hk, tt), BF16),
            sel(F32), sel(F32), sel(PEER_GATE_DTYPE), sel(PEER_GATE_DTYPE),
            pltpu.VMEM((PEER_A // PEER_SUB, PEER_SUB * PEER_NKEYS, tt), PEER_GATE_DTYPE),
            pltpu.VMEM((D_MODEL, tt), F32),
            pltpu.VMEM((et, tt), BF16),
        ],
        compiler_params=_cparams(("parallel", "arbitrary")),
        name="peer",
    )(x, hnt, wqt, sk, u, vt, *extra)


def _final_norm_kernel(x_ref, g_ref, o_ref):
    o_ref[...] = _rms(x_ref[...], g_ref[...])


def _final_norm(x, g):
    t = x.shape[0]
    tm = ROW_TILE
    return pl.pallas_call(
        _final_norm_kernel,
        grid=(t // tm,),
        in_specs=[pl.BlockSpec((tm, D_MODEL), lambda i: (i, 0)), pl.BlockSpec((1, D_MODEL), lambda i: (0, 0))],
        out_specs=pl.BlockSpec((tm, D_MODEL), lambda i: (i, 0)),
        out_shape=jax.ShapeDtypeStruct((t, D_MODEL), F32),
        compiler_params=_cparams(("parallel",)),
        name="final_norm",
    )(x, g)


def _rope_table(pos, dim):
    inv = ROPE_THETA ** (-jnp.arange(0, dim, 2, dtype=F32) / dim)
    ang = pos.astype(F32)[:, None] * inv[None, :]
    ang = jnp.concatenate([ang, ang], axis=-1)
    sign = jnp.concatenate([-jnp.ones((dim // 2,), F32), jnp.ones((dim // 2,), F32)])
    return jnp.cos(ang), jnp.sin(ang) * sign


def _rope_tables(seq):
    pos = jnp.arange(seq)
    cos1, sin1 = _rope_table(pos, HEAD_DIM)
    cr, sr = _rope_table(pos // GRID_W, HEAD_DIM // 2)
    cc, sc = _rope_table(pos % GRID_W, HEAD_DIM // 2)
    cos2 = jnp.concatenate([cr, cc], axis=-1)
    sin2 = jnp.concatenate([sr, sc], axis=-1)
    tile = lambda a: jnp.tile(a, (1, N_Q_HEADS))
    return tile(cos1), tile(sin1), tile(cos2), tile(sin2)


def kernel(x, mix_norm_g, ffn_norm_g, final_norm_g, even_w_in, even_w_out, conv_w, conv_b, conv_ln_g, conv_ln_b, sink_logits, odd_w_in, odd_w_out, q_norm_g, k_norm_g, sgu_ln_g, sgu_ln_b, sgu_w, sgu_b, peer_wq, peer_subkeys, peer_u, peer_v):
    batch, seq, d = x.shape
    depth = mix_norm_g.shape[0]
    t = batch * seq
    cos1, sin1, cos2, sin2 = _rope_tables(seq)
    row = lambda a: a.reshape(1, -1).astype(F32)
    xf = x.reshape(t, d)
    u_all = peer_u.astype(BF16)
    vt_all = jnp.swapaxes(peer_v, 1, 2).astype(BF16)
    for layer in range(depth):
        i = layer // 2
        if layer % 2 == 0:
            a_in, q, k, v = _inproj_even(xf, row(mix_norm_g[layer]), even_w_in[i].astype(BF16), cos1, sin1, seq)
            m1 = _conv_module(a_in, conv_w[i, :, 0, :], row(conv_b[i]), row(conv_ln_g[i]), row(conv_ln_b[i]), batch, seq)
            sink_b = jnp.broadcast_to(sink_logits[i].astype(F32)[:, None], (N_Q_HEADS, LANE))
            m2 = _swa(q, k, v, sink_b, batch, seq)
            w_out = even_w_out[i]
        else:
            q, k, v, d_in = _inproj_odd(xf, row(mix_norm_g[layer]), odd_w_in[i].astype(BF16),
                                        row(jnp.tile(q_norm_g[i], N_Q_HEADS)), row(jnp.tile(k_norm_g[i], N_KV_HEADS)),
                                        cos2, sin2, seq)
            m1 = _dense_attn(q, k, v, batch, seq)
            bs = jnp.broadcast_to(sgu_b[i].astype(F32)[:, :, None], (SGU_GROUPS, CHUNK, MIX_WIDTH // SGU_GROUPS))
            m2 = _sgu(d_in, row(sgu_ln_g[i]), row(sgu_ln_b[i]), sgu_w[i].astype(BF16), bs)
            w_out = odd_w_out[i]
        xn, hnt = _outproj(xf, m1, m2, w_out.astype(BF16), row(ffn_norm_g[layer]))
        sk = peer_subkeys[layer].reshape(2 * PEER_HEADS, PEER_NKEYS, PEER_DKEY // 2).astype(BF16)
        out_g = row(final_norm_g) if layer == depth - 1 else None
        xf = _peer(xn, hnt, peer_wq[layer].astype(BF16).T, sk, u_all, vt_all, layer, out_g)
    if depth == 0:
        xf = _final_norm(xf, row(final_norm_g))
    return xf.reshape(batch, seq, d)
```

```python
import jax
import jax.numpy as jnp
from jax import lax
from jax.experimental import pallas as pl
from jax.experimental.pallas import tpu as pltpu

F32 = jnp.float32
BF16 = jnp.bfloat16

D_MODEL = 1024
HEAD_DIM = 64
N_Q_HEADS = 8
N_KV_HEADS = 2
GQA_GROUPS = N_Q_HEADS // N_KV_HEADS
ATTN_WIDTH = N_Q_HEADS * HEAD_DIM
KV_WIDTH = N_KV_HEADS * HEAD_DIM
MIX_WIDTH = D_MODEL // 2
CONV_WIDTH = 31
CONV_HALO = 16
SGU_GROUPS = 4
CHUNK = 128
WINDOW = 128
BLOCK_Q = 128
GRID_W = 64
ROPE_THETA = 10000.0
PEER_HEADS = 8
PEER_NKEYS = 128
PEER_DKEY = 128
PEER_TOPK = 16
EPS = 1e-6
NEG = -1e30

VMEM_LIMIT_BYTES = 56 * 1024 * 1024

ROW_TILE = 1024
CONV_ROWS = 64
SWA_BLOCKS = 4
ATT_TQ = 1024
ATT_CHAINS = 8
PEER_TT = 512
PEER_A = 16
PEER_SUB = 8
PEER_GATE_BLOCK = 4
PEER_GATE_DTYPE = BF16
LANE = 128
LOG2E = 1.4426950408889634
BF16_ROWS = 16


def _cparams(sem):
    return pltpu.CompilerParams(dimension_semantics=sem, vmem_limit_bytes=VMEM_LIMIT_BYTES)


def _dot(a, b):
    return jnp.dot(a, b, preferred_element_type=F32)


def _rms(x, g):
    return x * lax.rsqrt(jnp.mean(x * x, axis=-1, keepdims=True) + EPS) * g


def _layernorm(x, g, b):
    mu = jnp.mean(x, axis=-1, keepdims=True)
    xc = x - mu
    var = jnp.mean(xc * xc, axis=-1, keepdims=True)
    return xc * lax.rsqrt(var + EPS) * g + b


def _rope(x, cos, sin_signed, group, half):
    n = x.shape[-1]
    lane = lax.broadcasted_iota(jnp.int32, x.shape, x.ndim - 1)
    first = (lane % group) < half
    partner = jnp.where(first, pltpu.roll(x, n - half, x.ndim - 1), pltpu.roll(x, half, x.ndim - 1))
    return x * cos + partner * sin_signed


def _inproj_even_kernel(x_ref, g_ref, w_ref, cos_ref, sin_ref, a_ref, q_ref, k_ref, v_ref):
    h = _rms(x_ref[...], g_ref[...]).astype(BF16)
    o1 = 2 * MIX_WIDTH
    o2 = o1 + ATTN_WIDTH
    o3 = o2 + KV_WIDTH
    a_ref[...] = _dot(h, w_ref[:, :o1])
    cos = cos_ref[...]
    sin = sin_ref[...]
    q = _rope(_dot(h, w_ref[:, o1:o2]), cos, sin, HEAD_DIM, HEAD_DIM // 2)
    q_ref[...] = (q * (HEAD_DIM ** -0.5)).astype(BF16)
    k = _rope(_dot(h, w_ref[:, o2:o3]), cos[:, :KV_WIDTH], sin[:, :KV_WIDTH], HEAD_DIM, HEAD_DIM // 2)
    k_ref[...] = k.astype(BF16)
    v_ref[...] = _dot(h, w_ref[:, o3:]).astype(BF16)


def _inproj_even(x, g, w, cos, sin, seq):
    t = x.shape[0]
    tm = ROW_TILE
    nseq = seq // tm
    in_w = w.shape[1]
    row = lambda i: (i, 0)
    fixed = lambda i: (0, 0)
    pos = lambda i: (i % nseq, 0)
    return pl.pallas_call(
        _inproj_even_kernel,
        grid=(t // tm,),
        in_specs=[
            pl.BlockSpec((tm, D_MODEL), row),
            pl.BlockSpec((1, D_MODEL), fixed),
            pl.BlockSpec((D_MODEL, in_w), fixed),
            pl.BlockSpec((tm, ATTN_WIDTH), pos),
            pl.BlockSpec((tm, ATTN_WIDTH), pos),
        ],
        out_specs=[
            pl.BlockSpec((tm, 2 * MIX_WIDTH), row),
            pl.BlockSpec((tm, ATTN_WIDTH), row),
            pl.BlockSpec((tm, KV_WIDTH), row),
            pl.BlockSpec((tm, KV_WIDTH), row),
        ],
        out_shape=[
            jax.ShapeDtypeStruct((t, 2 * MIX_WIDTH), F32),
            jax.ShapeDtypeStruct((t, ATTN_WIDTH), BF16),
            jax.ShapeDtypeStruct((t, KV_WIDTH), BF16),
            jax.ShapeDtypeStruct((t, KV_WIDTH), BF16),
        ],
        compiler_params=_cparams(("parallel",)),
        name="inproj_even",
    )(x, g, w, cos, sin)


def _head_sumsq(x):
    n = x.shape[-1]
    r = lax.broadcasted_iota(jnp.int32, (n, n), 0) // HEAD_DIM
    c = lax.broadcasted_iota(jnp.int32, (n, n), 1) // HEAD_DIM
    ones = jnp.where(r == c, 1.0, 0.0).astype(BF16)
    sq = x * x
    hi = sq.astype(BF16)
    lo = (sq - hi.astype(F32)).astype(BF16)
    return _dot(hi, ones) + _dot(lo, ones)


def _inproj_odd_kernel(x_ref, g_ref, w_ref, qg_ref, kg_ref, cos_ref, sin_ref, q_ref, k_ref, vt_ref, d_ref):
    h = _rms(x_ref[...], g_ref[...]).astype(BF16)
    c1 = ATTN_WIDTH
    c2 = c1 + KV_WIDTH
    c3 = c2 + KV_WIDTH
    cos = cos_ref[...]
    sin = sin_ref[...]
    half = HEAD_DIM // 2
    q = _dot(h, w_ref[:, :c1])
    q = q * lax.rsqrt(_head_sumsq(q) * (1.0 / HEAD_DIM) + EPS) * qg_ref[...]
    q = _rope(q, cos, sin, half, half // 2)
    q_ref[...] = (q * (HEAD_DIM ** -0.5 * LOG2E)).astype(BF16)
    k = _dot(h, w_ref[:, c1:c2])
    k = k * lax.rsqrt(_head_sumsq(k) * (1.0 / HEAD_DIM) + EPS) * kg_ref[...]
    k = _rope(k, cos[:, :KV_WIDTH], sin[:, :KV_WIDTH], half, half // 2).astype(BF16)
    v_t = _dot(h, w_ref[:, c2:c3]).T.astype(BF16)
    for hh in range(N_KV_HEADS):
        k_ref[hh] = k[:, hh * HEAD_DIM:(hh + 1) * HEAD_DIM]
        vt_ref[hh] = v_t[hh * HEAD_DIM:(hh + 1) * HEAD_DIM, :]
    d_ref[...] = _dot(h, w_ref[:, c3:])


def _inproj_odd(x, g, w, qg, kg, cos, sin, seq):
    t = x.shape[0]
    tm = ROW_TILE
    nseq = seq // tm
    in_w = w.shape[1]
    row = lambda i: (i, 0)
    fixed = lambda i: (0, 0)
    pos = lambda i: (i % nseq, 0)
    kv = lambda i: (0, i, 0)
    return pl.pallas_call(
        _inproj_odd_kernel,
        grid=(t // tm,),
        in_specs=[
            pl.BlockSpec((tm, D_MODEL), row),
            pl.BlockSpec((1, D_MODEL), fixed),
            pl.BlockSpec((D_MODEL, in_w), fixed),
            pl.BlockSpec((1, ATTN_WIDTH), fixed),
            pl.BlockSpec((1, KV_WIDTH), fixed),
            pl.BlockSpec((tm, ATTN_WIDTH), pos),
            pl.BlockSpec((tm, ATTN_WIDTH), pos),
        ],
        out_specs=[
            pl.BlockSpec((tm, ATTN_WIDTH), row),
            pl.BlockSpec((N_KV_HEADS, tm, HEAD_DIM), kv),
            pl.BlockSpec((N_KV_HEADS, HEAD_DIM, tm), lambda i: (0, 0, i)),
            pl.BlockSpec((tm, 2 * MIX_WIDTH), row),
        ],
        out_shape=[
            jax.ShapeDtypeStruct((t, ATTN_WIDTH), BF16),
            jax.ShapeDtypeStruct((N_KV_HEADS, t, HEAD_DIM), BF16),
            jax.ShapeDtypeStruct((N_KV_HEADS, HEAD_DIM, t), BF16),
            jax.ShapeDtypeStruct((t, 2 * MIX_WIDTH), F32),
        ],
        compiler_params=_cparams(("parallel",)),
        name="inproj_odd",
    )(x, g, w, qg, kg, cos, sin)


def _glu(x):
    return x[:, :MIX_WIDTH] * jax.nn.sigmoid(x[:, MIX_WIDTH:])


def _conv_kernel(prev_ref, cur_ref, next_ref, w_ref, b_ref, g_ref, beta_ref, o_ref, buf_ref):
    i = pl.program_id(1)
    ns = pl.num_programs(1)
    ts = cur_ref.shape[0]
    halo = CONV_HALO
    buf_ref[0, 0:halo, :] = jnp.where(i > 0, _glu(prev_ref[...]), 0.0)
    buf_ref[0, halo:halo + ts, :] = _glu(cur_ref[...])
    buf_ref[0, halo + ts:, :] = jnp.where(i < ns - 1, _glu(next_ref[...]), 0.0)
    span = ts + 2 * halo - SUBLANES
    for s in range(1, SUBLANES):
        buf_ref[s, 0:span, :] = buf_ref[0, s:s + span, :]
    w = w_ref[...]
    bias = b_ref[...]
    g = g_ref[...]
    beta = beta_ref[...]
    first = halo - CONV_WIDTH // 2
    for c in range(ts // CONV_ROWS):
        r0 = c * CONV_ROWS
        acc = jnp.zeros((CONV_ROWS, MIX_WIDTH), F32)
        for k in range(CONV_WIDTH):
            m, s = divmod(first + k, SUBLANES)
            lo = r0 + SUBLANES * m
            acc = acc + w[k:k + 1, :] * buf_ref[s, lo:lo + CONV_ROWS, :]
        y = _layernorm(acc + bias, g, beta)
        o_ref[r0:r0 + CONV_ROWS, :] = (y * jax.nn.sigmoid(y)).astype(o_ref.dtype)


def _conv_module(a_in, w, b, g, beta, batch, seq):
    t = a_in.shape[0]
    ts = ROW_TILE
    ns = seq // ts
    hb = ts // CONV_HALO
    last = t // CONV_HALO - 1
    fixed = lambda bb, i: (0, 0)
    return pl.pallas_call(
        _conv_kernel,
        grid=(batch, ns),
        in_specs=[
            pl.BlockSpec((CONV_HALO, 2 * MIX_WIDTH), lambda bb, i: (jnp.maximum((bb * ns + i) * hb - 1, 0), 0)),
            pl.BlockSpec((ts, 2 * MIX_WIDTH), lambda bb, i: (bb * ns + i, 0)),
            pl.BlockSpec((CONV_HALO, 2 * MIX_WIDTH), lambda bb, i: (jnp.minimum((bb * ns + i + 1) * hb, last), 0)),
            pl.BlockSpec((CONV_WIDTH, MIX_WIDTH), fixed),
            pl.BlockSpec((1, MIX_WIDTH), fixed),
            pl.BlockSpec((1, MIX_WIDTH), fixed),
            pl.BlockSpec((1, MIX_WIDTH), fixed),
        ],
        out_specs=pl.BlockSpec((ts, MIX_WIDTH), lambda bb, i: (bb * ns + i, 0)),
        out_shape=jax.ShapeDtypeStruct((t, MIX_WIDTH), BF16),
        scratch_shapes=[pltpu.VMEM((SUBLANES, ts + 2 * CONV_HALO, MIX_WIDTH), F32)],
        compiler_params=_cparams(("parallel", "parallel")),
        name="conv_module",
    )(a_in, a_in, a_in, w, b, g, beta)


def _swa_kernel(q_ref, kp_ref, kc_ref, kn_ref, vp_ref, vc_ref, vn_ref, sink_ref, o_ref):
    n = pl.program_id(1)
    nsteps = pl.num_programs(1)
    bq = BLOCK_Q
    cols = GQA_GROUPS * bq
    jj = lax.broadcasted_iota(jnp.int32, (3 * bq, cols), 0)
    ii = lax.broadcasted_iota(jnp.int32, (3 * bq, cols), 1) % bq
    band = (jj >= ii) & (jj <= ii + 2 * WINDOW)
    kcat = jnp.concatenate([kp_ref[...], kc_ref[...], kn_ref[...]], axis=0)
    vcat_t = jnp.concatenate([vp_ref[...], vc_ref[...], vn_ref[...]], axis=0).astype(F32).T.astype(BF16)
    for qb in range(SWA_BLOCKS):
        mask = band
        if qb == 0:
            mask = mask & (jj >= jnp.where(n > 0, 0, bq))
        if qb == SWA_BLOCKS - 1:
            mask = mask & (jj < jnp.where(n < nsteps - 1, 3 * bq, 2 * bq))
        q_t = q_ref[qb * bq:(qb + 1) * bq, :].astype(F32).T.astype(BF16)
        outs = []
        for hk in range(N_KV_HEADS):
            heads = range(hk * GQA_GROUPS, (hk + 1) * GQA_GROUPS)
            kh = kcat[qb * bq:(qb + 3) * bq, hk * HEAD_DIM:(hk + 1) * HEAD_DIM]
            vh_t = jnp.concatenate([vcat_t[hk * HEAD_DIM:(hk + 1) * HEAD_DIM, qb * bq:(qb + 3) * bq],
                                    jnp.ones((BF16_ROWS, 3 * bq), BF16)], axis=0)
            qs_t = jnp.concatenate([q_t[hd * HEAD_DIM:(hd + 1) * HEAD_DIM, :] for hd in heads], axis=1)
            sink = jnp.concatenate([jnp.broadcast_to(sink_ref[hd:hd + 1, 0:1], (1, bq)) for hd in heads], axis=1)
            s = jnp.where(mask, _dot(kh, qs_t), NEG)
            m = jnp.maximum(jnp.max(s, axis=0, keepdims=True), sink)
            o_ext = _dot(vh_t, jnp.exp(s - m).astype(BF16))
            denom = o_ext[HEAD_DIM:HEAD_DIM + 1, :] + jnp.exp(sink - m)
            o_t = o_ext[:HEAD_DIM, :] / denom
            outs.extend(o_t[:, g * bq:(g + 1) * bq] for g in range(GQA_GROUPS))
        o_ref[qb * bq:(qb + 1) * bq, :] = jnp.concatenate(outs, axis=0).T.astype(o_ref.dtype)


def _swa(q, k, v, sink_b, batch, seq):
    t = q.shape[0]
    bq = BLOCK_Q
    tq = SWA_BLOCKS * bq
    ns = seq // tq
    last = t // bq - 1
    cur = lambda bb, n: (bb * ns + n, 0)
    prev = lambda bb, n: (jnp.maximum((bb * ns + n) * SWA_BLOCKS - 1, 0), 0)
    nxt = lambda bb, n: (jnp.minimum((bb * ns + n + 1) * SWA_BLOCKS, last), 0)
    halo = lambda im: pl.BlockSpec((bq, KV_WIDTH), im)
    body = pl.BlockSpec((tq, KV_WIDTH), cur)
    return pl.pallas_call(
        _swa_kernel,
        grid=(batch, ns),
        in_specs=[pl.BlockSpec((tq, ATTN_WIDTH), cur), halo(prev), body, halo(nxt), halo(prev), body, halo(nxt),
                  pl.BlockSpec((N_Q_HEADS, LANE), lambda bb, n: (0, 0))],
        out_specs=pl.BlockSpec((tq, ATTN_WIDTH), cur),
        out_shape=jax.ShapeDtypeStruct((t, ATTN_WIDTH), BF16),
        compiler_params=_cparams(("parallel", "parallel")),
        name="window_attn",
    )(q, k, k, k, v, v, v, sink_b)


def _dense_attn_kernel(q_ref, k_ref, vt_ref, o_ref):
    ts = q_ref.shape[0] // ATT_CHAINS
    k = k_ref[...]
    v_ext = jnp.concatenate([vt_ref[...], jnp.ones((BF16_ROWS, vt_ref.shape[1]), BF16)], axis=0)
    q_t = q_ref[...].astype(F32).T.astype(BF16)

    def scores(c):
        cols = slice(c * ts, (c + 1) * ts)
        qs_t = jnp.concatenate([q_t[g * HEAD_DIM:(g + 1) * HEAD_DIM, cols] for g in range(GQA_GROUPS)], axis=1)
        return _dot(k, qs_t)

    s_next = scores(0)
    for c in range(ATT_CHAINS):
        cols = slice(c * ts, (c + 1) * ts)
        s = s_next
        if c + 1 < ATT_CHAINS:
            s_next = scores(c + 1)
        p = jnp.exp2(s - jnp.max(s, axis=0, keepdims=True))
        o_ext = _dot(v_ext, p.astype(BF16))
        o_t = o_ext[:HEAD_DIM, :] / o_ext[HEAD_DIM:HEAD_DIM + 1, :]
        o_ref[cols, :] = jnp.concatenate(
            [o_t[:, g * ts:(g + 1) * ts] for g in range(GQA_GROUPS)], axis=0).T.astype(o_ref.dtype)


def _dense_attn(q, k, v, batch, seq):
    t = q.shape[0]
    tq = min(ATT_TQ, seq)
    nq = seq // tq
    gw = GQA_GROUPS * HEAD_DIM
    qmap = lambda bb, hk, qi: (bb * nq + qi, hk)
    return pl.pallas_call(
        _dense_attn_kernel,
        grid=(batch, N_KV_HEADS, nq),
        in_specs=[
            pl.BlockSpec((tq, gw), qmap),
            pl.BlockSpec((None, seq, HEAD_DIM), lambda bb, hk, qi: (hk, bb, 0)),
            pl.BlockSpec((None, HEAD_DIM, seq), lambda bb, hk, qi: (hk, 0, bb)),
        ],
        out_specs=pl.BlockSpec((tq, gw), qmap),
        out_shape=jax.ShapeDtypeStruct((t, ATTN_WIDTH), BF16),
        compiler_params=_cparams(("parallel", "parallel", "parallel")),
        name="dense_attn",
    )(q, k, v)


def _sgu_kernel(d_ref, g_ref, b_ref, w_ref, bs_ref, o_ref):
    z = jax.nn.gelu(d_ref[...])
    u = z[:, :MIX_WIDTH]
    v = _layernorm(z[:, MIX_WIDTH:], g_ref[...], b_ref[...]).astype(BF16)
    gc = MIX_WIDTH // SGU_GROUPS
    for n in range(d_ref.shape[0] // CHUNK):
        rows = slice(n * CHUNK, (n + 1) * CHUNK)
        for g in range(SGU_GROUPS):
            cols = slice(g * gc, (g + 1) * gc)
            mixed = _dot(w_ref[g], v[rows, cols]) + bs_ref[g]
            o_ref[rows, cols] = (u[rows, cols] * mixed).astype(o_ref.dtype)


def _sgu(d_in, g, b, w, bs):
    t = d_in.shape[0]
    ts = ROW_TILE
    gc = MIX_WIDTH // SGU_GROUPS
    row = lambda i: (i, 0)
    fixed = lambda i: (0, 0)
    fixed3 = lambda i: (0, 0, 0)
    return pl.pallas_call(
        _sgu_kernel,
        grid=(t // ts,),
        in_specs=[
            pl.BlockSpec((ts, 2 * MIX_WIDTH), row),
            pl.BlockSpec((1, MIX_WIDTH), fixed),
            pl.BlockSpec((1, MIX_WIDTH), fixed),
            pl.BlockSpec((SGU_GROUPS, CHUNK, CHUNK), fixed3),
            pl.BlockSpec((SGU_GROUPS, CHUNK, gc), fixed3),
        ],
        out_specs=pl.BlockSpec((ts, MIX_WIDTH), row),
        out_shape=jax.ShapeDtypeStruct((t, MIX_WIDTH), BF16),
        compiler_params=_cparams(("parallel",)),
        name="spatial_gating",
    )(d_in, g, b, w, bs)


def _outproj_kernel(x_ref, a_ref, b_ref, w_ref, g_ref, xn_ref, hnt_ref):
    y = _dot(a_ref[...], w_ref[:MIX_WIDTH, :]) + _dot(b_ref[...], w_ref[MIX_WIDTH:, :])
    xn = x_ref[...] + y
    xn_ref[...] = xn
    hnt_ref[...] = _rms(xn, g_ref[...]).T.astype(BF16)


def _outproj(x, a, b, w, g):
    t = x.shape[0]
    tm = ROW_TILE
    row = lambda i: (i, 0)
    fixed = lambda i: (0, 0)
    return pl.pallas_call(
        _outproj_kernel,
        grid=(t // tm,),
        in_specs=[
            pl.BlockSpec((tm, D_MODEL), row),
            pl.BlockSpec((tm, MIX_WIDTH), row),
            pl.BlockSpec((tm, MIX_WIDTH), row),
            pl.BlockSpec((2 * MIX_WIDTH, D_MODEL), fixed),
            pl.BlockSpec((1, D_MODEL), fixed),
        ],
        out_specs=[pl.BlockSpec((tm, D_MODEL), row), pl.BlockSpec((D_MODEL, tm), lambda i: (0, i))],
        out_shape=[jax.ShapeDtypeStruct((t, D_MODEL), F32), jax.ShapeDtypeStruct((D_MODEL, t), BF16)],
        compiler_params=_cparams(("parallel",)),
        name="outproj",
    )(x, a, b, w, g)


_PEER_PAIRS = [(i, j) for i in range(PEER_TOPK) for j in range(PEER_TOPK) if (i + 1) * (j + 1) <= PEER_TOPK]
_PEER_CAND_ROWS = -(-len(_PEER_PAIRS) // 8) * 8


SUBLANES = 8


def _bitonic_merge_desc(v, lo, n):
    step = n // 2
    while step >= 1:
        for i in range(lo, lo + n):
            if (i - lo) & step == 0:
                hi_, lo_ = jnp.maximum(v[i], v[i + step]), jnp.minimum(v[i], v[i + step])
                v[i], v[i + step] = hi_, lo_
        step //= 2


def _oddeven_merge(lo, hi, r):
    step = r * 2
    if step < hi - lo:
        yield from _oddeven_merge(lo, hi, step)
        yield from _oddeven_merge(lo + r, hi, step)
        yield from [(i, i + r) for i in range(lo + r, hi - r, step)]
    else:
        yield (lo, lo + r)


def _oddeven_merge_sort(lo, hi):
    if hi - lo >= 1:
        mid = lo + (hi - lo) // 2
        yield from _oddeven_merge_sort(lo, mid)
        yield from _oddeven_merge_sort(mid + 1, hi)
        yield from _oddeven_merge(lo, hi, 1)


def _sort16_desc(v):
    for i, j in _oddeven_merge_sort(0, len(v) - 1):
        v[i], v[j] = jnp.maximum(v[i], v[j]), jnp.minimum(v[i], v[j])
    return v


def _rank_among(b, v):
    lt = lambda t: t > b
    c1 = lt(v[7])
    c2 = lt(jnp.where(c1, v[11], v[3]))
    c3 = lt(jnp.where(c1, jnp.where(c2, v[13], v[9]), jnp.where(c2, v[5], v[1])))
    hi4 = jnp.where(c2, jnp.where(c3, v[14], v[12]), jnp.where(c3, v[10], v[8]))
    lo4 = jnp.where(c2, jnp.where(c3, v[6], v[4]), jnp.where(c3, v[2], v[0]))
    c4 = lt(jnp.where(c1, hi4, lo4))
    one = lambda c, x: jnp.where(c, x, 0.0)
    r = one(c1, 8.0) + one(c2, 4.0) + one(c3, 2.0) + one(c4, 1.0)
    return jnp.where(lt(v[15]), float(PEER_TOPK), r)


def _top16_sorted(s):
    v = _sort16_desc([s[k * SUBLANES:(k + 1) * SUBLANES, :] for k in range(PEER_NKEYS // SUBLANES)])
    for shift in (4, 2, 1):
        v = [jnp.maximum(v[k], pltpu.roll(v[PEER_TOPK - 1 - k], shift, 0)) for k in range(PEER_TOPK)]
        _bitonic_merge_desc(v, 0, PEER_TOPK)
    return v


def _peer_select(h, s1_all, s2_all, n_ref, e1_ref, r2_ref, e2_ref):
    tt = s1_all.shape[-1]
    nv = PEER_NKEYS // SUBLANES
    sub = lax.broadcasted_iota(jnp.int32, (SUBLANES, LANE), 0)
    for tc in range(tt // LANE):
        lanes = slice(tc * LANE, (tc + 1) * LANE)
        s1 = s1_all[:, lanes]
        s2 = s2_all[:, lanes]
        v1 = _top16_sorted(s1)
        v2 = _top16_sorted(s2)
        pair_sum = {p: v1[p[0]] + v2[p[1]] for p in _PEER_PAIRS}
        cand = []
        for g in range(_PEER_CAND_ROWS // SUBLANES):
            c = jnp.full((SUBLANES, LANE), -jnp.inf, F32)
            for r, p in enumerate(_PEER_PAIRS[g * SUBLANES:(g + 1) * SUBLANES]):
                c = jnp.where(sub == r, pair_sum[p], c)
            cand.append(c)
        cand.append(jnp.full((SUBLANES, LANE), -jnp.inf, F32))
        f = _sort16_desc(cand)
        f = f + [pltpu.roll(c, 4, 0) for c in reversed(f)]
        _bitonic_merge_desc(f, 0, PEER_TOPK)
        for shift in (2, 1):
            f = [jnp.maximum(f[k], pltpu.roll(f[PEER_TOPK - 1 - k], shift, 0)) for k in range(PEER_TOPK)]
            _bitonic_merge_desc(f, 0, PEER_TOPK)
        thr = f[PEER_TOPK - 1]
        z = jnp.ones((SUBLANES, LANE), F32)
        for r in range(1, PEER_TOPK):
            z = z + jnp.exp(f[r] - f[0])
        rz = 1.0 / z
        cnt = []
        for i in range(PEER_TOPK):
            c = jnp.zeros((SUBLANES, LANE), F32)
            for j in range(PEER_TOPK // (i + 1)):
                c = c + jnp.where(pair_sum[(i, j)] >= thr, 1.0, 0.0)
            cnt.append(c)
        ranks = []
        e2s = []
        for k in range(nv):
            rows = slice(k * SUBLANES, (k + 1) * SUBLANES)
            a = s1[rows, :]
            b = s2[rows, :]
            n = jnp.zeros((SUBLANES, LANE), F32)
            for r in range(PEER_TOPK):
                n = jnp.where(a == v1[r], cnt[r], n)
            n_ref[h, rows, lanes] = n
            e1_ref[h, rows, lanes] = jnp.exp(a - v1[0]) * rz
            ranks.append(_rank_among(b, v2))
            e2s.append(jnp.exp(b - v2[0]))
        r2_ref[h, :, lanes] = jnp.concatenate(ranks, axis=0).astype(r2_ref.dtype)
        e2_ref[h, :, lanes] = jnp.concatenate(e2s, axis=0).astype(e2_ref.dtype)


def _peer_gates(step, sub, n_ref, e1_ref, r2_ref, e2_ref, g_ref):
    tt = g_ref.shape[-1]
    gdt = g_ref.dtype
    first = sub * PEER_SUB
    a8 = pl.multiple_of(step * PEER_A + first // SUBLANES * SUBLANES, SUBLANES)
    nload = -(-(first % SUBLANES + PEER_SUB) // SUBLANES) * SUBLANES
    for tc in range(tt // LANE):
        lanes = slice(tc * LANE, (tc + 1) * LANE)
        nrows = [n_ref[h, pl.ds(a8, nload), lanes].astype(gdt) for h in range(PEER_HEADS)]
        erows = [e1_ref[h, pl.ds(a8, nload), lanes].astype(gdt) for h in range(PEER_HEADS)]
        for blk in range(PEER_SUB // PEER_GATE_BLOCK):
            gates = [jnp.zeros((PEER_NKEYS, LANE), gdt) for _ in range(PEER_GATE_BLOCK)]
            for h in range(PEER_HEADS):
                r2 = r2_ref[h, :, lanes]
                e2 = e2_ref[h, :, lanes]
                for k in range(PEER_GATE_BLOCK):
                    al = first % SUBLANES + blk * PEER_GATE_BLOCK + k
                    keep = r2 < nrows[h][al:al + 1, :]
                    gates[k] = gates[k] + jnp.where(keep, e2, 0.0) * erows[h][al:al + 1, :]
            for k in range(PEER_GATE_BLOCK):
                al = blk * PEER_GATE_BLOCK + k
                g_ref[sub, al * PEER_NKEYS:(al + 1) * PEER_NKEYS, lanes] = gates[k]


def _peer_kernel(x_ref, hnt_ref, wqt_ref, sk_ref, u_ref, vt_ref, *rest):
    out_g_ref = rest[0] if len(rest) == 10 else None
    o_ref, qt_ref, n_ref, e1_ref, r2_ref, e2_ref, g_ref, acc_ref, ht_ref = rest[-9:]
    j = pl.program_id(1)
    nj = pl.num_programs(1)
    half = PEER_DKEY // 2
    gdt = r2_ref.dtype
    gate_refs = (n_ref, e1_ref, r2_ref, e2_ref, g_ref)

    @pl.when(j == 0)
    def _():
        qt_ref[...] = _dot(wqt_ref[...], hnt_ref[...]).astype(BF16)

        def body(h, carry):
            r0 = pl.multiple_of(h * PEER_DKEY, PEER_DKEY)
            s1 = _dot(sk_ref[2 * h], qt_ref[pl.ds(r0, half), :])
            s2 = _dot(sk_ref[2 * h + 1], qt_ref[pl.ds(r0 + half, half), :])
            _peer_select(h, s1, s2, n_ref, e1_ref, r2_ref, e2_ref)
            return carry

        lax.fori_loop(0, PEER_HEADS, body, 0, unroll=True)
        acc_ref[...] = jnp.zeros(acc_ref.shape, F32)

    sub_rows = lambda sub: slice(sub * PEER_SUB * PEER_NKEYS, (sub + 1) * PEER_SUB * PEER_NKEYS)
    for sub in range(PEER_A // PEER_SUB):
        _peer_gates(j, sub, *gate_refs)
    for sub in range(PEER_A // PEER_SUB):
        at = _dot(u_ref[sub_rows(sub), :], hnt_ref[...]).astype(gdt)
        ht_ref[sub_rows(sub), :] = jax.nn.gelu(at) * g_ref[sub]
    acc_ref[...] += _dot(vt_ref[...], ht_ref[...])

    @pl.when(j == nj - 1)
    def _():
        y = x_ref[...] + acc_ref[...].T
        o_ref[...] = y if out_g_ref is None else _rms(y, out_g_ref[...])


def _peer(x, hnt, wqt, sk, u, vt, layer, out_g=None):
    t = x.shape[0]
    extra = [] if out_g is None else [out_g]
    extra_specs = [] if out_g is None else [pl.BlockSpec((1, D_MODEL), lambda i, j: (0, 0))]
    tt = PEER_TT
    ne = u.shape[1]
    et = PEER_A * PEER_NKEYS
    hk = PEER_HEADS * PEER_DKEY
    sel = lambda dt: pltpu.VMEM((PEER_HEADS, PEER_NKEYS, tt), dt)
    return pl.pallas_call(
        _peer_kernel,
        grid=(t // tt, ne // et),
        in_specs=[
            pl.BlockSpec((tt, D_MODEL), lambda i, j: (i, 0)),
            pl.BlockSpec((D_MODEL, tt), lambda i, j: (0, i)),
            pl.BlockSpec((hk, D_MODEL), lambda i, j: (0, 0), pipeline_mode=pl.Buffered(1)),
            pl.BlockSpec((2 * PEER_HEADS, PEER_NKEYS, PEER_DKEY // 2), lambda i, j: (0, 0, 0)),
            pl.BlockSpec((None, et, D_MODEL), lambda i, j: (layer, j, 0)),
            pl.BlockSpec((None, D_MODEL, et), lambda i, j: (layer, 0, j)),
            *extra_specs,
        ],
        out_specs=pl.BlockSpec((tt, D_MODEL), lambda i, j: (i, 0)),
        out_shape=jax.ShapeDtypeStruct((t, D_MODEL), F32),
        scratch_shapes=[
            pltpu.VMEM((hk, tt), BF16),
            sel(F32), sel(F32), sel(PEER_GATE_DTYPE), sel(PEER_GATE_DTYPE),
            pltpu.VMEM((PEER_A // PEER_SUB, PEER_SUB * PEER_NKEYS, tt), PEER_GATE_DTYPE),
            pltpu.VMEM((D_MODEL, tt), F32),
            pltpu.VMEM((et, tt), BF16),
        ],
        compiler_params=_cparams(("parallel", "arbitrary")),
        name="peer",
    )(x, hnt, wqt, sk, u, vt, *extra)


def _final_norm_kernel(x_ref, g_ref, o_ref):
    o_ref[...] = _rms(x_ref[...], g_ref[...])


def _final_norm(x, g):
    t = x.shape[0]
    tm = ROW_TILE
    return pl.pallas_call(
        _final_norm_kernel,
        grid=(t // tm,),
        in_specs=[pl.BlockSpec((tm, D_MODEL), lambda i: (i, 0)), pl.BlockSpec((1, D_MODEL), lambda i: (0, 0))],
        out_specs=pl.BlockSpec((tm, D_MODEL), lambda i: (i, 0)),
        out_shape=jax.ShapeDtypeStruct((t, D_MODEL), F32),
        compiler_params=_cparams(("parallel",)),
        name="final_norm",
    )(x, g)


def _rope_table(pos, dim):
    inv = ROPE_THETA ** (-jnp.arange(0, dim, 2, dtype=F32) / dim)
    ang = pos.astype(F32)[:, None] * inv[None, :]
    ang = jnp.concatenate([ang, ang], axis=-1)
    sign = jnp.concatenate([-jnp.ones((dim // 2,), F32), jnp.ones((dim // 2,), F32)])
    return jnp.cos(ang), jnp.sin(ang) * sign


def _rope_tables(seq):
    pos = jnp.arange(seq)
    cos1, sin1 = _rope_table(pos, HEAD_DIM)
    cr, sr = _rope_table(pos // GRID_W, HEAD_DIM // 2)
    cc, sc = _rope_table(pos % GRID_W, HEAD_DIM // 2)
    cos2 = jnp.concatenate([cr, cc], axis=-1)
    sin2 = jnp.concatenate([sr, sc], axis=-1)
    tile = lambda a: jnp.tile(a, (1, N_Q_HEADS))
    return tile(cos1), tile(sin1), tile(cos2), tile(sin2)


def kernel(x, mix_norm_g, ffn_norm_g, final_norm_g, even_w_in, even_w_out, conv_w, conv_b, conv_ln_g, conv_ln_b, sink_logits, odd_w_in, odd_w_out, q_norm_g, k_norm_g, sgu_ln_g, sgu_ln_b, sgu_w, sgu_b, peer_wq, peer_subkeys, peer_u, peer_v):
    batch, seq, d = x.shape
    depth = mix_norm_g.shape[0]
    t = batch * seq
    cos1, sin1, cos2, sin2 = _rope_tables(seq)
    row = lambda a: a.reshape(1, -1).astype(F32)
    xf = x.reshape(t, d)
    u_all = peer_u.astype(BF16)
    vt_all = jnp.swapaxes(peer_v, 1, 2).astype(BF16)
    for layer in range(depth):
        i = layer // 2
        if layer % 2 == 0:
            a_in, q, k, v = _inproj_even(xf, row(mix_norm_g[layer]), even_w_in[i].astype(BF16), cos1, sin1, seq)
            m1 = _conv_module(a_in, conv_w[i, :, 0, :], row(conv_b[i]), row(conv_ln_g[i]), row(conv_ln_b[i]), batch, seq)
            sink_b = jnp.broadcast_to(sink_logits[i].astype(F32)[:, None], (N_Q_HEADS, LANE))
            m2 = _swa(q, k, v, sink_b, batch, seq)
            w_out = even_w_out[i]
        else:
            q, k, v, d_in = _inproj_odd(xf, row(mix_norm_g[layer]), odd_w_in[i].astype(BF16),
                                        row(jnp.tile(q_norm_g[i], N_Q_HEADS)), row(jnp.tile(k_norm_g[i], N_KV_HEADS)),
                                        cos2, sin2, seq)
            m1 = _dense_attn(q, k, v, batch, seq)
            bs = jnp.broadcast_to(sgu_b[i].astype(F32)[:, :, None], (SGU_GROUPS, CHUNK, MIX_WIDTH // SGU_GROUPS))
            m2 = _sgu(d_in, row(sgu_ln_g[i]), row(sgu_ln_b[i]), sgu_w[i].astype(BF16), bs)
            w_out = odd_w_out[i]
        xn, hnt = _outproj(xf, m1, m2, w_out.astype(BF16), row(ffn_norm_g[layer]))
        sk = peer_subkeys[layer].reshape(2 * PEER_HEADS, PEER_NKEYS, PEER_DKEY // 2).astype(BF16)
        out_g = row(final_norm_g) if layer == depth - 1 else None
        xf = _peer(xn, hnt, peer_wq[layer].astype(BF16).T, sk, u_all, vt_all, layer, out_g)
    if depth == 0:
        xf = _final_norm(xf, row(final_norm_g))
    return xf.reshape(batch, seq, d)
```
